```python
import math
import jax, jax.numpy as jnp
from jax import lax
import numpy as np

D_MODEL = 1024
BATCH = 16
SEQ = 2048
DEPTH = 2

CHUNK = 64
Q_BLOCK = 128
ROPE_THETA = 500000.0
NORM_EPS = 1e-6
ROPE_FRACTION_DEN = 4

DSA_HEADS = 4
DSA_HEAD_DIM = 64
DSA_IDX_HEADS = 8
DSA_IDX_DIM = 64
DSA_TOPK_MAX = 256
DSA_IDX_SCALE = (DSA_IDX_HEADS * DSA_IDX_DIM) ** -0.5

DIFF_HEADS = 4
DIFF_QK_DIM = 32
DIFF_V_DIM = 2 * DIFF_QK_DIM

MLA_HEADS = 8
MLA_Q_LORA = 256
MLA_KV_LORA = 128
MLA_NOPE_DIM = 64
MLA_ROPE_DIM = 32
MLA_V_DIM = 64

D_MIX = DSA_HEADS * DSA_HEAD_DIM + DIFF_HEADS * DIFF_V_DIM + MLA_HEADS * MLA_V_DIM

IN_SPLITS = (
    DSA_HEADS * DSA_HEAD_DIM,
    DSA_HEAD_DIM,
    DSA_HEAD_DIM,
    DSA_IDX_HEADS * DSA_IDX_DIM,
    DSA_IDX_DIM,
    DSA_IDX_HEADS,
    DIFF_HEADS * 2 * DIFF_QK_DIM,
    DIFF_HEADS * 2 * DIFF_QK_DIM,
    DIFF_HEADS * DIFF_V_DIM,
    MLA_Q_LORA,
    MLA_KV_LORA,
    MLA_ROPE_DIM,
)
N_IN = sum(IN_SPLITS)

D_FF = 3584
MOE_EXPERTS = 8
MOE_TOP_K = 2
MOE_D_FF = 3584
N_DENSE = (DEPTH + 1) // 2
N_MOE = DEPTH // 2

kernel_name = 'hybrid_dsa_diff_mla_moe_block'


def rms_norm(x, g):
    xf = x.astype(jnp.float32)
    y = xf * lax.rsqrt(jnp.mean(xf * xf, axis=-1, keepdims=True) + NORM_EPS)
    return (y * g.astype(jnp.float32)).astype(x.dtype)


def rope(x, pos, rot_dim):
    half = rot_dim // 2
    inv_freq = ROPE_THETA ** (-(jnp.arange(half, dtype=jnp.float32) * 2.0 / rot_dim))
    ang = pos.astype(jnp.float32)[..., None] * inv_freq
    ang = ang.reshape(ang.shape[:2] + (1,) * (x.ndim - 3) + (half,))
    cos, sin = jnp.cos(ang), jnp.sin(ang)
    xr = x[..., :rot_dim].astype(jnp.float32)
    x1, x2 = xr[..., :half], xr[..., half:]
    rot = jnp.concatenate([x1 * cos - x2 * sin, x2 * cos + x1 * sin], axis=-1).astype(x.dtype)
    return jnp.concatenate([rot, x[..., rot_dim:]], axis=-1)


def chunk_mask(blk, s_len):
    t = blk * Q_BLOCK + jnp.arange(Q_BLOCK)
    s = jnp.arange(s_len)
    return (s[None, :] // CHUNK) <= (t[:, None] // CHUNK)


def masked_softmax(logits, mask):
    return jax.nn.softmax(jnp.where(mask, logits.astype(jnp.float32), -jnp.inf), axis=-1)


def sweep_query_blocks(fn, *q_arrays):
    b, s = q_arrays[0].shape[:2]
    nb = s // Q_BLOCK
    xs = tuple(jnp.swapaxes(a.reshape((b, nb, Q_BLOCK) + a.shape[2:]), 0, 1) for a in q_arrays)
    out = lax.map(lambda args: fn(args[0], *args[1]), (jnp.arange(nb), xs))
    return jnp.swapaxes(out, 0, 1).reshape((b, s) + out.shape[3:])


def dsa_attention(q, k, v, q_idx, k_idx, w_idx):
    s_len = k.shape[1]
    top_k = min(DSA_TOPK_MAX, s_len // 4)

    def block(blk, qb, qib, wb):
        mask = chunk_mask(blk, s_len)
        rel = jax.nn.relu(jnp.einsum('bqhd,bsd->bqhs', qib, k_idx).astype(jnp.float32))
        score = jnp.einsum('bqh,bqhs->bqs', wb.astype(jnp.float32), rel) * DSA_IDX_SCALE
        score = jnp.where(mask[None], score, -jnp.inf)
        _, idx = lax.top_k(score, top_k)
        t = blk * Q_BLOCK + jnp.arange(Q_BLOCK)
        valid = (idx // CHUNK) <= (t // CHUNK)[None, :, None]
        k_sel = jax.vmap(lambda kb, ib: kb[ib])(k, idx)
        v_sel = jax.vmap(lambda vb, ib: vb[ib])(v, idx)
        logits = jnp.einsum('bqhd,bqkd->bqhk', qb, k_sel) * DSA_HEAD_DIM ** -0.5
        p = masked_softmax(logits, valid[:, :, None, :])
        return jnp.einsum('bqhk,bqkd->bqhd', p.astype(v.dtype), v_sel)

    return sweep_query_blocks(block, q, q_idx, w_idx)


def diff_attention(q, k, v, lam, lam_init, subln):
    s_len = k.shape[1]

    def block(blk, qb):
        mask = chunk_mask(blk, s_len)
        logits = jnp.einsum('bqhcd,bshcd->bhcqs', qb, k) * DIFF_QK_DIM ** -0.5
        p = masked_softmax(logits, mask)
        a = p[:, :, 0] - lam * p[:, :, 1]
        return jnp.einsum('bhqs,bshd->bqhd', a.astype(v.dtype), v)

    o = sweep_query_blocks(block, q)
    return rms_norm(o, subln) * (1.0 - lam_init)


def mla_attention(q_nope, q_rope, k_nope, k_rope, v):
    s_len = k_nope.shape[1]
    scale = (MLA_NOPE_DIM + MLA_ROPE_DIM) ** -0.5

    def block(blk, qn, qr):
        mask = chunk_mask(blk, s_len)
        logits = (jnp.einsum('bqhd,bshd->bhqs', qn, k_nope)
                  + jnp.einsum('bqhd,bsd->bhqs', qr, k_rope)) * scale
        p = masked_softmax(logits, mask)
        return jnp.einsum('bhqs,bshd->bqhd', p.astype(v.dtype), v)

    return sweep_query_blocks(block, q_nope, q_rope)


def hybrid_mixer(h, pos, layer, w_in, q_norm, w_uq, kv_norm, w_ukv,
                 lam_q1, lam_k1, lam_q2, lam_k2, subln, w_out):
    b, s, _ = h.shape
    f32 = jnp.float32
    rf = ROPE_FRACTION_DEN
    splits = np.cumsum(IN_SPLITS)[:-1].tolist()
    (q_a, k_a, v_a, q_i, k_i, w_i, q_b, k_b, v_b, c_q, c_kv, k_r) = jnp.split(h @ w_in, splits, axis=-1)
    q_a = rope(q_a.reshape(b, s, DSA_HEADS, DSA_HEAD_DIM), pos, DSA_HEAD_DIM // rf)
    k_a = rope(k_a, pos, DSA_HEAD_DIM // rf)
    q_i = rope(q_i.reshape(b, s, DSA_IDX_HEADS, DSA_IDX_DIM), pos, DSA_IDX_DIM // rf)
    k_i = rope(k_i, pos, DSA_IDX_DIM // rf)
    o_a = dsa_attention(q_a, k_a, v_a, q_i, k_i, w_i)
    q_b = rope(q_b.reshape(b, s, DIFF_HEADS, 2, DIFF_QK_DIM), pos, DIFF_QK_DIM // rf)
    k_b = rope(k_b.reshape(b, s, DIFF_HEADS, 2, DIFF_QK_DIM), pos, DIFF_QK_DIM // rf)
    v_b = v_b.reshape(b, s, DIFF_HEADS, DIFF_V_DIM)
    lam_init = 0.8 - 0.6 * math.exp(-0.3 * layer)
    lam = (jnp.exp(jnp.sum(lam_q1.astype(f32) * lam_k1.astype(f32)))
           - jnp.exp(jnp.sum(lam_q2.astype(f32) * lam_k2.astype(f32))) + lam_init)
    o_b = diff_attention(q_b, k_b, v_b, lam, lam_init, subln)
    q_c = (rms_norm(c_q, q_norm) @ w_uq).reshape(b, s, MLA_HEADS, MLA_NOPE_DIM + MLA_ROPE_DIM)
    q_nope = q_c[..., :MLA_NOPE_DIM]
    q_rope = rope(q_c[..., MLA_NOPE_DIM:], pos, MLA_ROPE_DIM)
    kv = (rms_norm(c_kv, kv_norm) @ w_ukv).reshape(b, s, MLA_HEADS, MLA_NOPE_DIM + MLA_V_DIM)
    k_nope = kv[..., :MLA_NOPE_DIM]
    v_c = kv[..., MLA_NOPE_DIM:]
    k_rope = rope(k_r, pos, MLA_ROPE_DIM)
    o_c = mla_attention(q_nope, q_rope, k_nope, k_rope, v_c)
    o = jnp.concatenate([o_a.reshape(b, s, -1), o_b.reshape(b, s, -1), o_c.reshape(b, s, -1)], axis=-1)
    return o @ w_out


def swiglu(h, wg, wu, wd):
    return (jax.nn.silu(h @ wg) * (h @ wu)) @ wd


def moe_ffn(h, router, wg, wu, wd):
    b, s, d = h.shape
    hf = h.reshape(-1, d)
    logits = (hf @ router).astype(jnp.float32)
    top_val, top_idx = lax.top_k(logits, MOE_TOP_K)
    gates = jax.nn.softmax(top_val, axis=-1)
    combine = jnp.sum(jax.nn.one_hot(top_idx, MOE_EXPERTS, dtype=jnp.float32) * gates[..., None], axis=1)
    out = jnp.zeros_like(hf)
    for e in range(MOE_EXPERTS):
        out = out + combine[:, e:e + 1].astype(h.dtype) * swiglu(hf, wg[e], wu[e], wd[e])
    return out.reshape(b, s, d)


def setup_inputs(seed: int = 0) -> dict:
    key = jax.random.key(seed)
    ks = jax.random.split(key, 24)
    f32 = jnp.float32

    def w(k, shape, fan_in):
        return jax.random.normal(k, shape, f32) * fan_in ** -0.5

    def gain(k, shape):
        return 1.0 + 0.02 * jax.random.normal(k, shape, f32)

    x = jax.random.normal(ks[0], (BATCH, SEQ, D_MODEL), f32)
    offsets = jax.random.randint(ks[1], (BATCH,), 0, 64) * CHUNK
    positions = (offsets[:, None] + jnp.arange(SEQ)[None, :]).astype(jnp.int32)
    return {
        'x': x,
        'positions': positions,
        'attn_norm': gain(ks[2], (DEPTH, D_MODEL)),
        'w_in': w(ks[3], (DEPTH, D_MODEL, N_IN), D_MODEL),
        'mla_q_norm': gain(ks[4], (DEPTH, MLA_Q_LORA)),
        'w_uq': w(ks[5], (DEPTH, MLA_Q_LORA, MLA_HEADS * (MLA_NOPE_DIM + MLA_ROPE_DIM)), MLA_Q_LORA),
        'mla_kv_norm': gain(ks[6], (DEPTH, MLA_KV_LORA)),
        'w_ukv': w(ks[7], (DEPTH, MLA_KV_LORA, MLA_HEADS * (MLA_NOPE_DIM + MLA_V_DIM)), MLA_KV_LORA),
        'diff_lambda_q1': 0.1 * jax.random.normal(ks[8], (DEPTH, DIFF_QK_DIM), f32),
        'diff_lambda_k1': 0.1 * jax.random.normal(ks[9], (DEPTH, DIFF_QK_DIM), f32),
        'diff_lambda_q2': 0.1 * jax.random.normal(ks[10], (DEPTH, DIFF_QK_DIM), f32),
        'diff_lambda_k2': 0.1 * jax.random.normal(ks[11], (DEPTH, DIFF_QK_DIM), f32),
        'diff_subln': gain(ks[12], (DEPTH, DIFF_V_DIM)),
        'w_out': w(ks[13], (DEPTH, D_MIX, D_MODEL), D_MIX),
        'ffn_norm': gain(ks[14], (DEPTH, D_MODEL)),
        'dense_w_gate': w(ks[15], (N_DENSE, D_MODEL, D_FF), D_MODEL),
        'dense_w_up': w(ks[16], (N_DENSE, D_MODEL, D_FF), D_MODEL),
        'dense_w_down': w(ks[17], (N_DENSE, D_FF, D_MODEL), D_FF),
        'moe_router': w(ks[18], (N_MOE, D_MODEL, MOE_EXPERTS), D_MODEL),
        'moe_w_gate': w(ks[19], (N_MOE, MOE_EXPERTS, D_MODEL, MOE_D_FF), D_MODEL),
        'moe_w_up': w(ks[20], (N_MOE, MOE_EXPERTS, D_MODEL, MOE_D_FF), D_MODEL),
        'moe_w_down': w(ks[21], (N_MOE, MOE_EXPERTS, MOE_D_FF, D_MODEL), MOE_D_FF),
        'final_norm': gain(ks[22], (D_MODEL,)),
    }


def reference(x, positions, attn_norm, w_in, mla_q_norm, w_uq, mla_kv_norm, w_ukv,
              diff_lambda_q1, diff_lambda_k1, diff_lambda_q2, diff_lambda_k2, diff_subln,
              w_out, ffn_norm, dense_w_gate, dense_w_up, dense_w_down,
              moe_router, moe_w_gate, moe_w_up, moe_w_down, final_norm):
    h = x
    for layer in range(DEPTH):
        h = h + hybrid_mixer(rms_norm(h, attn_norm[layer]), positions, layer, w_in[layer],
                             mla_q_norm[layer], w_uq[layer], mla_kv_norm[layer], w_ukv[layer],
                             diff_lambda_q1[layer], diff_lambda_k1[layer],
                             diff_lambda_q2[layer], diff_lambda_k2[layer],
                             diff_subln[layer], w_out[layer])
        hn = rms_norm(h, ffn_norm[layer])
        j = layer // 2
        if layer % 2 == 0:
            h = h + swiglu(hn, dense_w_gate[j], dense_w_up[j], dense_w_down[j])
        else:
            h = h + moe_ffn(hn, moe_router[j], moe_w_gate[j], moe_w_up[j], moe_w_down[j])
    return rms_norm(h, final_norm)
```

```python
import functools
import math

import jax
import jax.numpy as jnp
import numpy as np
from jax import lax
from jax.experimental import pallas as pl
from jax.experimental.pallas import tpu as pltpu

F32 = jnp.float32
BF16 = jnp.bfloat16

LANES = 128
VMEM_LIMIT_BYTES = 56 * 1024 * 1024

D_MODEL = 1024
CHUNK = 64
ROPE_THETA = 500000.0
NORM_EPS = 1e-6
ROPE_FRACTION_DEN = 4

DSA_HEADS = 4
DSA_HEAD_DIM = 64
DSA_IDX_HEADS = 8
DSA_IDX_DIM = 64
DSA_TOPK_MAX = 256
DSA_IDX_SCALE = (DSA_IDX_HEADS * DSA_IDX_DIM) ** -0.5

DIFF_HEADS = 4
DIFF_QK_DIM = 32
DIFF_V_DIM = 2 * DIFF_QK_DIM

MLA_HEADS = 8
MLA_Q_LORA = 256
MLA_KV_LORA = 128
MLA_NOPE_DIM = 64
MLA_ROPE_DIM = 32
MLA_V_DIM = 64

MOE_EXPERTS = 8
MOE_TOP_K = 2

IN_SPLITS = (
    DSA_HEADS * DSA_HEAD_DIM, DSA_HEAD_DIM, DSA_HEAD_DIM, DSA_IDX_HEADS * DSA_IDX_DIM, DSA_IDX_DIM,
    DSA_IDX_HEADS, DIFF_HEADS * 2 * DIFF_QK_DIM, DIFF_HEADS * 2 * DIFF_QK_DIM, DIFF_HEADS * DIFF_V_DIM,
    MLA_Q_LORA, MLA_KV_LORA, MLA_ROPE_DIM,
)

INT_MIN = -(2 ** 31)
IDX_BIG = 2 ** 30


def _cparams(sem):
    return pltpu.CompilerParams(dimension_semantics=sem, vmem_limit_bytes=VMEM_LIMIT_BYTES)


def _rms(x, g):
    return x * lax.rsqrt(jnp.mean(x * x, axis=-1, keepdims=True) + NORM_EPS) * g


def _dot(a, b):
    return jnp.dot(a, b, preferred_element_type=F32)


def _dot_nt(a, b):
    return lax.dot_general(a, b, (((1,), (1,)), ((), ())), preferred_element_type=F32)


def _split3(x):
    hi = x.astype(BF16)
    r1 = x - hi.astype(F32)
    mid = r1.astype(BF16)
    lo = (r1 - mid.astype(F32)).astype(BF16)
    return hi, mid, lo


def _dot_f32_by_exact(x, m_bf16):
    hi, mid, lo = _split3(x)
    return _dot(hi, m_bf16) + _dot(mid, m_bf16) + _dot(lo, m_bf16)


def _inv_freq(rot_dim):
    half = rot_dim // 2
    return ROPE_THETA ** (-(jnp.arange(half, dtype=F32) * 2.0 / rot_dim))


def _rope_patterns():
    lane = np.arange(LANES)
    freqs, signs = [], []
    rot = DSA_HEAD_DIM // ROPE_FRACTION_DEN
    f = _inv_freq(rot)
    pos = lane % DSA_HEAD_DIM
    active = pos < rot
    freqs.append(jnp.where(active, f[(pos % (rot // 2))], 0.0))
    signs.append(np.where(active, np.where(pos < rot // 2, -1.0, 1.0), 0.0))
    rot = DIFF_QK_DIM // ROPE_FRACTION_DEN
    f = _inv_freq(rot)
    pos = lane % DIFF_QK_DIM
    active = pos < rot
    freqs.append(jnp.where(active, f[(pos % (rot // 2))], 0.0))
    signs.append(np.where(active, np.where(pos < rot // 2, -1.0, 1.0), 0.0))
    rot = MLA_ROPE_DIM
    f = _inv_freq(rot)
    pos = lane - MLA_NOPE_DIM
    active = (pos >= 0) & (pos < rot)
    freqs.append(jnp.where(active, f[(np.clip(pos, 0, rot - 1) % (rot // 2))], 0.0))
    signs.append(np.where(active, np.where(pos < rot // 2, -1.0, 1.0), 0.0))
    freq = jnp.stack(freqs).astype(F32)
    sign = jnp.asarray(np.stack(signs), dtype=F32)
    pad = jnp.zeros((2, LANES), F32)
    return jnp.concatenate([freq, sign, pad], axis=0)


ROPE_HALF = (DSA_HEAD_DIM // ROPE_FRACTION_DEN // 2, DIFF_QK_DIM // ROPE_FRACTION_DEN // 2, MLA_ROPE_DIM // 2)


def _x1_mask(pattern):
    lane = lax.broadcasted_iota(jnp.int32, (1, LANES), 1)
    if pattern == 0:
        return (lane % DSA_HEAD_DIM) < ROPE_HALF[0]
    if pattern == 1:
        return (lane % DIFF_QK_DIM) < ROPE_HALF[1]
    return (lane >= MLA_NOPE_DIM) & (lane < MLA_NOPE_DIM + ROPE_HALF[2])


def _rope_tables_kernel(pos_ref, pat_ref, out_ref):
    pos = pos_ref[...]
    for p in range(3):
        ang = pos * pat_ref[p:p + 1, :]
        out_ref[2 * p] = jnp.cos(ang)
        out_ref[2 * p + 1] = jnp.sin(ang) * pat_ref[3 + p:4 + p, :]


def _rope_tables(pos_f, tm):
    n = pos_f.shape[0]
    return pl.pallas_call(
        _rope_tables_kernel,
        grid=(n // tm,),
        in_specs=[pl.BlockSpec((tm, 1), lambda i: (i, 0)),
                  pl.BlockSpec((8, LANES), lambda i: (0, 0))],
        out_specs=pl.BlockSpec((6, tm, LANES), lambda i: (0, i, 0)),
        out_shape=jax.ShapeDtypeStruct((6, n, LANES), F32),
        compiler_params=_cparams(("parallel",)),
        name="rope_tables",
    )(pos_f, _rope_patterns())


def _rope128(y, cos, sin, pattern):
    half = ROPE_HALF[pattern]
    up = pltpu.roll(y, LANES - half, 1)
    dn = pltpu.roll(y, half, 1)
    return y * cos + jnp.where(_x1_mask(pattern), up, dn) * sin


PROJ_COLS = (
    ("qa", 256, 0), ("ka2", 128, 0), ("qi", 512, 0), ("ki2", 128, 0),
    ("qb", 256, 1), ("kb", 256, 1),
    ("va2", 128, None), ("wi", 128, None), ("vb", 256, None),
    ("cq", 256, None), ("ckv", 128, None), ("kr", 128, 2),
)
PROJ_WIDTH = sum(c[1] for c in PROJ_COLS)
PROJ_OUTS = (("qa", 256, BF16), ("ka2", 128, BF16), ("qi", 512, BF16), ("ki2", 128, BF16),
             ("qb", 256, BF16), ("kb", 256, BF16), ("va2", 128, BF16), ("wi", 128, F32),
             ("vb", 256, BF16), ("qc", 1024, BF16), ("kc", 1024, BF16), ("vc", 512, BF16))


def _prep_proj_weights(w_in, w_uq, w_ukv):
    offs = np.cumsum((0,) + IN_SPLITS)
    (q_a, k_a, v_a, q_i, k_i, w_i, q_b, k_b, v_b, c_q, c_kv, k_r) = [
        w_in[:, offs[j]:offs[j + 1]] for j in range(len(IN_SPLITS))]
    d = w_in.shape[0]
    z = lambda n: jnp.zeros((d, n), w_in.dtype)
    cols = {
        "qa": q_a, "ka2": jnp.concatenate([k_a, k_a], 1), "qi": q_i, "ki2": jnp.concatenate([k_i, k_i], 1),
        "qb": q_b, "kb": k_b, "va2": jnp.concatenate([v_a, v_a], 1),
        "wi": jnp.concatenate([w_i, z(LANES - DSA_IDX_HEADS)], 1), "vb": v_b, "cq": c_q, "ckv": c_kv,
        "kr": jnp.concatenate([z(MLA_NOPE_DIM), k_r, z(LANES - MLA_NOPE_DIM - MLA_ROPE_DIM)], 1),
    }
    w_p = jnp.concatenate([cols[name] for name, _, _ in PROJ_COLS], axis=1).astype(BF16)
    qd = MLA_NOPE_DIM + MLA_ROPE_DIM
    uq = w_uq.reshape(MLA_Q_LORA, MLA_HEADS, qd)
    uq = jnp.pad(uq, ((0, 0), (0, 0), (0, LANES - qd))).reshape(MLA_Q_LORA, MLA_HEADS * LANES).astype(BF16)
    ukv = w_ukv.reshape(MLA_KV_LORA, MLA_HEADS, MLA_NOPE_DIM + MLA_V_DIM)
    ukn = jnp.pad(ukv[:, :, :MLA_NOPE_DIM], ((0, 0), (0, 0), (0, LANES - MLA_NOPE_DIM)))
    ukn = ukn.reshape(MLA_KV_LORA, MLA_HEADS * LANES).astype(BF16)
    uv = ukv[:, :, MLA_NOPE_DIM:].reshape(MLA_KV_LORA, MLA_HEADS * MLA_V_DIM).astype(BF16)
    return w_p, uq, ukn, uv


def _proj_kernel(h_ref, g_ref, tab_ref, w_ref, qn_ref, uq_ref, kvn_ref, ukn_ref, uv_ref, *out_refs):
    outs = {name: ref for (name, _, _), ref in zip(PROJ_OUTS, out_refs)}
    xn = _rms(h_ref[...], g_ref[...]).astype(BF16)

    def roped(y, pattern):
        return _rope128(y, tab_ref[2 * pattern], tab_ref[2 * pattern + 1], pattern)

    c0 = 0
    vals = {}
    for name, width, pattern in PROJ_COLS:
        for s in range(width // LANES):
            y = _dot(xn, w_ref[:, c0 + s * LANES:c0 + (s + 1) * LANES])
            if pattern is not None:
                y = roped(y, pattern)
            if name in outs:
                outs[name][:, s * LANES:(s + 1) * LANES] = y.astype(outs[name].dtype)
            else:
                vals.setdefault(name, []).append(y)
        c0 += width

    cq = jnp.concatenate(vals["cq"], axis=1)
    cqn = _rms(cq, qn_ref[...]).astype(BF16)
    for hd in range(MLA_HEADS):
        y = _dot(cqn, uq_ref[:, hd * LANES:(hd + 1) * LANES])
        outs["qc"][:, hd * LANES:(hd + 1) * LANES] = roped(y, 2).astype(BF16)

    ckvn = _rms(vals["ckv"][0], kvn_ref[...]).astype(BF16)
    kr = vals["kr"][0]
    for hd in range(MLA_HEADS):
        y = _dot(ckvn, ukn_ref[:, hd * LANES:(hd + 1) * LANES])
        outs["kc"][:, hd * LANES:(hd + 1) * LANES] = (y + kr).astype(BF16)
    for s in range(MLA_HEADS * MLA_V_DIM // LANES):
        y = _dot(ckvn, uv_ref[:, s * LANES:(s + 1) * LANES])
        outs["vc"][:, s * LANES:(s + 1) * LANES] = y.astype(BF16)


def _project(h, gain, tables, w_p, q_norm, uq, kv_norm, ukn, uv, tm):
    n, d = h.shape
    full = lambda a: pl.BlockSpec(a.shape, lambda i: (0,) * a.ndim)
    gain = gain.reshape(1, d)
    q_norm = q_norm.reshape(1, -1)
    kv_norm = kv_norm.reshape(1, -1)
    return pl.pallas_call(
        _proj_kernel,
        grid=(n // tm,),
        in_specs=[pl.BlockSpec((tm, d), lambda i: (i, 0)), full(gain),
                  pl.BlockSpec((6, tm, LANES), lambda i: (0, i, 0)),
                  full(w_p), full(q_norm), full(uq), full(kv_norm), full(ukn), full(uv)],
        out_specs=[pl.BlockSpec((tm, w), lambda i: (i, 0)) for _, w, _ in PROJ_OUTS],
        out_shape=[jax.ShapeDtypeStruct((n, w), dt) for _, w, dt in PROJ_OUTS],
        compiler_params=_cparams(("parallel",)),
        name="projection",
    )(h, gain, tables, w_p, q_norm, uq, kv_norm, ukn, uv)


def _half_masks(dtype, group):
    lane = lax.broadcasted_iota(jnp.int32, (1, LANES), 1)
    return [jnp.where((lane // group) == u, 1.0, 0.0).astype(dtype) for u in range(LANES // group)]


def _flash_rows(q_st, k_ref, k_lane0, v_ref, v_lane0, qi, tq, scale, m_ref, l_ref, acc_ref):
    m_rows = q_st.shape[0]
    m_ref[0:m_rows, :] = jnp.full((m_rows, 1), -jnp.inf, F32)
    l_ref[0:m_rows, :] = jnp.zeros((m_rows, 1), F32)
    acc_ref[0:m_rows, :] = jnp.zeros((m_rows, LANES), F32)

    def step(j, masked):
        k0 = pl.multiple_of(j * tq, tq)
        k = k_ref[pl.ds(k0, tq), k_lane0:k_lane0 + LANES]
        v = v_ref[pl.ds(k0, tq), v_lane0:v_lane0 + LANES]
        s = _dot_nt(q_st, k) * scale
        if masked:
            row = lax.broadcasted_iota(jnp.int32, (m_rows, tq), 0) % tq
            col = lax.broadcasted_iota(jnp.int32, (m_rows, tq), 1)
            s = jnp.where((col // CHUNK) <= (row // CHUNK), s, -jnp.inf)
        m_old = m_ref[0:m_rows, :]
        m_new = jnp.maximum(m_old, jnp.max(s, axis=1, keepdims=True))
        alpha = jnp.exp(m_old - m_new)
        p = jnp.exp(s - m_new)
        l_ref[0:m_rows, :] = alpha * l_ref[0:m_rows, :] + jnp.sum(p, axis=1, keepdims=True)
        acc_ref[0:m_rows, :] = alpha * acc_ref[0:m_rows, :] + _dot(p.astype(BF16), v)
        m_ref[0:m_rows, :] = m_new

    def body(j, carry):
        step(j, False)
        return carry

    lax.fori_loop(0, qi, body, 0)
    step(qi, True)
    return acc_ref[0:m_rows, :] / l_ref[0:m_rows, :]


def _diff_kernel(lam_ref, subln_ref, gsum_ref, q_ref, k_ref, v_ref, o_ref, m_ref, l_ref, acc_ref, *, tq, lam_init):
    qi = pl.program_id(1)
    lam_rows = lam_ref[...]
    s1 = jnp.sum(lam_rows[0:1] * lam_rows[1:2], axis=1, keepdims=True)
    s2 = jnp.sum(lam_rows[2:3] * lam_rows[3:4], axis=1, keepdims=True)
    lam = jnp.exp(s1) - jnp.exp(s2) + lam_init
    masks = _half_masks(BF16, DIFF_QK_DIM)
    lane = lax.broadcasted_iota(jnp.int32, (1, LANES), 1)
    scale = DIFF_QK_DIM ** -0.5
    for s in range(DIFF_HEADS // 2):
        q = q_ref[0, :, s * LANES:(s + 1) * LANES]
        q_st = jnp.concatenate([q * mk for mk in masks], axis=0)
        o = _flash_rows(q_st, k_ref.at[0], s * LANES, v_ref.at[0], s * LANES, qi, tq, scale, m_ref, l_ref, acc_ref)
        a0 = o[0 * tq:1 * tq] - lam * o[1 * tq:2 * tq]
        a1 = o[2 * tq:3 * tq] - lam * o[3 * tq:4 * tq]
        a = jnp.where(lane < DIFF_V_DIM, a0, a1)
        ss = _dot_f32_by_exact(a * a, gsum_ref[...])
        y = a * lax.rsqrt(ss * (1.0 / DIFF_V_DIM) + NORM_EPS) * subln_ref[...]
        o_ref[0, :, s * LANES:(s + 1) * LANES] = (y * (1.0 - lam_init)).astype(o_ref.dtype)


def _diff_attention(qb, kb, vb, lam_rows, subln, layer, tq):
    b, s, w = qb.shape
    lam_init = 0.8 - 0.6 * math.exp(-0.3 * layer)
    subln2 = jnp.concatenate([subln, subln]).reshape(1, LANES).astype(F32)
    lane = np.arange(LANES)
    gsum = jnp.asarray((lane[:, None] // DIFF_V_DIM) == (lane[None, :] // DIFF_V_DIM), dtype=BF16)
    return pl.pallas_call(
        functools.partial(_diff_kernel, tq=tq, lam_init=lam_init),
        grid=(b, s // tq),
        in_specs=[pl.BlockSpec((8, LANES), lambda i, j: (0, 0)),
                  pl.BlockSpec((1, LANES), lambda i, j: (0, 0)),
                  pl.BlockSpec((LANES, LANES), lambda i, j: (0, 0)),
                  pl.BlockSpec((1, tq, w), lambda i, j: (i, j, 0)),
                  pl.BlockSpec((1, s, w), lambda i, j: (i, 0, 0)),
                  pl.BlockSpec((1, s, w), lambda i, j: (i, 0, 0))],
        out_specs=pl.BlockSpec((1, tq, w), lambda i, j: (i, j, 0)),
        out_shape=jax.ShapeDtypeStruct((b, s, w), BF16),
        scratch_shapes=[pltpu.VMEM((4 * tq, 1), F32), pltpu.VMEM((4 * tq, 1), F32),
                        pltpu.VMEM((4 * tq, LANES), F32)],
        compiler_params=_cparams(("parallel", "arbitrary")),
        name="diff_attention",
    )(lam_rows, subln2, gsum, qb, kb, vb)


def _mla_kernel(q_ref, k_ref, v_ref, o_ref, m_ref, l_ref, acc_ref, *, tq):
    qi = pl.program_id(1)
    lane = lax.broadcasted_iota(jnp.int32, (1, LANES), 1)
    scale = (MLA_NOPE_DIM + MLA_ROPE_DIM) ** -0.5
    for pair in range(MLA_HEADS // 2):
        halves = []
        for hh in range(2):
            hd = 2 * pair + hh
            q = q_ref[0, :, hd * LANES:(hd + 1) * LANES]
            halves.append(_flash_rows(q, k_ref.at[0], hd * LANES, v_ref.at[0], pair * LANES,
                                      qi, tq, scale, m_ref, l_ref, acc_ref))
        o_ref[0, :, pair * LANES:(pair + 1) * LANES] = jnp.where(lane < MLA_V_DIM, halves[0], halves[1]).astype(o_ref.dtype)


def _mla_attention(qc, kc, vc, tq):
    b, s, wq = qc.shape
    wv = vc.shape[2]
    return pl.pallas_call(
        functools.partial(_mla_kernel, tq=tq),
        grid=(b, s // tq),
        in_specs=[pl.BlockSpec((1, tq, wq), lambda i, j: (i, j, 0)),
                  pl.BlockSpec((1, s, wq), lambda i, j: (i, 0, 0)),
                  pl.BlockSpec((1, s, wv), lambda i, j: (i, 0, 0))],
        out_specs=pl.BlockSpec((1, tq, wv), lambda i, j: (i, j, 0)),
        out_shape=jax.ShapeDtypeStruct((b, s, wv), BF16),
        scratch_shapes=[pltpu.VMEM((tq, 1), F32), pltpu.VMEM((tq, 1), F32), pltpu.VMEM((tq, LANES), F32)],
        compiler_params=_cparams(("parallel", "arbitrary")),
        name="mla_attention",
    )(qc, kc, vc)


def _dsa_kernel(qa_ref, qi_ref, wi_ref, ka_ref, va_ref, ki_ref, o_ref, key_ref, aux_ref, *, tq, top_k, kc):
    blk = pl.program_id(1)
    s_len = ka_ref.shape[1]
    masks = _half_masks(BF16, DSA_IDX_DIM)
    lane = lax.broadcasted_iota(jnp.int32, (1, LANES), 1)

    q_st = jnp.concatenate([qi_ref[0, :, (hd // 2) * LANES:(hd // 2 + 1) * LANES] * masks[hd % 2]
                            for hd in range(DSA_IDX_HEADS)], axis=0)
    w = wi_ref[0]
    t_chunk = (blk * tq + lax.broadcasted_iota(jnp.int32, (tq, 1), 0)) // CHUNK
    for c in range(s_len // kc):
        rel = jnp.maximum(_dot_nt(q_st, ki_ref[0, c * kc:(c + 1) * kc, :]), 0.0)
        score = jnp.zeros((tq, kc), F32)
        for hd in range(DSA_IDX_HEADS):
            score = score + w[:, hd:hd + 1] * rel[hd * tq:(hd + 1) * tq]
        score = score * DSA_IDX_SCALE
        s_chunk = (c * kc + lax.broadcasted_iota(jnp.int32, (1, kc), 1)) // CHUNK
        score = jnp.where(s_chunk <= t_chunk, score, -jnp.inf)
        bits = lax.bitcast_convert_type(score, jnp.int32)
        key_ref[:, c * kc:(c + 1) * kc] = bits ^ ((bits >> 31) & 0x7FFFFFFF)

    def count(pred_fn):
        tot = jnp.zeros((tq, 1), F32)
        for c in range(s_len // kc):
            tot = tot + jnp.sum(jnp.where(pred_fn(c), 1.0, 0.0), axis=1, keepdims=True)
        return tot

    cnt = count(lambda c: key_ref[:, c * kc:(c + 1) * kc] >= 0)
    thr = jnp.where(cnt >= top_k, 0, INT_MIN).astype(jnp.int32)

    def thr_body(i, thr):
        cand = thr | (jnp.int32(1) << (30 - i))
        cnt = count(lambda c: key_ref[:, c * kc:(c + 1) * kc] >= cand)
        return jnp.where(cnt >= top_k, cand, thr)

    thr = lax.fori_loop(0, 31, thr_body, thr)

    n_gt = count(lambda c: key_ref[:, c * kc:(c + 1) * kc] > thr)
    need = top_k - n_gt
    for c in range(s_len // kc):
        idx = c * kc + lax.broadcasted_iota(jnp.int32, (1, kc), 1)
        s_chunk = idx // CHUNK
        tie = jnp.where(key_ref[:, c * kc:(c + 1) * kc] == thr, jnp.where(s_chunk <= t_chunk, idx, IDX_BIG), IDX_BIG)
        aux_ref[:, c * kc:(c + 1) * kc] = tie

    n_bits = max(1, (s_len - 1).bit_length())

    def tie_body(i, last):
        cand = last | (jnp.int32(1) << (n_bits - 1 - i))
        cnt = count(lambda c: aux_ref[:, c * kc:(c + 1) * kc] < cand)
        return jnp.where(cnt < need, cand, last)

    last = lax.fori_loop(0, n_bits, tie_body, jnp.zeros((tq, 1), jnp.int32))

    hmask = _half_masks(BF16, DSA_HEAD_DIM)
    qa_st = jnp.concatenate([qa_ref[0, :, (hd // 2) * LANES:(hd // 2 + 1) * LANES] * hmask[hd % 2]
                             for hd in range(DSA_HEADS)], axis=0)
    scale = DSA_HEAD_DIM ** -0.5
    logits = []
    m = jnp.full((DSA_HEADS * tq, 1), -jnp.inf, F32)
    for c in range(s_len // kc):
        keys = key_ref[:, c * kc:(c + 1) * kc]
        bias = jnp.where(keys > thr, 0.0, jnp.where(aux_ref[:, c * kc:(c + 1) * kc] <= last, 0.0, -jnp.inf))
        lg = _dot_nt(qa_st, ka_ref[0, c * kc:(c + 1) * kc, :]) * scale + jnp.concatenate([bias] * DSA_HEADS, axis=0)
        logits.append(lg)
        m = jnp.maximum(m, jnp.max(lg, axis=1, keepdims=True))
    l = jnp.zeros((DSA_HEADS * tq, 1), F32)
    acc = jnp.zeros((DSA_HEADS * tq, LANES), F32)
    for c in range(s_len // kc):
        p = jnp.exp(logits[c] - m)
        l = l + jnp.sum(p, axis=1, keepdims=True)
        acc = acc + _dot(p.astype(BF16), va_ref[0, c * kc:(c + 1) * kc, :])
    o = acc / l
    for pair in range(DSA_HEADS // 2):
        o_ref[0, :, pair * LANES:(pair + 1) * LANES] = jnp.where(
            lane < DSA_HEAD_DIM, o[(2 * pair) * tq:(2 * pair + 1) * tq],
            o[(2 * pair + 1) * tq:(2 * pair + 2) * tq]).astype(o_ref.dtype)


def _dsa_attention(qa, qi, wi, ka2, va2, ki2, tq):
    b, s, _ = qa.shape
    top_k = min(DSA_TOPK_MAX, s // 4)
    kc = min(512, s)
    qspec = lambda w: pl.BlockSpec((1, tq, w), lambda i, j: (i, j, 0))
    kspec = lambda w: pl.BlockSpec((1, s, w), lambda i, j: (i, 0, 0))
    return pl.pallas_call(
        functools.partial(_dsa_kernel, tq=tq, top_k=top_k, kc=kc),
        grid=(b, s // tq),
        in_specs=[qspec(qa.shape[2]), qspec(qi.shape[2]), qspec(LANES), kspec(LANES), kspec(LANES), kspec(LANES)],
        out_specs=qspec(qa.shape[2]),
        out_shape=jax.ShapeDtypeStruct(qa.shape, BF16),
        scratch_shapes=[pltpu.VMEM((tq, s), jnp.int32), pltpu.VMEM((tq, s), jnp.int32)],
        compiler_params=_cparams(("parallel", "arbitrary")),
        name="dsa_attention",
    )(qa, qi, wi, ka2, va2, ki2)


def _block_kernel(h_ref, oa_ref, ob_ref, oc_ref, wo_ref, g_ref, r_ref, wg_ref, wu_ref, wd_ref, fg_ref,
                  out_ref, h1_ref, xn_ref, comb_ref, acc_ref, acce_ref, *, n_exp, routed, final_norm):
    e = pl.program_id(1)
    f = pl.program_id(2)
    n_f = pl.num_programs(2)

    @pl.when((e == 0) & (f == 0))
    def _():
        wa, wb = oa_ref.shape[1], ob_ref.shape[1]
        h1 = (h_ref[...] + _dot(oa_ref[...], wo_ref[0:wa, :]) + _dot(ob_ref[...], wo_ref[wa:wa + wb, :])
              + _dot(oc_ref[...], wo_ref[wa + wb:, :]))
        h1_ref[...] = h1
        hn = _rms(h1, g_ref[...])
        xn_ref[...] = hn.astype(BF16)
        acc_ref[...] = jnp.zeros_like(acc_ref)
        if routed:
            logits = jnp.dot(hn, r_ref[...], preferred_element_type=F32, precision=lax.Precision.HIGHEST)
            lane = lax.broadcasted_iota(jnp.int32, logits.shape, 1)
            logits = jnp.where(lane < n_exp, logits, -jnp.inf)
            m1 = jnp.max(logits, axis=1, keepdims=True)
            i1 = jnp.min(jnp.where(logits == m1, lane, IDX_BIG), axis=1, keepdims=True)
            rest = jnp.where(lane == i1, -jnp.inf, logits)
            m2 = jnp.max(rest, axis=1, keepdims=True)
            i2 = jnp.min(jnp.where(rest == m2, lane, IDX_BIG), axis=1, keepdims=True)
            e2 = jnp.exp(m2 - m1)
            den = 1.0 + e2
            comb_ref[...] = jnp.where(lane == i1, 1.0 / den, jnp.where(lane == i2, e2 / den, 0.0))

    @pl.when(f == 0)
    def _():
        acce_ref[...] = jnp.zeros_like(acce_ref)

    xn = xn_ref[...]
    g = _dot(xn, wg_ref[0])
    u = _dot(xn, wu_ref[0])
    a = (g * jax.nn.sigmoid(g)) * u
    acce_ref[...] += _dot(a.astype(BF16), wd_ref[0])

    @pl.when(f == n_f - 1)
    def _():
        if routed:
            lane = lax.broadcasted_iota(jnp.int32, comb_ref.shape, 1)
            ce = jnp.sum(jnp.where(lane == e, comb_ref[...], 0.0), axis=1, keepdims=True)
            acc_ref[...] += ce * acce_ref[...]
        else:
            acc_ref[...] += acce_ref[...]

    @pl.when((e == n_exp - 1) & (f == n_f - 1))
    def _():
        y = h1_ref[...] + acc_ref[...]
        if final_norm:
            y = _rms(y, fg_ref[...])
        out_ref[...] = y


def _block(h, oa, ob, oc, w_out, gain, router_p, wg, wu, wd, final_gain, *, routed, final_norm, tm, tf):
    n, d = h.shape
    n_exp, _, dff = wg.shape
    full = lambda a: pl.BlockSpec(a.shape, lambda i, e, f: (0,) * a.ndim)
    row = lambda w: pl.BlockSpec((tm, w), lambda i, e, f: (i, 0))
    gain = gain.reshape(1, d)
    final_gain = final_gain.reshape(1, d)
    return pl.pallas_call(
        functools.partial(_block_kernel, n_exp=n_exp, routed=routed, final_norm=final_norm),
        grid=(n // tm, n_exp, dff // tf),
        in_specs=[row(d), row(oa.shape[1]), row(ob.shape[1]), row(oc.shape[1]), full(w_out), full(gain),
                  full(router_p),
                  pl.BlockSpec((1, d, tf), lambda i, e, f: (e, 0, f)),
                  pl.BlockSpec((1, d, tf), lambda i, e, f: (e, 0, f)),
                  pl.BlockSpec((1, tf, d), lambda i, e, f: (e, f, 0)),
                  full(final_gain)],
        out_specs=row(d),
        out_shape=jax.ShapeDtypeStruct((n, d), F32),
        scratch_shapes=[pltpu.VMEM((tm, d), F32), pltpu.VMEM((tm, d), BF16), pltpu.VMEM((tm, LANES), F32),
                        pltpu.VMEM((tm, d), F32), pltpu.VMEM((tm, d), F32)],
        compiler_params=_cparams(("parallel", "arbitrary", "arbitrary")),
        name="block_routed" if routed else "block_dense",
    )(h, oa, ob, oc, w_out, gain, router_p, wg, wu, wd, final_gain)


def _pick(n, pref):
    t = min(pref, n)
    while n % t:
        t //= 2
    return t


def kernel(x, positions, attn_norm, w_in, mla_q_norm, w_uq, mla_kv_norm, w_ukv, diff_lambda_q1, diff_lambda_k1, diff_lambda_q2, diff_lambda_k2, diff_subln, w_out, ffn_norm, dense_w_gate, dense_w_up, dense_w_down, moe_router, moe_w_gate, moe_w_up, moe_w_down, final_norm):
    b, s, d = x.shape
    n = b * s
    depth = w_in.shape[0]
    tm_proj = _pick(n, 512)
    tm_blk = _pick(n, 1024)
    tq = _pick(s, 128)
    tq_mla = _pick(s, 256)

    tables = _rope_tables(positions.astype(F32).reshape(n, 1), _pick(n, 1024))
    h = x.reshape(n, d)
    r3 = lambda a: a.reshape(b, s, a.shape[-1])
    for layer in range(depth):
        w_p, uq, ukn, uv = _prep_proj_weights(w_in[layer], w_uq[layer], w_ukv[layer])
        (qa, ka2, qi, ki2, qb, kb, va2, wi, vb, qc, kc, vc) = _project(
            h, attn_norm[layer], tables, w_p, mla_q_norm[layer], uq, mla_kv_norm[layer], ukn, uv, tm_proj)
        oa = _dsa_attention(r3(qa), r3(qi), r3(wi), r3(ka2), r3(va2), r3(ki2), tq)
        lam_rows = jnp.zeros((8, LANES), F32).at[0:4, 0:DIFF_QK_DIM].set(jnp.stack(
            [diff_lambda_q1[layer], diff_lambda_k1[layer], diff_lambda_q2[layer], diff_lambda_k2[layer]]))
        ob = _diff_attention(r3(qb), r3(kb), r3(vb), lam_rows, diff_subln[layer], layer, tq)
        oc = _mla_attention(r3(qc), r3(kc), r3(vc), tq_mla)
        j = layer // 2
        last = layer == depth - 1
        wo = w_out[layer].astype(BF16)
        if layer % 2 == 0:
            router_p = jnp.zeros((d, LANES), F32)
            h = _block(h, oa.reshape(n, -1), ob.reshape(n, -1), oc.reshape(n, -1), wo, ffn_norm[layer], router_p,
                       dense_w_gate[j:j + 1].astype(BF16), dense_w_up[j:j + 1].astype(BF16),
                       dense_w_down[j:j + 1].astype(BF16), final_norm,
                       routed=False, final_norm=last, tm=tm_blk, tf=_pick(dense_w_gate.shape[2], 512))
        else:
            router_p = jnp.pad(moe_router[j], ((0, 0), (0, LANES - MOE_EXPERTS)))
            h = _block(h, oa.reshape(n, -1), ob.reshape(n, -1), oc.reshape(n, -1), wo, ffn_norm[layer], router_p,
                       moe_w_gate[j].astype(BF16), moe_w_up[j].astype(BF16), moe_w_down[j].astype(BF16), final_norm,
                       routed=True, final_norm=last, tm=tm_blk, tf=_pick(moe_w_gate.shape[3], 512))
    return h.reshape(b, s, d)
```

```python
import functools
import math

import jax
import jax.numpy as jnp
import numpy as np
from jax import lax
from jax.experimental import pallas as pl
from jax.experimental.pallas import tpu as pltpu

F32 = jnp.float32
BF16 = jnp.bfloat16

LANES = 128
VMEM_LIMIT_BYTES = 56 * 1024 * 1024

D_MODEL = 1024
CHUNK = 64
ROPE_THETA = 500000.0
NORM_EPS = 1e-6
ROPE_FRACTION_DEN = 4

DSA_HEADS = 4
DSA_HEAD_DIM = 64
DSA_IDX_HEADS = 8
DSA_IDX_DIM = 64
DSA_TOPK_MAX = 256
DSA_IDX_SCALE = (DSA_IDX_HEADS * DSA_IDX_DIM) ** -0.5

DIFF_HEADS = 4
DIFF_QK_DIM = 32
DIFF_V_DIM = 2 * DIFF_QK_DIM

MLA_HEADS = 8
MLA_Q_LORA = 256
MLA_KV_LORA = 128
MLA_NOPE_DIM = 64
MLA_ROPE_DIM = 32
MLA_V_DIM = 64

MOE_EXPERTS = 8
MOE_TOP_K = 2

IN_SPLITS = (
    DSA_HEADS * DSA_HEAD_DIM, DSA_HEAD_DIM, DSA_HEAD_DIM, DSA_IDX_HEADS * DSA_IDX_DIM, DSA_IDX_DIM,
    DSA_IDX_HEADS, DIFF_HEADS * 2 * DIFF_QK_DIM, DIFF_HEADS * 2 * DIFF_QK_DIM, DIFF_HEADS * DIFF_V_DIM,
    MLA_Q_LORA, MLA_KV_LORA, MLA_ROPE_DIM,
)

INT_MIN = -(2 ** 31)
IDX_BIG = 2 ** 30


def _cparams(sem):
    return pltpu.CompilerParams(dimension_semantics=sem, vmem_limit_bytes=VMEM_LIMIT_BYTES)


def _rms(x, g):
    return x * lax.rsqrt(jnp.mean(x * x, axis=-1, keepdims=True) + NORM_EPS) * g


def _dot(a, b):
    return jnp.dot(a, b, preferred_element_type=F32)


def _dot_nt(a, b):
    return lax.dot_general(a, b, (((1,), (1,)), ((), ())), preferred_element_type=F32)


def _split3(x):
    hi = x.astype(BF16)
    r1 = x - hi.astype(F32)
    mid = r1.astype(BF16)
    lo = (r1 - mid.astype(F32)).astype(BF16)
    return hi, mid, lo


def _dot_f32_by_exact(x, m_bf16):
    hi, mid, lo = _split3(x)
    return _dot(hi, m_bf16) + _dot(mid, m_bf16) + _dot(lo, m_bf16)


def _inv_freq(rot_dim):
    half = rot_dim // 2
    return ROPE_THETA ** (-(jnp.arange(half, dtype=F32) * 2.0 / rot_dim))


def _rope_patterns():
    lane = np.arange(LANES)
    freqs, signs = [], []
    rot = DSA_HEAD_DIM // ROPE_FRACTION_DEN
    f = _inv_freq(rot)
    pos = lane % DSA_HEAD_DIM
    active = pos < rot
    freqs.append(jnp.where(active, f[(pos % (rot // 2))], 0.0))
    signs.append(np.where(active, np.where(pos < rot // 2, -1.0, 1.0), 0.0))
    rot = DIFF_QK_DIM // ROPE_FRACTION_DEN
    f = _inv_freq(rot)
    pos = lane % DIFF_QK_DIM
    active = pos < rot
    freqs.append(jnp.where(active, f[(pos % (rot // 2))], 0.0))
    signs.append(np.where(active, np.where(pos < rot // 2, -1.0, 1.0), 0.0))
    rot = MLA_ROPE_DIM
    f = _inv_freq(rot)
    pos = lane - MLA_NOPE_DIM
    active = (pos >= 0) & (pos < rot)
    freqs.append(jnp.where(active, f[(np.clip(pos, 0, rot - 1) % (rot // 2))], 0.0))
    signs.append(np.where(active, np.where(pos < rot // 2, -1.0, 1.0), 0.0))
    freq = jnp.stack(freqs).astype(F32)
    sign = jnp.asarray(np.stack(signs), dtype=F32)
    pad = jnp.zeros((2, LANES), F32)
    return jnp.concatenate([freq, sign, pad], axis=0)


ROPE_HALF = (DSA_HEAD_DIM // ROPE_FRACTION_DEN // 2, DIFF_QK_DIM // ROPE_FRACTION_DEN // 2, MLA_ROPE_DIM // 2)


def _x1_mask(pattern):
    lane = lax.broadcasted_iota(jnp.int32, (1, LANES), 1)
    if pattern == 0:
        return (lane % DSA_HEAD_DIM) < ROPE_HALF[0]
    if pattern == 1:
        return (lane % DIFF_QK_DIM) < ROPE_HALF[1]
    return (lane >= MLA_NOPE_DIM) & (lane < MLA_NOPE_DIM + ROPE_HALF[2])


def _rope_tables_kernel(pos_ref, pat_ref, out_ref):
    pos = pos_ref[...]
    for p in range(3):
        ang = pos * pat_ref[p:p + 1, :]
        out_ref[2 * p] = jnp.cos(ang)
        out_ref[2 * p + 1] = jnp.sin(ang) * pat_ref[3 + p:4 + p, :]


def _rope_tables(pos_f, tm):
    n = pos_f.shape[0]
    return pl.pallas_call(
        _rope_tables_kernel,
        grid=(n // tm,),
        in_specs=[pl.BlockSpec((tm, 1), lambda i: (i, 0)),
                  pl.BlockSpec((8, LANES), lambda i: (0, 0))],
        out_specs=pl.BlockSpec((6, tm, LANES), lambda i: (0, i, 0)),
        out_shape=jax.ShapeDtypeStruct((6, n, LANES), F32),
        compiler_params=_cparams(("parallel",)),
        name="rope_tables",
    )(pos_f, _rope_patterns())


def _rope128(y, cos, sin, pattern):
    half = ROPE_HALF[pattern]
    up = pltpu.roll(y, LANES - half, 1)
    dn = pltpu.roll(y, half, 1)
    return y * cos + jnp.where(_x1_mask(pattern), up, dn) * sin


PROJ_COLS = (
    ("qa", 256, 0), ("ka2", 128, 0), ("qi", 512, 0), ("ki2", 128, 0),
    ("qb", 256, 1), ("kb", 256, 1),
    ("va2", 128, None), ("wi", 128, None), ("vb", 256, None),
    ("cq", 256, None), ("ckv", 128, None), ("kr", 128, 2),
)
PROJ_WIDTH = sum(c[1] for c in PROJ_COLS)
PROJ_OUTS = (("qa", 256, BF16), ("ka2", 128, BF16), ("qi", 512, BF16), ("ki2", 128, BF16),
             ("qb", 256, BF16), ("kb", 256, BF16), ("va2", 128, BF16), ("wi", 128, F32),
             ("vb", 256, BF16), ("qc", 1024, BF16), ("kc", 1024, BF16), ("vc", 512, BF16))
PROJ_TRANSPOSED = ("va2", "vb", "vc")
KV_TILE = 256


def _prep_proj_weights(w_in, w_uq, w_ukv):
    offs = np.cumsum((0,) + IN_SPLITS)
    (q_a, k_a, v_a, q_i, k_i, w_i, q_b, k_b, v_b, c_q, c_kv, k_r) = [
        w_in[:, offs[j]:offs[j + 1]] for j in range(len(IN_SPLITS))]
    d = w_in.shape[0]
    z = lambda n: jnp.zeros((d, n), w_in.dtype)
    cols = {
        "qa": q_a, "ka2": jnp.concatenate([k_a, k_a], 1), "qi": q_i, "ki2": jnp.concatenate([k_i, k_i], 1),
        "qb": q_b, "kb": k_b, "va2": jnp.concatenate([v_a, v_a], 1),
        "wi": jnp.concatenate([w_i, z(LANES - DSA_IDX_HEADS)], 1), "vb": v_b, "cq": c_q, "ckv": c_kv,
        "kr": jnp.concatenate([z(MLA_NOPE_DIM), k_r, z(LANES - MLA_NOPE_DIM - MLA_ROPE_DIM)], 1),
    }
    w_p = jnp.concatenate([cols[name] for name, _, _ in PROJ_COLS], axis=1).astype(BF16)
    qd = MLA_NOPE_DIM + MLA_ROPE_DIM
    uq = w_uq.reshape(MLA_Q_LORA, MLA_HEADS, qd)
    uq = jnp.pad(uq, ((0, 0), (0, 0), (0, LANES - qd))).reshape(MLA_Q_LORA, MLA_HEADS * LANES).astype(BF16)
    ukv = w_ukv.reshape(MLA_KV_LORA, MLA_HEADS, MLA_NOPE_DIM + MLA_V_DIM)
    ukn = jnp.pad(ukv[:, :, :MLA_NOPE_DIM], ((0, 0), (0, 0), (0, LANES - MLA_NOPE_DIM)))
    ukn = ukn.reshape(MLA_KV_LORA, MLA_HEADS * LANES).astype(BF16)
    uv = ukv[:, :, MLA_NOPE_DIM:].reshape(MLA_KV_LORA, MLA_HEADS * MLA_V_DIM).astype(BF16)
    return w_p, uq, ukn, uv


def _proj_kernel(h_ref, g_ref, tab_ref, w_ref, qn_ref, uq_ref, kvn_ref, ukn_ref, uv_ref, *out_refs):
    outs = {name: ref for (name, _, _), ref in zip(PROJ_OUTS, out_refs)}
    xn = _rms(h_ref[...], g_ref[...]).astype(BF16)

    def roped(y, pattern):
        return _rope128(y, tab_ref[2 * pattern], tab_ref[2 * pattern + 1], pattern)

    def emit(name, s, y):
        ref = outs[name]
        if name in PROJ_TRANSPOSED:
            tk = ref.shape[2]
            for t in range(ref.shape[0]):
                ref[t, s * LANES:(s + 1) * LANES, :] = y[t * tk:(t + 1) * tk].T.astype(ref.dtype)
        else:
            ref[:, s * LANES:(s + 1) * LANES] = y.astype(ref.dtype)

    c0 = 0
    vals = {}
    for name, width, pattern in PROJ_COLS:
        for s in range(width // LANES):
            y = _dot(xn, w_ref[:, c0 + s * LANES:c0 + (s + 1) * LANES])
            if pattern is not None:
                y = roped(y, pattern)
            if name in outs:
                emit(name, s, y)
            else:
                vals.setdefault(name, []).append(y)
        c0 += width

    cq = jnp.concatenate(vals["cq"], axis=1)
    cqn = _rms(cq, qn_ref[...]).astype(BF16)
    for hd in range(MLA_HEADS):
        y = _dot(cqn, uq_ref[:, hd * LANES:(hd + 1) * LANES])
        outs["qc"][:, hd * LANES:(hd + 1) * LANES] = roped(y, 2).astype(BF16)

    ckvn = _rms(vals["ckv"][0], kvn_ref[...]).astype(BF16)
    kr = vals["kr"][0]
    for hd in range(MLA_HEADS):
        y = _dot(ckvn, ukn_ref[:, hd * LANES:(hd + 1) * LANES])
        outs["kc"][:, hd * LANES:(hd + 1) * LANES] = (y + kr).astype(BF16)
    for s in range(MLA_HEADS * MLA_V_DIM // LANES):
        emit("vc", s, _dot(ckvn, uv_ref[:, s * LANES:(s + 1) * LANES]))


def _project(h, gain, tables, w_p, q_norm, uq, kv_norm, ukn, uv, tm, tk):
    n, d = h.shape
    full = lambda a: pl.BlockSpec(a.shape, lambda i: (0,) * a.ndim)
    gain = gain.reshape(1, d)
    q_norm = q_norm.reshape(1, -1)
    kv_norm = kv_norm.reshape(1, -1)
    out_specs, out_shape = [], []
    for name, w, dt in PROJ_OUTS:
        if name in PROJ_TRANSPOSED:
            out_specs.append(pl.BlockSpec((tm // tk, w, tk), lambda i: (i, 0, 0)))
            out_shape.append(jax.ShapeDtypeStruct((n // tk, w, tk), dt))
        else:
            out_specs.append(pl.BlockSpec((tm, w), lambda i: (i, 0)))
            out_shape.append(jax.ShapeDtypeStruct((n, w), dt))
    return pl.pallas_call(
        _proj_kernel,
        grid=(n // tm,),
        in_specs=[pl.BlockSpec((tm, d), lambda i: (i, 0)), full(gain),
                  pl.BlockSpec((6, tm, LANES), lambda i: (0, i, 0)),
                  full(w_p), full(q_norm), full(uq), full(kv_norm), full(ukn), full(uv)],
        out_specs=out_specs,
        out_shape=out_shape,
        compiler_params=_cparams(("parallel",)),
        name="projection",
    )(h, gain, tables, w_p, q_norm, uq, kv_norm, ukn, uv)


def _half_masks(dtype, group):
    lane = lax.broadcasted_iota(jnp.int32, (1, LANES), 1)
    return [jnp.where((lane // group) == u, 1.0, 0.0).astype(dtype) for u in range(LANES // group)]


def _softmax_step(st, v_t, m, l, acc_ref, guard_empty=False):
    m_new = jnp.maximum(m, jnp.max(st, axis=0, keepdims=True))
    m_use = jnp.where(m_new == -jnp.inf, 0.0, m_new) if guard_empty else m_new
    alpha = jnp.exp(m - m_use)
    p = jnp.exp(st - m_use)
    l_new = alpha * l + jnp.sum(p, axis=0, keepdims=True)
    acc_ref[...] = alpha * acc_ref[...] + _dot(v_t, p.astype(BF16))
    return m_new, l_new


def _diag_mask(tk, m_cols, tq):
    key_chunk = lax.broadcasted_iota(jnp.int32, (tk, 1), 0) // CHUNK
    qry_chunk = (lax.broadcasted_iota(jnp.int32, (1, m_cols), 1) % tq) // CHUNK
    return key_chunk <= qry_chunk


def _causal_flash(problems, k_ref, v_ref, acc_ref, qi, tq, scale):
    m_cols = problems[0][0].shape[0]
    n_p = len(problems)
    acc_ref[...] = jnp.zeros_like(acc_ref)
    init = tuple(jnp.full((1, m_cols), -jnp.inf, F32) for _ in range(n_p)) + \
        tuple(jnp.zeros((1, m_cols), F32) for _ in range(n_p))

    def step(j, carry, mask):
        r0 = pl.multiple_of(j * tq, tq)
        ms, ls = list(carry[:n_p]), list(carry[n_p:])
        for p, (q_st, k_lane0, v_row0) in enumerate(problems):
            k = k_ref[pl.ds(r0, tq), k_lane0:k_lane0 + LANES]
            st = _dot_nt(k, q_st) * scale
            if mask is not None:
                st = jnp.where(mask, st, -jnp.inf)
            ms[p], ls[p] = _softmax_step(st, v_ref[j, v_row0:v_row0 + LANES, :], ms[p], ls[p], acc_ref.at[p])
        return tuple(ms) + tuple(ls)

    carry = lax.fori_loop(0, qi, lambda j, c: step(j, c, None), init)
    carry = step(qi, carry, _diag_mask(tq, m_cols, tq))
    return [acc_ref[p] / carry[n_p + p] for p in range(n_p)]


def _diff_kernel(lam_ref, subln_ref, gsum_ref, q_ref, k_ref, v_ref, o_ref, acc_ref, *, tq, lam_init):
    qi = pl.program_id(1)
    lam_rows = lam_ref[...]
    s1 = jnp.sum(lam_rows[0:1] * lam_rows[1:2], axis=1, keepdims=True)
    s2 = jnp.sum(lam_rows[2:3] * lam_rows[3:4], axis=1, keepdims=True)
    lam = jnp.exp(s1) - jnp.exp(s2) + lam_init
    masks = _half_masks(BF16, DIFF_QK_DIM)
    row = lax.broadcasted_iota(jnp.int32, (LANES, 1), 0)
    n_slices = DIFF_HEADS // 2
    problems = []
    for s in range(n_slices):
        q = q_ref[0, :, s * LANES:(s + 1) * LANES]
        problems.append((jnp.concatenate([q * mk for mk in masks], axis=0), s * LANES, s * LANES))
    outs = _causal_flash(problems, k_ref.at[0], v_ref, acc_ref, qi, tq, DIFF_QK_DIM ** -0.5)
    for s in range(n_slices):
        o_t = outs[s]
        a0 = o_t[:, 0 * tq:1 * tq] - lam * o_t[:, 1 * tq:2 * tq]
        a1 = o_t[:, 2 * tq:3 * tq] - lam * o_t[:, 3 * tq:4 * tq]
        a = jnp.where(row < DIFF_V_DIM, a0, a1).T
        ss = _dot_f32_by_exact(a * a, gsum_ref[...])
        y = a * lax.rsqrt(ss * (1.0 / DIFF_V_DIM) + NORM_EPS) * subln_ref[...]
        o_ref[0, :, s * LANES:(s + 1) * LANES] = (y * (1.0 - lam_init)).astype(o_ref.dtype)


def _diff_attention(qb, kb, vb_t, lam_rows, subln, layer, tq):
    b, s, w = qb.shape
    lam_init = 0.8 - 0.6 * math.exp(-0.3 * layer)
    subln2 = jnp.concatenate([subln, subln]).reshape(1, LANES).astype(F32)
    lane = np.arange(LANES)
    gsum = jnp.asarray((lane[:, None] // DIFF_V_DIM) == (lane[None, :] // DIFF_V_DIM), dtype=BF16)
    return pl.pallas_call(
        functools.partial(_diff_kernel, tq=tq, lam_init=lam_init),
        grid=(b, s // tq),
        in_specs=[pl.BlockSpec((8, LANES), lambda i, j: (0, 0)),
                  pl.BlockSpec((1, LANES), lambda i, j: (0, 0)),
                  pl.BlockSpec((LANES, LANES), lambda i, j: (0, 0)),
                  pl.BlockSpec((1, tq, w), lambda i, j: (i, j, 0)),
                  pl.BlockSpec((1, s, w), lambda i, j: (i, 0, 0)),
                  pl.BlockSpec((s // tq, w, tq), lambda i, j: (i, 0, 0))],
        out_specs=pl.BlockSpec((1, tq, w), lambda i, j: (i, j, 0)),
        out_shape=jax.ShapeDtypeStruct((b, s, w), BF16),
        scratch_shapes=[pltpu.VMEM((DIFF_HEADS // 2, LANES, 4 * tq), F32)],
        compiler_params=_cparams(("parallel", "arbitrary")),
        name="diff_attention",
    )(lam_rows, subln2, gsum, qb, kb, vb_t)


def _mla_kernel(q_ref, k_ref, v_ref, o_ref, acc_ref, *, tq):
    qi = pl.program_id(1)
    row = lax.broadcasted_iota(jnp.int32, (LANES, 1), 0)
    scale = (MLA_NOPE_DIM + MLA_ROPE_DIM) ** -0.5
    problems = [(q_ref[0, :, hd * LANES:(hd + 1) * LANES], hd * LANES, (hd // 2) * LANES)
                for hd in range(MLA_HEADS)]
    outs = _causal_flash(problems, k_ref.at[0], v_ref, acc_ref, qi, tq, scale)
    for pair in range(MLA_HEADS // 2):
        o_t = jnp.where(row < MLA_V_DIM, outs[2 * pair], outs[2 * pair + 1])
        o_ref[0, :, pair * LANES:(pair + 1) * LANES] = o_t.T.astype(o_ref.dtype)


def _mla_attention(qc, kc, vc_t, tq):
    b, s, wq = qc.shape
    wv = vc_t.shape[1]
    return pl.pallas_call(
        functools.partial(_mla_kernel, tq=tq),
        grid=(b, s // tq),
        in_specs=[pl.BlockSpec((1, tq, wq), lambda i, j: (i, j, 0)),
                  pl.BlockSpec((1, s, wq), lambda i, j: (i, 0, 0)),
                  pl.BlockSpec((s // tq, wv, tq), lambda i, j: (i, 0, 0))],
        out_specs=pl.BlockSpec((1, tq, wv), lambda i, j: (i, j, 0)),
        out_shape=jax.ShapeDtypeStruct((b, s, wv), BF16),
        scratch_shapes=[pltpu.VMEM((MLA_HEADS, LANES, tq), F32)],
        compiler_params=_cparams(("parallel", "arbitrary")),
        name="mla_attention",
    )(qc, kc, vc_t)


KEY_NEG_INF = -2139095041


def _score_keys(score):
    bits = lax.bitcast_convert_type(score, jnp.int32)
    return bits ^ ((bits >> 31) & 0x7FFFFFFF)


def _dsa_kernel(qa_ref, qi_ref, wi_ref, ka_ref, va_ref, ki_ref, o_ref, key_ref, aux_ref, bias_ref, acc_ref,
                *, tq, tk, top_k):
    blk = pl.program_id(1)
    s_len = ka_ref.shape[1]
    n_kv = ((blk + 1) * tq + tk - 1) // tk
    masks = _half_masks(BF16, DSA_IDX_DIM)
    row = lax.broadcasted_iota(jnp.int32, (LANES, 1), 0)
    t_chunk = (blk * tq + lax.broadcasted_iota(jnp.int32, (1, tq), 1)) // CHUNK
    sub_idx = lax.broadcasted_iota(jnp.int32, (tk, 1), 0)

    def tile_start(c):
        return pl.multiple_of(c * tk, tk)

    qi_st = jnp.concatenate([qi_ref[0, :, (hd // 2) * LANES:(hd // 2 + 1) * LANES] * masks[hd % 2]
                             for hd in range(DSA_IDX_HEADS)], axis=0)
    w_t = wi_ref[0].T

    def score_body(c, carry):
        r0 = tile_start(c)
        rel = jnp.maximum(_dot_nt(ki_ref[0, pl.ds(r0, tk), :], qi_st), 0.0)
        score = jnp.zeros((tk, tq), F32)
        for hd in range(DSA_IDX_HEADS):
            score = score + w_t[hd:hd + 1, :] * rel[:, hd * tq:(hd + 1) * tq]
        score = score * DSA_IDX_SCALE
        score = jnp.where((r0 + sub_idx) // CHUNK <= t_chunk, score, -jnp.inf)
        key_ref[pl.ds(r0, tk), :] = _score_keys(score)
        return carry

    lax.fori_loop(0, n_kv, score_body, 0)

    def count(pred_fn):
        def body(c, tot):
            ones = jnp.where(pred_fn(tile_start(c)), 1.0, 0.0)
            return tot + jnp.sum(ones.reshape(tk // 8, 8, tq), axis=0)
        tot = lax.fori_loop(0, n_kv, body, jnp.zeros((8, tq), F32))
        return jnp.sum(tot, axis=0, keepdims=True)

    cnt_all = count(lambda r0: key_ref[pl.ds(r0, tk), :] >= INT_MIN)
    cnt_pos = count(lambda r0: key_ref[pl.ds(r0, tk), :] >= 0)
    thr = jnp.where(cnt_pos >= top_k, 0, INT_MIN).astype(jnp.int32)
    cnt_thr = jnp.where(cnt_pos >= top_k, cnt_pos, cnt_all)

    def thr_body(i, carry):
        thr, cnt_thr = carry
        cand = thr | (jnp.int32(1) << (30 - i))
        cnt = count(lambda r0: key_ref[pl.ds(r0, tk), :] >= cand)
        ok = cnt >= top_k
        return jnp.where(ok, cand, thr), jnp.where(ok, cnt, cnt_thr)

    thr, cnt_thr = lax.fori_loop(0, 31, thr_body, (thr, cnt_thr))
    thr = jnp.maximum(thr, KEY_NEG_INF)
    tied = jnp.where(cnt_thr > top_k, jnp.where(thr > KEY_NEG_INF, 1.0, 0.0), 0.0)

    def select_with_ties():
        n_gt = count(lambda r0: key_ref[pl.ds(r0, tk), :] > thr)
        need = top_k - n_gt

        def aux_body(c, carry):
            r0 = tile_start(c)
            idx = r0 + sub_idx
            tie = jnp.where(key_ref[pl.ds(r0, tk), :] == thr, jnp.where(idx // CHUNK <= t_chunk, idx, IDX_BIG), IDX_BIG)
            aux_ref[pl.ds(r0, tk), :] = tie
            return carry

        lax.fori_loop(0, n_kv, aux_body, 0)
        n_bits = max(1, (s_len - 1).bit_length())

        def tie_body(i, last):
            cand = last | (jnp.int32(1) << (n_bits - 1 - i))
            cnt = count(lambda r0: aux_ref[pl.ds(r0, tk), :] < cand)
            return jnp.where(cnt < need, cand, last)

        last = lax.fori_loop(0, n_bits, tie_body, jnp.zeros((1, tq), jnp.int32))

        def bias_body(c, carry):
            r0 = tile_start(c)
            bias_ref[pl.ds(r0, tk), :] = jnp.where(
                key_ref[pl.ds(r0, tk), :] > thr, 0.0, jnp.where(aux_ref[pl.ds(r0, tk), :] <= last, 0.0, -jnp.inf))
            return carry

        lax.fori_loop(0, n_kv, bias_body, 0)

    def select_no_ties():
        low = jnp.where(thr > KEY_NEG_INF, thr - 1, thr)

        def bias_body(c, carry):
            r0 = tile_start(c)
            bias_ref[pl.ds(r0, tk), :] = jnp.where(key_ref[pl.ds(r0, tk), :] > low, 0.0, -jnp.inf)
            return carry

        lax.fori_loop(0, n_kv, bias_body, 0)

    lax.cond(jnp.max(tied) > 0.0, select_with_ties, select_no_ties)

    hmask = _half_masks(BF16, DSA_HEAD_DIM)
    qa_st = jnp.concatenate([qa_ref[0, :, (hd // 2) * LANES:(hd // 2 + 1) * LANES] * hmask[hd % 2]
                             for hd in range(DSA_HEADS)], axis=0)
    scale = DSA_HEAD_DIM ** -0.5
    m_cols = DSA_HEADS * tq
    acc_ref[...] = jnp.zeros_like(acc_ref)

    def att_body(c, carry):
        r0 = tile_start(c)
        bias = bias_ref[pl.ds(r0, tk), :]
        st = _dot_nt(ka_ref[0, pl.ds(r0, tk), :], qa_st) * scale + jnp.concatenate([bias] * DSA_HEADS, axis=1)
        return _softmax_step(st, va_ref[c], carry[0], carry[1], acc_ref, guard_empty=True)

    m, l = lax.fori_loop(0, n_kv, att_body, (jnp.full((1, m_cols), -jnp.inf, F32), jnp.zeros((1, m_cols), F32)))
    o_t = acc_ref[...] / l
    for pair in range(DSA_HEADS // 2):
        pair_t = jnp.where(row < DSA_HEAD_DIM, o_t[:, (2 * pair) * tq:(2 * pair + 1) * tq],
                           o_t[:, (2 * pair + 1) * tq:(2 * pair + 2) * tq])
        o_ref[0, :, pair * LANES:(pair + 1) * LANES] = pair_t.T.astype(o_ref.dtype)


def _dsa_attention(qa, qi, wi, ka2, va2_t, ki2, tq, tk):
    b, s, _ = qa.shape
    top_k = min(DSA_TOPK_MAX, s // 4)
    qspec = lambda w: pl.BlockSpec((1, tq, w), lambda i, j: (i, j, 0))
    kspec = lambda w: pl.BlockSpec((1, s, w), lambda i, j: (i, 0, 0))
    return pl.pallas_call(
        functools.partial(_dsa_kernel, tq=tq, tk=tk, top_k=top_k),
        grid=(b, s // tq),
        in_specs=[qspec(qa.shape[2]), qspec(qi.shape[2]), qspec(LANES), kspec(LANES),
                  pl.BlockSpec((s // tk, LANES, tk), lambda i, j: (i, 0, 0)), kspec(LANES)],
        out_specs=qspec(qa.shape[2]),
        out_shape=jax.ShapeDtypeStruct(qa.shape, BF16),
        scratch_shapes=[pltpu.VMEM((s, tq), jnp.int32), pltpu.VMEM((s, tq), jnp.int32), pltpu.VMEM((s, tq), F32),
                        pltpu.VMEM((LANES, DSA_HEADS * tq), F32)],
        compiler_params=_cparams(("parallel", "arbitrary")),
        name="dsa_attention",
    )(qa, qi, wi, ka2, va2_t, ki2)


def _block_kernel(h_ref, oa_ref, ob_ref, oc_ref, wo_ref, g_ref, r_ref, wg_ref, wu_ref, wd_ref, fg_ref,
                  out_ref, h1_ref, xn_ref, comb_ref, acc_ref, acce_ref, *, n_exp, routed, final_norm):
    e = pl.program_id(1)
    f = pl.program_id(2)
    n_f = pl.num_programs(2)

    @pl.when((e == 0) & (f == 0))
    def _():
        wa, wb = oa_ref.shape[1], ob_ref.shape[1]
        h1 = (h_ref[...] + _dot(oa_ref[...], wo_ref[0:wa, :]) + _dot(ob_ref[...], wo_ref[wa:wa + wb, :])
              + _dot(oc_ref[...], wo_ref[wa + wb:, :]))
        h1_ref[...] = h1
        hn = _rms(h1, g_ref[...])
        xn_ref[...] = hn.astype(BF16)
        acc_ref[...] = jnp.zeros_like(acc_ref)
        if routed:
            logits = jnp.dot(hn, r_ref[...], preferred_element_type=F32, precision=lax.Precision.HIGHEST)
            lane = lax.broadcasted_iota(jnp.int32, logits.shape, 1)
            logits = jnp.where(lane < n_exp, logits, -jnp.inf)
            m1 = jnp.max(logits, axis=1, keepdims=True)
            i1 = jnp.min(jnp.where(logits == m1, lane, IDX_BIG), axis=1, keepdims=True)
            rest = jnp.where(lane == i1, -jnp.inf, logits)
            m2 = jnp.max(rest, axis=1, keepdims=True)
            i2 = jnp.min(jnp.where(rest == m2, lane, IDX_BIG), axis=1, keepdims=True)
            e2 = jnp.exp(m2 - m1)
            den = 1.0 + e2
            comb_ref[...] = jnp.where(lane == i1, 1.0 / den, jnp.where(lane == i2, e2 / den, 0.0))

    @pl.when(f == 0)
    def _():
        acce_ref[...] = jnp.zeros_like(acce_ref)

    xn = xn_ref[...]
    g = _dot(xn, wg_ref[0])
    u = _dot(xn, wu_ref[0])
    a = (g * jax.nn.sigmoid(g)) * u
    acce_ref[...] += _dot(a.astype(BF16), wd_ref[0])

    @pl.when(f == n_f - 1)
    def _():
        if routed:
            lane = lax.broadcasted_iota(jnp.int32, comb_ref.shape, 1)
            ce = jnp.sum(jnp.where(lane == e, comb_ref[...], 0.0), axis=1, keepdims=True)
            acc_ref[...] += ce * acce_ref[...]
        else:
            acc_ref[...] += acce_ref[...]

    @pl.when((e == n_exp - 1) & (f == n_f - 1))
    def _():
        y = h1_ref[...] + acc_ref[...]
        if final_norm:
            y = _rms(y, fg_ref[...])
        out_ref[...] = y


def _block(h, oa, ob, oc, w_out, gain, router_p, wg, wu, wd, final_gain, *, routed, final_norm, tm, tf):
    n, d = h.shape
    n_exp, _, dff = wg.shape
    full = lambda a: pl.BlockSpec(a.shape, lambda i, e, f: (0,) * a.ndim)
    row = lambda w: pl.BlockSpec((tm, w), lambda i, e, f: (i, 0))
    gain = gain.reshape(1, d)
    final_gain = final_gain.reshape(1, d)
    return pl.pallas_call(
        functools.partial(_block_kernel, n_exp=n_exp, routed=routed, final_norm=final_norm),
        grid=(n // tm, n_exp, dff // tf),
        in_specs=[row(d), row(oa.shape[1]), row(ob.shape[1]), row(oc.shape[1]), full(w_out), full(gain),
                  full(router_p),
                  pl.BlockSpec((1, d, tf), lambda i, e, f: (e, 0, f)),
                  pl.BlockSpec((1, d, tf), lambda i, e, f: (e, 0, f)),
                  pl.BlockSpec((1, tf, d), lambda i, e, f: (e, f, 0)),
                  full(final_gain)],
        out_specs=row(d),
        out_shape=jax.ShapeDtypeStruct((n, d), F32),
        scratch_shapes=[pltpu.VMEM((tm, d), F32), pltpu.VMEM((tm, d), BF16), pltpu.VMEM((tm, LANES), F32),
                        pltpu.VMEM((tm, d), F32), pltpu.VMEM((tm, d), F32)],
        compiler_params=_cparams(("parallel", "arbitrary", "arbitrary")),
        name="block_routed" if routed else "block_dense",
    )(h, oa, ob, oc, w_out, gain, router_p, wg, wu, wd, final_gain)


def _pick(n, pref):
    t = min(pref, n)
    while n % t:
        t //= 2
    return t


def kernel(x, positions, attn_norm, w_in, mla_q_norm, w_uq, mla_kv_norm, w_ukv, diff_lambda_q1, diff_lambda_k1, diff_lambda_q2, diff_lambda_k2, diff_subln, w_out, ffn_norm, dense_w_gate, dense_w_up, dense_w_down, moe_router, moe_w_gate, moe_w_up, moe_w_down, final_norm):
    b, s, d = x.shape
    n = b * s
    depth = w_in.shape[0]
    tm_proj = _pick(n, 512)
    tm_blk = _pick(n, 1024)
    tk = _pick(s, KV_TILE)
    tq_dsa = _pick(s, 128)

    tables = _rope_tables(positions.astype(F32).reshape(n, 1), _pick(n, 1024))
    h = x.reshape(n, d)
    r3 = lambda a: a.reshape(b, s, a.shape[-1])
    for layer in range(depth):
        w_p, uq, ukn, uv = _prep_proj_weights(w_in[layer], w_uq[layer], w_ukv[layer])
        (qa, ka2, qi, ki2, qb, kb, va2, wi, vb, qc, kc, vc) = _project(
            h, attn_norm[layer], tables, w_p, mla_q_norm[layer], uq, mla_kv_norm[layer], ukn, uv, tm_proj, tk)
        oa = _dsa_attention(r3(qa), r3(qi), r3(wi), r3(ka2), va2, r3(ki2), tq_dsa, tk)
        lam_rows = jnp.zeros((8, LANES), F32).at[0:4, 0:DIFF_QK_DIM].set(jnp.stack(
            [diff_lambda_q1[layer], diff_lambda_k1[layer], diff_lambda_q2[layer], diff_lambda_k2[layer]]))
        ob = _diff_attention(r3(qb), r3(kb), vb, lam_rows, diff_subln[layer], layer, tk)
        oc = _mla_attention(r3(qc), r3(kc), vc, tk)
        j = layer // 2
        last = layer == depth - 1
        wo = w_out[layer].astype(BF16)
        if layer % 2 == 0:
            router_p = jnp.zeros((d, LANES), F32)
            h = _block(h, oa.reshape(n, -1), ob.reshape(n, -1), oc.reshape(n, -1), wo, ffn_norm[layer], router_p,
                       dense_w_gate[j:j + 1].astype(BF16), dense_w_up[j:j + 1].astype(BF16),
                       dense_w_down[j:j + 1].astype(BF16), final_norm,
                       routed=False, final_norm=last, tm=tm_blk, tf=_pick(dense_w_gate.shape[2], 512))
        else:
            router_p = jnp.pad(moe_router[j], ((0, 0), (0, LANES - MOE_EXPERTS)))
            h = _block(h, oa.reshape(n, -1), ob.reshape(n, -1), oc.reshape(n, -1), wo, ffn_norm[layer], router_p,
                       moe_w_gate[j].astype(BF16), moe_w_up[j].astype(BF16), moe_w_down[j].astype(BF16), final_norm,
                       routed=True, final_norm=last, tm=tm_blk, tf=_pick(moe_w_gate.shape[3], 512))
    return h.reshape(b, s, d)
```

```python
import functools
import math

import jax
import jax.numpy as jnp
import numpy as np
from jax import lax
from jax.experimental import pallas as pl
from jax.experimental.pallas import tpu as pltpu

F32 = jnp.float32
BF16 = jnp.bfloat16

LANES = 128
VMEM_LIMIT_BYTES = 56 * 1024 * 1024

D_MODEL = 1024
CHUNK = 64
ROPE_THETA = 500000.0
NORM_EPS = 1e-6
ROPE_FRACTION_DEN = 4

DSA_HEADS = 4
DSA_HEAD_DIM = 64
DSA_IDX_HEADS = 8
DSA_IDX_DIM = 64
DSA_TOPK_MAX = 256
DSA_IDX_SCALE = (DSA_IDX_HEADS * DSA_IDX_DIM) ** -0.5

DIFF_HEADS = 4
DIFF_QK_DIM = 32
DIFF_V_DIM = 2 * DIFF_QK_DIM

MLA_HEADS = 8
MLA_Q_LORA = 256
MLA_KV_LORA = 128
MLA_NOPE_DIM = 64
MLA_ROPE_DIM = 32
MLA_V_DIM = 64

MOE_EXPERTS = 8
MOE_TOP_K = 2
FFN_TILE = 896
MOE_MAIN_ROWS = 320
MOE_EXTRA_ROWS = 128

IN_SPLITS = (
    DSA_HEADS * DSA_HEAD_DIM, DSA_HEAD_DIM, DSA_HEAD_DIM, DSA_IDX_HEADS * DSA_IDX_DIM, DSA_IDX_DIM,
    DSA_IDX_HEADS, DIFF_HEADS * 2 * DIFF_QK_DIM, DIFF_HEADS * 2 * DIFF_QK_DIM, DIFF_HEADS * DIFF_V_DIM,
    MLA_Q_LORA, MLA_KV_LORA, MLA_ROPE_DIM,
)

INT_MIN = -(2 ** 31)
IDX_BIG = 2 ** 30


def _cparams(sem):
    return pltpu.CompilerParams(dimension_semantics=sem, vmem_limit_bytes=VMEM_LIMIT_BYTES)


def _rms(x, g):
    return x * lax.rsqrt(jnp.mean(x * x, axis=-1, keepdims=True) + NORM_EPS) * g


def _dot(a, b):
    return jnp.dot(a, b, preferred_element_type=F32)


def _dot_nt(a, b):
    return lax.dot_general(a, b, (((1,), (1,)), ((), ())), preferred_element_type=F32)


def _split3(x):
    hi = x.astype(BF16)
    r1 = x - hi.astype(F32)
    mid = r1.astype(BF16)
    lo = (r1 - mid.astype(F32)).astype(BF16)
    return hi, mid, lo


def _dot_f32_by_exact(x, m_bf16):
    hi, mid, lo = _split3(x)
    return _dot(hi, m_bf16) + _dot(mid, m_bf16) + _dot(lo, m_bf16)


def _inv_freq(rot_dim):
    half = rot_dim // 2
    return ROPE_THETA ** (-(jnp.arange(half, dtype=F32) * 2.0 / rot_dim))


def _rope_patterns():
    lane = np.arange(LANES)
    freqs, signs = [], []
    rot = DSA_HEAD_DIM // ROPE_FRACTION_DEN
    f = _inv_freq(rot)
    pos = lane % DSA_HEAD_DIM
    active = pos < rot
    freqs.append(jnp.where(active, f[(pos % (rot // 2))], 0.0))
    signs.append(np.where(active, np.where(pos < rot // 2, -1.0, 1.0), 0.0))
    rot = DIFF_QK_DIM // ROPE_FRACTION_DEN
    f = _inv_freq(rot)
    pos = lane % DIFF_QK_DIM
    active = pos < rot
    freqs.append(jnp.where(active, f[(pos % (rot // 2))], 0.0))
    signs.append(np.where(active, np.where(pos < rot // 2, -1.0, 1.0), 0.0))
    rot = MLA_ROPE_DIM
    f = _inv_freq(rot)
    pos = lane - MLA_NOPE_DIM
    active = (pos >= 0) & (pos < rot)
    freqs.append(jnp.where(active, f[(np.clip(pos, 0, rot - 1) % (rot // 2))], 0.0))
    signs.append(np.where(active, np.where(pos < rot // 2, -1.0, 1.0), 0.0))
    freq = jnp.stack(freqs).astype(F32)
    sign = jnp.asarray(np.stack(signs), dtype=F32)
    pad = jnp.zeros((2, LANES), F32)
    return jnp.concatenate([freq, sign, pad], axis=0)


ROPE_HALF = (DSA_HEAD_DIM // ROPE_FRACTION_DEN // 2, DIFF_QK_DIM // ROPE_FRACTION_DEN // 2, MLA_ROPE_DIM // 2)


def _x1_mask(pattern):
    lane = lax.broadcasted_iota(jnp.int32, (1, LANES), 1)
    if pattern == 0:
        return (lane % DSA_HEAD_DIM) < ROPE_HALF[0]
    if pattern == 1:
        return (lane % DIFF_QK_DIM) < ROPE_HALF[1]
    return (lane >= MLA_NOPE_DIM) & (lane < MLA_NOPE_DIM + ROPE_HALF[2])


def _rope_tables_kernel(pos_ref, pat_ref, out_ref):
    pos = pos_ref[...]
    for p in range(3):
        ang = pos * pat_ref[p:p + 1, :]
        out_ref[2 * p] = jnp.cos(ang)
        out_ref[2 * p + 1] = jnp.sin(ang) * pat_ref[3 + p:4 + p, :]


def _rope_tables(pos_f, tm):
    n = pos_f.shape[0]
    return pl.pallas_call(
        _rope_tables_kernel,
        grid=(n // tm,),
        in_specs=[pl.BlockSpec((tm, 1), lambda i: (i, 0)),
                  pl.BlockSpec((8, LANES), lambda i: (0, 0))],
        out_specs=pl.BlockSpec((6, tm, LANES), lambda i: (0, i, 0)),
        out_shape=jax.ShapeDtypeStruct((6, n, LANES), F32),
        compiler_params=_cparams(("parallel",)),
        name="rope_tables",
    )(pos_f, _rope_patterns())


def _rope128(y, cos, sin, pattern):
    half = ROPE_HALF[pattern]
    up = pltpu.roll(y, LANES - half, 1)
    dn = pltpu.roll(y, half, 1)
    return y * cos + jnp.where(_x1_mask(pattern), up, dn) * sin


PROJ_COLS = (
    ("qa", 256, 0), ("ka2", 128, 0), ("qi", 512, 0), ("ki2", 128, 0),
    ("qb", 256, 1), ("kb", 256, 1),
    ("va2", 128, None), ("wi", 128, None), ("vb", 256, None),
    ("cq", 256, None), ("ckv", 128, None), ("kr", 128, 2),
)
PROJ_WIDTH = sum(c[1] for c in PROJ_COLS)
PROJ_OUTS = (("qa", 256, BF16), ("ka2", 128, BF16), ("qi", 512, BF16), ("ki2", 128, BF16),
             ("qb", 256, BF16), ("kb", 256, BF16), ("va2", 128, BF16), ("wi", 128, F32),
             ("vb", 256, BF16), ("qc", 1024, BF16), ("kc", 1024, BF16), ("vc", 512, BF16))
PROJ_TRANSPOSED = ("va2", "vb", "vc")
KV_TILE = 256


def _prep_proj_weights(w_in, w_uq, w_ukv):
    offs = np.cumsum((0,) + IN_SPLITS)
    (q_a, k_a, v_a, q_i, k_i, w_i, q_b, k_b, v_b, c_q, c_kv, k_r) = [
        w_in[:, offs[j]:offs[j + 1]] for j in range(len(IN_SPLITS))]
    d = w_in.shape[0]
    z = lambda n: jnp.zeros((d, n), w_in.dtype)
    cols = {
        "qa": q_a, "ka2": jnp.concatenate([k_a, k_a], 1), "qi": q_i, "ki2": jnp.concatenate([k_i, k_i], 1),
        "qb": q_b, "kb": k_b, "va2": jnp.concatenate([v_a, v_a], 1),
        "wi": jnp.concatenate([w_i, z(LANES - DSA_IDX_HEADS)], 1), "vb": v_b, "cq": c_q, "ckv": c_kv,
        "kr": jnp.concatenate([z(MLA_NOPE_DIM), k_r, z(LANES - MLA_NOPE_DIM - MLA_ROPE_DIM)], 1),
    }
    w_p = jnp.concatenate([cols[name] for name, _, _ in PROJ_COLS], axis=1).astype(BF16)
    qd = MLA_NOPE_DIM + MLA_ROPE_DIM
    uq = w_uq.reshape(MLA_Q_LORA, MLA_HEADS, qd)
    uq = jnp.pad(uq, ((0, 0), (0, 0), (0, LANES - qd))).reshape(MLA_Q_LORA, MLA_HEADS * LANES).astype(BF16)
    ukv = w_ukv.reshape(MLA_KV_LORA, MLA_HEADS, MLA_NOPE_DIM + MLA_V_DIM)
    ukn = jnp.pad(ukv[:, :, :MLA_NOPE_DIM], ((0, 0), (0, 0), (0, LANES - MLA_NOPE_DIM)))
    ukn = ukn.reshape(MLA_KV_LORA, MLA_HEADS * LANES).astype(BF16)
    uv = ukv[:, :, MLA_NOPE_DIM:].reshape(MLA_KV_LORA, MLA_HEADS * MLA_V_DIM).astype(BF16)
    return w_p, uq, ukn, uv


def _proj_kernel(h_ref, g_ref, tab_ref, w_ref, qn_ref, uq_ref, kvn_ref, ukn_ref, uv_ref, *out_refs):
    outs = {name: ref for (name, _, _), ref in zip(PROJ_OUTS, out_refs)}
    xn = _rms(h_ref[...], g_ref[...]).astype(BF16)

    def roped(y, pattern):
        return _rope128(y, tab_ref[2 * pattern], tab_ref[2 * pattern + 1], pattern)

    def emit(name, s, y):
        ref = outs[name]
        if name in PROJ_TRANSPOSED:
            tk = ref.shape[2]
            for t in range(ref.shape[0]):
                ref[t, s * LANES:(s + 1) * LANES, :] = y[t * tk:(t + 1) * tk].T.astype(ref.dtype)
        else:
            ref[:, s * LANES:(s + 1) * LANES] = y.astype(ref.dtype)

    c0 = 0
    vals = {}
    for name, width, pattern in PROJ_COLS:
        for s in range(width // LANES):
            y = _dot(xn, w_ref[:, c0 + s * LANES:c0 + (s + 1) * LANES])
            if pattern is not None:
                y = roped(y, pattern)
            if name in outs:
                emit(name, s, y)
            else:
                vals.setdefault(name, []).append(y)
        c0 += width

    cq = jnp.concatenate(vals["cq"], axis=1)
    cqn = _rms(cq, qn_ref[...]).astype(BF16)
    for hd in range(MLA_HEADS):
        y = _dot(cqn, uq_ref[:, hd * LANES:(hd + 1) * LANES])
        outs["qc"][:, hd * LANES:(hd + 1) * LANES] = roped(y, 2).astype(BF16)

    ckvn = _rms(vals["ckv"][0], kvn_ref[...]).astype(BF16)
    kr = vals["kr"][0]
    for hd in range(MLA_HEADS):
        y = _dot(ckvn, ukn_ref[:, hd * LANES:(hd + 1) * LANES])
        outs["kc"][:, hd * LANES:(hd + 1) * LANES] = (y + kr).astype(BF16)
    for s in range(MLA_HEADS * MLA_V_DIM // LANES):
        emit("vc", s, _dot(ckvn, uv_ref[:, s * LANES:(s + 1) * LANES]))


def _project(h, gain, tables, w_p, q_norm, uq, kv_norm, ukn, uv, tm, tk):
    n, d = h.shape
    full = lambda a: pl.BlockSpec(a.shape, lambda i: (0,) * a.ndim)
    gain = gain.reshape(1, d)
    q_norm = q_norm.reshape(1, -1)
    kv_norm = kv_norm.reshape(1, -1)
    out_specs, out_shape = [], []
    for name, w, dt in PROJ_OUTS:
        if name in PROJ_TRANSPOSED:
            out_specs.append(pl.BlockSpec((tm // tk, w, tk), lambda i: (i, 0, 0)))
            out_shape.append(jax.ShapeDtypeStruct((n // tk, w, tk), dt))
        else:
            out_specs.append(pl.BlockSpec((tm, w), lambda i: (i, 0)))
            out_shape.append(jax.ShapeDtypeStruct((n, w), dt))
    return pl.pallas_call(
        _proj_kernel,
        grid=(n // tm,),
        in_specs=[pl.BlockSpec((tm, d), lambda i: (i, 0)), full(gain),
                  pl.BlockSpec((6, tm, LANES), lambda i: (0, i, 0)),
                  full(w_p), full(q_norm), full(uq), full(kv_norm), full(ukn), full(uv)],
        out_specs=out_specs,
        out_shape=out_shape,
        compiler_params=_cparams(("parallel",)),
        name="projection",
    )(h, gain, tables, w_p, q_norm, uq, kv_norm, ukn, uv)


def _half_masks(dtype, group):
    lane = lax.broadcasted_iota(jnp.int32, (1, LANES), 1)
    return [jnp.where((lane // group) == u, 1.0, 0.0).astype(dtype) for u in range(LANES // group)]


LOG2E = math.log2(math.e)


def _softmax_step(st, v_t, m, l, acc_ref, scale, guard_empty=False):
    c = scale * LOG2E
    m_new = jnp.maximum(m, jnp.max(st, axis=0, keepdims=True))
    m_use = jnp.where(m_new == -jnp.inf, 0.0, m_new) if guard_empty else m_new
    alpha = jnp.exp2((m - m_use) * c)
    p = jnp.exp2((st - m_use) * c)
    l_new = alpha * l + jnp.sum(p, axis=0, keepdims=True)
    acc_ref[...] = alpha * acc_ref[...] + _dot(v_t, p.astype(BF16))
    return m_new, l_new


def _diag_mask(tk, m_cols, tq):
    key_chunk = lax.broadcasted_iota(jnp.int32, (tk, 1), 0) // CHUNK
    qry_chunk = (lax.broadcasted_iota(jnp.int32, (1, m_cols), 1) % tq) // CHUNK
    return key_chunk <= qry_chunk


def _causal_flash(problems, k_ref, v_ref, acc_ref, qi, tq, scale):
    m_cols = problems[0][0].shape[0]
    n_p = len(problems)
    acc_ref[...] = jnp.zeros_like(acc_ref)
    init = tuple(jnp.full((1, m_cols), -jnp.inf, F32) for _ in range(n_p)) + \
        tuple(jnp.zeros((1, m_cols), F32) for _ in range(n_p))

    def step(j, carry, mask):
        r0 = pl.multiple_of(j * tq, tq)
        ms, ls = list(carry[:n_p]), list(carry[n_p:])
        for p, (q_st, k_lane0, v_row0) in enumerate(problems):
            k = k_ref[pl.ds(r0, tq), k_lane0:k_lane0 + LANES]
            st = _dot_nt(k, q_st)
            if mask is not None:
                st = jnp.where(mask, st, -jnp.inf)
            ms[p], ls[p] = _softmax_step(st, v_ref[j, v_row0:v_row0 + LANES, :], ms[p], ls[p], acc_ref.at[p], scale)
        return tuple(ms) + tuple(ls)

    carry = lax.fori_loop(0, qi, lambda j, c: step(j, c, None), init)
    carry = step(qi, carry, _diag_mask(tq, m_cols, tq))
    return [acc_ref[p] / carry[n_p + p] for p in range(n_p)]


def _diff_kernel(lam_ref, subln_ref, gsum_ref, q_ref, k_ref, v_ref, o_ref, acc_ref, *, tq, lam_init):
    qi = pl.program_id(1)
    lam_rows = lam_ref[...]
    s1 = jnp.sum(lam_rows[0:1] * lam_rows[1:2], axis=1, keepdims=True)
    s2 = jnp.sum(lam_rows[2:3] * lam_rows[3:4], axis=1, keepdims=True)
    lam = jnp.exp(s1) - jnp.exp(s2) + lam_init
    masks = _half_masks(BF16, DIFF_QK_DIM)
    row = lax.broadcasted_iota(jnp.int32, (LANES, 1), 0)
    n_slices = DIFF_HEADS // 2
    problems = []
    for s in range(n_slices):
        q = q_ref[0, :, s * LANES:(s + 1) * LANES]
        problems.append((jnp.concatenate([q * mk for mk in masks], axis=0), s * LANES, s * LANES))
    outs = _causal_flash(problems, k_ref.at[0], v_ref, acc_ref, qi, tq, DIFF_QK_DIM ** -0.5)
    for s in range(n_slices):
        o_t = outs[s]
        a0 = o_t[:, 0 * tq:1 * tq] - lam * o_t[:, 1 * tq:2 * tq]
        a1 = o_t[:, 2 * tq:3 * tq] - lam * o_t[:, 3 * tq:4 * tq]
        a = jnp.where(row < DIFF_V_DIM, a0, a1).T
        ss = _dot_f32_by_exact(a * a, gsum_ref[...])
        y = a * lax.rsqrt(ss * (1.0 / DIFF_V_DIM) + NORM_EPS) * subln_ref[...]
        o_ref[0, :, s * LANES:(s + 1) * LANES] = (y * (1.0 - lam_init)).astype(o_ref.dtype)


def _diff_attention(qb, kb, vb_t, lam_rows, subln, layer, tq):
    b, s, w = qb.shape
    lam_init = 0.8 - 0.6 * math.exp(-0.3 * layer)
    subln2 = jnp.concatenate([subln, subln]).reshape(1, LANES).astype(F32)
    lane = np.arange(LANES)
    gsum = jnp.asarray((lane[:, None] // DIFF_V_DIM) == (lane[None, :] // DIFF_V_DIM), dtype=BF16)
    return pl.pallas_call(
        functools.partial(_diff_kernel, tq=tq, lam_init=lam_init),
        grid=(b, s // tq),
        in_specs=[pl.BlockSpec((8, LANES), lambda i, j: (0, 0)),
                  pl.BlockSpec((1, LANES), lambda i, j: (0, 0)),
                  pl.BlockSpec((LANES, LANES), lambda i, j: (0, 0)),
                  pl.BlockSpec((1, tq, w), lambda i, j: (i, j, 0)),
                  pl.BlockSpec((1, s, w), lambda i, j: (i, 0, 0)),
                  pl.BlockSpec((s // tq, w, tq), lambda i, j: (i, 0, 0))],
        out_specs=pl.BlockSpec((1, tq, w), lambda i, j: (i, j, 0)),
        out_shape=jax.ShapeDtypeStruct((b, s, w), BF16),
        scratch_shapes=[pltpu.VMEM((DIFF_HEADS // 2, LANES, 4 * tq), F32)],
        compiler_params=_cparams(("parallel", "arbitrary")),
        name="diff_attention",
    )(lam_rows, subln2, gsum, qb, kb, vb_t)


def _mla_kernel(q_ref, k_ref, v_ref, o_ref, acc_ref, *, tq):
    qi = pl.program_id(1)
    row = lax.broadcasted_iota(jnp.int32, (LANES, 1), 0)
    scale = (MLA_NOPE_DIM + MLA_ROPE_DIM) ** -0.5
    problems = [(q_ref[0, :, hd * LANES:(hd + 1) * LANES], hd * LANES, (hd // 2) * LANES)
                for hd in range(MLA_HEADS)]
    outs = _causal_flash(problems, k_ref.at[0], v_ref, acc_ref, qi, tq, scale)
    for pair in range(MLA_HEADS // 2):
        o_t = jnp.where(row < MLA_V_DIM, outs[2 * pair], outs[2 * pair + 1])
        o_ref[0, :, pair * LANES:(pair + 1) * LANES] = o_t.T.astype(o_ref.dtype)


def _mla_attention(qc, kc, vc_t, tq):
    b, s, wq = qc.shape
    wv = vc_t.shape[1]
    return pl.pallas_call(
        functools.partial(_mla_kernel, tq=tq),
        grid=(b, s // tq),
        in_specs=[pl.BlockSpec((1, tq, wq), lambda i, j: (i, j, 0)),
                  pl.BlockSpec((1, s, wq), lambda i, j: (i, 0, 0)),
                  pl.BlockSpec((s // tq, wv, tq), lambda i, j: (i, 0, 0))],
        out_specs=pl.BlockSpec((1, tq, wv), lambda i, j: (i, j, 0)),
        out_shape=jax.ShapeDtypeStruct((b, s, wv), BF16),
        scratch_shapes=[pltpu.VMEM((MLA_HEADS, LANES, tq), F32)],
        compiler_params=_cparams(("parallel", "arbitrary")),
        name="mla_attention",
    )(qc, kc, vc_t)


KEY_NEG_INF = -2139095041


def _score_keys(score):
    bits = lax.bitcast_convert_type(score, jnp.int32)
    return bits ^ ((bits >> 31) & 0x7FFFFFFF)


def _dsa_kernel(qa_ref, qi_ref, wi_ref, ka_ref, va_ref, ki_ref, o_ref, key_ref, aux_ref, bias_ref, acc_ref,
                *, tq, tk, top_k):
    blk = pl.program_id(1)
    s_len = ka_ref.shape[1]
    n_kv = ((blk + 1) * tq + tk - 1) // tk
    masks = _half_masks(BF16, DSA_IDX_DIM)
    row = lax.broadcasted_iota(jnp.int32, (LANES, 1), 0)
    t_chunk = (blk * tq + lax.broadcasted_iota(jnp.int32, (1, tq), 1)) // CHUNK
    sub_idx = lax.broadcasted_iota(jnp.int32, (tk, 1), 0)

    def tile_start(c):
        return pl.multiple_of(c * tk, tk)

    qi_st = jnp.concatenate([qi_ref[0, :, (hd // 2) * LANES:(hd // 2 + 1) * LANES] * masks[hd % 2]
                             for hd in range(DSA_IDX_HEADS)], axis=0)
    w_t = wi_ref[0].T

    def score_body(c, carry):
        r0 = tile_start(c)
        rel = jnp.maximum(_dot_nt(ki_ref[0, pl.ds(r0, tk), :], qi_st), 0.0)
        score = jnp.zeros((tk, tq), F32)
        for hd in range(DSA_IDX_HEADS):
            score = score + w_t[hd:hd + 1, :] * rel[:, hd * tq:(hd + 1) * tq]
        score = score * DSA_IDX_SCALE
        score = jnp.where((r0 + sub_idx) // CHUNK <= t_chunk, score, -jnp.inf)
        key_ref[pl.ds(r0, tk), :] = _score_keys(score)
        return carry

    lax.fori_loop(0, n_kv, score_body, 0)

    def count(pred_fn):
        n_part = 8

        def body(c, tot):
            ones = jnp.where(pred_fn(tile_start(c)), 1.0, 0.0)
            return tot + jnp.sum(ones.reshape(tk // (8 * n_part), n_part * 8, tq), axis=0)
        tot = lax.fori_loop(0, n_kv, body, jnp.zeros((n_part * 8, tq), F32))
        return jnp.sum(tot, axis=0, keepdims=True)

    cnt_all = count(lambda r0: key_ref[pl.ds(r0, tk), :] >= INT_MIN)
    cnt_pos = count(lambda r0: key_ref[pl.ds(r0, tk), :] >= 0)
    thr = jnp.where(cnt_pos >= top_k, 0, INT_MIN).astype(jnp.int32)
    cnt_thr = jnp.where(cnt_pos >= top_k, cnt_pos, cnt_all)

    def thr_body(i, carry):
        thr, cnt_thr = carry
        cand = thr | (jnp.int32(1) << (30 - i))
        cnt = count(lambda r0: key_ref[pl.ds(r0, tk), :] >= cand)
        ok = cnt >= top_k
        return jnp.where(ok, cand, thr), jnp.where(ok, cnt, cnt_thr)

    thr, cnt_thr = lax.fori_loop(0, 31, thr_body, (thr, cnt_thr))
    thr = jnp.maximum(thr, KEY_NEG_INF)
    tied = jnp.where(cnt_thr > top_k, jnp.where(thr > KEY_NEG_INF, 1.0, 0.0), 0.0)

    def select_with_ties():
        n_gt = count(lambda r0: key_ref[pl.ds(r0, tk), :] > thr)
        need = top_k - n_gt

        def aux_body(c, carry):
            r0 = tile_start(c)
            idx = r0 + sub_idx
            tie = jnp.where(key_ref[pl.ds(r0, tk), :] == thr, jnp.where(idx // CHUNK <= t_chunk, idx, IDX_BIG), IDX_BIG)
            aux_ref[pl.ds(r0, tk), :] = tie
            return carry

        lax.fori_loop(0, n_kv, aux_body, 0)
        n_bits = max(1, (s_len - 1).bit_length())

        def tie_body(i, last):
            cand = last | (jnp.int32(1) << (n_bits - 1 - i))
            cnt = count(lambda r0: aux_ref[pl.ds(r0, tk), :] < cand)
            return jnp.where(cnt < need, cand, last)

        last = lax.fori_loop(0, n_bits, tie_body, jnp.zeros((1, tq), jnp.int32))

        def bias_body(c, carry):
            r0 = tile_start(c)
            bias_ref[pl.ds(r0, tk), :] = jnp.where(
                key_ref[pl.ds(r0, tk), :] > thr, 0.0, jnp.where(aux_ref[pl.ds(r0, tk), :] <= last, 0.0, -jnp.inf))
            return carry

        lax.fori_loop(0, n_kv, bias_body, 0)

    def select_no_ties():
        low = jnp.where(thr > KEY_NEG_INF, thr - 1, thr)

        def bias_body(c, carry):
            r0 = tile_start(c)
            bias_ref[pl.ds(r0, tk), :] = jnp.where(key_ref[pl.ds(r0, tk), :] > low, 0.0, -jnp.inf)
            return carry

        lax.fori_loop(0, n_kv, bias_body, 0)

    lax.cond(jnp.max(tied) > 0.0, select_with_ties, select_no_ties)

    hmask = _half_masks(BF16, DSA_HEAD_DIM)
    qa_st = jnp.concatenate([qa_ref[0, :, (hd // 2) * LANES:(hd // 2 + 1) * LANES] * hmask[hd % 2]
                             for hd in range(DSA_HEADS)], axis=0)
    scale = DSA_HEAD_DIM ** -0.5
    m_cols = DSA_HEADS * tq
    acc_ref[...] = jnp.zeros_like(acc_ref)

    def att_body(c, carry):
        r0 = tile_start(c)
        bias = bias_ref[pl.ds(r0, tk), :]
        st = _dot_nt(ka_ref[0, pl.ds(r0, tk), :], qa_st) + jnp.concatenate([bias] * DSA_HEADS, axis=1)
        return _softmax_step(st, va_ref[c], carry[0], carry[1], acc_ref, scale, guard_empty=True)

    m, l = lax.fori_loop(0, n_kv, att_body, (jnp.full((1, m_cols), -jnp.inf, F32), jnp.zeros((1, m_cols), F32)))
    o_t = acc_ref[...] / l
    for pair in range(DSA_HEADS // 2):
        pair_t = jnp.where(row < DSA_HEAD_DIM, o_t[:, (2 * pair) * tq:(2 * pair + 1) * tq],
                           o_t[:, (2 * pair + 1) * tq:(2 * pair + 2) * tq])
        o_ref[0, :, pair * LANES:(pair + 1) * LANES] = pair_t.T.astype(o_ref.dtype)


def _dsa_attention(qa, qi, wi, ka2, va2_t, ki2, tq, tk):
    b, s, _ = qa.shape
    top_k = min(DSA_TOPK_MAX, s // 4)
    qspec = lambda w: pl.BlockSpec((1, tq, w), lambda i, j: (i, j, 0))
    kspec = lambda w: pl.BlockSpec((1, s, w), lambda i, j: (i, 0, 0))
    return pl.pallas_call(
        functools.partial(_dsa_kernel, tq=tq, tk=tk, top_k=top_k),
        grid=(b, s // tq),
        in_specs=[qspec(qa.shape[2]), qspec(qi.shape[2]), qspec(LANES), kspec(LANES),
                  pl.BlockSpec((s // tk, LANES, tk), lambda i, j: (i, 0, 0)), kspec(LANES)],
        out_specs=qspec(qa.shape[2]),
        out_shape=jax.ShapeDtypeStruct(qa.shape, BF16),
        scratch_shapes=[pltpu.VMEM((s, tq), jnp.int32), pltpu.VMEM((s, tq), jnp.int32), pltpu.VMEM((s, tq), F32),
                        pltpu.VMEM((LANES, DSA_HEADS * tq), F32)],
        compiler_params=_cparams(("parallel", "arbitrary")),
        name="dsa_attention",
    )(qa, qi, wi, ka2, va2_t, ki2)


def _attn_residual(h_ref, oa_ref, ob_ref, oc_ref, wo_ref):
    wa, wb = oa_ref.shape[1], ob_ref.shape[1]
    return (h_ref[...] + _dot(oa_ref[...], wo_ref[0:wa, :]) + _dot(ob_ref[...], wo_ref[wa:wa + wb, :])
            + _dot(oc_ref[...], wo_ref[wa + wb:, :]))


def _swiglu_partial(x, wg, wu, wd):
    g = _dot(x, wg)
    u = _dot(x, wu)
    return _dot(((g * jax.nn.sigmoid(g)) * u).astype(BF16), wd)


def _dense_block_kernel(h_ref, oa_ref, ob_ref, oc_ref, wo_ref, g_ref, wg_ref, wu_ref, wd_ref, fg_ref,
                        out_ref, xn_ref, acc_ref, *, final_norm):
    f = pl.program_id(1)

    @pl.when(f == 0)
    def _():
        h1 = _attn_residual(h_ref, oa_ref, ob_ref, oc_ref, wo_ref)
        acc_ref[...] = h1
        xn_ref[...] = _rms(h1, g_ref[...]).astype(BF16)

    acc_ref[...] += _swiglu_partial(xn_ref[...], wg_ref[...], wu_ref[...], wd_ref[...])

    @pl.when(f == pl.num_programs(1) - 1)
    def _():
        y = acc_ref[...]
        out_ref[...] = _rms(y, fg_ref[...]) if final_norm else y


def _dense_block(h, oa, ob, oc, w_out, gain, wg, wu, wd, final_gain, *, final_norm, tm, tf):
    n, d = h.shape
    dff = wg.shape[1]
    const = lambda a: pl.BlockSpec(a.shape, lambda i, f: (0,) * a.ndim, pipeline_mode=pl.Buffered(1))
    row = lambda w: pl.BlockSpec((tm, w), lambda i, f: (i, 0))
    gain = gain.reshape(1, d)
    final_gain = final_gain.reshape(1, d)
    return pl.pallas_call(
        functools.partial(_dense_block_kernel, final_norm=final_norm),
        grid=(n // tm, dff // tf),
        in_specs=[row(d), row(oa.shape[1]), row(ob.shape[1]), row(oc.shape[1]), const(w_out), const(gain),
                  pl.BlockSpec((d, tf), lambda i, f: (0, f)),
                  pl.BlockSpec((d, tf), lambda i, f: (0, f)),
                  pl.BlockSpec((tf, d), lambda i, f: (f, 0)),
                  const(final_gain)],
        out_specs=row(d),
        out_shape=jax.ShapeDtypeStruct((n, d), F32),
        scratch_shapes=[pltpu.VMEM((tm, d), BF16), pltpu.VMEM((tm, d), F32)],
        compiler_params=_cparams(("parallel", "arbitrary")),
        name="block_dense",
    )(h, oa, ob, oc, w_out, gain, wg, wu, wd, final_gain)


def _moe_block_kernel(h_ref, oa_ref, ob_ref, oc_ref, wo_ref, g_ref, r_ref, tri_ref, wg_ref, wu_ref, wd_ref, fg_ref,
                      out_ref, xn_ref, acc_ref, gate_ref, pos_ref, pos_t_ref, cnt_ref, xs_ref, ye_ref,
                      *, n_exp, main, rows, final_norm):
    e = pl.program_id(1)
    f = pl.program_id(2)
    n_f = pl.num_programs(2)
    tm = xn_ref.shape[0]
    lane = lax.broadcasted_iota(jnp.int32, (1, LANES), 1)

    @pl.when((e == 0) & (f == 0))
    def _():
        h1 = _attn_residual(h_ref, oa_ref, ob_ref, oc_ref, wo_ref)
        acc_ref[...] = h1
        hn = _rms(h1, g_ref[...])
        xn_ref[...] = hn.astype(BF16)
        logits = jnp.dot(hn, r_ref[...], preferred_element_type=F32, precision=lax.Precision.HIGHEST)
        logits = jnp.where(lane < n_exp, logits, -jnp.inf)
        m1 = jnp.max(logits, axis=1, keepdims=True)
        i1 = jnp.min(jnp.where(logits == m1, lane, IDX_BIG), axis=1, keepdims=True)
        rest = jnp.where(lane == i1, -jnp.inf, logits)
        m2 = jnp.max(rest, axis=1, keepdims=True)
        i2 = jnp.min(jnp.where(rest == m2, lane, IDX_BIG), axis=1, keepdims=True)
        e2 = jnp.exp(m2 - m1)
        den = 1.0 + e2
        gate_ref[...] = jnp.where(lane == i1, 1.0 / den, jnp.where(lane == i2, e2 / den, 0.0))
        routed = jnp.where(lane == i1, 1.0, jnp.where(lane == i2, 1.0, 0.0))
        before = _dot(tri_ref[...], routed.astype(BF16))
        slot = jnp.where(routed > 0.0, before, -1.0)
        pos_ref[...] = slot
        pos_t_ref[...] = slot.T
        cnt_ref[...] = jnp.sum(routed, axis=0, keepdims=True)

    n_tok = jnp.sum(jnp.where(lane == e, cnt_ref[...], 0.0)).astype(jnp.int32)
    n_extra = (jnp.maximum(n_tok - main, 0) + rows - 1) // rows

    def for_blocks(fn):
        fn(0, main)
        lax.fori_loop(0, n_extra, lambda r, c: (fn(pl.multiple_of(main + r * rows, math.gcd(main, rows)), rows), c)[1], 0)

    @pl.when(f == 0)
    def _():
        slot_row = pos_t_ref[pl.ds(e, 1), :]

        def gather(r0, nr):
            want = (r0 + lax.broadcasted_iota(jnp.int32, (nr, 1), 0)).astype(F32)
            pick = jnp.where(slot_row == want, 1.0, 0.0).astype(BF16)
            xs_ref[pl.ds(r0, nr), :] = _dot(pick, xn_ref[...]).astype(BF16)
            ye_ref[pl.ds(r0, nr), :] = jnp.zeros((nr, ye_ref.shape[1]), F32)

        for_blocks(gather)

    def expert(r0, nr):
        ye_ref[pl.ds(r0, nr), :] += _swiglu_partial(xs_ref[pl.ds(r0, nr), :], wg_ref[0], wu_ref[0], wd_ref[0])

    for_blocks(expert)

    @pl.when(f == n_f - 1)
    def _():
        full_lane = lax.broadcasted_iota(jnp.int32, (tm, LANES), 1)
        slot_col = jnp.sum(jnp.where(full_lane == e, pos_ref[...], 0.0), axis=1, keepdims=True)
        gate_col = jnp.sum(jnp.where(full_lane == e, gate_ref[...], 0.0), axis=1, keepdims=True)

        def scatter(r0, nr):
            have = (r0 + lax.broadcasted_iota(jnp.int32, (1, nr), 1)).astype(F32)
            place = jnp.where(slot_col == have, 1.0, 0.0).astype(BF16)
            y = ye_ref[pl.ds(r0, nr), :]
            y_hi = y.astype(BF16)
            y_lo = (y - y_hi.astype(F32)).astype(BF16)
            acc_ref[...] += gate_col * (_dot(place, y_hi) + _dot(place, y_lo))

        for_blocks(scatter)

    @pl.when((e == n_exp - 1) & (f == n_f - 1))
    def _():
        y = acc_ref[...]
        out_ref[...] = _rms(y, fg_ref[...]) if final_norm else y


def _moe_block(h, oa, ob, oc, w_out, gain, router_p, wg, wu, wd, final_gain, *, final_norm, tm, tf, main, rows):
    n, d = h.shape
    n_exp, _, dff = wg.shape
    const = lambda a: pl.BlockSpec(a.shape, lambda i, e, f: (0,) * a.ndim, pipeline_mode=pl.Buffered(1))
    row = lambda w: pl.BlockSpec((tm, w), lambda i, e, f: (i, 0))
    row_in = lambda w: pl.BlockSpec((tm, w), lambda i, e, f: (i, 0), pipeline_mode=pl.Buffered(1))
    gain = gain.reshape(1, d)
    final_gain = final_gain.reshape(1, d)
    tok = np.arange(tm)
    tri = jnp.asarray(tok[None, :] < tok[:, None], dtype=BF16)
    main = min(main, tm)
    cap = main + -(-(tm - main) // rows) * rows
    return pl.pallas_call(
        functools.partial(_moe_block_kernel, n_exp=n_exp, main=main, rows=rows, final_norm=final_norm),
        grid=(n // tm, n_exp, dff // tf),
        in_specs=[row_in(d), row_in(oa.shape[1]), row_in(ob.shape[1]), row_in(oc.shape[1]), const(w_out), const(gain),
                  const(router_p), const(tri),
                  pl.BlockSpec((1, d, tf), lambda i, e, f: (e, 0, f)),
                  pl.BlockSpec((1, d, tf), lambda i, e, f: (e, 0, f)),
                  pl.BlockSpec((1, tf, d), lambda i, e, f: (e, f, 0)),
                  const(final_gain)],
        out_specs=row(d),
        out_shape=jax.ShapeDtypeStruct((n, d), F32),
        scratch_shapes=[pltpu.VMEM((tm, d), BF16), pltpu.VMEM((tm, d), F32), pltpu.VMEM((tm, LANES), F32),
                        pltpu.VMEM((tm, LANES), F32), pltpu.VMEM((LANES, tm), F32), pltpu.VMEM((1, LANES), F32),
                        pltpu.VMEM((cap, d), BF16), pltpu.VMEM((cap, d), F32)],
        compiler_params=_cparams(("parallel", "arbitrary", "arbitrary")),
        name="block_moe",
    )(h, oa, ob, oc, w_out, gain, router_p, tri, wg, wu, wd, final_gain)


def _pick(n, pref):
    t = min(pref, n)
    while n % t:
        t //= 2
    return t


def kernel(x, positions, attn_norm, w_in, mla_q_norm, w_uq, mla_kv_norm, w_ukv, diff_lambda_q1, diff_lambda_k1, diff_lambda_q2, diff_lambda_k2, diff_subln, w_out, ffn_norm, dense_w_gate, dense_w_up, dense_w_down, moe_router, moe_w_gate, moe_w_up, moe_w_down, final_norm):
    b, s, d = x.shape
    n = b * s
    depth = w_in.shape[0]
    tm_proj = _pick(n, 512)
    tm_blk = _pick(n, 1024)
    tk = _pick(s, KV_TILE)
    tq_dsa = _pick(s, 128)

    tables = _rope_tables(positions.astype(F32).reshape(n, 1), _pick(n, 1024))
    h = x.reshape(n, d)
    r3 = lambda a: a.reshape(b, s, a.shape[-1])
    for layer in range(depth):
        w_p, uq, ukn, uv = _prep_proj_weights(w_in[layer], w_uq[layer], w_ukv[layer])
        (qa, ka2, qi, ki2, qb, kb, va2, wi, vb, qc, kc, vc) = _project(
            h, attn_norm[layer], tables, w_p, mla_q_norm[layer], uq, mla_kv_norm[layer], ukn, uv, tm_proj, tk)
        oa = _dsa_attention(r3(qa), r3(qi), r3(wi), r3(ka2), va2, r3(ki2), tq_dsa, tk)
        lam_rows = jnp.zeros((8, LANES), F32).at[0:4, 0:DIFF_QK_DIM].set(jnp.stack(
            [diff_lambda_q1[layer], diff_lambda_k1[layer], diff_lambda_q2[layer], diff_lambda_k2[layer]]))
        ob = _diff_attention(r3(qb), r3(kb), vb, lam_rows, diff_subln[layer], layer, tk)
        oc = _mla_attention(r3(qc), r3(kc), vc, tk)
        j = layer // 2
        last = layer == depth - 1
        wo = w_out[layer].astype(BF16)
        attn = (oa.reshape(n, -1), ob.reshape(n, -1), oc.reshape(n, -1))
        if layer % 2 == 0:
            h = _dense_block(h, *attn, wo, ffn_norm[layer], dense_w_gate[j].astype(BF16), dense_w_up[j].astype(BF16),
                             dense_w_down[j].astype(BF16), final_norm,
                             final_norm=last, tm=tm_blk, tf=_pick(dense_w_gate.shape[2], FFN_TILE))
        else:
            router_p = jnp.pad(moe_router[j], ((0, 0), (0, LANES - MOE_EXPERTS)))
            h = _moe_block(h, *attn, wo, ffn_norm[layer], router_p,
                           moe_w_gate[j].astype(BF16), moe_w_up[j].astype(BF16), moe_w_down[j].astype(BF16), final_norm,
                           final_norm=last, tm=tm_blk, tf=_pick(moe_w_gate.shape[3], FFN_TILE),
                           main=MOE_MAIN_ROWS, rows=MOE_EXTRA_ROWS)
    return h.reshape(b, s, d)
```

```python
import functools
import math

import jax
import jax.numpy as jnp
import numpy as np
from jax import lax
from jax.experimental import pallas as pl
from jax.experimental.pallas import tpu as pltpu

F32 = jnp.float32
BF16 = jnp.bfloat16

LANES = 128
VMEM_LIMIT_BYTES = 56 * 1024 * 1024

D_MODEL = 1024
CHUNK = 64
ROPE_THETA = 500000.0
NORM_EPS = 1e-6
ROPE_FRACTION_DEN = 4

DSA_HEADS = 4
DSA_HEAD_DIM = 64
DSA_IDX_HEADS = 8
DSA_IDX_DIM = 64
DSA_TOPK_MAX = 256
DSA_IDX_SCALE = (DSA_IDX_HEADS * DSA_IDX_DIM) ** -0.5

DIFF_HEADS = 4
DIFF_QK_DIM = 32
DIFF_V_DIM = 2 * DIFF_QK_DIM

MLA_HEADS = 8
MLA_Q_LORA = 256
MLA_KV_LORA = 128
MLA_NOPE_DIM = 64
MLA_ROPE_DIM = 32
MLA_V_DIM = 64

MOE_EXPERTS = 8
MOE_TOP_K = 2
FFN_TILE = 896
MOE_MAIN_ROWS = 320
MOE_EXTRA_ROWS = 128

IN_SPLITS = (
    DSA_HEADS * DSA_HEAD_DIM, DSA_HEAD_DIM, DSA_HEAD_DIM, DSA_IDX_HEADS * DSA_IDX_DIM, DSA_IDX_DIM,
    DSA_IDX_HEADS, DIFF_HEADS * 2 * DIFF_QK_DIM, DIFF_HEADS * 2 * DIFF_QK_DIM, DIFF_HEADS * DIFF_V_DIM,
    MLA_Q_LORA, MLA_KV_LORA, MLA_ROPE_DIM,
)

INT_MIN = -(2 ** 31)
IDX_BIG = 2 ** 30


def _cparams(sem):
    return pltpu.CompilerParams(dimension_semantics=sem, vmem_limit_bytes=VMEM_LIMIT_BYTES)


def _rms(x, g):
    return x * lax.rsqrt(jnp.mean(x * x, axis=-1, keepdims=True) + NORM_EPS) * g


def _dot(a, b):
    return jnp.dot(a, b, preferred_element_type=F32)


def _transpose_bf16(x):
    return x.astype(F32).T.astype(BF16)


def _split3(x):
    hi = x.astype(BF16)
    r1 = x - hi.astype(F32)
    mid = r1.astype(BF16)
    lo = (r1 - mid.astype(F32)).astype(BF16)
    return hi, mid, lo


def _dot_f32_by_exact(x, m_bf16):
    hi, mid, lo = _split3(x)
    return _dot(hi, m_bf16) + _dot(mid, m_bf16) + _dot(lo, m_bf16)


def _inv_freq(rot_dim):
    half = rot_dim // 2
    return ROPE_THETA ** (-(jnp.arange(half, dtype=F32) * 2.0 / rot_dim))


def _rope_patterns():
    lane = np.arange(LANES)
    freqs, signs = [], []
    rot = DSA_HEAD_DIM // ROPE_FRACTION_DEN
    f = _inv_freq(rot)
    pos = lane % DSA_HEAD_DIM
    active = pos < rot
    freqs.append(jnp.where(active, f[(pos % (rot // 2))], 0.0))
    signs.append(np.where(active, np.where(pos < rot // 2, -1.0, 1.0), 0.0))
    rot = DIFF_QK_DIM // ROPE_FRACTION_DEN
    f = _inv_freq(rot)
    pos = lane % DIFF_QK_DIM
    active = pos < rot
    freqs.append(jnp.where(active, f[(pos % (rot // 2))], 0.0))
    signs.append(np.where(active, np.where(pos < rot // 2, -1.0, 1.0), 0.0))
    rot = MLA_ROPE_DIM
    f = _inv_freq(rot)
    pos = lane - MLA_NOPE_DIM
    active = (pos >= 0) & (pos < rot)
    freqs.append(jnp.where(active, f[(np.clip(pos, 0, rot - 1) % (rot // 2))], 0.0))
    signs.append(np.where(active, np.where(pos < rot // 2, -1.0, 1.0), 0.0))
    freq = jnp.stack(freqs).astype(F32)
    sign = jnp.asarray(np.stack(signs), dtype=F32)
    pad = jnp.zeros((2, LANES), F32)
    return jnp.concatenate([freq, sign, pad], axis=0)


ROPE_HALF = (DSA_HEAD_DIM // ROPE_FRACTION_DEN // 2, DIFF_QK_DIM // ROPE_FRACTION_DEN // 2, MLA_ROPE_DIM // 2)


def _x1_mask(pattern):
    lane = lax.broadcasted_iota(jnp.int32, (1, LANES), 1)
    if pattern == 0:
        return (lane % DSA_HEAD_DIM) < ROPE_HALF[0]
    if pattern == 1:
        return (lane % DIFF_QK_DIM) < ROPE_HALF[1]
    return (lane >= MLA_NOPE_DIM) & (lane < MLA_NOPE_DIM + ROPE_HALF[2])


def _rope_tables_kernel(pos_ref, pat_ref, out_ref):
    pos = pos_ref[...]
    for p in range(3):
        ang = pos * pat_ref[p:p + 1, :]
        out_ref[2 * p] = jnp.cos(ang)
        out_ref[2 * p + 1] = jnp.sin(ang) * pat_ref[3 + p:4 + p, :]


def _rope_tables(pos_f, tm):
    n = pos_f.shape[0]
    return pl.pallas_call(
        _rope_tables_kernel,
        grid=(n // tm,),
        in_specs=[pl.BlockSpec((tm, 1), lambda i: (i, 0)),
                  pl.BlockSpec((8, LANES), lambda i: (0, 0))],
        out_specs=pl.BlockSpec((6, tm, LANES), lambda i: (0, i, 0)),
        out_shape=jax.ShapeDtypeStruct((6, n, LANES), F32),
        compiler_params=_cparams(("parallel",)),
        name="rope_tables",
    )(pos_f, _rope_patterns())


def _rope128(y, cos, sin, pattern):
    half = ROPE_HALF[pattern]
    up = pltpu.roll(y, LANES - half, 1)
    dn = pltpu.roll(y, half, 1)
    return y * cos + jnp.where(_x1_mask(pattern), up, dn) * sin


PROJ_COLS = (
    ("qa", 256, 0), ("ka2", 128, 0), ("qi", 512, 0), ("ki2", 128, 0),
    ("qb", 256, 1), ("kb", 256, 1),
    ("va2", 128, None), ("wi", 128, None), ("vb", 256, None),
    ("cq", 256, None), ("ckv", 128, None), ("kr", 128, 2),
)
PROJ_WIDTH = sum(c[1] for c in PROJ_COLS)
PROJ_OUTS = (("qa", 256, BF16), ("ka2", 128, BF16), ("qi", 512, BF16), ("ki2", 128, BF16),
             ("qb", 256, BF16), ("kb", 256, BF16), ("va2", 128, BF16), ("wi", 128, F32),
             ("vb", 256, BF16), ("qc", 1024, BF16), ("kc", 1024, BF16), ("vc", 512, BF16))
PROJ_TRANSPOSED = ("va2", "vb", "vc")
KV_TILE = 256
SUB_KEYS = 128


def _prep_proj_weights(w_in, w_uq, w_ukv):
    offs = np.cumsum((0,) + IN_SPLITS)
    (q_a, k_a, v_a, q_i, k_i, w_i, q_b, k_b, v_b, c_q, c_kv, k_r) = [
        w_in[:, offs[j]:offs[j + 1]] for j in range(len(IN_SPLITS))]
    d = w_in.shape[0]
    z = lambda n: jnp.zeros((d, n), w_in.dtype)
    cols = {
        "qa": q_a, "ka2": jnp.concatenate([k_a, k_a], 1), "qi": q_i, "ki2": jnp.concatenate([k_i, k_i], 1),
        "qb": q_b, "kb": k_b, "va2": jnp.concatenate([v_a, v_a], 1),
        "wi": jnp.concatenate([w_i, z(LANES - DSA_IDX_HEADS)], 1), "vb": v_b, "cq": c_q, "ckv": c_kv,
        "kr": jnp.concatenate([z(MLA_NOPE_DIM), k_r, z(LANES - MLA_NOPE_DIM - MLA_ROPE_DIM)], 1),
    }
    w_p = jnp.concatenate([cols[name] for name, _, _ in PROJ_COLS], axis=1).astype(BF16)
    qd = MLA_NOPE_DIM + MLA_ROPE_DIM
    uq = w_uq.reshape(MLA_Q_LORA, MLA_HEADS, qd)
    uq = jnp.pad(uq, ((0, 0), (0, 0), (0, LANES - qd))).reshape(MLA_Q_LORA, MLA_HEADS * LANES).astype(BF16)
    ukv = w_ukv.reshape(MLA_KV_LORA, MLA_HEADS, MLA_NOPE_DIM + MLA_V_DIM)
    ukn = jnp.pad(ukv[:, :, :MLA_NOPE_DIM], ((0, 0), (0, 0), (0, LANES - MLA_NOPE_DIM)))
    ukn = ukn.reshape(MLA_KV_LORA, MLA_HEADS * LANES).astype(BF16)
    uv = ukv[:, :, MLA_NOPE_DIM:].reshape(MLA_KV_LORA, MLA_HEADS * MLA_V_DIM).astype(BF16)
    return w_p, uq, ukn, uv


def _proj_kernel(h_ref, g_ref, tab_ref, w_ref, qn_ref, uq_ref, kvn_ref, ukn_ref, uv_ref, *out_refs):
    outs = {name: ref for (name, _, _), ref in zip(PROJ_OUTS, out_refs)}
    xn = _rms(h_ref[...], g_ref[...]).astype(BF16)

    def roped(y, pattern):
        return _rope128(y, tab_ref[2 * pattern], tab_ref[2 * pattern + 1], pattern)

    def emit(name, s, y):
        ref = outs[name]
        if name in PROJ_TRANSPOSED:
            tk = ref.shape[2]
            for t in range(ref.shape[0]):
                ref[t, s * LANES:(s + 1) * LANES, :] = y[t * tk:(t + 1) * tk].T.astype(ref.dtype)
        else:
            ref[:, s * LANES:(s + 1) * LANES] = y.astype(ref.dtype)

    c0 = 0
    vals = {}
    for name, width, pattern in PROJ_COLS:
        for s in range(width // LANES):
            y = _dot(xn, w_ref[:, c0 + s * LANES:c0 + (s + 1) * LANES])
            if pattern is not None:
                y = roped(y, pattern)
            if name in outs:
                emit(name, s, y)
            else:
                vals.setdefault(name, []).append(y)
        c0 += width

    cq = jnp.concatenate(vals["cq"], axis=1)
    cqn = _rms(cq, qn_ref[...]).astype(BF16)
    for hd in range(MLA_HEADS):
        y = _dot(cqn, uq_ref[:, hd * LANES:(hd + 1) * LANES])
        outs["qc"][:, hd * LANES:(hd + 1) * LANES] = roped(y, 2).astype(BF16)

    ckvn = _rms(vals["ckv"][0], kvn_ref[...]).astype(BF16)
    kr = vals["kr"][0]
    for hd in range(MLA_HEADS):
        y = _dot(ckvn, ukn_ref[:, hd * LANES:(hd + 1) * LANES])
        outs["kc"][:, hd * LANES:(hd + 1) * LANES] = (y + kr).astype(BF16)
    for s in range(MLA_HEADS * MLA_V_DIM // LANES):
        emit("vc", s, _dot(ckvn, uv_ref[:, s * LANES:(s + 1) * LANES]))


def _project(h, gain, tables, w_p, q_norm, uq, kv_norm, ukn, uv, tm, tk):
    n, d = h.shape
    full = lambda a: pl.BlockSpec(a.shape, lambda i: (0,) * a.ndim)
    gain = gain.reshape(1, d)
    q_norm = q_norm.reshape(1, -1)
    kv_norm = kv_norm.reshape(1, -1)
    out_specs, out_shape = [], []
    for name, w, dt in PROJ_OUTS:
        if name in PROJ_TRANSPOSED:
            out_specs.append(pl.BlockSpec((tm // tk, w, tk), lambda i: (i, 0, 0)))
            out_shape.append(jax.ShapeDtypeStruct((n // tk, w, tk), dt))
        else:
            out_specs.append(pl.BlockSpec((tm, w), lambda i: (i, 0)))
            out_shape.append(jax.ShapeDtypeStruct((n, w), dt))
    return pl.pallas_call(
        _proj_kernel,
        grid=(n // tm,),
        in_specs=[pl.BlockSpec((tm, d), lambda i: (i, 0)), full(gain),
                  pl.BlockSpec((6, tm, LANES), lambda i: (0, i, 0)),
                  full(w_p), full(q_norm), full(uq), full(kv_norm), full(ukn), full(uv)],
        out_specs=out_specs,
        out_shape=out_shape,
        compiler_params=_cparams(("parallel",)),
        name="projection",
    )(h, gain, tables, w_p, q_norm, uq, kv_norm, ukn, uv)


def _half_masks(dtype, group):
    lane = lax.broadcasted_iota(jnp.int32, (1, LANES), 1)
    return [jnp.where((lane // group) == u, 1.0, 0.0).astype(dtype) for u in range(LANES // group)]


LOG2E = math.log2(math.e)


def _softmax_step(st, v_t, m, l, acc_ref, scale, guard_empty=False):
    c = scale * LOG2E
    m_new = jnp.maximum(m, jnp.max(st, axis=0, keepdims=True))
    m_use = jnp.where(m_new == -jnp.inf, 0.0, m_new) if guard_empty else m_new
    alpha = jnp.exp2((m - m_use) * c)
    p = jnp.exp2((st - m_use) * c)
    l_new = alpha * l + jnp.sum(p, axis=0, keepdims=True)
    acc_ref[...] = alpha * acc_ref[...] + _dot(v_t, p.astype(BF16))
    return m_new, l_new


def _diag_mask(tk, m_cols, tq):
    key_chunk = lax.broadcasted_iota(jnp.int32, (tk, 1), 0) // CHUNK
    qry_chunk = (lax.broadcasted_iota(jnp.int32, (1, m_cols), 1) % tq) // CHUNK
    return key_chunk <= qry_chunk


def _causal_flash(problems, k_ref, v_ref, acc_ref, qi, tq, scale):
    problems = [(_transpose_bf16(q_st), k_lane0, v_row0) for q_st, k_lane0, v_row0 in problems]
    m_cols = problems[0][0].shape[1]
    n_p = len(problems)
    acc_ref[...] = jnp.zeros_like(acc_ref)
    init = tuple(jnp.full((1, m_cols), -jnp.inf, F32) for _ in range(n_p)) + \
        tuple(jnp.zeros((1, m_cols), F32) for _ in range(n_p))

    sub = min(tq, SUB_KEYS)

    def step(j, carry, mask):
        ms, ls = list(carry[:n_p]), list(carry[n_p:])
        for s0 in range(0, tq, sub):
            r0 = pl.multiple_of(j * tq + s0, sub)
            for p, (q_t, k_lane0, v_row0) in enumerate(problems):
                st = _dot(k_ref[pl.ds(r0, sub), k_lane0:k_lane0 + LANES], q_t)
                if mask is not None:
                    st = jnp.where(mask[s0:s0 + sub], st, -jnp.inf)
                ms[p], ls[p] = _softmax_step(st, v_ref[j, v_row0:v_row0 + LANES, s0:s0 + sub], ms[p], ls[p],
                                             acc_ref.at[p], scale)
        return tuple(ms) + tuple(ls)

    carry = lax.fori_loop(0, qi, lambda j, c: step(j, c, None), init)
    carry = step(qi, carry, _diag_mask(tq, m_cols, tq))
    return [acc_ref[p] / carry[n_p + p] for p in range(n_p)]


def _diff_kernel(lam_ref, subln_ref, gsum_ref, q_ref, k_ref, v_ref, o_ref, acc_ref, *, tq, lam_init):
    qi = pl.program_id(1)
    lam_rows = lam_ref[...]
    s1 = jnp.sum(lam_rows[0:1] * lam_rows[1:2], axis=1, keepdims=True)
    s2 = jnp.sum(lam_rows[2:3] * lam_rows[3:4], axis=1, keepdims=True)
    lam = jnp.exp(s1) - jnp.exp(s2) + lam_init
    masks = _half_masks(BF16, DIFF_QK_DIM)
    row = lax.broadcasted_iota(jnp.int32, (LANES, 1), 0)
    n_slices = DIFF_HEADS // 2
    problems = []
    for s in range(n_slices):
        q = q_ref[0, :, s * LANES:(s + 1) * LANES]
        problems += [(q * mk, s * LANES, s * LANES) for mk in masks]
    outs = _causal_flash(problems, k_ref.at[0], v_ref, acc_ref, qi, tq, DIFF_QK_DIM ** -0.5)
    for s in range(n_slices):
        o0, o1, o2, o3 = outs[4 * s:4 * s + 4]
        a0 = o0 - lam * o1
        a1 = o2 - lam * o3
        a = jnp.where(row < DIFF_V_DIM, a0, a1).T
        ss = _dot_f32_by_exact(a * a, gsum_ref[...])
        y = a * lax.rsqrt(ss * (1.0 / DIFF_V_DIM) + NORM_EPS) * subln_ref[...]
        o_ref[0, :, s * LANES:(s + 1) * LANES] = (y * (1.0 - lam_init)).astype(o_ref.dtype)


def _diff_attention(qb, kb, vb_t, lam_rows, subln, layer, tq):
    b, s, w = qb.shape
    lam_init = 0.8 - 0.6 * math.exp(-0.3 * layer)
    subln2 = jnp.concatenate([subln, subln]).reshape(1, LANES).astype(F32)
    lane = np.arange(LANES)
    gsum = jnp.asarray((lane[:, None] // DIFF_V_DIM) == (lane[None, :] // DIFF_V_DIM), dtype=BF16)
    return pl.pallas_call(
        functools.partial(_diff_kernel, tq=tq, lam_init=lam_init),
        grid=(b, s // tq),
        in_specs=[pl.BlockSpec((8, LANES), lambda i, j: (0, 0)),
                  pl.BlockSpec((1, LANES), lambda i, j: (0, 0)),
                  pl.BlockSpec((LANES, LANES), lambda i, j: (0, 0)),
                  pl.BlockSpec((1, tq, w), lambda i, j: (i, j, 0)),
                  pl.BlockSpec((1, s, w), lambda i, j: (i, 0, 0)),
                  pl.BlockSpec((s // tq, w, tq), lambda i, j: (i, 0, 0))],
        out_specs=pl.BlockSpec((1, tq, w), lambda i, j: (i, j, 0)),
        out_shape=jax.ShapeDtypeStruct((b, s, w), BF16),
        scratch_shapes=[pltpu.VMEM((2 * DIFF_HEADS, LANES, tq), F32)],
        compiler_params=_cparams(("parallel", "arbitrary")),
        name="diff_attention",
    )(lam_rows, subln2, gsum, qb, kb, vb_t)


def _mla_kernel(q_ref, k_ref, v_ref, o_ref, acc_ref, *, tq):
    qi = pl.program_id(1)
    row = lax.broadcasted_iota(jnp.int32, (LANES, 1), 0)
    scale = (MLA_NOPE_DIM + MLA_ROPE_DIM) ** -0.5
    problems = [(q_ref[0, :, hd * LANES:(hd + 1) * LANES], hd * LANES, (hd // 2) * LANES)
                for hd in range(MLA_HEADS)]
    outs = _causal_flash(problems, k_ref.at[0], v_ref, acc_ref, qi, tq, scale)
    for pair in range(MLA_HEADS // 2):
        o_t = jnp.where(row < MLA_V_DIM, outs[2 * pair], outs[2 * pair + 1])
        o_ref[0, :, pair * LANES:(pair + 1) * LANES] = o_t.T.astype(o_ref.dtype)


def _mla_attention(qc, kc, vc_t, tq):
    b, s, wq = qc.shape
    wv = vc_t.shape[1]
    return pl.pallas_call(
        functools.partial(_mla_kernel, tq=tq),
        grid=(b, s // tq),
        in_specs=[pl.BlockSpec((1, tq, wq), lambda i, j: (i, j, 0)),
                  pl.BlockSpec((1, s, wq), lambda i, j: (i, 0, 0)),
                  pl.BlockSpec((s // tq, wv, tq), lambda i, j: (i, 0, 0))],
        out_specs=pl.BlockSpec((1, tq, wv), lambda i, j: (i, j, 0)),
        out_shape=jax.ShapeDtypeStruct((b, s, wv), BF16),
        scratch_shapes=[pltpu.VMEM((MLA_HEADS, LANES, tq), F32)],
        compiler_params=_cparams(("parallel", "arbitrary")),
        name="mla_attention",
    )(qc, kc, vc_t)


KEY_NEG_INF = -2139095041


def _score_keys(score):
    bits = lax.bitcast_convert_type(score, jnp.int32)
    return bits ^ ((bits >> 31) & 0x7FFFFFFF)


def _dsa_kernel(qa_ref, qi_ref, wi_ref, ka_ref, va_ref, ki_ref, o_ref, key_ref, aux_ref, bias_ref, acc_ref,
                *, tq, tk, top_k):
    blk = pl.program_id(1)
    s_len = ka_ref.shape[1]
    n_kv = ((blk + 1) * tq + tk - 1) // tk
    masks = _half_masks(BF16, DSA_IDX_DIM)
    row = lax.broadcasted_iota(jnp.int32, (LANES, 1), 0)
    t_chunk = (blk * tq + lax.broadcasted_iota(jnp.int32, (1, tq), 1)) // CHUNK
    sub_idx = lax.broadcasted_iota(jnp.int32, (tk, 1), 0)

    def tile_start(c):
        return pl.multiple_of(c * tk, tk)

    qi_st = jnp.concatenate([qi_ref[0, :, (hd // 2) * LANES:(hd // 2 + 1) * LANES] * masks[hd % 2]
                             for hd in range(DSA_IDX_HEADS)], axis=0)
    qi_t = _transpose_bf16(qi_st)
    w_t = wi_ref[0].T

    sub = min(tk, SUB_KEYS)
    sub_iota = lax.broadcasted_iota(jnp.int32, (sub, 1), 0)

    def score_body(c, carry):
        for s0 in range(0, tk, sub):
            r0 = pl.multiple_of(c * tk + s0, sub)
            ki = ki_ref[0, pl.ds(r0, sub), :]
            score = jnp.zeros((sub, tq), F32)
            for hd in range(DSA_IDX_HEADS):
                rel = jnp.maximum(_dot(ki, qi_t[:, hd * tq:(hd + 1) * tq]), 0.0)
                score = score + w_t[hd:hd + 1, :] * rel
            score = score * DSA_IDX_SCALE
            score = jnp.where((r0 + sub_iota) // CHUNK <= t_chunk, score, -jnp.inf)
            key_ref[pl.ds(r0, sub), :] = _score_keys(score)
        return carry

    lax.fori_loop(0, n_kv, score_body, 0)

    def count(pred_fn):
        n_part = 8

        def body(c, tot):
            ones = jnp.where(pred_fn(tile_start(c)), 1.0, 0.0)
            return tot + jnp.sum(ones.reshape(tk // (8 * n_part), n_part * 8, tq), axis=0)
        tot = lax.fori_loop(0, n_kv, body, jnp.zeros((n_part * 8, tq), F32))
        return jnp.sum(tot, axis=0, keepdims=True)

    cnt_all = count(lambda r0: key_ref[pl.ds(r0, tk), :] >= INT_MIN)
    cnt_pos = count(lambda r0: key_ref[pl.ds(r0, tk), :] >= 0)
    thr = jnp.where(cnt_pos >= top_k, 0, INT_MIN).astype(jnp.int32)
    cnt_thr = jnp.where(cnt_pos >= top_k, cnt_pos, cnt_all)

    def thr_body(i, carry):
        thr, cnt_thr = carry
        cand = thr | (jnp.int32(1) << (30 - i))
        cnt = count(lambda r0: key_ref[pl.ds(r0, tk), :] >= cand)
        ok = cnt >= top_k
        return jnp.where(ok, cand, thr), jnp.where(ok, cnt, cnt_thr)

    thr, cnt_thr = lax.fori_loop(0, 31, thr_body, (thr, cnt_thr))
    thr = jnp.maximum(thr, KEY_NEG_INF)
    tied = jnp.where(cnt_thr > top_k, jnp.where(thr > KEY_NEG_INF, 1.0, 0.0), 0.0)

    def select_with_ties():
        n_gt = count(lambda r0: key_ref[pl.ds(r0, tk), :] > thr)
        need = top_k - n_gt

        def aux_body(c, carry):
            r0 = tile_start(c)
            idx = r0 + sub_idx
            tie = jnp.where(key_ref[pl.ds(r0, tk), :] == thr, jnp.where(idx // CHUNK <= t_chunk, idx, IDX_BIG), IDX_BIG)
            aux_ref[pl.ds(r0, tk), :] = tie
            return carry

        lax.fori_loop(0, n_kv, aux_body, 0)
        n_bits = max(1, (s_len - 1).bit_length())

        def tie_body(i, last):
            cand = last | (jnp.int32(1) << (n_bits - 1 - i))
            cnt = count(lambda r0: aux_ref[pl.ds(r0, tk), :] < cand)
            return jnp.where(cnt < need, cand, last)

        last = lax.fori_loop(0, n_bits, tie_body, jnp.zeros((1, tq), jnp.int32))

        def bias_body(c, carry):
            r0 = tile_start(c)
            bias_ref[pl.ds(r0, tk), :] = jnp.where(
                key_ref[pl.ds(r0, tk), :] > thr, 0.0, jnp.where(aux_ref[pl.ds(r0, tk), :] <= last, 0.0, -jnp.inf))
            return carry

        lax.fori_loop(0, n_kv, bias_body, 0)

    def select_no_ties():
        low = jnp.where(thr > KEY_NEG_INF, thr - 1, thr)

        def bias_body(c, carry):
            r0 = tile_start(c)
            bias_ref[pl.ds(r0, tk), :] = jnp.where(key_ref[pl.ds(r0, tk), :] > low, 0.0, -jnp.inf)
            return carry

        lax.fori_loop(0, n_kv, bias_body, 0)

    lax.cond(jnp.max(tied) > 0.0, select_with_ties, select_no_ties)

    hmask = _half_masks(BF16, DSA_HEAD_DIM)
    qa_st = jnp.concatenate([qa_ref[0, :, (hd // 2) * LANES:(hd // 2 + 1) * LANES] * hmask[hd % 2]
                             for hd in range(DSA_HEADS)], axis=0)
    qa_t = _transpose_bf16(qa_st)
    scale = DSA_HEAD_DIM ** -0.5
    n_h = DSA_HEADS
    acc_ref[...] = jnp.zeros_like(acc_ref)

    def att_body(c, carry):
        ms, ls = list(carry[:n_h]), list(carry[n_h:])
        for s0 in range(0, tk, sub):
            r0 = pl.multiple_of(c * tk + s0, sub)
            bias = bias_ref[pl.ds(r0, sub), :]
            ka = ka_ref[0, pl.ds(r0, sub), :]
            for hd in range(n_h):
                st = _dot(ka, qa_t[:, hd * tq:(hd + 1) * tq]) + bias
                ms[hd], ls[hd] = _softmax_step(st, va_ref[c, :, s0:s0 + sub], ms[hd], ls[hd], acc_ref.at[hd],
                                               scale, guard_empty=True)
        return tuple(ms) + tuple(ls)

    init = tuple(jnp.full((1, tq), -jnp.inf, F32) for _ in range(n_h)) + \
        tuple(jnp.zeros((1, tq), F32) for _ in range(n_h))
    carry = lax.fori_loop(0, n_kv, att_body, init)
    for p in range(n_h // 2):
        pair_t = jnp.where(row < DSA_HEAD_DIM, acc_ref[2 * p] / carry[n_h + 2 * p],
                           acc_ref[2 * p + 1] / carry[n_h + 2 * p + 1])
        o_ref[0, :, p * LANES:(p + 1) * LANES] = pair_t.T.astype(o_ref.dtype)


def _dsa_attention(qa, qi, wi, ka2, va2_t, ki2, tq, tk):
    b, s, _ = qa.shape
    top_k = min(DSA_TOPK_MAX, s // 4)
    qspec = lambda w: pl.BlockSpec((1, tq, w), lambda i, j: (i, j, 0))
    kspec = lambda w: pl.BlockSpec((1, s, w), lambda i, j: (i, 0, 0))
    return pl.pallas_call(
        functools.partial(_dsa_kernel, tq=tq, tk=tk, top_k=top_k),
        grid=(b, s // tq),
        in_specs=[qspec(qa.shape[2]), qspec(qi.shape[2]), qspec(LANES), kspec(LANES),
                  pl.BlockSpec((s // tk, LANES, tk), lambda i, j: (i, 0, 0)), kspec(LANES)],
        out_specs=qspec(qa.shape[2]),
        out_shape=jax.ShapeDtypeStruct(qa.shape, BF16),
        scratch_shapes=[pltpu.VMEM((s, tq), jnp.int32), pltpu.VMEM((s, tq), jnp.int32), pltpu.VMEM((s, tq), F32),
                        pltpu.VMEM((DSA_HEADS, LANES, tq), F32)],
        compiler_params=_cparams(("parallel", "arbitrary")),
        name="dsa_attention",
    )(qa, qi, wi, ka2, va2_t, ki2)


def _attn_residual(h_ref, oa_ref, ob_ref, oc_ref, wo_ref):
    wa, wb = oa_ref.shape[1], ob_ref.shape[1]
    return (h_ref[...] + _dot(oa_ref[...], wo_ref[0:wa, :]) + _dot(ob_ref[...], wo_ref[wa:wa + wb, :])
            + _dot(oc_ref[...], wo_ref[wa + wb:, :]))


def _swiglu_partial(x, wg, wu, wd):
    g = _dot(x, wg)
    u = _dot(x, wu)
    return _dot(((g * jax.nn.sigmoid(g)) * u).astype(BF16), wd)


def _dense_block_kernel(h_ref, oa_ref, ob_ref, oc_ref, wo_ref, g_ref, wg_ref, wu_ref, wd_ref, fg_ref,
                        out_ref, xn_ref, acc_ref, *, final_norm):
    f = pl.program_id(1)

    @pl.when(f == 0)
    def _():
        h1 = _attn_residual(h_ref, oa_ref, ob_ref, oc_ref, wo_ref)
        acc_ref[...] = h1
        xn_ref[...] = _rms(h1, g_ref[...]).astype(BF16)

    acc_ref[...] += _swiglu_partial(xn_ref[...], wg_ref[...], wu_ref[...], wd_ref[...])

    @pl.when(f == pl.num_programs(1) - 1)
    def _():
        y = acc_ref[...]
        out_ref[...] = _rms(y, fg_ref[...]) if final_norm else y


def _dense_block(h, oa, ob, oc, w_out, gain, wg, wu, wd, final_gain, *, final_norm, tm, tf):
    n, d = h.shape
    dff = wg.shape[1]
    const = lambda a: pl.BlockSpec(a.shape, lambda i, f: (0,) * a.ndim, pipeline_mode=pl.Buffered(1))
    row = lambda w: pl.BlockSpec((tm, w), lambda i, f: (i, 0))
    gain = gain.reshape(1, d)
    final_gain = final_gain.reshape(1, d)
    return pl.pallas_call(
        functools.partial(_dense_block_kernel, final_norm=final_norm),
        grid=(n // tm, dff // tf),
        in_specs=[row(d), row(oa.shape[1]), row(ob.shape[1]), row(oc.shape[1]), const(w_out), const(gain),
                  pl.BlockSpec((d, tf), lambda i, f: (0, f)),
                  pl.BlockSpec((d, tf), lambda i, f: (0, f)),
                  pl.BlockSpec((tf, d), lambda i, f: (f, 0)),
                  const(final_gain)],
        out_specs=row(d),
        out_shape=jax.ShapeDtypeStruct((n, d), F32),
        scratch_shapes=[pltpu.VMEM((tm, d), BF16), pltpu.VMEM((tm, d), F32)],
        compiler_params=_cparams(("parallel", "arbitrary")),
        name="block_dense",
    )(h, oa, ob, oc, w_out, gain, wg, wu, wd, final_gain)


def _moe_block_kernel(h_ref, oa_ref, ob_ref, oc_ref, wo_ref, g_ref, r_ref, tri_ref, wg_ref, wu_ref, wd_ref, fg_ref,
                      out_ref, xn_ref, acc_ref, gate_ref, pos_ref, pos_t_ref, cnt_ref, xs_ref, ye_ref,
                      *, n_exp, main, rows, final_norm):
    e = pl.program_id(1)
    f = pl.program_id(2)
    n_f = pl.num_programs(2)
    tm = xn_ref.shape[0]
    lane = lax.broadcasted_iota(jnp.int32, (1, LANES), 1)

    @pl.when((e == 0) & (f == 0))
    def _():
        h1 = _attn_residual(h_ref, oa_ref, ob_ref, oc_ref, wo_ref)
        acc_ref[...] = h1
        hn = _rms(h1, g_ref[...])
        xn_ref[...] = hn.astype(BF16)
        logits = jnp.dot(hn, r_ref[...], preferred_element_type=F32, precision=lax.Precision.HIGHEST)
        logits = jnp.where(lane < n_exp, logits, -jnp.inf)
        m1 = jnp.max(logits, axis=1, keepdims=True)
        i1 = jnp.min(jnp.where(logits == m1, lane, IDX_BIG), axis=1, keepdims=True)
        rest = jnp.where(lane == i1, -jnp.inf, logits)
        m2 = jnp.max(rest, axis=1, keepdims=True)
        i2 = jnp.min(jnp.where(rest == m2, lane, IDX_BIG), axis=1, keepdims=True)
        e2 = jnp.exp(m2 - m1)
        den = 1.0 + e2
        gate_ref[...] = jnp.where(lane == i1, 1.0 / den, jnp.where(lane == i2, e2 / den, 0.0))
        routed = jnp.where(lane == i1, 1.0, jnp.where(lane == i2, 1.0, 0.0))
        before = _dot(tri_ref[...], routed.astype(BF16))
        slot = jnp.where(routed > 0.0, before, -1.0)
        pos_ref[...] = slot
        pos_t_ref[...] = slot.T
        cnt_ref[...] = jnp.sum(routed, axis=0, keepdims=True)

    n_tok = jnp.sum(jnp.where(lane == e, cnt_ref[...], 0.0)).astype(jnp.int32)
    n_extra = (jnp.maximum(n_tok - main, 0) + rows - 1) // rows

    def for_blocks(fn):
        fn(0, main)
        lax.fori_loop(0, n_extra, lambda r, c: (fn(pl.multiple_of(main + r * rows, math.gcd(main, rows)), rows), c)[1], 0)

    @pl.when(f == 0)
    def _():
        slot_row = pos_t_ref[pl.ds(e, 1), :]

        def gather(r0, nr):
            want = (r0 + lax.broadcasted_iota(jnp.int32, (nr, 1), 0)).astype(F32)
            pick = jnp.where(slot_row == want, 1.0, 0.0).astype(BF16)
            xs_ref[pl.ds(r0, nr), :] = _dot(pick, xn_ref[...]).astype(BF16)
            ye_ref[pl.ds(r0, nr), :] = jnp.zeros((nr, ye_ref.shape[1]), F32)

        for_blocks(gather)

    def expert(r0, nr):
        ye_ref[pl.ds(r0, nr), :] += _swiglu_partial(xs_ref[pl.ds(r0, nr), :], wg_ref[0], wu_ref[0], wd_ref[0])

    for_blocks(expert)

    @pl.when(f == n_f - 1)
    def _():
        full_lane = lax.broadcasted_iota(jnp.int32, (tm, LANES), 1)
        slot_col = jnp.sum(jnp.where(full_lane == e, pos_ref[...], 0.0), axis=1, keepdims=True)
        gate_col = jnp.sum(jnp.where(full_lane == e, gate_ref[...], 0.0), axis=1, keepdims=True)

        def scatter(r0, nr):
            have = (r0 + lax.broadcasted_iota(jnp.int32, (1, nr), 1)).astype(F32)
            place = jnp.where(slot_col == have, 1.0, 0.0).astype(BF16)
            y = ye_ref[pl.ds(r0, nr), :]
            y_hi = y.astype(BF16)
            y_lo = (y - y_hi.astype(F32)).astype(BF16)
            acc_ref[...] += gate_col * (_dot(place, y_hi) + _dot(place, y_lo))

        for_blocks(scatter)

    @pl.when((e == n_exp - 1) & (f == n_f - 1))
    def _():
        y = acc_ref[...]
        out_ref[...] = _rms(y, fg_ref[...]) if final_norm else y


def _moe_block(h, oa, ob, oc, w_out, gain, router_p, wg, wu, wd, final_gain, *, final_norm, tm, tf, main, rows):
    n, d = h.shape
    n_exp, _, dff = wg.shape
    const = lambda a: pl.BlockSpec(a.shape, lambda i, e, f: (0,) * a.ndim, pipeline_mode=pl.Buffered(1))
    row = lambda w: pl.BlockSpec((tm, w), lambda i, e, f: (i, 0))
    row_in = lambda w: pl.BlockSpec((tm, w), lambda i, e, f: (i, 0), pipeline_mode=pl.Buffered(1))
    gain = gain.reshape(1, d)
    final_gain = final_gain.reshape(1, d)
    tok = np.arange(tm)
    tri = jnp.asarray(tok[None, :] < tok[:, None], dtype=BF16)
    main = min(main, tm)
    cap = main + -(-(tm - main) // rows) * rows
    return pl.pallas_call(
        functools.partial(_moe_block_kernel, n_exp=n_exp, main=main, rows=rows, final_norm=final_norm),
        grid=(n // tm, n_exp, dff // tf),
        in_specs=[row_in(d), row_in(oa.shape[1]), row_in(ob.shape[1]), row_in(oc.shape[1]), const(w_out), const(gain),
                  const(router_p), const(tri),
                  pl.BlockSpec((1, d, tf), lambda i, e, f: (e, 0, f)),
                  pl.BlockSpec((1, d, tf), lambda i, e, f: (e, 0, f)),
                  pl.BlockSpec((1, tf, d), lambda i, e, f: (e, f, 0)),
                  const(final_gain)],
        out_specs=row(d),
        out_shape=jax.ShapeDtypeStruct((n, d), F32),
        scratch_shapes=[pltpu.VMEM((tm, d), BF16), pltpu.VMEM((tm, d), F32), pltpu.VMEM((tm, LANES), F32),
                        pltpu.VMEM((tm, LANES), F32), pltpu.VMEM((LANES, tm), F32), pltpu.VMEM((1, LANES), F32),
                        pltpu.VMEM((cap, d), BF16), pltpu.VMEM((cap, d), F32)],
        compiler_params=_cparams(("parallel", "arbitrary", "arbitrary")),
        name="block_moe",
    )(h, oa, ob, oc, w_out, gain, router_p, tri, wg, wu, wd, final_gain)


def _pick(n, pref):
    t = min(pref, n)
    while n % t:
        t //= 2
    return t


def kernel(x, positions, attn_norm, w_in, mla_q_norm, w_uq, mla_kv_norm, w_ukv, diff_lambda_q1, diff_lambda_k1, diff_lambda_q2, diff_lambda_k2, diff_subln, w_out, ffn_norm, dense_w_gate, dense_w_up, dense_w_down, moe_router, moe_w_gate, moe_w_up, moe_w_down, final_norm):
    b, s, d = x.shape
    n = b * s
    depth = w_in.shape[0]
    tm_proj = _pick(n, 512)
    tm_blk = _pick(n, 1024)
    tk = _pick(s, KV_TILE)
    tq_dsa = _pick(s, 256)

    tables = _rope_tables(positions.astype(F32).reshape(n, 1), _pick(n, 1024))
    h = x.reshape(n, d)
    r3 = lambda a: a.reshape(b, s, a.shape[-1])
    for layer in range(depth):
        w_p, uq, ukn, uv = _prep_proj_weights(w_in[layer], w_uq[layer], w_ukv[layer])
        (qa, ka2, qi, ki2, qb, kb, va2, wi, vb, qc, kc, vc) = _project(
            h, attn_norm[layer], tables, w_p, mla_q_norm[layer], uq, mla_kv_norm[layer], ukn, uv, tm_proj, tk)
        oa = _dsa_attention(r3(qa), r3(qi), r3(wi), r3(ka2), va2, r3(ki2), tq_dsa, tk)
        lam_rows = jnp.zeros((8, LANES), F32).at[0:4, 0:DIFF_QK_DIM].set(jnp.stack(
            [diff_lambda_q1[layer], diff_lambda_k1[layer], diff_lambda_q2[layer], diff_lambda_k2[layer]]))
        ob = _diff_attention(r3(qb), r3(kb), vb, lam_rows, diff_subln[layer], layer, tk)
        oc = _mla_attention(r3(qc), r3(kc), vc, tk)
        j = layer // 2
        last = layer == depth - 1
        wo = w_out[layer].astype(BF16)
        attn = (oa.reshape(n, -1), ob.reshape(n, -1), oc.reshape(n, -1))
        if layer % 2 == 0:
            h = _dense_block(h, *attn, wo, ffn_norm[layer], dense_w_gate[j].astype(BF16), dense_w_up[j].astype(BF16),
                             dense_w_down[j].astype(BF16), final_norm,
                             final_norm=last, tm=tm_blk, tf=_pick(dense_w_gate.shape[2], FFN_TILE))
        else:
            router_p = jnp.pad(moe_router[j], ((0, 0), (0, LANES - MOE_EXPERTS)))
            h = _moe_block(h, *attn, wo, ffn_norm[layer], router_p,
                           moe_w_gate[j].astype(BF16), moe_w_up[j].astype(BF16), moe_w_down[j].astype(BF16), final_norm,
                           final_norm=last, tm=tm_blk, tf=_pick(moe_w_gate.shape[3], FFN_TILE),
                           main=MOE_MAIN_ROWS, rows=MOE_EXTRA_ROWS)
    return h.reshape(b, s, d)
```

```python
import functools
import math

import jax
import jax.numpy as jnp
import numpy as np
from jax import lax
from jax.experimental import pallas as pl
from jax.experimental.pallas import tpu as pltpu

F32 = jnp.float32
BF16 = jnp.bfloat16

LANES = 128
MXU_COLS = 256
VMEM_LIMIT_BYTES = 56 * 1024 * 1024

D_MODEL = 1024
CHUNK = 64
ROPE_THETA = 500000.0
NORM_EPS = 1e-6
ROPE_FRACTION_DEN = 4

DSA_HEADS = 4
DSA_HEAD_DIM = 64
DSA_IDX_HEADS = 8
DSA_IDX_DIM = 64
DSA_TOPK_MAX = 256
DSA_IDX_SCALE = (DSA_IDX_HEADS * DSA_IDX_DIM) ** -0.5

DIFF_HEADS = 4
DIFF_QK_DIM = 32
DIFF_V_DIM = 2 * DIFF_QK_DIM

MLA_HEADS = 8
MLA_Q_LORA = 256
MLA_KV_LORA = 128
MLA_NOPE_DIM = 64
MLA_ROPE_DIM = 32
MLA_V_DIM = 64

MOE_EXPERTS = 8
MOE_TOP_K = 2
FFN_TILE = 896
DENSE_FFN_TILE = 1792
MOE_MAIN_ROWS = (256, 288, 320)
MOE_EXTRA_ROWS = 128

IN_SPLITS = (
    DSA_HEADS * DSA_HEAD_DIM, DSA_HEAD_DIM, DSA_HEAD_DIM, DSA_IDX_HEADS * DSA_IDX_DIM, DSA_IDX_DIM,
    DSA_IDX_HEADS, DIFF_HEADS * 2 * DIFF_QK_DIM, DIFF_HEADS * 2 * DIFF_QK_DIM, DIFF_HEADS * DIFF_V_DIM,
    MLA_Q_LORA, MLA_KV_LORA, MLA_ROPE_DIM,
)

INT_MIN = -(2 ** 31)
IDX_BIG = 2 ** 30


def _cparams(sem):
    return pltpu.CompilerParams(dimension_semantics=sem, vmem_limit_bytes=VMEM_LIMIT_BYTES)


def _rms(x, g):
    return x * lax.rsqrt(jnp.mean(x * x, axis=-1, keepdims=True) + NORM_EPS) * g


def _dot(a, b):
    return jnp.dot(a, b, preferred_element_type=F32)


def _transpose_bf16(x):
    return x.astype(F32).T.astype(BF16)


def _split3(x):
    hi = x.astype(BF16)
    r1 = x - hi.astype(F32)
    mid = r1.astype(BF16)
    lo = (r1 - mid.astype(F32)).astype(BF16)
    return hi, mid, lo


def _dot_f32_by_exact(x, m_bf16):
    hi, mid, lo = _split3(x)
    return _dot(hi, m_bf16) + _dot(mid, m_bf16) + _dot(lo, m_bf16)


def _inv_freq(rot_dim):
    half = rot_dim // 2
    return ROPE_THETA ** (-(jnp.arange(half, dtype=F32) * 2.0 / rot_dim))


def _rope_patterns():
    lane = np.arange(LANES)
    freqs, signs = [], []
    rot = DSA_HEAD_DIM // ROPE_FRACTION_DEN
    f = _inv_freq(rot)
    pos = lane % DSA_HEAD_DIM
    active = pos < rot
    freqs.append(jnp.where(active, f[(pos % (rot // 2))], 0.0))
    signs.append(np.where(active, np.where(pos < rot // 2, -1.0, 1.0), 0.0))
    rot = DIFF_QK_DIM // ROPE_FRACTION_DEN
    f = _inv_freq(rot)
    pos = lane % DIFF_QK_DIM
    active = pos < rot
    freqs.append(jnp.where(active, f[(pos % (rot // 2))], 0.0))
    signs.append(np.where(active, np.where(pos < rot // 2, -1.0, 1.0), 0.0))
    rot = MLA_ROPE_DIM
    f = _inv_freq(rot)
    pos = lane - MLA_NOPE_DIM
    active = (pos >= 0) & (pos < rot)
    freqs.append(jnp.where(active, f[(np.clip(pos, 0, rot - 1) % (rot // 2))], 0.0))
    signs.append(np.where(active, np.where(pos < rot // 2, -1.0, 1.0), 0.0))
    freq = jnp.stack(freqs).astype(F32)
    sign = jnp.asarray(np.stack(signs), dtype=F32)
    pad = jnp.zeros((2, LANES), F32)
    return jnp.concatenate([freq, sign, pad], axis=0)


ROPE_HALF = (DSA_HEAD_DIM // ROPE_FRACTION_DEN // 2, DIFF_QK_DIM // ROPE_FRACTION_DEN // 2, MLA_ROPE_DIM // 2)


def _x1_mask(pattern):
    lane = lax.broadcasted_iota(jnp.int32, (1, LANES), 1)
    if pattern == 0:
        return (lane % DSA_HEAD_DIM) < ROPE_HALF[0]
    if pattern == 1:
        return (lane % DIFF_QK_DIM) < ROPE_HALF[1]
    return (lane >= MLA_NOPE_DIM) & (lane < MLA_NOPE_DIM + ROPE_HALF[2])


def _rope_tables_kernel(pos_ref, pat_ref, out_ref):
    pos = pos_ref[...]
    for p in range(3):
        ang = pos * pat_ref[p:p + 1, :]
        out_ref[2 * p] = jnp.cos(ang)
        out_ref[2 * p + 1] = jnp.sin(ang) * pat_ref[3 + p:4 + p, :]


def _rope_tables(pos_f, tm):
    n = pos_f.shape[0]
    return pl.pallas_call(
        _rope_tables_kernel,
        grid=(n // tm,),
        in_specs=[pl.BlockSpec((tm, 1), lambda i: (i, 0)),
                  pl.BlockSpec((8, LANES), lambda i: (0, 0))],
        out_specs=pl.BlockSpec((6, tm, LANES), lambda i: (0, i, 0)),
        out_shape=jax.ShapeDtypeStruct((6, n, LANES), F32),
        compiler_params=_cparams(("parallel",)),
        name="rope_tables",
    )(pos_f, _rope_patterns())


def _rope128(y, cos, sin, pattern):
    half = ROPE_HALF[pattern]
    up = pltpu.roll(y, LANES - half, 1)
    dn = pltpu.roll(y, half, 1)
    return y * cos + jnp.where(_x1_mask(pattern), up, dn) * sin


PROJ_COLS = (
    ("qa", 256, 0), ("ka2", 128, 0), ("qi", 512, 0), ("ki2", 128, 0),
    ("qb", 256, 1), ("kb", 256, 1),
    ("va2", 128, None), ("wi", 128, None), ("vb", 256, None),
    ("cq", 256, None), ("ckv", 128, None), ("kr", 128, 2),
)
PROJ_WIDTH = sum(c[1] for c in PROJ_COLS)
PROJ_OUTS = (("qa", 256, BF16), ("ka2", 128, BF16), ("qi", 512, BF16), ("ki2", 128, BF16),
             ("qb", 256, BF16), ("kb", 256, BF16), ("va2", 128, BF16), ("wi", 128, F32),
             ("vb", 256, BF16), ("qc", 1024, BF16), ("kc", 1024, BF16), ("vc", 512, BF16))
PROJ_TRANSPOSED = ("va2", "vb", "vc")
KV_TILE = 256
SUB_KEYS = 128


def _prep_proj_weights(w_in, w_uq, w_ukv):
    offs = np.cumsum((0,) + IN_SPLITS)
    (q_a, k_a, v_a, q_i, k_i, w_i, q_b, k_b, v_b, c_q, c_kv, k_r) = [
        w_in[:, offs[j]:offs[j + 1]] for j in range(len(IN_SPLITS))]
    d = w_in.shape[0]
    z = lambda n: jnp.zeros((d, n), w_in.dtype)
    cols = {
        "qa": q_a, "ka2": jnp.concatenate([k_a, k_a], 1), "qi": q_i, "ki2": jnp.concatenate([k_i, k_i], 1),
        "qb": q_b, "kb": k_b, "va2": jnp.concatenate([v_a, v_a], 1),
        "wi": jnp.concatenate([w_i, z(LANES - DSA_IDX_HEADS)], 1), "vb": v_b, "cq": c_q, "ckv": c_kv,
        "kr": jnp.concatenate([z(MLA_NOPE_DIM), k_r, z(LANES - MLA_NOPE_DIM - MLA_ROPE_DIM)], 1),
    }
    w_p = jnp.concatenate([cols[name] for name, _, _ in PROJ_COLS], axis=1).astype(BF16)
    qd = MLA_NOPE_DIM + MLA_ROPE_DIM
    uq = w_uq.reshape(MLA_Q_LORA, MLA_HEADS, qd)
    uq = jnp.pad(uq, ((0, 0), (0, 0), (0, LANES - qd))).reshape(MLA_Q_LORA, MLA_HEADS * LANES).astype(BF16)
    ukv = w_ukv.reshape(MLA_KV_LORA, MLA_HEADS, MLA_NOPE_DIM + MLA_V_DIM)
    ukn = jnp.pad(ukv[:, :, :MLA_NOPE_DIM], ((0, 0), (0, 0), (0, LANES - MLA_NOPE_DIM)))
    ukn = ukn.reshape(MLA_KV_LORA, MLA_HEADS * LANES).astype(BF16)
    uv = ukv[:, :, MLA_NOPE_DIM:].reshape(MLA_KV_LORA, MLA_HEADS * MLA_V_DIM).astype(BF16)
    return w_p, uq, ukn, uv


def _proj_kernel(h_ref, g_ref, tab_ref, w_ref, qn_ref, uq_ref, kvn_ref, ukn_ref, uv_ref, *out_refs):
    outs = {name: ref for (name, _, _), ref in zip(PROJ_OUTS, out_refs)}
    xn = _rms(h_ref[...], g_ref[...]).astype(BF16)

    def roped(y, pattern):
        return _rope128(y, tab_ref[2 * pattern], tab_ref[2 * pattern + 1], pattern)

    def emit(name, s, y):
        ref = outs[name]
        if name in PROJ_TRANSPOSED:
            tk = ref.shape[2]
            for t in range(ref.shape[0]):
                ref[t, s * LANES:(s + 1) * LANES, :] = y[t * tk:(t + 1) * tk].T.astype(ref.dtype)
        else:
            ref[:, s * LANES:(s + 1) * LANES] = y.astype(ref.dtype)

    def wide_dot(x, w, n_slices):
        per = MXU_COLS // LANES
        res = []
        for c in range(0, n_slices, per):
            y = _dot(x, w[:, c * LANES:(c + per) * LANES])
            res += [y[:, k * LANES:(k + 1) * LANES] for k in range(min(per, n_slices - c))]
        return res

    slices = wide_dot(xn, w_ref, PROJ_WIDTH // LANES)
    vals = {}
    i = 0
    for name, width, pattern in PROJ_COLS:
        for s in range(width // LANES):
            y = slices[i] if pattern is None else roped(slices[i], pattern)
            i += 1
            if name in outs:
                emit(name, s, y)
            else:
                vals.setdefault(name, []).append(y)

    cq = jnp.concatenate(vals["cq"], axis=1)
    cqn = _rms(cq, qn_ref[...]).astype(BF16)
    for hd, y in enumerate(wide_dot(cqn, uq_ref, MLA_HEADS)):
        outs["qc"][:, hd * LANES:(hd + 1) * LANES] = roped(y, 2).astype(BF16)

    ckvn = _rms(vals["ckv"][0], kvn_ref[...]).astype(BF16)
    kr = vals["kr"][0]
    for hd, y in enumerate(wide_dot(ckvn, ukn_ref, MLA_HEADS)):
        outs["kc"][:, hd * LANES:(hd + 1) * LANES] = (y + kr).astype(BF16)
    for s, y in enumerate(wide_dot(ckvn, uv_ref, MLA_HEADS * MLA_V_DIM // LANES)):
        emit("vc", s, y)


def _project(h, gain, tables, w_p, q_norm, uq, kv_norm, ukn, uv, tm, tk):
    n, d = h.shape
    full = lambda a: pl.BlockSpec(a.shape, lambda i: (0,) * a.ndim)
    gain = gain.reshape(1, d)
    q_norm = q_norm.reshape(1, -1)
    kv_norm = kv_norm.reshape(1, -1)
    out_specs, out_shape = [], []
    for name, w, dt in PROJ_OUTS:
        if name in PROJ_TRANSPOSED:
            out_specs.append(pl.BlockSpec((tm // tk, w, tk), lambda i: (i, 0, 0)))
            out_shape.append(jax.ShapeDtypeStruct((n // tk, w, tk), dt))
        else:
            out_specs.append(pl.BlockSpec((tm, w), lambda i: (i, 0)))
            out_shape.append(jax.ShapeDtypeStruct((n, w), dt))
    return pl.pallas_call(
        _proj_kernel,
        grid=(n // tm,),
        in_specs=[pl.BlockSpec((tm, d), lambda i: (i, 0)), full(gain),
                  pl.BlockSpec((6, tm, LANES), lambda i: (0, i, 0)),
                  full(w_p), full(q_norm), full(uq), full(kv_norm), full(ukn), full(uv)],
        out_specs=out_specs,
        out_shape=out_shape,
        compiler_params=_cparams(("parallel",)),
        name="projection",
    )(h, gain, tables, w_p, q_norm, uq, kv_norm, ukn, uv)


def _half_masks(dtype, group):
    lane = lax.broadcasted_iota(jnp.int32, (1, LANES), 1)
    return [jnp.where((lane // group) == u, 1.0, 0.0).astype(dtype) for u in range(LANES // group)]


LOG2E = math.log2(math.e)


def _softmax_step(st, v_t, m, l, acc_ref, scale, guard_empty=False):
    c = scale * LOG2E
    m_new = jnp.maximum(m, jnp.max(st, axis=0, keepdims=True))
    m_use = jnp.where(m_new == -jnp.inf, 0.0, m_new) if guard_empty else m_new
    alpha = jnp.exp2((m - m_use) * c)
    p = jnp.exp2((st - m_use) * c)
    l_new = alpha * l + jnp.sum(p, axis=0, keepdims=True)
    acc_ref[...] = alpha * acc_ref[...] + _dot(v_t, p.astype(BF16))
    return m_new, l_new


def _diag_mask(tk, m_cols, tq):
    key_chunk = lax.broadcasted_iota(jnp.int32, (tk, 1), 0) // CHUNK
    qry_chunk = (lax.broadcasted_iota(jnp.int32, (1, m_cols), 1) % tq) // CHUNK
    return key_chunk <= qry_chunk


def _causal_flash(problems, k_ref, v_ref, acc_ref, qi, tq, scale):
    problems = [(_transpose_bf16(q_st), k_lane0, v_row0) for q_st, k_lane0, v_row0 in problems]
    m_cols = problems[0][0].shape[1]
    n_p = len(problems)
    acc_ref[...] = jnp.zeros_like(acc_ref)
    init = tuple(jnp.full((1, m_cols), -jnp.inf, F32) for _ in range(n_p)) + \
        tuple(jnp.zeros((1, m_cols), F32) for _ in range(n_p))

    sub = min(tq, SUB_KEYS)

    def step(j, carry, mask):
        ms, ls = list(carry[:n_p]), list(carry[n_p:])
        for s0 in range(0, tq, sub):
            r0 = pl.multiple_of(j * tq + s0, sub)
            for p, (q_t, k_lane0, v_row0) in enumerate(problems):
                st = _dot(k_ref[pl.ds(r0, sub), k_lane0:k_lane0 + LANES], q_t)
                if mask is not None:
                    st = jnp.where(mask[s0:s0 + sub], st, -jnp.inf)
                ms[p], ls[p] = _softmax_step(st, v_ref[j, v_row0:v_row0 + LANES, s0:s0 + sub], ms[p], ls[p],
                                             acc_ref.at[p], scale)
        return tuple(ms) + tuple(ls)

    carry = lax.fori_loop(0, qi, lambda j, c: step(j, c, None), init)
    carry = step(qi, carry, _diag_mask(tq, m_cols, tq))
    return [acc_ref[p] / carry[n_p + p] for p in range(n_p)]


def _diff_kernel(lam_ref, subln_ref, gsum_ref, q_ref, k_ref, v_ref, o_ref, acc_ref, *, tq, lam_init):
    qi = pl.program_id(1)
    lam_rows = lam_ref[...]
    s1 = jnp.sum(lam_rows[0:1] * lam_rows[1:2], axis=1, keepdims=True)
    s2 = jnp.sum(lam_rows[2:3] * lam_rows[3:4], axis=1, keepdims=True)
    lam = jnp.exp(s1) - jnp.exp(s2) + lam_init
    masks = _half_masks(BF16, DIFF_QK_DIM)
    row = lax.broadcasted_iota(jnp.int32, (LANES, 1), 0)
    n_slices = DIFF_HEADS // 2
    problems = []
    for s in range(n_slices):
        q = q_ref[0, :, s * LANES:(s + 1) * LANES]
        problems += [(q * mk, s * LANES, s * LANES) for mk in masks]
    outs = _causal_flash(problems, k_ref.at[0], v_ref, acc_ref, qi, tq, DIFF_QK_DIM ** -0.5)
    for s in range(n_slices):
        o0, o1, o2, o3 = outs[4 * s:4 * s + 4]
        a0 = o0 - lam * o1
        a1 = o2 - lam * o3
        a = jnp.where(row < DIFF_V_DIM, a0, a1).T
        ss = _dot_f32_by_exact(a * a, gsum_ref[...])
        y = a * lax.rsqrt(ss * (1.0 / DIFF_V_DIM) + NORM_EPS) * subln_ref[...]
        o_ref[0, :, s * LANES:(s + 1) * LANES] = (y * (1.0 - lam_init)).astype(o_ref.dtype)


def _diff_attention(qb, kb, vb_t, lam_rows, subln, layer, tq):
    b, s, w = qb.shape
    lam_init = 0.8 - 0.6 * math.exp(-0.3 * layer)
    subln2 = jnp.concatenate([subln, subln]).reshape(1, LANES).astype(F32)
    lane = np.arange(LANES)
    gsum = jnp.asarray((lane[:, None] // DIFF_V_DIM) == (lane[None, :] // DIFF_V_DIM), dtype=BF16)
    return pl.pallas_call(
        functools.partial(_diff_kernel, tq=tq, lam_init=lam_init),
        grid=(b, s // tq),
        in_specs=[pl.BlockSpec((8, LANES), lambda i, j: (0, 0)),
                  pl.BlockSpec((1, LANES), lambda i, j: (0, 0)),
                  pl.BlockSpec((LANES, LANES), lambda i, j: (0, 0)),
                  pl.BlockSpec((1, tq, w), lambda i, j: (i, j, 0)),
                  pl.BlockSpec((1, s, w), lambda i, j: (i, 0, 0)),
                  pl.BlockSpec((s // tq, w, tq), lambda i, j: (i, 0, 0))],
        out_specs=pl.BlockSpec((1, tq, w), lambda i, j: (i, j, 0)),
        out_shape=jax.ShapeDtypeStruct((b, s, w), BF16),
        scratch_shapes=[pltpu.VMEM((2 * DIFF_HEADS, LANES, tq), F32)],
        compiler_params=_cparams(("parallel", "arbitrary")),
        name="diff_attention",
    )(lam_rows, subln2, gsum, qb, kb, vb_t)


def _mla_kernel(q_ref, k_ref, v_ref, o_ref, acc_ref, *, tq):
    qi = pl.program_id(1)
    row = lax.broadcasted_iota(jnp.int32, (LANES, 1), 0)
    scale = (MLA_NOPE_DIM + MLA_ROPE_DIM) ** -0.5
    problems = [(q_ref[0, :, hd * LANES:(hd + 1) * LANES], hd * LANES, (hd // 2) * LANES)
                for hd in range(MLA_HEADS)]
    outs = _causal_flash(problems, k_ref.at[0], v_ref, acc_ref, qi, tq, scale)
    for pair in range(MLA_HEADS // 2):
        o_t = jnp.where(row < MLA_V_DIM, outs[2 * pair], outs[2 * pair + 1])
        o_ref[0, :, pair * LANES:(pair + 1) * LANES] = o_t.T.astype(o_ref.dtype)


def _mla_attention(qc, kc, vc_t, tq):
    b, s, wq = qc.shape
    wv = vc_t.shape[1]
    return pl.pallas_call(
        functools.partial(_mla_kernel, tq=tq),
        grid=(b, s // tq),
        in_specs=[pl.BlockSpec((1, tq, wq), lambda i, j: (i, j, 0)),
                  pl.BlockSpec((1, s, wq), lambda i, j: (i, 0, 0)),
                  pl.BlockSpec((s // tq, wv, tq), lambda i, j: (i, 0, 0))],
        out_specs=pl.BlockSpec((1, tq, wv), lambda i, j: (i, j, 0)),
        out_shape=jax.ShapeDtypeStruct((b, s, wv), BF16),
        scratch_shapes=[pltpu.VMEM((MLA_HEADS, LANES, tq), F32)],
        compiler_params=_cparams(("parallel", "arbitrary")),
        name="mla_attention",
    )(qc, kc, vc_t)


KEY_NEG_INF = -2139095041


def _score_keys(score):
    bits = lax.bitcast_convert_type(score, jnp.int32)
    return bits ^ ((bits >> 31) & 0x7FFFFFFF)


def _dsa_kernel(qa_ref, qi_ref, wi_ref, ka_ref, va_ref, ki_ref, o_ref, key_ref, aux_ref, bias_ref, acc_ref,
                *, tq, tk, top_k):
    blk = pl.program_id(1)
    s_len = ka_ref.shape[1]
    n_kv = ((blk + 1) * tq + tk - 1) // tk
    masks = _half_masks(BF16, DSA_IDX_DIM)
    row = lax.broadcasted_iota(jnp.int32, (LANES, 1), 0)
    t_chunk = (blk * tq + lax.broadcasted_iota(jnp.int32, (1, tq), 1)) // CHUNK
    sub_idx = lax.broadcasted_iota(jnp.int32, (tk, 1), 0)

    def tile_start(c):
        return pl.multiple_of(c * tk, tk)

    qi_st = jnp.concatenate([qi_ref[0, :, (hd // 2) * LANES:(hd // 2 + 1) * LANES] * masks[hd % 2]
                             for hd in range(DSA_IDX_HEADS)], axis=0)
    qi_t = _transpose_bf16(qi_st)
    w_t = wi_ref[0].T

    sub = min(tk, SUB_KEYS)
    sub_iota = lax.broadcasted_iota(jnp.int32, (sub, 1), 0)

    def score_body(c, carry):
        for s0 in range(0, tk, sub):
            r0 = pl.multiple_of(c * tk + s0, sub)
            ki = ki_ref[0, pl.ds(r0, sub), :]
            score = jnp.zeros((sub, tq), F32)
            for hd in range(DSA_IDX_HEADS):
                rel = jnp.maximum(_dot(ki, qi_t[:, hd * tq:(hd + 1) * tq]), 0.0)
                score = score + w_t[hd:hd + 1, :] * rel
            score = score * DSA_IDX_SCALE
            score = jnp.where((r0 + sub_iota) // CHUNK <= t_chunk, score, -jnp.inf)
            key_ref[pl.ds(r0, sub), :] = _score_keys(score)
        return carry

    lax.fori_loop(0, n_kv, score_body, 0)

    def count(pred_fn):
        n_part = 8

        def body(c, tot):
            ones = jnp.where(pred_fn(tile_start(c)), 1.0, 0.0)
            return tot + jnp.sum(ones.reshape(tk // (8 * n_part), n_part * 8, tq), axis=0)
        tot = lax.fori_loop(0, n_kv, body, jnp.zeros((n_part * 8, tq), F32))
        return jnp.sum(tot, axis=0, keepdims=True)

    cnt_all = count(lambda r0: key_ref[pl.ds(r0, tk), :] >= INT_MIN)
    cnt_pos = count(lambda r0: key_ref[pl.ds(r0, tk), :] >= 0)
    thr = jnp.where(cnt_pos >= top_k, 0, INT_MIN).astype(jnp.int32)
    cnt_thr = jnp.where(cnt_pos >= top_k, cnt_pos, cnt_all)

    def thr_body(i, carry):
        thr, cnt_thr = carry
        cand = thr | (jnp.int32(1) << (30 - i))
        cnt = count(lambda r0: key_ref[pl.ds(r0, tk), :] >= cand)
        ok = cnt >= top_k
        return jnp.where(ok, cand, thr), jnp.where(ok, cnt, cnt_thr)

    thr, cnt_thr = lax.fori_loop(0, 31, thr_body, (thr, cnt_thr))
    thr = jnp.maximum(thr, KEY_NEG_INF)
    tied = jnp.where(cnt_thr > top_k, jnp.where(thr > KEY_NEG_INF, 1.0, 0.0), 0.0)

    def select_with_ties():
        n_gt = count(lambda r0: key_ref[pl.ds(r0, tk), :] > thr)
        need = top_k - n_gt

        def aux_body(c, carry):
            r0 = tile_start(c)
            idx = r0 + sub_idx
            tie = jnp.where(key_ref[pl.ds(r0, tk), :] == thr, jnp.where(idx // CHUNK <= t_chunk, idx, IDX_BIG), IDX_BIG)
            aux_ref[pl.ds(r0, tk), :] = tie
            return carry

        lax.fori_loop(0, n_kv, aux_body, 0)
        n_bits = max(1, (s_len - 1).bit_length())

        def tie_body(i, last):
            cand = last | (jnp.int32(1) << (n_bits - 1 - i))
            cnt = count(lambda r0: aux_ref[pl.ds(r0, tk), :] < cand)
            return jnp.where(cnt < need, cand, last)

        last = lax.fori_loop(0, n_bits, tie_body, jnp.zeros((1, tq), jnp.int32))

        def bias_body(c, carry):
            r0 = tile_start(c)
            bias_ref[pl.ds(r0, tk), :] = jnp.where(
                key_ref[pl.ds(r0, tk), :] > thr, 0.0, jnp.where(aux_ref[pl.ds(r0, tk), :] <= last, 0.0, -jnp.inf))
            return carry

        lax.fori_loop(0, n_kv, bias_body, 0)

    def select_no_ties():
        low = jnp.where(thr > KEY_NEG_INF, thr - 1, thr)

        def bias_body(c, carry):
            r0 = tile_start(c)
            bias_ref[pl.ds(r0, tk), :] = jnp.where(key_ref[pl.ds(r0, tk), :] > low, 0.0, -jnp.inf)
            return carry

        lax.fori_loop(0, n_kv, bias_body, 0)

    lax.cond(jnp.max(tied) > 0.0, select_with_ties, select_no_ties)

    hmask = _half_masks(BF16, DSA_HEAD_DIM)
    qa_st = jnp.concatenate([qa_ref[0, :, (hd // 2) * LANES:(hd // 2 + 1) * LANES] * hmask[hd % 2]
                             for hd in range(DSA_HEADS)], axis=0)
    qa_t = _transpose_bf16(qa_st)
    scale = DSA_HEAD_DIM ** -0.5
    n_h = DSA_HEADS
    acc_ref[...] = jnp.zeros_like(acc_ref)

    def att_body(c, carry):
        ms, ls = list(carry[:n_h]), list(carry[n_h:])
        for s0 in range(0, tk, sub):
            r0 = pl.multiple_of(c * tk + s0, sub)
            bias = bias_ref[pl.ds(r0, sub), :]
            ka = ka_ref[0, pl.ds(r0, sub), :]
            for hd in range(n_h):
                st = _dot(ka, qa_t[:, hd * tq:(hd + 1) * tq]) + bias
                ms[hd], ls[hd] = _softmax_step(st, va_ref[c, :, s0:s0 + sub], ms[hd], ls[hd], acc_ref.at[hd],
                                               scale, guard_empty=True)
        return tuple(ms) + tuple(ls)

    init = tuple(jnp.full((1, tq), -jnp.inf, F32) for _ in range(n_h)) + \
        tuple(jnp.zeros((1, tq), F32) for _ in range(n_h))
    carry = lax.fori_loop(0, n_kv, att_body, init)
    for p in range(n_h // 2):
        pair_t = jnp.where(row < DSA_HEAD_DIM, acc_ref[2 * p] / carry[n_h + 2 * p],
                           acc_ref[2 * p + 1] / carry[n_h + 2 * p + 1])
        o_ref[0, :, p * LANES:(p + 1) * LANES] = pair_t.T.astype(o_ref.dtype)


def _dsa_attention(qa, qi, wi, ka2, va2_t, ki2, tq, tk):
    b, s, _ = qa.shape
    top_k = min(DSA_TOPK_MAX, s // 4)
    qspec = lambda w: pl.BlockSpec((1, tq, w), lambda i, j: (i, j, 0))
    kspec = lambda w: pl.BlockSpec((1, s, w), lambda i, j: (i, 0, 0))
    return pl.pallas_call(
        functools.partial(_dsa_kernel, tq=tq, tk=tk, top_k=top_k),
        grid=(b, s // tq),
        in_specs=[qspec(qa.shape[2]), qspec(qi.shape[2]), qspec(LANES), kspec(LANES),
                  pl.BlockSpec((s // tk, LANES, tk), lambda i, j: (i, 0, 0)), kspec(LANES)],
        out_specs=qspec(qa.shape[2]),
        out_shape=jax.ShapeDtypeStruct(qa.shape, BF16),
        scratch_shapes=[pltpu.VMEM((s, tq), jnp.int32), pltpu.VMEM((s, tq), jnp.int32), pltpu.VMEM((s, tq), F32),
                        pltpu.VMEM((DSA_HEADS, LANES, tq), F32)],
        compiler_params=_cparams(("parallel", "arbitrary")),
        name="dsa_attention",
    )(qa, qi, wi, ka2, va2_t, ki2)


def _attn_residual(h_ref, oa_ref, ob_ref, oc_ref, wo_ref):
    wa, wb = oa_ref.shape[1], ob_ref.shape[1]
    return (h_ref[...] + _dot(oa_ref[...], wo_ref[0:wa, :]) + _dot(ob_ref[...], wo_ref[wa:wa + wb, :])
            + _dot(oc_ref[...], wo_ref[wa + wb:, :]))


def _swiglu_partial(x, wg, wu, wd):
    g = _dot(x, wg)
    u = _dot(x, wu)
    return _dot(((g * jax.nn.sigmoid(g)) * u).astype(BF16), wd)


def _dense_block_kernel(h_ref, oa_ref, ob_ref, oc_ref, wo_ref, g_ref, wg_ref, wu_ref, wd_ref, fg_ref,
                        out_ref, xn_ref, acc_ref, *, final_norm):
    f = pl.program_id(1)

    @pl.when(f == 0)
    def _():
        h1 = _attn_residual(h_ref, oa_ref, ob_ref, oc_ref, wo_ref)
        acc_ref[...] = h1
        xn_ref[...] = _rms(h1, g_ref[...]).astype(BF16)

    acc_ref[...] += _swiglu_partial(xn_ref[...], wg_ref[...], wu_ref[...], wd_ref[...])

    @pl.when(f == pl.num_programs(1) - 1)
    def _():
        y = acc_ref[...]
        out_ref[...] = _rms(y, fg_ref[...]) if final_norm else y


def _dense_block(h, oa, ob, oc, w_out, gain, wg, wu, wd, final_gain, *, final_norm, tm, tf):
    n, d = h.shape
    dff = wg.shape[1]
    const = lambda a: pl.BlockSpec(a.shape, lambda i, f: (0,) * a.ndim, pipeline_mode=pl.Buffered(1))
    row = lambda w: pl.BlockSpec((tm, w), lambda i, f: (i, 0))
    row_in = lambda w: pl.BlockSpec((tm, w), lambda i, f: (i, 0), pipeline_mode=pl.Buffered(1))
    gain = gain.reshape(1, d)
    final_gain = final_gain.reshape(1, d)
    return pl.pallas_call(
        functools.partial(_dense_block_kernel, final_norm=final_norm),
        grid=(n // tm, dff // tf),
        in_specs=[row(d), row_in(oa.shape[1]), row_in(ob.shape[1]), row_in(oc.shape[1]), const(w_out), const(gain),
                  pl.BlockSpec((d, tf), lambda i, f: (0, f)),
                  pl.BlockSpec((d, tf), lambda i, f: (0, f)),
                  pl.BlockSpec((tf, d), lambda i, f: (f, 0)),
                  const(final_gain)],
        out_specs=row(d),
        out_shape=jax.ShapeDtypeStruct((n, d), F32),
        scratch_shapes=[pltpu.VMEM((tm, d), BF16), pltpu.VMEM((tm, d), F32)],
        compiler_params=_cparams(("parallel", "arbitrary")),
        name="block_dense",
    )(h, oa, ob, oc, w_out, gain, wg, wu, wd, final_gain)


def _moe_block_kernel(h_ref, oa_ref, ob_ref, oc_ref, wo_ref, g_ref, r_ref, tri_ref, wg_ref, wu_ref, wd_ref, fg_ref,
                      out_ref, xn_ref, acc_ref, gate_ref, pos_ref, pos_t_ref, cnt_ref, xs_ref, ye_ref,
                      *, n_exp, mains, rows, final_norm):
    e = pl.program_id(1)
    f = pl.program_id(2)
    n_f = pl.num_programs(2)
    tm = xn_ref.shape[0]
    lane = lax.broadcasted_iota(jnp.int32, (1, LANES), 1)

    @pl.when((e == 0) & (f == 0))
    def _():
        h1 = _attn_residual(h_ref, oa_ref, ob_ref, oc_ref, wo_ref)
        acc_ref[...] = h1
        hn = _rms(h1, g_ref[...])
        xn_ref[...] = hn.astype(BF16)
        logits = jnp.dot(hn, r_ref[...], preferred_element_type=F32, precision=lax.Precision.HIGHEST)
        logits = jnp.where(lane < n_exp, logits, -jnp.inf)
        m1 = jnp.max(logits, axis=1, keepdims=True)
        i1 = jnp.min(jnp.where(logits == m1, lane, IDX_BIG), axis=1, keepdims=True)
        rest = jnp.where(lane == i1, -jnp.inf, logits)
        m2 = jnp.max(rest, axis=1, keepdims=True)
        i2 = jnp.min(jnp.where(rest == m2, lane, IDX_BIG), axis=1, keepdims=True)
        e2 = jnp.exp(m2 - m1)
        den = 1.0 + e2
        gate_ref[...] = jnp.where(lane == i1, 1.0 / den, jnp.where(lane == i2, e2 / den, 0.0))
        routed = jnp.where(lane == i1, 1.0, jnp.where(lane == i2, 1.0, 0.0))
        before = _dot(tri_ref[...], routed.astype(BF16))
        slot = jnp.where(routed > 0.0, before, -1.0)
        pos_ref[...] = slot
        pos_t_ref[...] = slot.T
        cnt_ref[...] = jnp.sum(routed, axis=0, keepdims=True)

    n_tok = jnp.sum(jnp.where(lane == e, cnt_ref[...], 0.0)).astype(jnp.int32)
    top = mains[-1]
    n_extra = (jnp.maximum(n_tok - top, 0) + rows - 1) // rows

    def for_blocks(fn):
        below = 0
        for size in mains:
            fits = (n_tok > below) if size == top else ((n_tok > below) & (n_tok <= size))

            @pl.when(fits)
            def _(size=size):
                fn(0, size)
                if size == top:
                    lax.fori_loop(0, n_extra, lambda r, c: (
                        fn(pl.multiple_of(top + r * rows, math.gcd(top, rows)), rows), c)[1], 0)

            below = size

    @pl.when(f == 0)
    def _():
        slot_row = pos_t_ref[pl.ds(e, 1), :]

        def gather(r0, nr):
            want = (r0 + lax.broadcasted_iota(jnp.int32, (nr, 1), 0)).astype(F32)
            pick = jnp.where(slot_row == want, 1.0, 0.0).astype(BF16)
            xs_ref[pl.ds(r0, nr), :] = _dot(pick, xn_ref[...]).astype(BF16)
            ye_ref[pl.ds(r0, nr), :] = jnp.zeros((nr, ye_ref.shape[1]), F32)

        for_blocks(gather)

    def expert(r0, nr):
        ye_ref[pl.ds(r0, nr), :] += _swiglu_partial(xs_ref[pl.ds(r0, nr), :], wg_ref[0], wu_ref[0], wd_ref[0])

    for_blocks(expert)

    @pl.when(f == n_f - 1)
    def _():
        full_lane = lax.broadcasted_iota(jnp.int32, (tm, LANES), 1)
        slot_col = jnp.sum(jnp.where(full_lane == e, pos_ref[...], 0.0), axis=1, keepdims=True)
        gate_col = jnp.sum(jnp.where(full_lane == e, gate_ref[...], 0.0), axis=1, keepdims=True)

        def scatter(r0, nr):
            have = (r0 + lax.broadcasted_iota(jnp.int32, (1, nr), 1)).astype(F32)
            place = jnp.where(slot_col == have, 1.0, 0.0).astype(BF16)
            y = ye_ref[pl.ds(r0, nr), :]
            y_hi = y.astype(BF16)
            y_lo = (y - y_hi.astype(F32)).astype(BF16)
            acc_ref[...] += gate_col * (_dot(place, y_hi) + _dot(place, y_lo))

        for_blocks(scatter)

    @pl.when((e == n_exp - 1) & (f == n_f - 1))
    def _():
        y = acc_ref[...]
        out_ref[...] = _rms(y, fg_ref[...]) if final_norm else y


def _moe_block(h, oa, ob, oc, w_out, gain, router_p, wg, wu, wd, final_gain, *, final_norm, tm, tf, mains, rows):
    n, d = h.shape
    n_exp, _, dff = wg.shape
    const = lambda a: pl.BlockSpec(a.shape, lambda i, e, f: (0,) * a.ndim, pipeline_mode=pl.Buffered(1))
    row = lambda w: pl.BlockSpec((tm, w), lambda i, e, f: (i, 0))
    row_in = lambda w: pl.BlockSpec((tm, w), lambda i, e, f: (i, 0), pipeline_mode=pl.Buffered(1))
    gain = gain.reshape(1, d)
    final_gain = final_gain.reshape(1, d)
    tok = np.arange(tm)
    tri = jnp.asarray(tok[None, :] < tok[:, None], dtype=BF16)
    mains = tuple(sorted({min(m, tm) for m in mains}))
    cap = mains[-1] + -(-(tm - mains[-1]) // rows) * rows
    return pl.pallas_call(
        functools.partial(_moe_block_kernel, n_exp=n_exp, mains=mains, rows=rows, final_norm=final_norm),
        grid=(n // tm, n_exp, dff // tf),
        in_specs=[row_in(d), row_in(oa.shape[1]), row_in(ob.shape[1]), row_in(oc.shape[1]), const(w_out), const(gain),
                  const(router_p), const(tri),
                  pl.BlockSpec((1, d, tf), lambda i, e, f: (e, 0, f)),
                  pl.BlockSpec((1, d, tf), lambda i, e, f: (e, 0, f)),
                  pl.BlockSpec((1, tf, d), lambda i, e, f: (e, f, 0)),
                  const(final_gain)],
        out_specs=row(d),
        out_shape=jax.ShapeDtypeStruct((n, d), F32),
        scratch_shapes=[pltpu.VMEM((tm, d), BF16), pltpu.VMEM((tm, d), F32), pltpu.VMEM((tm, LANES), F32),
                        pltpu.VMEM((tm, LANES), F32), pltpu.VMEM((LANES, tm), F32), pltpu.VMEM((1, LANES), F32),
                        pltpu.VMEM((cap, d), BF16), pltpu.VMEM((cap, d), F32)],
        compiler_params=_cparams(("parallel", "arbitrary", "arbitrary")),
        name="block_moe",
    )(h, oa, ob, oc, w_out, gain, router_p, tri, wg, wu, wd, final_gain)


def _pick(n, pref):
    t = min(pref, n)
    while n % t:
        t //= 2
    return t


def kernel(x, positions, attn_norm, w_in, mla_q_norm, w_uq, mla_kv_norm, w_ukv, diff_lambda_q1, diff_lambda_k1, diff_lambda_q2, diff_lambda_k2, diff_subln, w_out, ffn_norm, dense_w_gate, dense_w_up, dense_w_down, moe_router, moe_w_gate, moe_w_up, moe_w_down, final_norm):
    b, s, d = x.shape
    n = b * s
    depth = w_in.shape[0]
    tm_proj = _pick(n, 512)
    tm_blk = _pick(n, 1024)
    tk = _pick(s, KV_TILE)
    tq_dsa = _pick(s, 256)

    tables = _rope_tables(positions.astype(F32).reshape(n, 1), _pick(n, 1024))
    h = x.reshape(n, d)
    r3 = lambda a: a.reshape(b, s, a.shape[-1])
    for layer in range(depth):
        w_p, uq, ukn, uv = _prep_proj_weights(w_in[layer], w_uq[layer], w_ukv[layer])
        (qa, ka2, qi, ki2, qb, kb, va2, wi, vb, qc, kc, vc) = _project(
            h, attn_norm[layer], tables, w_p, mla_q_norm[layer], uq, mla_kv_norm[layer], ukn, uv, tm_proj, tk)
        oa = _dsa_attention(r3(qa), r3(qi), r3(wi), r3(ka2), va2, r3(ki2), tq_dsa, tk)
        lam_rows = jnp.zeros((8, LANES), F32).at[0:4, 0:DIFF_QK_DIM].set(jnp.stack(
            [diff_lambda_q1[layer], diff_lambda_k1[layer], diff_lambda_q2[layer], diff_lambda_k2[layer]]))
        ob = _diff_attention(r3(qb), r3(kb), vb, lam_rows, diff_subln[layer], layer, tk)
        oc = _mla_attention(r3(qc), r3(kc), vc, tk)
        j = layer // 2
        last = layer == depth - 1
        wo = w_out[layer].astype(BF16)
        attn = (oa.reshape(n, -1), ob.reshape(n, -1), oc.reshape(n, -1))
        if layer % 2 == 0:
            h = _dense_block(h, *attn, wo, ffn_norm[layer], dense_w_gate[j].astype(BF16), dense_w_up[j].astype(BF16),
                             dense_w_down[j].astype(BF16), final_norm,
                             final_norm=last, tm=_pick(n, 512), tf=_pick(dense_w_gate.shape[2], DENSE_FFN_TILE))
        else:
            router_p = jnp.pad(moe_router[j], ((0, 0), (0, LANES - MOE_EXPERTS)))
            h = _moe_block(h, *attn, wo, ffn_norm[layer], router_p,
                           moe_w_gate[j].astype(BF16), moe_w_up[j].astype(BF16), moe_w_down[j].astype(BF16), final_norm,
                           final_norm=last, tm=tm_blk, tf=_pick(moe_w_gate.shape[3], FFN_TILE),
                           mains=MOE_MAIN_ROWS, rows=MOE_EXTRA_ROWS)
    return h.reshape(b, s, d)
```

```python
import functools
import math

import jax
import jax.numpy as jnp
import numpy as np
from jax import lax
from jax.experimental import pallas as pl
from jax.experimental.pallas import tpu as pltpu

F32 = jnp.float32
BF16 = jnp.bfloat16

LANES = 128
MXU_COLS = 256
VMEM_LIMIT_BYTES = 56 * 1024 * 1024

D_MODEL = 1024
CHUNK = 64
ROPE_THETA = 500000.0
NORM_EPS = 1e-6
ROPE_FRACTION_DEN = 4

DSA_HEADS = 4
DSA_HEAD_DIM = 64
DSA_IDX_HEADS = 8
DSA_IDX_DIM = 64
DSA_TOPK_MAX = 256
DSA_IDX_SCALE = (DSA_IDX_HEADS * DSA_IDX_DIM) ** -0.5

DIFF_HEADS = 4
DIFF_QK_DIM = 32
DIFF_V_DIM = 2 * DIFF_QK_DIM

MLA_HEADS = 8
MLA_Q_LORA = 256
MLA_KV_LORA = 128
MLA_NOPE_DIM = 64
MLA_ROPE_DIM = 32
MLA_V_DIM = 64

MOE_EXPERTS = 8
MOE_TOP_K = 2
FFN_TILE = 896
DENSE_FFN_TILE = 512
DENSE_ROW_CHUNK = 256
MOE_MAIN_ROWS = (256, 288, 320)
MOE_EXTRA_ROWS = 128

IN_SPLITS = (
    DSA_HEADS * DSA_HEAD_DIM, DSA_HEAD_DIM, DSA_HEAD_DIM, DSA_IDX_HEADS * DSA_IDX_DIM, DSA_IDX_DIM,
    DSA_IDX_HEADS, DIFF_HEADS * 2 * DIFF_QK_DIM, DIFF_HEADS * 2 * DIFF_QK_DIM, DIFF_HEADS * DIFF_V_DIM,
    MLA_Q_LORA, MLA_KV_LORA, MLA_ROPE_DIM,
)

INT_MIN = -(2 ** 31)
IDX_BIG = 2 ** 30


def _cparams(sem):
    return pltpu.CompilerParams(dimension_semantics=sem, vmem_limit_bytes=VMEM_LIMIT_BYTES)


def _rms(x, g):
    return x * lax.rsqrt(jnp.mean(x * x, axis=-1, keepdims=True) + NORM_EPS) * g


def _dot(a, b):
    return jnp.dot(a, b, preferred_element_type=F32)


def _transpose_bf16(x):
    return x.astype(F32).T.astype(BF16)


def _split3(x):
    hi = x.astype(BF16)
    r1 = x - hi.astype(F32)
    mid = r1.astype(BF16)
    lo = (r1 - mid.astype(F32)).astype(BF16)
    return hi, mid, lo


def _dot_f32_by_exact(x, m_bf16):
    hi, mid, lo = _split3(x)
    return _dot(hi, m_bf16) + _dot(mid, m_bf16) + _dot(lo, m_bf16)


def _inv_freq(rot_dim):
    half = rot_dim // 2
    return ROPE_THETA ** (-(jnp.arange(half, dtype=F32) * 2.0 / rot_dim))


def _rope_patterns():
    lane = np.arange(LANES)
    rots = (DSA_HEAD_DIM // ROPE_FRACTION_DEN, DIFF_QK_DIM // ROPE_FRACTION_DEN, MLA_ROPE_DIM)
    offs = (lane % DSA_HEAD_DIM, lane % DIFF_QK_DIM, lane - MLA_NOPE_DIM)
    zero_lane = LANES - 1
    freq = jnp.zeros((LANES,), F32)
    signs, expand = [], np.zeros((3, LANES, LANES), np.float32)
    base = 0
    for p, (rot, off) in enumerate(zip(rots, offs)):
        half = rot // 2
        active = (off >= 0) & (off < rot)
        freq = freq.at[base:base + half].set(_inv_freq(rot))
        src = np.where(active, base + np.clip(off, 0, rot - 1) % half, zero_lane)
        expand[p, src, lane] = 1.0
        signs.append(np.where(active, np.where(off < half, -1.0, 1.0), 0.0))
        base += half
    assert base < zero_lane
    rows = jnp.concatenate([freq[None, :], jnp.asarray(np.stack(signs), dtype=F32), jnp.zeros((4, LANES), F32)], axis=0)
    return rows, jnp.asarray(expand, dtype=BF16)


ROPE_HALF = (DSA_HEAD_DIM // ROPE_FRACTION_DEN // 2, DIFF_QK_DIM // ROPE_FRACTION_DEN // 2, MLA_ROPE_DIM // 2)


def _x1_mask(pattern):
    lane = lax.broadcasted_iota(jnp.int32, (1, LANES), 1)
    if pattern == 0:
        return (lane % DSA_HEAD_DIM) < ROPE_HALF[0]
    if pattern == 1:
        return (lane % DIFF_QK_DIM) < ROPE_HALF[1]
    return (lane >= MLA_NOPE_DIM) & (lane < MLA_NOPE_DIM + ROPE_HALF[2])


def _rope_tables_kernel(pos_ref, rows_ref, expand_ref, out_ref):
    ang = pos_ref[...] * rows_ref[0:1, :]
    cos, sin = jnp.cos(ang), jnp.sin(ang)
    for p in range(3):
        out_ref[2 * p] = _dot_f32_by_exact(cos, expand_ref[p])
        out_ref[2 * p + 1] = _dot_f32_by_exact(sin, expand_ref[p]) * rows_ref[1 + p:2 + p, :]


def _rope_tables(pos_f, tm):
    n = pos_f.shape[0]
    rows, expand = _rope_patterns()
    return pl.pallas_call(
        _rope_tables_kernel,
        grid=(n // tm,),
        in_specs=[pl.BlockSpec((tm, 1), lambda i: (i, 0)),
                  pl.BlockSpec((8, LANES), lambda i: (0, 0)),
                  pl.BlockSpec((3, LANES, LANES), lambda i: (0, 0, 0))],
        out_specs=pl.BlockSpec((6, tm, LANES), lambda i: (0, i, 0)),
        out_shape=jax.ShapeDtypeStruct((6, n, LANES), F32),
        compiler_params=_cparams(("parallel",)),
        name="rope_tables",
    )(pos_f, rows, expand)


def _rope128(y, cos, sin, pattern):
    half = ROPE_HALF[pattern]
    up = pltpu.roll(y, LANES - half, 1)
    dn = pltpu.roll(y, half, 1)
    return y * cos + jnp.where(_x1_mask(pattern), up, dn) * sin


PROJ_COLS = (
    ("qa", 256, 0), ("ka2", 128, 0), ("qi", 512, 0), ("ki2", 128, 0),
    ("qb", 256, 1), ("kb", 256, 1),
    ("va2", 128, None), ("wi", 128, None), ("vb", 256, None),
    ("cq", 256, None), ("ckv", 128, None), ("kr", 128, 2),
)
PROJ_WIDTH = sum(c[1] for c in PROJ_COLS)
PROJ_OUTS = (("qa", 256, BF16), ("ka2", 128, BF16), ("qi", 512, BF16), ("ki2", 128, BF16),
             ("qb", 256, BF16), ("kb", 256, BF16), ("va2", 128, BF16), ("wi", 128, F32),
             ("vb", 256, BF16), ("qc", 1024, BF16), ("kc", 1024, BF16), ("vc", 512, BF16))
PROJ_TRANSPOSED = ("va2", "vb", "vc")
KV_TILE = 256
SUB_KEYS = 128


def _prep_proj_weights(w_in, w_uq, w_ukv):
    offs = np.cumsum((0,) + IN_SPLITS)
    (q_a, k_a, v_a, q_i, k_i, w_i, q_b, k_b, v_b, c_q, c_kv, k_r) = [
        w_in[:, offs[j]:offs[j + 1]] for j in range(len(IN_SPLITS))]
    d = w_in.shape[0]
    z = lambda n: jnp.zeros((d, n), w_in.dtype)
    cols = {
        "qa": q_a, "ka2": jnp.concatenate([k_a, k_a], 1), "qi": q_i, "ki2": jnp.concatenate([k_i, k_i], 1),
        "qb": q_b, "kb": k_b, "va2": jnp.concatenate([v_a, v_a], 1),
        "wi": jnp.concatenate([w_i, z(LANES - DSA_IDX_HEADS)], 1), "vb": v_b, "cq": c_q, "ckv": c_kv,
        "kr": jnp.concatenate([z(MLA_NOPE_DIM), k_r, z(LANES - MLA_NOPE_DIM - MLA_ROPE_DIM)], 1),
    }
    w_p = jnp.concatenate([cols[name] for name, _, _ in PROJ_COLS], axis=1).astype(BF16)
    qd = MLA_NOPE_DIM + MLA_ROPE_DIM
    uq = w_uq.reshape(MLA_Q_LORA, MLA_HEADS, qd)
    uq = jnp.pad(uq, ((0, 0), (0, 0), (0, LANES - qd))).reshape(MLA_Q_LORA, MLA_HEADS * LANES).astype(BF16)
    ukv = w_ukv.reshape(MLA_KV_LORA, MLA_HEADS, MLA_NOPE_DIM + MLA_V_DIM)
    ukn = jnp.pad(ukv[:, :, :MLA_NOPE_DIM], ((0, 0), (0, 0), (0, LANES - MLA_NOPE_DIM)))
    ukn = ukn.reshape(MLA_KV_LORA, MLA_HEADS * LANES).astype(BF16)
    uv = ukv[:, :, MLA_NOPE_DIM:].reshape(MLA_KV_LORA, MLA_HEADS * MLA_V_DIM).astype(BF16)
    return w_p, uq, ukn, uv


def _proj_kernel(h_ref, g_ref, tab_ref, w_ref, qn_ref, uq_ref, kvn_ref, ukn_ref, uv_ref, *out_refs):
    outs = {name: ref for (name, _, _), ref in zip(PROJ_OUTS, out_refs)}
    xn = _rms(h_ref[...], g_ref[...]).astype(BF16)

    def roped(y, pattern):
        return _rope128(y, tab_ref[2 * pattern], tab_ref[2 * pattern + 1], pattern)

    def emit(name, s, y):
        ref = outs[name]
        if name in PROJ_TRANSPOSED:
            tk = ref.shape[2]
            for t in range(ref.shape[0]):
                ref[t, s * LANES:(s + 1) * LANES, :] = y[t * tk:(t + 1) * tk].T.astype(ref.dtype)
        else:
            ref[:, s * LANES:(s + 1) * LANES] = y.astype(ref.dtype)

    def wide_dot(x, w, n_slices):
        per = MXU_COLS // LANES
        res = []
        for c in range(0, n_slices, per):
            y = _dot(x, w[:, c * LANES:(c + per) * LANES])
            res += [y[:, k * LANES:(k + 1) * LANES] for k in range(min(per, n_slices - c))]
        return res

    slices = wide_dot(xn, w_ref, PROJ_WIDTH // LANES)
    vals = {}
    i = 0
    for name, width, pattern in PROJ_COLS:
        for s in range(width // LANES):
            y = slices[i] if pattern is None else roped(slices[i], pattern)
            i += 1
            if name in outs:
                emit(name, s, y)
            else:
                vals.setdefault(name, []).append(y)

    cq = jnp.concatenate(vals["cq"], axis=1)
    cqn = _rms(cq, qn_ref[...]).astype(BF16)
    for hd, y in enumerate(wide_dot(cqn, uq_ref, MLA_HEADS)):
        outs["qc"][:, hd * LANES:(hd + 1) * LANES] = roped(y, 2).astype(BF16)

    ckvn = _rms(vals["ckv"][0], kvn_ref[...]).astype(BF16)
    kr = vals["kr"][0]
    for hd, y in enumerate(wide_dot(ckvn, ukn_ref, MLA_HEADS)):
        outs["kc"][:, hd * LANES:(hd + 1) * LANES] = (y + kr).astype(BF16)
    for s, y in enumerate(wide_dot(ckvn, uv_ref, MLA_HEADS * MLA_V_DIM // LANES)):
        emit("vc", s, y)


def _project(h, gain, tables, w_p, q_norm, uq, kv_norm, ukn, uv, tm, tk):
    n, d = h.shape
    full = lambda a: pl.BlockSpec(a.shape, lambda i: (0,) * a.ndim)
    gain = gain.reshape(1, d)
    q_norm = q_norm.reshape(1, -1)
    kv_norm = kv_norm.reshape(1, -1)
    out_specs, out_shape = [], []
    for name, w, dt in PROJ_OUTS:
        if name in PROJ_TRANSPOSED:
            out_specs.append(pl.BlockSpec((tm // tk, w, tk), lambda i: (i, 0, 0)))
            out_shape.append(jax.ShapeDtypeStruct((n // tk, w, tk), dt))
        else:
            out_specs.append(pl.BlockSpec((tm, w), lambda i: (i, 0)))
            out_shape.append(jax.ShapeDtypeStruct((n, w), dt))
    return pl.pallas_call(
        _proj_kernel,
        grid=(n // tm,),
        in_specs=[pl.BlockSpec((tm, d), lambda i: (i, 0)), full(gain),
                  pl.BlockSpec((6, tm, LANES), lambda i: (0, i, 0)),
                  full(w_p), full(q_norm), full(uq), full(kv_norm), full(ukn), full(uv)],
        out_specs=out_specs,
        out_shape=out_shape,
        compiler_params=_cparams(("parallel",)),
        name="projection",
    )(h, gain, tables, w_p, q_norm, uq, kv_norm, ukn, uv)


def _half_masks(dtype, group):
    lane = lax.broadcasted_iota(jnp.int32, (1, LANES), 1)
    return [jnp.where((lane // group) == u, 1.0, 0.0).astype(dtype) for u in range(LANES // group)]


LOG2E = math.log2(math.e)


def _softmax_step(st, v_t, m, l, acc_ref, scale, guard_empty=False):
    c = scale * LOG2E
    m_new = jnp.maximum(m, jnp.max(st, axis=0, keepdims=True))
    m_use = jnp.where(m_new == -jnp.inf, 0.0, m_new) if guard_empty else m_new
    alpha = jnp.exp2((m - m_use) * c)
    p = jnp.exp2((st - m_use) * c)
    l_new = alpha * l + jnp.sum(p, axis=0, keepdims=True)
    acc_ref[...] = alpha * acc_ref[...] + _dot(v_t, p.astype(BF16))
    return m_new, l_new


def _diag_mask(tk, m_cols, tq):
    key_chunk = lax.broadcasted_iota(jnp.int32, (tk, 1), 0) // CHUNK
    qry_chunk = (lax.broadcasted_iota(jnp.int32, (1, m_cols), 1) % tq) // CHUNK
    return key_chunk <= qry_chunk


def _causal_flash(problems, k_ref, v_ref, acc_ref, qi, tq, scale):
    problems = [(_transpose_bf16(q_st), k_lane0, v_row0) for q_st, k_lane0, v_row0 in problems]
    m_cols = problems[0][0].shape[1]
    n_p = len(problems)
    acc_ref[...] = jnp.zeros_like(acc_ref)
    init = tuple(jnp.full((1, m_cols), -jnp.inf, F32) for _ in range(n_p)) + \
        tuple(jnp.zeros((1, m_cols), F32) for _ in range(n_p))

    sub = min(tq, SUB_KEYS)

    def step(j, carry, mask):
        ms, ls = list(carry[:n_p]), list(carry[n_p:])
        for s0 in range(0, tq, sub):
            r0 = pl.multiple_of(j * tq + s0, sub)
            for p, (q_t, k_lane0, v_row0) in enumerate(problems):
                st = _dot(k_ref[pl.ds(r0, sub), k_lane0:k_lane0 + LANES], q_t)
                if mask is not None:
                    st = jnp.where(mask[s0:s0 + sub], st, -jnp.inf)
                ms[p], ls[p] = _softmax_step(st, v_ref[j, v_row0:v_row0 + LANES, s0:s0 + sub], ms[p], ls[p],
                                             acc_ref.at[p], scale)
        return tuple(ms) + tuple(ls)

    carry = lax.fori_loop(0, qi, lambda j, c: step(j, c, None), init)
    carry = step(qi, carry, _diag_mask(tq, m_cols, tq))
    return [acc_ref[p] / carry[n_p + p] for p in range(n_p)]


def _diff_kernel(lam_ref, subln_ref, gsum_ref, q_ref, k_ref, v_ref, o_ref, acc_ref, *, tq, lam_init):
    qi = pl.program_id(1)
    lam_rows = lam_ref[...]
    s1 = jnp.sum(lam_rows[0:1] * lam_rows[1:2], axis=1, keepdims=True)
    s2 = jnp.sum(lam_rows[2:3] * lam_rows[3:4], axis=1, keepdims=True)
    lam = jnp.exp(s1) - jnp.exp(s2) + lam_init
    masks = _half_masks(BF16, DIFF_QK_DIM)
    row = lax.broadcasted_iota(jnp.int32, (LANES, 1), 0)
    n_slices = DIFF_HEADS // 2
    problems = []
    for s in range(n_slices):
        q = q_ref[0, :, s * LANES:(s + 1) * LANES]
        problems += [(q * mk, s * LANES, s * LANES) for mk in masks]
    outs = _causal_flash(problems, k_ref.at[0], v_ref, acc_ref, qi, tq, DIFF_QK_DIM ** -0.5)
    for s in range(n_slices):
        o0, o1, o2, o3 = outs[4 * s:4 * s + 4]
        a0 = o0 - lam * o1
        a1 = o2 - lam * o3
        a = jnp.where(row < DIFF_V_DIM, a0, a1).T
        ss = _dot_f32_by_exact(a * a, gsum_ref[...])
        y = a * lax.rsqrt(ss * (1.0 / DIFF_V_DIM) + NORM_EPS) * subln_ref[...]
        o_ref[0, :, s * LANES:(s + 1) * LANES] = (y * (1.0 - lam_init)).astype(o_ref.dtype)


def _diff_attention(qb, kb, vb_t, lam_rows, subln, layer, tq):
    b, s, w = qb.shape
    lam_init = 0.8 - 0.6 * math.exp(-0.3 * layer)
    subln2 = jnp.concatenate([subln, subln]).reshape(1, LANES).astype(F32)
    lane = np.arange(LANES)
    gsum = jnp.asarray((lane[:, None] // DIFF_V_DIM) == (lane[None, :] // DIFF_V_DIM), dtype=BF16)
    return pl.pallas_call(
        functools.partial(_diff_kernel, tq=tq, lam_init=lam_init),
        grid=(b, s // tq),
        in_specs=[pl.BlockSpec((8, LANES), lambda i, j: (0, 0)),
                  pl.BlockSpec((1, LANES), lambda i, j: (0, 0)),
                  pl.BlockSpec((LANES, LANES), lambda i, j: (0, 0)),
                  pl.BlockSpec((1, tq, w), lambda i, j: (i, j, 0)),
                  pl.BlockSpec((1, s, w), lambda i, j: (i, 0, 0)),
                  pl.BlockSpec((s // tq, w, tq), lambda i, j: (i, 0, 0))],
        out_specs=pl.BlockSpec((1, tq, w), lambda i, j: (i, j, 0)),
        out_shape=jax.ShapeDtypeStruct((b, s, w), BF16),
        scratch_shapes=[pltpu.VMEM((2 * DIFF_HEADS, LANES, tq), F32)],
        compiler_params=_cparams(("parallel", "arbitrary")),
        name="diff_attention",
    )(lam_rows, subln2, gsum, qb, kb, vb_t)


def _mla_kernel(q_ref, k_ref, v_ref, o_ref, acc_ref, *, tq):
    qi = pl.program_id(1)
    row = lax.broadcasted_iota(jnp.int32, (LANES, 1), 0)
    scale = (MLA_NOPE_DIM + MLA_ROPE_DIM) ** -0.5
    problems = [(q_ref[0, :, hd * LANES:(hd + 1) * LANES], hd * LANES, (hd // 2) * LANES)
                for hd in range(MLA_HEADS)]
    outs = _causal_flash(problems, k_ref.at[0], v_ref, acc_ref, qi, tq, scale)
    for pair in range(MLA_HEADS // 2):
        o_t = jnp.where(row < MLA_V_DIM, outs[2 * pair], outs[2 * pair + 1])
        o_ref[0, :, pair * LANES:(pair + 1) * LANES] = o_t.T.astype(o_ref.dtype)


def _mla_attention(qc, kc, vc_t, tq):
    b, s, wq = qc.shape
    wv = vc_t.shape[1]
    return pl.pallas_call(
        functools.partial(_mla_kernel, tq=tq),
        grid=(b, s // tq),
        in_specs=[pl.BlockSpec((1, tq, wq), lambda i, j: (i, j, 0)),
                  pl.BlockSpec((1, s, wq), lambda i, j: (i, 0, 0)),
                  pl.BlockSpec((s // tq, wv, tq), lambda i, j: (i, 0, 0))],
        out_specs=pl.BlockSpec((1, tq, wv), lambda i, j: (i, j, 0)),
        out_shape=jax.ShapeDtypeStruct((b, s, wv), BF16),
        scratch_shapes=[pltpu.VMEM((MLA_HEADS, LANES, tq), F32)],
        compiler_params=_cparams(("parallel", "arbitrary")),
        name="mla_attention",
    )(qc, kc, vc_t)


KEY_NEG_INF = -2139095041


def _score_keys(score):
    bits = lax.bitcast_convert_type(score, jnp.int32)
    return bits ^ ((bits >> 31) & 0x7FFFFFFF)


def _dsa_kernel(qa_ref, qi_ref, wi_ref, ka_ref, va_ref, ki_ref, o_ref, key_ref, aux_ref, bias_ref, acc_ref,
                *, tq, tk, top_k):
    blk = pl.program_id(1)
    s_len = ka_ref.shape[1]
    n_kv = ((blk + 1) * tq + tk - 1) // tk
    masks = _half_masks(BF16, DSA_IDX_DIM)
    row = lax.broadcasted_iota(jnp.int32, (LANES, 1), 0)
    t_chunk = (blk * tq + lax.broadcasted_iota(jnp.int32, (1, tq), 1)) // CHUNK
    sub_idx = lax.broadcasted_iota(jnp.int32, (tk, 1), 0)

    def tile_start(c):
        return pl.multiple_of(c * tk, tk)

    qi_st = jnp.concatenate([qi_ref[0, :, (hd // 2) * LANES:(hd // 2 + 1) * LANES] * masks[hd % 2]
                             for hd in range(DSA_IDX_HEADS)], axis=0)
    qi_t = _transpose_bf16(qi_st)
    w_t = wi_ref[0].T

    sub = min(tk, SUB_KEYS)
    sub_iota = lax.broadcasted_iota(jnp.int32, (sub, 1), 0)

    def score_body(c, carry):
        for s0 in range(0, tk, sub):
            r0 = pl.multiple_of(c * tk + s0, sub)
            ki = ki_ref[0, pl.ds(r0, sub), :]
            score = jnp.zeros((sub, tq), F32)
            for hd in range(DSA_IDX_HEADS):
                rel = jnp.maximum(_dot(ki, qi_t[:, hd * tq:(hd + 1) * tq]), 0.0)
                score = score + w_t[hd:hd + 1, :] * rel
            score = score * DSA_IDX_SCALE
            score = jnp.where((r0 + sub_iota) // CHUNK <= t_chunk, score, -jnp.inf)
            key_ref[pl.ds(r0, sub), :] = _score_keys(score)
        return carry

    lax.fori_loop(0, n_kv, score_body, 0)

    def count(pred_fn):
        n_part = 8

        def body(c, tot):
            ones = jnp.where(pred_fn(tile_start(c)), 1.0, 0.0)
            return tot + jnp.sum(ones.reshape(tk // (8 * n_part), n_part * 8, tq), axis=0)
        tot = lax.fori_loop(0, n_kv, body, jnp.zeros((n_part * 8, tq), F32))
        return jnp.sum(tot, axis=0, keepdims=True)

    cnt_all = count(lambda r0: key_ref[pl.ds(r0, tk), :] >= INT_MIN)
    cnt_pos = count(lambda r0: key_ref[pl.ds(r0, tk), :] >= 0)
    thr = jnp.where(cnt_pos >= top_k, 0, INT_MIN).astype(jnp.int32)
    cnt_thr = jnp.where(cnt_pos >= top_k, cnt_pos, cnt_all)

    def thr_body(i, carry):
        thr, cnt_thr = carry
        cand = thr | (jnp.int32(1) << (30 - i))
        cnt = count(lambda r0: key_ref[pl.ds(r0, tk), :] >= cand)
        ok = cnt >= top_k
        return jnp.where(ok, cand, thr), jnp.where(ok, cnt, cnt_thr)

    thr, cnt_thr = lax.fori_loop(0, 31, thr_body, (thr, cnt_thr))
    thr = jnp.maximum(thr, KEY_NEG_INF)
    tied = jnp.where(cnt_thr > top_k, jnp.where(thr > KEY_NEG_INF, 1.0, 0.0), 0.0)

    def select_with_ties():
        n_gt = count(lambda r0: key_ref[pl.ds(r0, tk), :] > thr)
        need = top_k - n_gt

        def aux_body(c, carry):
            r0 = tile_start(c)
            idx = r0 + sub_idx
            tie = jnp.where(key_ref[pl.ds(r0, tk), :] == thr, jnp.where(idx // CHUNK <= t_chunk, idx, IDX_BIG), IDX_BIG)
            aux_ref[pl.ds(r0, tk), :] = tie
            return carry

        lax.fori_loop(0, n_kv, aux_body, 0)
        n_bits = max(1, (s_len - 1).bit_length())

        def tie_body(i, last):
            cand = last | (jnp.int32(1) << (n_bits - 1 - i))
            cnt = count(lambda r0: aux_ref[pl.ds(r0, tk), :] < cand)
            return jnp.where(cnt < need, cand, last)

        last = lax.fori_loop(0, n_bits, tie_body, jnp.zeros((1, tq), jnp.int32))

        def bias_body(c, carry):
            r0 = tile_start(c)
            bias_ref[pl.ds(r0, tk), :] = jnp.where(
                key_ref[pl.ds(r0, tk), :] > thr, 0.0, jnp.where(aux_ref[pl.ds(r0, tk), :] <= last, 0.0, -jnp.inf))
            return carry

        lax.fori_loop(0, n_kv, bias_body, 0)

    def select_no_ties():
        low = jnp.where(thr > KEY_NEG_INF, thr - 1, thr)

        def bias_body(c, carry):
            r0 = tile_start(c)
            bias_ref[pl.ds(r0, tk), :] = jnp.where(key_ref[pl.ds(r0, tk), :] > low, 0.0, -jnp.inf)
            return carry

        lax.fori_loop(0, n_kv, bias_body, 0)

    lax.cond(jnp.max(tied) > 0.0, select_with_ties, select_no_ties)

    hmask = _half_masks(BF16, DSA_HEAD_DIM)
    qa_st = jnp.concatenate([qa_ref[0, :, (hd // 2) * LANES:(hd // 2 + 1) * LANES] * hmask[hd % 2]
                             for hd in range(DSA_HEADS)], axis=0)
    qa_t = _transpose_bf16(qa_st)
    scale = DSA_HEAD_DIM ** -0.5
    n_h = DSA_HEADS
    acc_ref[...] = jnp.zeros_like(acc_ref)

    def att_body(c, carry):
        ms, ls = list(carry[:n_h]), list(carry[n_h:])
        for s0 in range(0, tk, sub):
            r0 = pl.multiple_of(c * tk + s0, sub)
            bias = bias_ref[pl.ds(r0, sub), :]
            ka = ka_ref[0, pl.ds(r0, sub), :]
            for hd in range(n_h):
                st = _dot(ka, qa_t[:, hd * tq:(hd + 1) * tq]) + bias
                ms[hd], ls[hd] = _softmax_step(st, va_ref[c, :, s0:s0 + sub], ms[hd], ls[hd], acc_ref.at[hd],
                                               scale, guard_empty=True)
        return tuple(ms) + tuple(ls)

    init = tuple(jnp.full((1, tq), -jnp.inf, F32) for _ in range(n_h)) + \
        tuple(jnp.zeros((1, tq), F32) for _ in range(n_h))
    carry = lax.fori_loop(0, n_kv, att_body, init)
    for p in range(n_h // 2):
        pair_t = jnp.where(row < DSA_HEAD_DIM, acc_ref[2 * p] / carry[n_h + 2 * p],
                           acc_ref[2 * p + 1] / carry[n_h + 2 * p + 1])
        o_ref[0, :, p * LANES:(p + 1) * LANES] = pair_t.T.astype(o_ref.dtype)


def _dsa_attention(qa, qi, wi, ka2, va2_t, ki2, tq, tk):
    b, s, _ = qa.shape
    top_k = min(DSA_TOPK_MAX, s // 4)
    qspec = lambda w: pl.BlockSpec((1, tq, w), lambda i, j: (i, j, 0))
    kspec = lambda w: pl.BlockSpec((1, s, w), lambda i, j: (i, 0, 0))
    return pl.pallas_call(
        functools.partial(_dsa_kernel, tq=tq, tk=tk, top_k=top_k),
        grid=(b, s // tq),
        in_specs=[qspec(qa.shape[2]), qspec(qi.shape[2]), qspec(LANES), kspec(LANES),
                  pl.BlockSpec((s // tk, LANES, tk), lambda i, j: (i, 0, 0)), kspec(LANES)],
        out_specs=qspec(qa.shape[2]),
        out_shape=jax.ShapeDtypeStruct(qa.shape, BF16),
        scratch_shapes=[pltpu.VMEM((s, tq), jnp.int32), pltpu.VMEM((s, tq), jnp.int32), pltpu.VMEM((s, tq), F32),
                        pltpu.VMEM((DSA_HEADS, LANES, tq), F32)],
        compiler_params=_cparams(("parallel", "arbitrary")),
        name="dsa_attention",
    )(qa, qi, wi, ka2, va2_t, ki2)


def _attn_residual(h_ref, oa_ref, ob_ref, oc_ref, wo_ref):
    wa, wb = oa_ref.shape[1], ob_ref.shape[1]
    return (h_ref[...] + _dot(oa_ref[...], wo_ref[0:wa, :]) + _dot(ob_ref[...], wo_ref[wa:wa + wb, :])
            + _dot(oc_ref[...], wo_ref[wa + wb:, :]))


def _swiglu_partial(x, wg, wu, wd):
    g = _dot(x, wg)
    u = _dot(x, wu)
    return _dot(((g * jax.nn.sigmoid(g)) * u).astype(BF16), wd)


def _dense_block_kernel(h_ref, oa_ref, ob_ref, oc_ref, wo_ref, g_ref, wg_ref, wu_ref, wd_ref, fg_ref,
                        out_ref, xn_ref, acc_ref, *, final_norm):
    f = pl.program_id(1)

    @pl.when(f == 0)
    def _():
        h1 = _attn_residual(h_ref, oa_ref, ob_ref, oc_ref, wo_ref)
        acc_ref[...] = h1
        xn_ref[...] = _rms(h1, g_ref[...]).astype(BF16)

    chunk = min(DENSE_ROW_CHUNK, xn_ref.shape[0])
    for r0 in range(0, xn_ref.shape[0], chunk):
        rows = slice(r0, r0 + chunk)
        acc_ref[rows, :] += _swiglu_partial(xn_ref[rows, :], wg_ref[...], wu_ref[...], wd_ref[...])

    @pl.when(f == pl.num_programs(1) - 1)
    def _():
        y = acc_ref[...]
        out_ref[...] = _rms(y, fg_ref[...]) if final_norm else y


def _dense_block(h, oa, ob, oc, w_out, gain, wg, wu, wd, final_gain, *, final_norm, tm, tf):
    n, d = h.shape
    dff = wg.shape[1]
    const = lambda a: pl.BlockSpec(a.shape, lambda i, f: (0,) * a.ndim, pipeline_mode=pl.Buffered(1))
    row = lambda w: pl.BlockSpec((tm, w), lambda i, f: (i, 0))
    row_in = lambda w: pl.BlockSpec((tm, w), lambda i, f: (i, 0), pipeline_mode=pl.Buffered(1))
    gain = gain.reshape(1, d)
    final_gain = final_gain.reshape(1, d)
    return pl.pallas_call(
        functools.partial(_dense_block_kernel, final_norm=final_norm),
        grid=(n // tm, dff // tf),
        in_specs=[row(d), row_in(oa.shape[1]), row_in(ob.shape[1]), row_in(oc.shape[1]), const(w_out), const(gain),
                  pl.BlockSpec((d, tf), lambda i, f: (0, f)),
                  pl.BlockSpec((d, tf), lambda i, f: (0, f)),
                  pl.BlockSpec((tf, d), lambda i, f: (f, 0)),
                  const(final_gain)],
        out_specs=row(d),
        out_shape=jax.ShapeDtypeStruct((n, d), F32),
        scratch_shapes=[pltpu.VMEM((tm, d), BF16), pltpu.VMEM((tm, d), F32)],
        compiler_params=_cparams(("parallel", "arbitrary")),
        name="block_dense",
    )(h, oa, ob, oc, w_out, gain, wg, wu, wd, final_gain)


def _moe_block_kernel(h_ref, oa_ref, ob_ref, oc_ref, wo_ref, g_ref, r_ref, tri_ref, wg_ref, wu_ref, wd_ref, fg_ref,
                      out_ref, xn_ref, acc_ref, gate_ref, pos_ref, pos_t_ref, cnt_ref, xs_ref, ye_ref,
                      *, n_exp, mains, rows, final_norm):
    e = pl.program_id(1)
    f = pl.program_id(2)
    n_f = pl.num_programs(2)
    tm = xn_ref.shape[0]
    lane = lax.broadcasted_iota(jnp.int32, (1, LANES), 1)

    @pl.when((e == 0) & (f == 0))
    def _():
        h1 = _attn_residual(h_ref, oa_ref, ob_ref, oc_ref, wo_ref)
        acc_ref[...] = h1
        hn = _rms(h1, g_ref[...])
        xn_ref[...] = hn.astype(BF16)
        hn_hi = hn.astype(BF16)
        hn_lo = (hn - hn_hi.astype(F32)).astype(BF16)
        parts = _dot(hn_hi, r_ref[...]) + _dot(hn_lo, r_ref[...])
        logits = parts[:, :LANES] + parts[:, LANES:]
        logits = jnp.where(lane < n_exp, logits, -jnp.inf)
        m1 = jnp.max(logits, axis=1, keepdims=True)
        i1 = jnp.min(jnp.where(logits == m1, lane, IDX_BIG), axis=1, keepdims=True)
        rest = jnp.where(lane == i1, -jnp.inf, logits)
        m2 = jnp.max(rest, axis=1, keepdims=True)
        i2 = jnp.min(jnp.where(rest == m2, lane, IDX_BIG), axis=1, keepdims=True)
        e2 = jnp.exp(m2 - m1)
        den = 1.0 + e2
        gate_ref[...] = jnp.where(lane == i1, 1.0 / den, jnp.where(lane == i2, e2 / den, 0.0))
        routed = jnp.where(lane == i1, 1.0, jnp.where(lane == i2, 1.0, 0.0))
        before = _dot(tri_ref[...], routed.astype(BF16))
        slot = jnp.where(routed > 0.0, before, -1.0)
        pos_ref[...] = slot
        pos_t_ref[...] = slot.T
        cnt_ref[...] = jnp.sum(routed, axis=0, keepdims=True)

    n_tok = jnp.sum(jnp.where(lane == e, cnt_ref[...], 0.0)).astype(jnp.int32)
    top = mains[-1]
    n_extra = (jnp.maximum(n_tok - top, 0) + rows - 1) // rows

    def for_blocks(fn):
        below = 0
        for size in mains:
            fits = (n_tok > below) if size == top else ((n_tok > below) & (n_tok <= size))

            @pl.when(fits)
            def _(size=size):
                fn(0, size)
                if size == top:
                    lax.fori_loop(0, n_extra, lambda r, c: (
                        fn(pl.multiple_of(top + r * rows, math.gcd(top, rows)), rows), c)[1], 0)

            below = size

    @pl.when(f == 0)
    def _():
        slot_row = pos_t_ref[pl.ds(e, 1), :]

        def gather(r0, nr):
            want = (r0 + lax.broadcasted_iota(jnp.int32, (nr, 1), 0)).astype(F32)
            pick = jnp.where(slot_row == want, 1.0, 0.0).astype(BF16)
            xs_ref[pl.ds(r0, nr), :] = _dot(pick, xn_ref[...]).astype(BF16)
            ye_ref[pl.ds(r0, nr), :] = jnp.zeros((nr, ye_ref.shape[1]), F32)

        for_blocks(gather)

    def expert(r0, nr):
        ye_ref[pl.ds(r0, nr), :] += _swiglu_partial(xs_ref[pl.ds(r0, nr), :], wg_ref[0], wu_ref[0], wd_ref[0])

    for_blocks(expert)

    @pl.when(f == n_f - 1)
    def _():
        full_lane = lax.broadcasted_iota(jnp.int32, (tm, LANES), 1)
        slot_col = jnp.sum(jnp.where(full_lane == e, pos_ref[...], 0.0), axis=1, keepdims=True)
        gate_col = jnp.sum(jnp.where(full_lane == e, gate_ref[...], 0.0), axis=1, keepdims=True)

        def scatter(r0, nr):
            have = (r0 + lax.broadcasted_iota(jnp.int32, (1, nr), 1)).astype(F32)
            place = jnp.where(slot_col == have, 1.0, 0.0).astype(BF16)
            y = ye_ref[pl.ds(r0, nr), :]
            y_hi = y.astype(BF16)
            y_lo = (y - y_hi.astype(F32)).astype(BF16)
            acc_ref[...] += gate_col * (_dot(place, y_hi) + _dot(place, y_lo))

        for_blocks(scatter)

    @pl.when((e == n_exp - 1) & (f == n_f - 1))
    def _():
        y = acc_ref[...]
        out_ref[...] = _rms(y, fg_ref[...]) if final_norm else y


def _moe_block(h, oa, ob, oc, w_out, gain, router_p, wg, wu, wd, final_gain, *, final_norm, tm, tf, mains, rows):
    n, d = h.shape
    n_exp, _, dff = wg.shape
    const = lambda a: pl.BlockSpec(a.shape, lambda i, e, f: (0,) * a.ndim, pipeline_mode=pl.Buffered(1))
    row = lambda w: pl.BlockSpec((tm, w), lambda i, e, f: (i, 0))
    row_in = lambda w: pl.BlockSpec((tm, w), lambda i, e, f: (i, 0), pipeline_mode=pl.Buffered(1))
    gain = gain.reshape(1, d)
    final_gain = final_gain.reshape(1, d)
    tok = np.arange(tm)
    tri = jnp.asarray(tok[None, :] < tok[:, None], dtype=BF16)
    mains = tuple(sorted({min(m, tm) for m in mains}))
    cap = mains[-1] + -(-(tm - mains[-1]) // rows) * rows
    return pl.pallas_call(
        functools.partial(_moe_block_kernel, n_exp=n_exp, mains=mains, rows=rows, final_norm=final_norm),
        grid=(n // tm, n_exp, dff // tf),
        in_specs=[row_in(d), row_in(oa.shape[1]), row_in(ob.shape[1]), row_in(oc.shape[1]), const(w_out), const(gain),
                  const(router_p), const(tri),
                  pl.BlockSpec((1, d, tf), lambda i, e, f: (e, 0, f)),
                  pl.BlockSpec((1, d, tf), lambda i, e, f: (e, 0, f)),
                  pl.BlockSpec((1, tf, d), lambda i, e, f: (e, f, 0)),
                  const(final_gain)],
        out_specs=row(d),
        out_shape=jax.ShapeDtypeStruct((n, d), F32),
        scratch_shapes=[pltpu.VMEM((tm, d), BF16), pltpu.VMEM((tm, d), F32), pltpu.VMEM((tm, LANES), F32),
                        pltpu.VMEM((tm, LANES), F32), pltpu.VMEM((LANES, tm), F32), pltpu.VMEM((1, LANES), F32),
                        pltpu.VMEM((cap, d), BF16), pltpu.VMEM((cap, d), F32)],
        compiler_params=_cparams(("parallel", "arbitrary", "arbitrary")),
        name="block_moe",
    )(h, oa, ob, oc, w_out, gain, router_p, tri, wg, wu, wd, final_gain)


def _pick(n, pref):
    t = min(pref, n)
    while n % t:
        t //= 2
    return t


def kernel(x, positions, attn_norm, w_in, mla_q_norm, w_uq, mla_kv_norm, w_ukv, diff_lambda_q1, diff_lambda_k1, diff_lambda_q2, diff_lambda_k2, diff_subln, w_out, ffn_norm, dense_w_gate, dense_w_up, dense_w_down, moe_router, moe_w_gate, moe_w_up, moe_w_down, final_norm):
    b, s, d = x.shape
    n = b * s
    depth = w_in.shape[0]
    tm_proj = _pick(n, 512)
    tm_blk = _pick(n, 1024)
    tk = _pick(s, KV_TILE)
    tq_dsa = _pick(s, 256)

    tables = _rope_tables(positions.astype(F32).reshape(n, 1), _pick(n, 1024))
    h = x.reshape(n, d)
    r3 = lambda a: a.reshape(b, s, a.shape[-1])
    for layer in range(depth):
        w_p, uq, ukn, uv = _prep_proj_weights(w_in[layer], w_uq[layer], w_ukv[layer])
        (qa, ka2, qi, ki2, qb, kb, va2, wi, vb, qc, kc, vc) = _project(
            h, attn_norm[layer], tables, w_p, mla_q_norm[layer], uq, mla_kv_norm[layer], ukn, uv, tm_proj, tk)
        oa = _dsa_attention(r3(qa), r3(qi), r3(wi), r3(ka2), va2, r3(ki2), tq_dsa, tk)
        lam_rows = jnp.zeros((8, LANES), F32).at[0:4, 0:DIFF_QK_DIM].set(jnp.stack(
            [diff_lambda_q1[layer], diff_lambda_k1[layer], diff_lambda_q2[layer], diff_lambda_k2[layer]]))
        ob = _diff_attention(r3(qb), r3(kb), vb, lam_rows, diff_subln[layer], layer, tk)
        oc = _mla_attention(r3(qc), r3(kc), vc, tk)
        j = layer // 2
        last = layer == depth - 1
        wo = w_out[layer].astype(BF16)
        attn = (oa.reshape(n, -1), ob.reshape(n, -1), oc.reshape(n, -1))
        if layer % 2 == 0:
            h = _dense_block(h, *attn, wo, ffn_norm[layer], dense_w_gate[j].astype(BF16), dense_w_up[j].astype(BF16),
                             dense_w_down[j].astype(BF16), final_norm,
                             final_norm=last, tm=tm_blk, tf=_pick(dense_w_gate.shape[2], DENSE_FFN_TILE))
        else:
            router_f = jnp.pad(moe_router[j], ((0, 0), (0, LANES - MOE_EXPERTS)))
            router_hi = router_f.astype(BF16)
            router_p = jnp.concatenate([router_hi, (router_f - router_hi.astype(F32)).astype(BF16)], axis=1)
            h = _moe_block(h, *attn, wo, ffn_norm[layer], router_p,
                           moe_w_gate[j].astype(BF16), moe_w_up[j].astype(BF16), moe_w_down[j].astype(BF16), final_norm,
                           final_norm=last, tm=tm_blk, tf=_pick(moe_w_gate.shape[3], FFN_TILE),
                           mains=MOE_MAIN_ROWS, rows=MOE_EXTRA_ROWS)
    return h.reshape(b, s, d)
```

```python
import functools
import math

import jax
import jax.numpy as jnp
import numpy as np
from jax import lax
from jax.experimental import pallas as pl
from jax.experimental.pallas import tpu as pltpu

F32 = jnp.float32
BF16 = jnp.bfloat16

LANES = 128
MXU_COLS = 256
VMEM_LIMIT_BYTES = 56 * 1024 * 1024

D_MODEL = 1024
CHUNK = 64
ROPE_THETA = 500000.0
NORM_EPS = 1e-6
ROPE_FRACTION_DEN = 4

DSA_HEADS = 4
DSA_HEAD_DIM = 64
DSA_IDX_HEADS = 8
DSA_IDX_DIM = 64
DSA_TOPK_MAX = 256
DSA_IDX_SCALE = (DSA_IDX_HEADS * DSA_IDX_DIM) ** -0.5

DIFF_HEADS = 4
DIFF_QK_DIM = 32
DIFF_V_DIM = 2 * DIFF_QK_DIM

MLA_HEADS = 8
MLA_Q_LORA = 256
MLA_KV_LORA = 128
MLA_NOPE_DIM = 64
MLA_ROPE_DIM = 32
MLA_V_DIM = 64

MOE_EXPERTS = 8
MOE_TOP_K = 2
FFN_TILE = 896
MOE_MAIN_ROWS = (256, 288, 320)
MOE_EXTRA_ROWS = 128

IN_SPLITS = (
    DSA_HEADS * DSA_HEAD_DIM, DSA_HEAD_DIM, DSA_HEAD_DIM, DSA_IDX_HEADS * DSA_IDX_DIM, DSA_IDX_DIM,
    DSA_IDX_HEADS, DIFF_HEADS * 2 * DIFF_QK_DIM, DIFF_HEADS * 2 * DIFF_QK_DIM, DIFF_HEADS * DIFF_V_DIM,
    MLA_Q_LORA, MLA_KV_LORA, MLA_ROPE_DIM,
)

INT_MIN = -(2 ** 31)
IDX_BIG = 2 ** 30


def _cparams(sem):
    return pltpu.CompilerParams(dimension_semantics=sem, vmem_limit_bytes=VMEM_LIMIT_BYTES)


def _rms(x, g):
    return x * lax.rsqrt(jnp.mean(x * x, axis=-1, keepdims=True) + NORM_EPS) * g


def _dot(a, b):
    return jnp.dot(a, b, preferred_element_type=F32)


def _transpose_bf16(x):
    return x.astype(F32).T.astype(BF16)


def _split3(x):
    hi = x.astype(BF16)
    r1 = x - hi.astype(F32)
    mid = r1.astype(BF16)
    lo = (r1 - mid.astype(F32)).astype(BF16)
    return hi, mid, lo


def _dot_f32_by_exact(x, m_bf16):
    hi, mid, lo = _split3(x)
    return _dot(hi, m_bf16) + _dot(mid, m_bf16) + _dot(lo, m_bf16)


def _inv_freq(rot_dim):
    half = rot_dim // 2
    return ROPE_THETA ** (-(jnp.arange(half, dtype=F32) * 2.0 / rot_dim))


def _rope_patterns():
    lane = np.arange(LANES)
    rots = (DSA_HEAD_DIM // ROPE_FRACTION_DEN, DIFF_QK_DIM // ROPE_FRACTION_DEN, MLA_ROPE_DIM)
    offs = (lane % DSA_HEAD_DIM, lane % DIFF_QK_DIM, lane - MLA_NOPE_DIM)
    zero_lane = LANES - 1
    freq = jnp.zeros((LANES,), F32)
    signs, expand = [], np.zeros((3, LANES, LANES), np.float32)
    base = 0
    for p, (rot, off) in enumerate(zip(rots, offs)):
        half = rot // 2
        active = (off >= 0) & (off < rot)
        freq = freq.at[base:base + half].set(_inv_freq(rot))
        src = np.where(active, base + np.clip(off, 0, rot - 1) % half, zero_lane)
        expand[p, src, lane] = 1.0
        signs.append(np.where(active, np.where(off < half, -1.0, 1.0), 0.0))
        base += half
    assert base < zero_lane
    rows = jnp.concatenate([freq[None, :], jnp.asarray(np.stack(signs), dtype=F32), jnp.zeros((4, LANES), F32)], axis=0)
    return rows, jnp.asarray(expand, dtype=BF16)


ROPE_HALF = (DSA_HEAD_DIM // ROPE_FRACTION_DEN // 2, DIFF_QK_DIM // ROPE_FRACTION_DEN // 2, MLA_ROPE_DIM // 2)


def _x1_mask(pattern):
    lane = lax.broadcasted_iota(jnp.int32, (1, LANES), 1)
    if pattern == 0:
        return (lane % DSA_HEAD_DIM) < ROPE_HALF[0]
    if pattern == 1:
        return (lane % DIFF_QK_DIM) < ROPE_HALF[1]
    return (lane >= MLA_NOPE_DIM) & (lane < MLA_NOPE_DIM + ROPE_HALF[2])


def _rope_tables_kernel(pos_ref, rows_ref, expand_ref, out_ref):
    ang = pos_ref[...] * rows_ref[0:1, :]
    cos, sin = jnp.cos(ang), jnp.sin(ang)
    for p in range(3):
        out_ref[2 * p] = _dot_f32_by_exact(cos, expand_ref[p])
        out_ref[2 * p + 1] = _dot_f32_by_exact(sin, expand_ref[p]) * rows_ref[1 + p:2 + p, :]


def _rope_tables(pos_f, tm):
    n = pos_f.shape[0]
    rows, expand = _rope_patterns()
    return pl.pallas_call(
        _rope_tables_kernel,
        grid=(n // tm,),
        in_specs=[pl.BlockSpec((tm, 1), lambda i: (i, 0)),
                  pl.BlockSpec((8, LANES), lambda i: (0, 0)),
                  pl.BlockSpec((3, LANES, LANES), lambda i: (0, 0, 0))],
        out_specs=pl.BlockSpec((6, tm, LANES), lambda i: (0, i, 0)),
        out_shape=jax.ShapeDtypeStruct((6, n, LANES), F32),
        compiler_params=_cparams(("parallel",)),
        name="rope_tables",
    )(pos_f, rows, expand)


def _rope128(y, cos, sin, pattern):
    half = ROPE_HALF[pattern]
    up = pltpu.roll(y, LANES - half, 1)
    dn = pltpu.roll(y, half, 1)
    return y * cos + jnp.where(_x1_mask(pattern), up, dn) * sin


PROJ_COLS = (
    ("qa", 256, 0), ("ka2", 128, 0), ("qi", 512, 0), ("ki2", 128, 0),
    ("qb", 256, 1), ("kb", 256, 1),
    ("va2", 128, None), ("wi", 128, None), ("vb", 256, None),
    ("cq", 256, None), ("ckv", 128, None), ("kr", 128, 2),
)
PROJ_WIDTH = sum(c[1] for c in PROJ_COLS)
PROJ_OUTS = (("qa", 256, BF16), ("ka2", 128, BF16), ("qi", 512, BF16), ("ki2", 128, BF16),
             ("qb", 256, BF16), ("kb", 256, BF16), ("va2", 128, BF16), ("wi", 128, F32),
             ("vb", 256, BF16), ("qc", 1024, BF16), ("kc", 1024, BF16), ("vc", 512, BF16))
PROJ_TRANSPOSED = ("va2", "vb", "vc")
KV_TILE = 256
SUB_KEYS = 128


def _prep_proj_weights(w_in, w_uq, w_ukv):
    offs = np.cumsum((0,) + IN_SPLITS)
    (q_a, k_a, v_a, q_i, k_i, w_i, q_b, k_b, v_b, c_q, c_kv, k_r) = [
        w_in[:, offs[j]:offs[j + 1]] for j in range(len(IN_SPLITS))]
    d = w_in.shape[0]
    z = lambda n: jnp.zeros((d, n), w_in.dtype)
    cols = {
        "qa": q_a, "ka2": jnp.concatenate([k_a, k_a], 1), "qi": q_i, "ki2": jnp.concatenate([k_i, k_i], 1),
        "qb": q_b, "kb": k_b, "va2": jnp.concatenate([v_a, v_a], 1),
        "wi": jnp.concatenate([w_i, z(LANES - DSA_IDX_HEADS)], 1), "vb": v_b, "cq": c_q, "ckv": c_kv,
        "kr": jnp.concatenate([z(MLA_NOPE_DIM), k_r, z(LANES - MLA_NOPE_DIM - MLA_ROPE_DIM)], 1),
    }
    w_p = jnp.concatenate([cols[name] for name, _, _ in PROJ_COLS], axis=1).astype(BF16)
    qd = MLA_NOPE_DIM + MLA_ROPE_DIM
    uq = w_uq.reshape(MLA_Q_LORA, MLA_HEADS, qd)
    uq = jnp.pad(uq, ((0, 0), (0, 0), (0, LANES - qd))).reshape(MLA_Q_LORA, MLA_HEADS * LANES).astype(BF16)
    ukv = w_ukv.reshape(MLA_KV_LORA, MLA_HEADS, MLA_NOPE_DIM + MLA_V_DIM)
    ukn = jnp.pad(ukv[:, :, :MLA_NOPE_DIM], ((0, 0), (0, 0), (0, LANES - MLA_NOPE_DIM)))
    ukn = ukn.reshape(MLA_KV_LORA, MLA_HEADS * LANES).astype(BF16)
    uv = ukv[:, :, MLA_NOPE_DIM:].reshape(MLA_KV_LORA, MLA_HEADS * MLA_V_DIM).astype(BF16)
    return w_p, uq, ukn, uv


def _proj_kernel(h_ref, g_ref, tab_ref, w_ref, qn_ref, uq_ref, kvn_ref, ukn_ref, uv_ref, *out_refs):
    outs = {name: ref for (name, _, _), ref in zip(PROJ_OUTS, out_refs)}
    xn = _rms(h_ref[...], g_ref[...]).astype(BF16)

    def roped(y, pattern):
        return _rope128(y, tab_ref[2 * pattern], tab_ref[2 * pattern + 1], pattern)

    def emit(name, s, y):
        ref = outs[name]
        if name in PROJ_TRANSPOSED:
            tk = ref.shape[2]
            for t in range(ref.shape[0]):
                ref[t, s * LANES:(s + 1) * LANES, :] = y[t * tk:(t + 1) * tk].T.astype(ref.dtype)
        else:
            ref[:, s * LANES:(s + 1) * LANES] = y.astype(ref.dtype)

    def wide_dot(x, w, n_slices):
        per = MXU_COLS // LANES
        res = []
        for c in range(0, n_slices, per):
            y = _dot(x, w[:, c * LANES:(c + per) * LANES])
            res += [y[:, k * LANES:(k + 1) * LANES] for k in range(min(per, n_slices - c))]
        return res

    slices = wide_dot(xn, w_ref, PROJ_WIDTH // LANES)
    vals = {}
    i = 0
    for name, width, pattern in PROJ_COLS:
        for s in range(width // LANES):
            y = slices[i] if pattern is None else roped(slices[i], pattern)
            i += 1
            if name in outs:
                emit(name, s, y)
            else:
                vals.setdefault(name, []).append(y)

    cq = jnp.concatenate(vals["cq"], axis=1)
    cqn = _rms(cq, qn_ref[...]).astype(BF16)
    for hd, y in enumerate(wide_dot(cqn, uq_ref, MLA_HEADS)):
        outs["qc"][:, hd * LANES:(hd + 1) * LANES] = roped(y, 2).astype(BF16)

    ckvn = _rms(vals["ckv"][0], kvn_ref[...]).astype(BF16)
    kr = vals["kr"][0]
    for hd, y in enumerate(wide_dot(ckvn, ukn_ref, MLA_HEADS)):
        outs["kc"][:, hd * LANES:(hd + 1) * LANES] = (y + kr).astype(BF16)
    for s, y in enumerate(wide_dot(ckvn, uv_ref, MLA_HEADS * MLA_V_DIM // LANES)):
        emit("vc", s, y)


def _project(h, gain, tables, w_p, q_norm, uq, kv_norm, ukn, uv, tm, tk):
    n, d = h.shape
    full = lambda a: pl.BlockSpec(a.shape, lambda i: (0,) * a.ndim)
    gain = gain.reshape(1, d)
    q_norm = q_norm.reshape(1, -1)
    kv_norm = kv_norm.reshape(1, -1)
    out_specs, out_shape = [], []
    for name, w, dt in PROJ_OUTS:
        if name in PROJ_TRANSPOSED:
            out_specs.append(pl.BlockSpec((tm // tk, w, tk), lambda i: (i, 0, 0)))
            out_shape.append(jax.ShapeDtypeStruct((n // tk, w, tk), dt))
        else:
            out_specs.append(pl.BlockSpec((tm, w), lambda i: (i, 0)))
            out_shape.append(jax.ShapeDtypeStruct((n, w), dt))
    return pl.pallas_call(
        _proj_kernel,
        grid=(n // tm,),
        in_specs=[pl.BlockSpec((tm, d), lambda i: (i, 0)), full(gain),
                  pl.BlockSpec((6, tm, LANES), lambda i: (0, i, 0)),
                  full(w_p), full(q_norm), full(uq), full(kv_norm), full(ukn), full(uv)],
        out_specs=out_specs,
        out_shape=out_shape,
        compiler_params=_cparams(("parallel",)),
        name="projection",
    )(h, gain, tables, w_p, q_norm, uq, kv_norm, ukn, uv)


def _half_masks(dtype, group):
    lane = lax.broadcasted_iota(jnp.int32, (1, LANES), 1)
    return [jnp.where((lane // group) == u, 1.0, 0.0).astype(dtype) for u in range(LANES // group)]


LOG2E = math.log2(math.e)


def _softmax_step(st, v_t, m, l, acc_ref, scale, guard_empty=False):
    c = scale * LOG2E
    m_new = jnp.maximum(m, jnp.max(st, axis=0, keepdims=True))
    m_use = jnp.where(m_new == -jnp.inf, 0.0, m_new) if guard_empty else m_new
    alpha = jnp.exp2((m - m_use) * c)
    p = jnp.exp2((st - m_use) * c)
    l_new = alpha * l + jnp.sum(p, axis=0, keepdims=True)
    acc_ref[...] = alpha * acc_ref[...] + _dot(v_t, p.astype(BF16))
    return m_new, l_new


def _diag_mask(tk, m_cols, tq):
    key_chunk = lax.broadcasted_iota(jnp.int32, (tk, 1), 0) // CHUNK
    qry_chunk = (lax.broadcasted_iota(jnp.int32, (1, m_cols), 1) % tq) // CHUNK
    return key_chunk <= qry_chunk


def _causal_flash(problems, k_ref, v_ref, acc_ref, qi, tq, scale):
    problems = [(_transpose_bf16(q_st), k_lane0, v_row0) for q_st, k_lane0, v_row0 in problems]
    m_cols = problems[0][0].shape[1]
    n_p = len(problems)
    acc_ref[...] = jnp.zeros_like(acc_ref)
    init = tuple(jnp.full((1, m_cols), -jnp.inf, F32) for _ in range(n_p)) + \
        tuple(jnp.zeros((1, m_cols), F32) for _ in range(n_p))

    sub = min(tq, SUB_KEYS)

    def step(j, carry, mask):
        ms, ls = list(carry[:n_p]), list(carry[n_p:])
        for s0 in range(0, tq, sub):
            r0 = pl.multiple_of(j * tq + s0, sub)
            for p, (q_t, k_lane0, v_row0) in enumerate(problems):
                st = _dot(k_ref[pl.ds(r0, sub), k_lane0:k_lane0 + LANES], q_t)
                if mask is not None:
                    st = jnp.where(mask[s0:s0 + sub], st, -jnp.inf)
                ms[p], ls[p] = _softmax_step(st, v_ref[j, v_row0:v_row0 + LANES, s0:s0 + sub], ms[p], ls[p],
                                             acc_ref.at[p], scale)
        return tuple(ms) + tuple(ls)

    carry = lax.fori_loop(0, qi, lambda j, c: step(j, c, None), init)
    carry = step(qi, carry, _diag_mask(tq, m_cols, tq))
    return [acc_ref[p] / carry[n_p + p] for p in range(n_p)]


def _diff_kernel(lam_ref, subln_ref, gsum_ref, q_ref, k_ref, v_ref, o_ref, acc_ref, *, tq, lam_init):
    qi = pl.program_id(1)
    lam_rows = lam_ref[...]
    s1 = jnp.sum(lam_rows[0:1] * lam_rows[1:2], axis=1, keepdims=True)
    s2 = jnp.sum(lam_rows[2:3] * lam_rows[3:4], axis=1, keepdims=True)
    lam = jnp.exp(s1) - jnp.exp(s2) + lam_init
    masks = _half_masks(BF16, DIFF_QK_DIM)
    row = lax.broadcasted_iota(jnp.int32, (LANES, 1), 0)
    n_slices = DIFF_HEADS // 2
    problems = []
    for s in range(n_slices):
        q = q_ref[0, :, s * LANES:(s + 1) * LANES]
        problems += [(q * mk, s * LANES, s * LANES) for mk in masks]
    outs = _causal_flash(problems, k_ref.at[0], v_ref, acc_ref, qi, tq, DIFF_QK_DIM ** -0.5)
    for s in range(n_slices):
        o0, o1, o2, o3 = outs[4 * s:4 * s + 4]
        a0 = o0 - lam * o1
        a1 = o2 - lam * o3
        a = jnp.where(row < DIFF_V_DIM, a0, a1).T
        ss = _dot_f32_by_exact(a * a, gsum_ref[...])
        y = a * lax.rsqrt(ss * (1.0 / DIFF_V_DIM) + NORM_EPS) * subln_ref[...]
        o_ref[0, :, s * LANES:(s + 1) * LANES] = (y * (1.0 - lam_init)).astype(o_ref.dtype)


def _diff_attention(qb, kb, vb_t, lam_rows, subln, layer, tq):
    b, s, w = qb.shape
    lam_init = 0.8 - 0.6 * math.exp(-0.3 * layer)
    subln2 = jnp.concatenate([subln, subln]).reshape(1, LANES).astype(F32)
    lane = np.arange(LANES)
    gsum = jnp.asarray((lane[:, None] // DIFF_V_DIM) == (lane[None, :] // DIFF_V_DIM), dtype=BF16)
    return pl.pallas_call(
        functools.partial(_diff_kernel, tq=tq, lam_init=lam_init),
        grid=(b, s // tq),
        in_specs=[pl.BlockSpec((8, LANES), lambda i, j: (0, 0)),
                  pl.BlockSpec((1, LANES), lambda i, j: (0, 0)),
                  pl.BlockSpec((LANES, LANES), lambda i, j: (0, 0)),
                  pl.BlockSpec((1, tq, w), lambda i, j: (i, j, 0)),
                  pl.BlockSpec((1, s, w), lambda i, j: (i, 0, 0)),
                  pl.BlockSpec((s // tq, w, tq), lambda i, j: (i, 0, 0))],
        out_specs=pl.BlockSpec((1, tq, w), lambda i, j: (i, j, 0)),
        out_shape=jax.ShapeDtypeStruct((b, s, w), BF16),
        scratch_shapes=[pltpu.VMEM((2 * DIFF_HEADS, LANES, tq), F32)],
        compiler_params=_cparams(("parallel", "arbitrary")),
        name="diff_attention",
    )(lam_rows, subln2, gsum, qb, kb, vb_t)


def _mla_kernel(q_ref, k_ref, v_ref, o_ref, acc_ref, *, tq):
    qi = pl.program_id(1)
    row = lax.broadcasted_iota(jnp.int32, (LANES, 1), 0)
    scale = (MLA_NOPE_DIM + MLA_ROPE_DIM) ** -0.5
    problems = [(q_ref[0, :, hd * LANES:(hd + 1) * LANES], hd * LANES, (hd // 2) * LANES)
                for hd in range(MLA_HEADS)]
    outs = _causal_flash(problems, k_ref.at[0], v_ref, acc_ref, qi, tq, scale)
    for pair in range(MLA_HEADS // 2):
        o_t = jnp.where(row < MLA_V_DIM, outs[2 * pair], outs[2 * pair + 1])
        o_ref[0, :, pair * LANES:(pair + 1) * LANES] = o_t.T.astype(o_ref.dtype)


def _mla_attention(qc, kc, vc_t, tq):
    b, s, wq = qc.shape
    wv = vc_t.shape[1]
    return pl.pallas_call(
        functools.partial(_mla_kernel, tq=tq),
        grid=(b, s // tq),
        in_specs=[pl.BlockSpec((1, tq, wq), lambda i, j: (i, j, 0)),
                  pl.BlockSpec((1, s, wq), lambda i, j: (i, 0, 0)),
                  pl.BlockSpec((s // tq, wv, tq), lambda i, j: (i, 0, 0))],
        out_specs=pl.BlockSpec((1, tq, wv), lambda i, j: (i, j, 0)),
        out_shape=jax.ShapeDtypeStruct((b, s, wv), BF16),
        scratch_shapes=[pltpu.VMEM((MLA_HEADS, LANES, tq), F32)],
        compiler_params=_cparams(("parallel", "arbitrary")),
        name="mla_attention",
    )(qc, kc, vc_t)


KEY_NEG_INF = -2139095041


def _score_keys(score):
    bits = lax.bitcast_convert_type(score, jnp.int32)
    return bits ^ ((bits >> 31) & 0x7FFFFFFF)


def _dsa_kernel(qa_ref, qi_ref, wi_ref, ka_ref, va_ref, ki_ref, o_ref, key_ref, aux_ref, bias_ref, acc_ref,
                *, tq, tk, top_k):
    blk = pl.program_id(1)
    s_len = ka_ref.shape[1]
    n_kv = ((blk + 1) * tq + tk - 1) // tk
    masks = _half_masks(BF16, DSA_IDX_DIM)
    row = lax.broadcasted_iota(jnp.int32, (LANES, 1), 0)
    t_chunk = (blk * tq + lax.broadcasted_iota(jnp.int32, (1, tq), 1)) // CHUNK
    sub_idx = lax.broadcasted_iota(jnp.int32, (tk, 1), 0)

    def tile_start(c):
        return pl.multiple_of(c * tk, tk)

    qi_st = jnp.concatenate([qi_ref[0, :, (hd // 2) * LANES:(hd // 2 + 1) * LANES] * masks[hd % 2]
                             for hd in range(DSA_IDX_HEADS)], axis=0)
    qi_t = _transpose_bf16(qi_st)
    w_t = wi_ref[0].T

    sub = min(tk, SUB_KEYS)
    sub_iota = lax.broadcasted_iota(jnp.int32, (sub, 1), 0)

    def score_body(c, carry):
        for s0 in range(0, tk, sub):
            r0 = pl.multiple_of(c * tk + s0, sub)
            ki = ki_ref[0, pl.ds(r0, sub), :]
            score = jnp.zeros((sub, tq), F32)
            for hd in range(DSA_IDX_HEADS):
                rel = jnp.maximum(_dot(ki, qi_t[:, hd * tq:(hd + 1) * tq]), 0.0)
                score = score + w_t[hd:hd + 1, :] * rel
            score = score * DSA_IDX_SCALE
            score = jnp.where((r0 + sub_iota) // CHUNK <= t_chunk, score, -jnp.inf)
            key_ref[pl.ds(r0, sub), :] = _score_keys(score)
        return carry

    lax.fori_loop(0, n_kv, score_body, 0)

    def count(pred_fn):
        n_part = 8

        def body(c, tot):
            ones = jnp.where(pred_fn(tile_start(c)), 1.0, 0.0)
            return tot + jnp.sum(ones.reshape(tk // (8 * n_part), n_part * 8, tq), axis=0)
        tot = lax.fori_loop(0, n_kv, body, jnp.zeros((n_part * 8, tq), F32))
        return jnp.sum(tot, axis=0, keepdims=True)

    cnt_all = count(lambda r0: key_ref[pl.ds(r0, tk), :] >= INT_MIN)
    cnt_pos = count(lambda r0: key_ref[pl.ds(r0, tk), :] >= 0)
    thr = jnp.where(cnt_pos >= top_k, 0, INT_MIN).astype(jnp.int32)
    cnt_thr = jnp.where(cnt_pos >= top_k, cnt_pos, cnt_all)

    def thr_body(i, carry):
        thr, cnt_thr = carry
        cand = thr | (jnp.int32(1) << (30 - i))
        cnt = count(lambda r0: key_ref[pl.ds(r0, tk), :] >= cand)
        ok = cnt >= top_k
        return jnp.where(ok, cand, thr), jnp.where(ok, cnt, cnt_thr)

    thr, cnt_thr = lax.fori_loop(0, 31, thr_body, (thr, cnt_thr))
    thr = jnp.maximum(thr, KEY_NEG_INF)
    tied = jnp.where(cnt_thr > top_k, jnp.where(thr > KEY_NEG_INF, 1.0, 0.0), 0.0)

    def select_with_ties():
        n_gt = count(lambda r0: key_ref[pl.ds(r0, tk), :] > thr)
        need = top_k - n_gt

        def aux_body(c, carry):
            r0 = tile_start(c)
            idx = r0 + sub_idx
            tie = jnp.where(key_ref[pl.ds(r0, tk), :] == thr, jnp.where(idx // CHUNK <= t_chunk, idx, IDX_BIG), IDX_BIG)
            aux_ref[pl.ds(r0, tk), :] = tie
            return carry

        lax.fori_loop(0, n_kv, aux_body, 0)
        n_bits = max(1, (s_len - 1).bit_length())

        def tie_body(i, last):
            cand = last | (jnp.int32(1) << (n_bits - 1 - i))
            cnt = count(lambda r0: aux_ref[pl.ds(r0, tk), :] < cand)
            return jnp.where(cnt < need, cand, last)

        last = lax.fori_loop(0, n_bits, tie_body, jnp.zeros((1, tq), jnp.int32))

        def bias_body(c, carry):
            r0 = tile_start(c)
            bias_ref[pl.ds(r0, tk), :] = jnp.where(
                key_ref[pl.ds(r0, tk), :] > thr, 0.0, jnp.where(aux_ref[pl.ds(r0, tk), :] <= last, 0.0, -jnp.inf))
            return carry

        lax.fori_loop(0, n_kv, bias_body, 0)

    def select_no_ties():
        low = jnp.where(thr > KEY_NEG_INF, thr - 1, thr)

        def bias_body(c, carry):
            r0 = tile_start(c)
            bias_ref[pl.ds(r0, tk), :] = jnp.where(key_ref[pl.ds(r0, tk), :] > low, 0.0, -jnp.inf)
            return carry

        lax.fori_loop(0, n_kv, bias_body, 0)

    lax.cond(jnp.max(tied) > 0.0, select_with_ties, select_no_ties)

    hmask = _half_masks(BF16, DSA_HEAD_DIM)
    qa_st = jnp.concatenate([qa_ref[0, :, (hd // 2) * LANES:(hd // 2 + 1) * LANES] * hmask[hd % 2]
                             for hd in range(DSA_HEADS)], axis=0)
    qa_t = _transpose_bf16(qa_st)
    scale = DSA_HEAD_DIM ** -0.5
    n_h = DSA_HEADS
    acc_ref[...] = jnp.zeros_like(acc_ref)

    def att_body(c, carry):
        ms, ls = list(carry[:n_h]), list(carry[n_h:])
        for s0 in range(0, tk, sub):
            r0 = pl.multiple_of(c * tk + s0, sub)
            bias = bias_ref[pl.ds(r0, sub), :]
            ka = ka_ref[0, pl.ds(r0, sub), :]
            for hd in range(n_h):
                st = _dot(ka, qa_t[:, hd * tq:(hd + 1) * tq]) + bias
                ms[hd], ls[hd] = _softmax_step(st, va_ref[c, :, s0:s0 + sub], ms[hd], ls[hd], acc_ref.at[hd],
                                               scale, guard_empty=True)
        return tuple(ms) + tuple(ls)

    init = tuple(jnp.full((1, tq), -jnp.inf, F32) for _ in range(n_h)) + \
        tuple(jnp.zeros((1, tq), F32) for _ in range(n_h))
    carry = lax.fori_loop(0, n_kv, att_body, init)
    for p in range(n_h // 2):
        pair_t = jnp.where(row < DSA_HEAD_DIM, acc_ref[2 * p] / carry[n_h + 2 * p],
                           acc_ref[2 * p + 1] / carry[n_h + 2 * p + 1])
        o_ref[0, :, p * LANES:(p + 1) * LANES] = pair_t.T.astype(o_ref.dtype)


def _dsa_attention(qa, qi, wi, ka2, va2_t, ki2, tq, tk):
    b, s, _ = qa.shape
    top_k = min(DSA_TOPK_MAX, s // 4)
    qspec = lambda w: pl.BlockSpec((1, tq, w), lambda i, j: (i, j, 0))
    kspec = lambda w: pl.BlockSpec((1, s, w), lambda i, j: (i, 0, 0))
    return pl.pallas_call(
        functools.partial(_dsa_kernel, tq=tq, tk=tk, top_k=top_k),
        grid=(b, s // tq),
        in_specs=[qspec(qa.shape[2]), qspec(qi.shape[2]), qspec(LANES), kspec(LANES),
                  pl.BlockSpec((s // tk, LANES, tk), lambda i, j: (i, 0, 0)), kspec(LANES)],
        out_specs=qspec(qa.shape[2]),
        out_shape=jax.ShapeDtypeStruct(qa.shape, BF16),
        scratch_shapes=[pltpu.VMEM((s, tq), jnp.int32), pltpu.VMEM((s, tq), jnp.int32), pltpu.VMEM((s, tq), F32),
                        pltpu.VMEM((DSA_HEADS, LANES, tq), F32)],
        compiler_params=_cparams(("parallel", "arbitrary")),
        name="dsa_attention",
    )(qa, qi, wi, ka2, va2_t, ki2)


def _attn_residual(h_ref, oa_ref, ob_ref, oc_ref, wo_ref):
    wa, wb = oa_ref.shape[1], ob_ref.shape[1]
    return (h_ref[...] + _dot(oa_ref[...], wo_ref[0:wa, :]) + _dot(ob_ref[...], wo_ref[wa:wa + wb, :])
            + _dot(oc_ref[...], wo_ref[wa + wb:, :]))


def _swiglu_partial(x, wg, wu, wd):
    g = _dot(x, wg)
    u = _dot(x, wu)
    return _dot(((g * jax.nn.sigmoid(g)) * u).astype(BF16), wd)


def _dense_block_kernel(h_ref, oa_ref, ob_ref, oc_ref, wo_ref, g_ref, wg_ref, wu_ref, wd_ref, fg_ref,
                        out_ref, xn_ref, acc_ref, *, final_norm):
    f = pl.program_id(1)

    @pl.when(f == 0)
    def _():
        h1 = _attn_residual(h_ref, oa_ref, ob_ref, oc_ref, wo_ref)
        acc_ref[...] = h1
        xn_ref[...] = _rms(h1, g_ref[...]).astype(BF16)

    acc_ref[...] += _swiglu_partial(xn_ref[...], wg_ref[...], wu_ref[...], wd_ref[...])

    @pl.when(f == pl.num_programs(1) - 1)
    def _():
        y = acc_ref[...]
        out_ref[...] = _rms(y, fg_ref[...]) if final_norm else y


def _dense_block(h, oa, ob, oc, w_out, gain, wg, wu, wd, final_gain, *, final_norm, tm, tf):
    n, d = h.shape
    dff = wg.shape[1]
    const = lambda a: pl.BlockSpec(a.shape, lambda i, f: (0,) * a.ndim, pipeline_mode=pl.Buffered(1))
    row = lambda w: pl.BlockSpec((tm, w), lambda i, f: (i, 0))
    gain = gain.reshape(1, d)
    final_gain = final_gain.reshape(1, d)
    return pl.pallas_call(
        functools.partial(_dense_block_kernel, final_norm=final_norm),
        grid=(n // tm, dff // tf),
        in_specs=[row(d), row(oa.shape[1]), row(ob.shape[1]), row(oc.shape[1]), const(w_out), const(gain),
                  pl.BlockSpec((d, tf), lambda i, f: (0, f)),
                  pl.BlockSpec((d, tf), lambda i, f: (0, f)),
                  pl.BlockSpec((tf, d), lambda i, f: (f, 0)),
                  const(final_gain)],
        out_specs=row(d),
        out_shape=jax.ShapeDtypeStruct((n, d), F32),
        scratch_shapes=[pltpu.VMEM((tm, d), BF16), pltpu.VMEM((tm, d), F32)],
        compiler_params=_cparams(("parallel", "arbitrary")),
        name="block_dense",
    )(h, oa, ob, oc, w_out, gain, wg, wu, wd, final_gain)


def _moe_block_kernel(h_ref, oa_ref, ob_ref, oc_ref, wo_ref, g_ref, r_ref, tri_ref, wg_ref, wu_ref, wd_ref, fg_ref,
                      out_ref, xn_ref, acc_ref, gate_ref, pos_ref, pos_t_ref, cnt_ref, xs_ref, ye_ref,
                      *, n_exp, mains, rows, final_norm):
    e = pl.program_id(1)
    f = pl.program_id(2)
    n_f = pl.num_programs(2)
    tm = xn_ref.shape[0]
    lane = lax.broadcasted_iota(jnp.int32, (1, LANES), 1)

    @pl.when((e == 0) & (f == 0))
    def _():
        h1 = _attn_residual(h_ref, oa_ref, ob_ref, oc_ref, wo_ref)
        acc_ref[...] = h1
        hn = _rms(h1, g_ref[...])
        xn_ref[...] = hn.astype(BF16)
        hn_hi = hn.astype(BF16)
        hn_lo = (hn - hn_hi.astype(F32)).astype(BF16)
        parts = _dot(hn_hi, r_ref[...]) + _dot(hn_lo, r_ref[...])
        logits = parts[:, :LANES] + parts[:, LANES:]
        logits = jnp.where(lane < n_exp, logits, -jnp.inf)
        m1 = jnp.max(logits, axis=1, keepdims=True)
        i1 = jnp.min(jnp.where(logits == m1, lane, IDX_BIG), axis=1, keepdims=True)
        rest = jnp.where(lane == i1, -jnp.inf, logits)
        m2 = jnp.max(rest, axis=1, keepdims=True)
        i2 = jnp.min(jnp.where(rest == m2, lane, IDX_BIG), axis=1, keepdims=True)
        e2 = jnp.exp(m2 - m1)
        den = 1.0 + e2
        gate_ref[...] = jnp.where(lane == i1, 1.0 / den, jnp.where(lane == i2, e2 / den, 0.0))
        routed = jnp.where(lane == i1, 1.0, jnp.where(lane == i2, 1.0, 0.0))
        before = _dot(tri_ref[...], routed.astype(BF16))
        slot = jnp.where(routed > 0.0, before, -1.0)
        pos_ref[...] = slot
        pos_t_ref[...] = slot.T
        cnt_ref[...] = jnp.sum(routed, axis=0, keepdims=True)

    n_tok = jnp.sum(jnp.where(lane == e, cnt_ref[...], 0.0)).astype(jnp.int32)
    top = mains[-1]
    n_extra = (jnp.maximum(n_tok - top, 0) + rows - 1) // rows

    def for_blocks(fn):
        below = 0
        for size in mains:
            fits = (n_tok > below) if size == top else ((n_tok > below) & (n_tok <= size))

            @pl.when(fits)
            def _(size=size):
                fn(0, size)
                if size == top:
                    lax.fori_loop(0, n_extra, lambda r, c: (
                        fn(pl.multiple_of(top + r * rows, math.gcd(top, rows)), rows), c)[1], 0)

            below = size

    @pl.when(f == 0)
    def _():
        slot_row = pos_t_ref[pl.ds(e, 1), :]

        def gather(r0, nr):
            want = (r0 + lax.broadcasted_iota(jnp.int32, (nr, 1), 0)).astype(F32)
            pick = jnp.where(slot_row == want, 1.0, 0.0).astype(BF16)
            xs_ref[pl.ds(r0, nr), :] = _dot(pick, xn_ref[...]).astype(BF16)
            ye_ref[pl.ds(r0, nr), :] = jnp.zeros((nr, ye_ref.shape[1]), F32)

        for_blocks(gather)

    def expert(r0, nr):
        ye_ref[pl.ds(r0, nr), :] += _swiglu_partial(xs_ref[pl.ds(r0, nr), :], wg_ref[0, 0], wu_ref[0, 0], wd_ref[0])

    for_blocks(expert)

    @pl.when(f == n_f - 1)
    def _():
        full_lane = lax.broadcasted_iota(jnp.int32, (tm, LANES), 1)
        slot_col = jnp.sum(jnp.where(full_lane == e, pos_ref[...], 0.0), axis=1, keepdims=True)
        gate_col = jnp.sum(jnp.where(full_lane == e, gate_ref[...], 0.0), axis=1, keepdims=True)

        def scatter(r0, nr):
            have = (r0 + lax.broadcasted_iota(jnp.int32, (1, nr), 1)).astype(F32)
            place = jnp.where(slot_col == have, 1.0, 0.0).astype(BF16)
            y = ye_ref[pl.ds(r0, nr), :]
            y_hi = y.astype(BF16)
            y_lo = (y - y_hi.astype(F32)).astype(BF16)
            acc_ref[...] += gate_col * (_dot(place, y_hi) + _dot(place, y_lo))

        for_blocks(scatter)

    @pl.when((e == n_exp - 1) & (f == n_f - 1))
    def _():
        y = acc_ref[...]
        out_ref[...] = _rms(y, fg_ref[...]) if final_norm else y


def _moe_block(h, oa, ob, oc, w_out, gain, router_p, wg, wu, wd, final_gain, *, final_norm, tm, tf, mains, rows):
    n, d = h.shape
    n_exp, _, dff = wg.shape
    const = lambda a: pl.BlockSpec(a.shape, lambda i, e, f: (0,) * a.ndim, pipeline_mode=pl.Buffered(1))
    row = lambda w: pl.BlockSpec((tm, w), lambda i, e, f: (i, 0))
    row_in = lambda w: pl.BlockSpec((tm, w), lambda i, e, f: (i, 0), pipeline_mode=pl.Buffered(1))
    gain = gain.reshape(1, d)
    final_gain = final_gain.reshape(1, d)
    tok = np.arange(tm)
    tri = jnp.asarray(tok[None, :] < tok[:, None], dtype=BF16)
    mains = tuple(sorted({min(m, tm) for m in mains}))
    n_f = dff // tf
    wg, wu = (w.reshape(n_exp, d, n_f, tf).transpose(0, 2, 1, 3) for w in (wg, wu))
    cap = mains[-1] + -(-(tm - mains[-1]) // rows) * rows
    return pl.pallas_call(
        functools.partial(_moe_block_kernel, n_exp=n_exp, mains=mains, rows=rows, final_norm=final_norm),
        grid=(n // tm, n_exp, n_f),
        in_specs=[row_in(d), row_in(oa.shape[1]), row_in(ob.shape[1]), row_in(oc.shape[1]), const(w_out), const(gain),
                  const(router_p), const(tri),
                  pl.BlockSpec((1, 1, d, tf), lambda i, e, f: (e, f, 0, 0)),
                  pl.BlockSpec((1, 1, d, tf), lambda i, e, f: (e, f, 0, 0)),
                  pl.BlockSpec((1, tf, d), lambda i, e, f: (e, f, 0)),
                  const(final_gain)],
        out_specs=row(d),
        out_shape=jax.ShapeDtypeStruct((n, d), F32),
        scratch_shapes=[pltpu.VMEM((tm, d), BF16), pltpu.VMEM((tm, d), F32), pltpu.VMEM((tm, LANES), F32),
                        pltpu.VMEM((tm, LANES), F32), pltpu.VMEM((LANES, tm), F32), pltpu.VMEM((1, LANES), F32),
                        pltpu.VMEM((cap, d), BF16), pltpu.VMEM((cap, d), F32)],
        compiler_params=_cparams(("parallel", "arbitrary", "arbitrary")),
        name="block_moe",
    )(h, oa, ob, oc, w_out, gain, router_p, tri, wg, wu, wd, final_gain)


def _pick(n, pref):
    t = min(pref, n)
    while n % t:
        t //= 2
    return t


def kernel(x, positions, attn_norm, w_in, mla_q_norm, w_uq, mla_kv_norm, w_ukv, diff_lambda_q1, diff_lambda_k1, diff_lambda_q2, diff_lambda_k2, diff_subln, w_out, ffn_norm, dense_w_gate, dense_w_up, dense_w_down, moe_router, moe_w_gate, moe_w_up, moe_w_down, final_norm):
    b, s, d = x.shape
    n = b * s
    depth = w_in.shape[0]
    tm_proj = _pick(n, 512)
    tm_blk = _pick(n, 1024)
    tk = _pick(s, KV_TILE)
    tq_dsa = _pick(s, 256)

    tables = _rope_tables(positions.astype(F32).reshape(n, 1), _pick(n, 1024))
    h = x.reshape(n, d)
    r3 = lambda a: a.reshape(b, s, a.shape[-1])
    for layer in range(depth):
        w_p, uq, ukn, uv = _prep_proj_weights(w_in[layer], w_uq[layer], w_ukv[layer])
        (qa, ka2, qi, ki2, qb, kb, va2, wi, vb, qc, kc, vc) = _project(
            h, attn_norm[layer], tables, w_p, mla_q_norm[layer], uq, mla_kv_norm[layer], ukn, uv, tm_proj, tk)
        oa = _dsa_attention(r3(qa), r3(qi), r3(wi), r3(ka2), va2, r3(ki2), tq_dsa, tk)
        lam_rows = jnp.zeros((8, LANES), F32).at[0:4, 0:DIFF_QK_DIM].set(jnp.stack(
            [diff_lambda_q1[layer], diff_lambda_k1[layer], diff_lambda_q2[layer], diff_lambda_k2[layer]]))
        ob = _diff_attention(r3(qb), r3(kb), vb, lam_rows, diff_subln[layer], layer, tk)
        oc = _mla_attention(r3(qc), r3(kc), vc, tk)
        j = layer // 2
        last = layer == depth - 1
        wo = w_out[layer].astype(BF16)
        attn = (oa.reshape(n, -1), ob.reshape(n, -1), oc.reshape(n, -1))
        if layer % 2 == 0:
            h = _dense_block(h, *attn, wo, ffn_norm[layer], dense_w_gate[j].astype(BF16), dense_w_up[j].astype(BF16),
                             dense_w_down[j].astype(BF16), final_norm,
                             final_norm=last, tm=tm_blk, tf=_pick(dense_w_gate.shape[2], FFN_TILE))
        else:
            router_f = jnp.pad(moe_router[j], ((0, 0), (0, LANES - MOE_EXPERTS)))
            router_hi = router_f.astype(BF16)
            router_p = jnp.concatenate([router_hi, (router_f - router_hi.astype(F32)).astype(BF16)], axis=1)
            h = _moe_block(h, *attn, wo, ffn_norm[layer], router_p,
                           moe_w_gate[j].astype(BF16), moe_w_up[j].astype(BF16), moe_w_down[j].astype(BF16), final_norm,
                           final_norm=last, tm=tm_blk, tf=_pick(moe_w_gate.shape[3], FFN_TILE),
                           mains=MOE_MAIN_ROWS, rows=MOE_EXTRA_ROWS)
    return h.reshape(b, s, d)
```

```python
import functools
import math

import jax
import jax.numpy as jnp
import numpy as np
from jax import lax
from jax.experimental import pallas as pl
from jax.experimental.pallas import tpu as pltpu

F32 = jnp.float32
BF16 = jnp.bfloat16

LANES = 128
MXU_COLS = 256
VMEM_LIMIT_BYTES = 56 * 1024 * 1024

D_MODEL = 1024
CHUNK = 64
ROPE_THETA = 500000.0
NORM_EPS = 1e-6
ROPE_FRACTION_DEN = 4

DSA_HEADS = 4
DSA_HEAD_DIM = 64
DSA_IDX_HEADS = 8
DSA_IDX_DIM = 64
DSA_TOPK_MAX = 256
DSA_IDX_SCALE = (DSA_IDX_HEADS * DSA_IDX_DIM) ** -0.5

DIFF_HEADS = 4
DIFF_QK_DIM = 32
DIFF_V_DIM = 2 * DIFF_QK_DIM

MLA_HEADS = 8
MLA_Q_LORA = 256
MLA_KV_LORA = 128
MLA_NOPE_DIM = 64
MLA_ROPE_DIM = 32
MLA_V_DIM = 64

MOE_EXPERTS = 8
MOE_TOP_K = 2
FFN_TILE = 896
MOE_MAIN_ROWS = (256, 288, 320)
MOE_EXTRA_ROWS = 128

IN_SPLITS = (
    DSA_HEADS * DSA_HEAD_DIM, DSA_HEAD_DIM, DSA_HEAD_DIM, DSA_IDX_HEADS * DSA_IDX_DIM, DSA_IDX_DIM,
    DSA_IDX_HEADS, DIFF_HEADS * 2 * DIFF_QK_DIM, DIFF_HEADS * 2 * DIFF_QK_DIM, DIFF_HEADS * DIFF_V_DIM,
    MLA_Q_LORA, MLA_KV_LORA, MLA_ROPE_DIM,
)

INT_MIN = -(2 ** 31)
IDX_BIG = 2 ** 30


def _cparams(sem):
    return pltpu.CompilerParams(dimension_semantics=sem, vmem_limit_bytes=VMEM_LIMIT_BYTES)


def _rms(x, g):
    return x * lax.rsqrt(jnp.mean(x * x, axis=-1, keepdims=True) + NORM_EPS) * g


def _dot(a, b):
    return jnp.dot(a, b, preferred_element_type=F32)


def _transpose_bf16(x):
    return x.astype(F32).T.astype(BF16)


def _split3(x):
    hi = x.astype(BF16)
    r1 = x - hi.astype(F32)
    mid = r1.astype(BF16)
    lo = (r1 - mid.astype(F32)).astype(BF16)
    return hi, mid, lo


def _dot_f32_by_exact(x, m_bf16):
    hi, mid, lo = _split3(x)
    return _dot(hi, m_bf16) + _dot(mid, m_bf16) + _dot(lo, m_bf16)


def _inv_freq(rot_dim):
    half = rot_dim // 2
    return ROPE_THETA ** (-(jnp.arange(half, dtype=F32) * 2.0 / rot_dim))


def _rope_patterns():
    lane = np.arange(LANES)
    rots = (DSA_HEAD_DIM // ROPE_FRACTION_DEN, DIFF_QK_DIM // ROPE_FRACTION_DEN, MLA_ROPE_DIM)
    offs = (lane % DSA_HEAD_DIM, lane % DIFF_QK_DIM, lane - MLA_NOPE_DIM)
    zero_lane = LANES - 1
    freq = jnp.zeros((LANES,), F32)
    signs, expand = [], np.zeros((3, LANES, LANES), np.float32)
    base = 0
    for p, (rot, off) in enumerate(zip(rots, offs)):
        half = rot // 2
        active = (off >= 0) & (off < rot)
        freq = freq.at[base:base + half].set(_inv_freq(rot))
        src = np.where(active, base + np.clip(off, 0, rot - 1) % half, zero_lane)
        expand[p, src, lane] = 1.0
        signs.append(np.where(active, np.where(off < half, -1.0, 1.0), 0.0))
        base += half
    assert base < zero_lane
    rows = jnp.concatenate([freq[None, :], jnp.asarray(np.stack(signs), dtype=F32), jnp.zeros((4, LANES), F32)], axis=0)
    return rows, jnp.asarray(expand, dtype=BF16)


ROPE_HALF = (DSA_HEAD_DIM // ROPE_FRACTION_DEN // 2, DIFF_QK_DIM // ROPE_FRACTION_DEN // 2, MLA_ROPE_DIM // 2)


def _x1_mask(pattern):
    lane = lax.broadcasted_iota(jnp.int32, (1, LANES), 1)
    if pattern == 0:
        return (lane % DSA_HEAD_DIM) < ROPE_HALF[0]
    if pattern == 1:
        return (lane % DIFF_QK_DIM) < ROPE_HALF[1]
    return (lane >= MLA_NOPE_DIM) & (lane < MLA_NOPE_DIM + ROPE_HALF[2])


def _rope_tables_kernel(pos_ref, rows_ref, expand_ref, out_ref):
    ang = pos_ref[...] * rows_ref[0:1, :]
    cos, sin = jnp.cos(ang), jnp.sin(ang)
    for p in range(3):
        out_ref[2 * p] = _dot_f32_by_exact(cos, expand_ref[p])
        out_ref[2 * p + 1] = _dot_f32_by_exact(sin, expand_ref[p]) * rows_ref[1 + p:2 + p, :]


def _rope_tables(pos_f, tm):
    n = pos_f.shape[0]
    rows, expand = _rope_patterns()
    return pl.pallas_call(
        _rope_tables_kernel,
        grid=(n // tm,),
        in_specs=[pl.BlockSpec((tm, 1), lambda i: (i, 0)),
                  pl.BlockSpec((8, LANES), lambda i: (0, 0)),
                  pl.BlockSpec((3, LANES, LANES), lambda i: (0, 0, 0))],
        out_specs=pl.BlockSpec((6, tm, LANES), lambda i: (0, i, 0)),
        out_shape=jax.ShapeDtypeStruct((6, n, LANES), F32),
        compiler_params=_cparams(("parallel",)),
        name="rope_tables",
    )(pos_f, rows, expand)


def _rope128(y, cos, sin, pattern):
    half = ROPE_HALF[pattern]
    up = pltpu.roll(y, LANES - half, 1)
    dn = pltpu.roll(y, half, 1)
    return y * cos + jnp.where(_x1_mask(pattern), up, dn) * sin


PROJ_COLS = (
    ("qa", 256, 0), ("ka2", 128, 0), ("qi", 512, 0), ("ki2", 128, 0),
    ("qb", 256, 1), ("kb", 256, 1),
    ("va2", 128, None), ("wi", 128, None), ("vb", 256, None),
    ("cq", 256, None), ("ckv", 128, None), ("kr", 128, 2),
)
PROJ_WIDTH = sum(c[1] for c in PROJ_COLS)
PROJ_OUTS = (("qa", 256, BF16), ("ka2", 128, BF16), ("qi", 512, BF16), ("ki2", 128, BF16),
             ("qb", 256, BF16), ("kb", 256, BF16), ("va2", 128, BF16), ("wi", 128, F32),
             ("vb", 256, BF16), ("qc", 1024, BF16), ("kc", 1024, BF16), ("vc", 512, BF16))
PROJ_TRANSPOSED = ("va2", "vb", "vc")
KV_TILE = 256
SUB_KEYS = 128


def _prep_proj_weights(w_in, w_uq, w_ukv):
    offs = np.cumsum((0,) + IN_SPLITS)
    (q_a, k_a, v_a, q_i, k_i, w_i, q_b, k_b, v_b, c_q, c_kv, k_r) = [
        w_in[:, offs[j]:offs[j + 1]] for j in range(len(IN_SPLITS))]
    d = w_in.shape[0]
    z = lambda n: jnp.zeros((d, n), w_in.dtype)
    cols = {
        "qa": q_a, "ka2": jnp.concatenate([k_a, k_a], 1), "qi": q_i, "ki2": jnp.concatenate([k_i, k_i], 1),
        "qb": q_b, "kb": k_b, "va2": jnp.concatenate([v_a, v_a], 1),
        "wi": jnp.concatenate([w_i, z(LANES - DSA_IDX_HEADS)], 1), "vb": v_b, "cq": c_q, "ckv": c_kv,
        "kr": jnp.concatenate([z(MLA_NOPE_DIM), k_r, z(LANES - MLA_NOPE_DIM - MLA_ROPE_DIM)], 1),
    }
    w_p = jnp.concatenate([cols[name] for name, _, _ in PROJ_COLS], axis=1).astype(BF16)
    qd = MLA_NOPE_DIM + MLA_ROPE_DIM
    uq = w_uq.reshape(MLA_Q_LORA, MLA_HEADS, qd)
    uq = jnp.pad(uq, ((0, 0), (0, 0), (0, LANES - qd))).reshape(MLA_Q_LORA, MLA_HEADS * LANES).astype(BF16)
    ukv = w_ukv.reshape(MLA_KV_LORA, MLA_HEADS, MLA_NOPE_DIM + MLA_V_DIM)
    ukn = jnp.pad(ukv[:, :, :MLA_NOPE_DIM], ((0, 0), (0, 0), (0, LANES - MLA_NOPE_DIM)))
    ukn = ukn.reshape(MLA_KV_LORA, MLA_HEADS * LANES).astype(BF16)
    uv = ukv[:, :, MLA_NOPE_DIM:].reshape(MLA_KV_LORA, MLA_HEADS * MLA_V_DIM).astype(BF16)
    return w_p, uq, ukn, uv


def _proj_kernel(h_ref, g_ref, tab_ref, w_ref, qn_ref, uq_ref, kvn_ref, ukn_ref, uv_ref, *out_refs):
    outs = {name: ref for (name, _, _), ref in zip(PROJ_OUTS, out_refs)}
    xn = _rms(h_ref[...], g_ref[...]).astype(BF16)

    def roped(y, pattern):
        return _rope128(y, tab_ref[2 * pattern], tab_ref[2 * pattern + 1], pattern)

    def emit(name, s, y):
        ref = outs[name]
        if name in PROJ_TRANSPOSED:
            tk = ref.shape[2]
            for t in range(ref.shape[0]):
                ref[t, s * LANES:(s + 1) * LANES, :] = y[t * tk:(t + 1) * tk].T.astype(ref.dtype)
        else:
            ref[:, s * LANES:(s + 1) * LANES] = y.astype(ref.dtype)

    def wide_dot(x, w, n_slices):
        per = MXU_COLS // LANES
        res = []
        for c in range(0, n_slices, per):
            y = _dot(x, w[:, c * LANES:(c + per) * LANES])
            res += [y[:, k * LANES:(k + 1) * LANES] for k in range(min(per, n_slices - c))]
        return res

    slices = wide_dot(xn, w_ref, PROJ_WIDTH // LANES)
    vals = {}
    i = 0
    for name, width, pattern in PROJ_COLS:
        for s in range(width // LANES):
            y = slices[i] if pattern is None else roped(slices[i], pattern)
            i += 1
            if name in outs:
                emit(name, s, y)
            else:
                vals.setdefault(name, []).append(y)

    cq = jnp.concatenate(vals["cq"], axis=1)
    cqn = _rms(cq, qn_ref[...]).astype(BF16)
    for hd, y in enumerate(wide_dot(cqn, uq_ref, MLA_HEADS)):
        outs["qc"][:, hd * LANES:(hd + 1) * LANES] = roped(y, 2).astype(BF16)

    ckvn = _rms(vals["ckv"][0], kvn_ref[...]).astype(BF16)
    kr = vals["kr"][0]
    for hd, y in enumerate(wide_dot(ckvn, ukn_ref, MLA_HEADS)):
        outs["kc"][:, hd * LANES:(hd + 1) * LANES] = (y + kr).astype(BF16)
    for s, y in enumerate(wide_dot(ckvn, uv_ref, MLA_HEADS * MLA_V_DIM // LANES)):
        emit("vc", s, y)


def _project(h, gain, tables, w_p, q_norm, uq, kv_norm, ukn, uv, tm, tk):
    n, d = h.shape
    full = lambda a: pl.BlockSpec(a.shape, lambda i: (0,) * a.ndim)
    gain = gain.reshape(1, d)
    q_norm = q_norm.reshape(1, -1)
    kv_norm = kv_norm.reshape(1, -1)
    out_specs, out_shape = [], []
    for name, w, dt in PROJ_OUTS:
        if name in PROJ_TRANSPOSED:
            out_specs.append(pl.BlockSpec((tm // tk, w, tk), lambda i: (i, 0, 0)))
            out_shape.append(jax.ShapeDtypeStruct((n // tk, w, tk), dt))
        else:
            out_specs.append(pl.BlockSpec((tm, w), lambda i: (i, 0)))
            out_shape.append(jax.ShapeDtypeStruct((n, w), dt))
    return pl.pallas_call(
        _proj_kernel,
        grid=(n // tm,),
        in_specs=[pl.BlockSpec((tm, d), lambda i: (i, 0)), full(gain),
                  pl.BlockSpec((6, tm, LANES), lambda i: (0, i, 0)),
                  full(w_p), full(q_norm), full(uq), full(kv_norm), full(ukn), full(uv)],
        out_specs=out_specs,
        out_shape=out_shape,
        compiler_params=_cparams(("parallel",)),
        name="projection",
    )(h, gain, tables, w_p, q_norm, uq, kv_norm, ukn, uv)


def _half_masks(dtype, group):
    lane = lax.broadcasted_iota(jnp.int32, (1, LANES), 1)
    return [jnp.where((lane // group) == u, 1.0, 0.0).astype(dtype) for u in range(LANES // group)]


LOG2E = math.log2(math.e)


def _softmax_step(st, v_t, m, l, acc_ref, scale, guard_empty=False):
    c = scale * LOG2E
    m_new = jnp.maximum(m, jnp.max(st, axis=0, keepdims=True))
    m_use = jnp.where(m_new == -jnp.inf, 0.0, m_new) if guard_empty else m_new
    alpha = jnp.exp2((m - m_use) * c)
    p = jnp.exp2((st - m_use) * c)
    l_new = alpha * l + jnp.sum(p, axis=0, keepdims=True)
    acc_ref[...] = alpha * acc_ref[...] + _dot(v_t, p.astype(BF16))
    return m_new, l_new


def _diag_mask(tk, m_cols, tq):
    key_chunk = lax.broadcasted_iota(jnp.int32, (tk, 1), 0) // CHUNK
    qry_chunk = (lax.broadcasted_iota(jnp.int32, (1, m_cols), 1) % tq) // CHUNK
    return key_chunk <= qry_chunk


def _causal_flash(problems, k_ref, v_ref, acc_ref, qi, tq, scale):
    problems = [(_transpose_bf16(q_st), k_lane0, v_row0) for q_st, k_lane0, v_row0 in problems]
    m_cols = problems[0][0].shape[1]
    n_p = len(problems)
    acc_ref[...] = jnp.zeros_like(acc_ref)
    init = tuple(jnp.full((1, m_cols), -jnp.inf, F32) for _ in range(n_p)) + \
        tuple(jnp.zeros((1, m_cols), F32) for _ in range(n_p))

    sub = min(tq, SUB_KEYS)

    def step(j, carry, mask):
        ms, ls = list(carry[:n_p]), list(carry[n_p:])
        for s0 in range(0, tq, sub):
            r0 = pl.multiple_of(j * tq + s0, sub)
            for p, (q_t, k_lane0, v_row0) in enumerate(problems):
                st = _dot(k_ref[pl.ds(r0, sub), k_lane0:k_lane0 + LANES], q_t)
                if mask is not None:
                    st = jnp.where(mask[s0:s0 + sub], st, -jnp.inf)
                ms[p], ls[p] = _softmax_step(st, v_ref[j, v_row0:v_row0 + LANES, s0:s0 + sub], ms[p], ls[p],
                                             acc_ref.at[p], scale)
        return tuple(ms) + tuple(ls)

    carry = lax.fori_loop(0, qi, lambda j, c: step(j, c, None), init)
    carry = step(qi, carry, _diag_mask(tq, m_cols, tq))
    return [acc_ref[p] / carry[n_p + p] for p in range(n_p)]


def _diff_kernel(lam_ref, subln_ref, gsum_ref, q_ref, k_ref, v_ref, o_ref, acc_ref, *, tq, lam_init):
    qi = pl.program_id(1)
    lam_rows = lam_ref[...]
    s1 = jnp.sum(lam_rows[0:1] * lam_rows[1:2], axis=1, keepdims=True)
    s2 = jnp.sum(lam_rows[2:3] * lam_rows[3:4], axis=1, keepdims=True)
    lam = jnp.exp(s1) - jnp.exp(s2) + lam_init
    masks = _half_masks(BF16, DIFF_QK_DIM)
    row = lax.broadcasted_iota(jnp.int32, (LANES, 1), 0)
    n_slices = DIFF_HEADS // 2
    problems = []
    for s in range(n_slices):
        q = q_ref[0, :, s * LANES:(s + 1) * LANES]
        problems += [(q * mk, s * LANES, s * LANES) for mk in masks]
    outs = _causal_flash(problems, k_ref.at[0], v_ref, acc_ref, qi, tq, DIFF_QK_DIM ** -0.5)
    for s in range(n_slices):
        o0, o1, o2, o3 = outs[4 * s:4 * s + 4]
        a0 = o0 - lam * o1
        a1 = o2 - lam * o3
        a = jnp.where(row < DIFF_V_DIM, a0, a1).T
        ss = _dot_f32_by_exact(a * a, gsum_ref[...])
        y = a * lax.rsqrt(ss * (1.0 / DIFF_V_DIM) + NORM_EPS) * subln_ref[...]
        o_ref[0, :, s * LANES:(s + 1) * LANES] = (y * (1.0 - lam_init)).astype(o_ref.dtype)


def _diff_attention(qb, kb, vb_t, lam_rows, subln, layer, tq):
    b, s, w = qb.shape
    lam_init = 0.8 - 0.6 * math.exp(-0.3 * layer)
    subln2 = jnp.concatenate([subln, subln]).reshape(1, LANES).astype(F32)
    lane = np.arange(LANES)
    gsum = jnp.asarray((lane[:, None] // DIFF_V_DIM) == (lane[None, :] // DIFF_V_DIM), dtype=BF16)
    return pl.pallas_call(
        functools.partial(_diff_kernel, tq=tq, lam_init=lam_init),
        grid=(b, s // tq),
        in_specs=[pl.BlockSpec((8, LANES), lambda i, j: (0, 0)),
                  pl.BlockSpec((1, LANES), lambda i, j: (0, 0)),
                  pl.BlockSpec((LANES, LANES), lambda i, j: (0, 0)),
                  pl.BlockSpec((1, tq, w), lambda i, j: (i, j, 0)),
                  pl.BlockSpec((1, s, w), lambda i, j: (i, 0, 0)),
                  pl.BlockSpec((s // tq, w, tq), lambda i, j: (i, 0, 0))],
        out_specs=pl.BlockSpec((1, tq, w), lambda i, j: (i, j, 0)),
        out_shape=jax.ShapeDtypeStruct((b, s, w), BF16),
        scratch_shapes=[pltpu.VMEM((2 * DIFF_HEADS, LANES, tq), F32)],
        compiler_params=_cparams(("parallel", "arbitrary")),
        name="diff_attention",
    )(lam_rows, subln2, gsum, qb, kb, vb_t)


def _mla_kernel(q_ref, k_ref, v_ref, o_ref, acc_ref, *, tq):
    qi = pl.program_id(1)
    row = lax.broadcasted_iota(jnp.int32, (LANES, 1), 0)
    scale = (MLA_NOPE_DIM + MLA_ROPE_DIM) ** -0.5
    problems = [(q_ref[0, :, hd * LANES:(hd + 1) * LANES], hd * LANES, (hd // 2) * LANES)
                for hd in range(MLA_HEADS)]
    outs = _causal_flash(problems, k_ref.at[0], v_ref, acc_ref, qi, tq, scale)
    for pair in range(MLA_HEADS // 2):
        o_t = jnp.where(row < MLA_V_DIM, outs[2 * pair], outs[2 * pair + 1])
        o_ref[0, :, pair * LANES:(pair + 1) * LANES] = o_t.T.astype(o_ref.dtype)


def _mla_attention(qc, kc, vc_t, tq):
    b, s, wq = qc.shape
    wv = vc_t.shape[1]
    return pl.pallas_call(
        functools.partial(_mla_kernel, tq=tq),
        grid=(b, s // tq),
        in_specs=[pl.BlockSpec((1, tq, wq), lambda i, j: (i, j, 0)),
                  pl.BlockSpec((1, s, wq), lambda i, j: (i, 0, 0)),
                  pl.BlockSpec((s // tq, wv, tq), lambda i, j: (i, 0, 0))],
        out_specs=pl.BlockSpec((1, tq, wv), lambda i, j: (i, j, 0)),
        out_shape=jax.ShapeDtypeStruct((b, s, wv), BF16),
        scratch_shapes=[pltpu.VMEM((MLA_HEADS, LANES, tq), F32)],
        compiler_params=_cparams(("parallel", "arbitrary")),
        name="mla_attention",
    )(qc, kc, vc_t)


KEY_NEG_INF = -2139095041


def _score_keys(score):
    bits = lax.bitcast_convert_type(score, jnp.int32)
    return bits ^ ((bits >> 31) & 0x7FFFFFFF)


def _dsa_kernel(qa_ref, qi_ref, wi_ref, ka_ref, va_ref, ki_ref, o_ref, key_ref, hi_ref, lo_ref, aux_ref, bias_ref, acc_ref,
                *, tq, tk, top_k):
    blk = pl.program_id(1)
    s_len = ka_ref.shape[1]
    n_kv = ((blk + 1) * tq + tk - 1) // tk
    masks = _half_masks(BF16, DSA_IDX_DIM)
    row = lax.broadcasted_iota(jnp.int32, (LANES, 1), 0)
    t_chunk = (blk * tq + lax.broadcasted_iota(jnp.int32, (1, tq), 1)) // CHUNK
    sub_idx = lax.broadcasted_iota(jnp.int32, (tk, 1), 0)

    def tile_start(c):
        return pl.multiple_of(c * tk, tk)

    qi_st = jnp.concatenate([qi_ref[0, :, (hd // 2) * LANES:(hd // 2 + 1) * LANES] * masks[hd % 2]
                             for hd in range(DSA_IDX_HEADS)], axis=0)
    qi_t = _transpose_bf16(qi_st)
    w_t = wi_ref[0].T

    sub = min(tk, SUB_KEYS)
    sub_iota = lax.broadcasted_iota(jnp.int32, (sub, 1), 0)

    def score_body(c, carry):
        for s0 in range(0, tk, sub):
            r0 = pl.multiple_of(c * tk + s0, sub)
            ki = ki_ref[0, pl.ds(r0, sub), :]
            score = jnp.zeros((sub, tq), F32)
            for hd in range(DSA_IDX_HEADS):
                rel = jnp.maximum(_dot(ki, qi_t[:, hd * tq:(hd + 1) * tq]), 0.0)
                score = score + w_t[hd:hd + 1, :] * rel
            score = score * DSA_IDX_SCALE
            score = jnp.where((r0 + sub_iota) // CHUNK <= t_chunk, score, -jnp.inf)
            key = _score_keys(score)
            key_ref[pl.ds(r0, sub), :] = key
            hi_ref[pl.ds(r0, sub), :] = (key >> 16).astype(jnp.int16)
            lo_ref[pl.ds(r0, sub), :] = ((key & 0xFFFF) - 32768).astype(jnp.int16)
        return carry

    lax.fori_loop(0, n_kv, score_body, 0)

    def count16(ref, cand, strict=False):
        cand16 = cand.astype(jnp.int16)

        def body(c, tot):
            x = ref[pl.ds(tile_start(c), tk), :]
            ones = jnp.where((x > cand16) if strict else (x >= cand16), jnp.bfloat16(1), jnp.bfloat16(0))
            parts = [ones[i * 16:(i + 1) * 16] for i in range(tk // 16)]
            while len(parts) > 1:
                parts = [parts[i] + parts[i + 1] for i in range(0, len(parts), 2)]
            return tot + parts[0].astype(F32)
        tot = lax.fori_loop(0, n_kv, body, jnp.zeros((16, tq), F32))
        return jnp.sum(tot, axis=0, keepdims=True)

    def kth_largest16(ref, need):
        start = jnp.where(count16(ref, jnp.zeros((1, tq), jnp.int32)) >= need, 0, -32768).astype(jnp.int32)

        def body(i, t):
            cand = t | (jnp.int32(1) << (14 - i))
            return jnp.where(count16(ref, cand) >= need, cand, t)
        return lax.fori_loop(0, 15, body, start)

    def count(pred_fn):
        n_part = 8

        def body(c, tot):
            ones = jnp.where(pred_fn(tile_start(c)), 1.0, 0.0)
            return tot + jnp.sum(ones.reshape(tk // (8 * n_part), n_part * 8, tq), axis=0)
        tot = lax.fori_loop(0, n_kv, body, jnp.zeros((n_part * 8, tq), F32))
        return jnp.sum(tot, axis=0, keepdims=True)

    thr_hi = kth_largest16(hi_ref, float(top_k))
    need_lo = top_k - count16(hi_ref, thr_hi, strict=True)
    thr_hi16 = thr_hi.astype(jnp.int16)

    def bucket_body(c, carry):
        r0 = tile_start(c)
        lo_ref[pl.ds(r0, tk), :] = jnp.where(hi_ref[pl.ds(r0, tk), :] == thr_hi16, lo_ref[pl.ds(r0, tk), :],
                                             jnp.int16(-32768))
        return carry

    lax.fori_loop(0, n_kv, bucket_body, 0)
    thr_lo = kth_largest16(lo_ref, need_lo)
    thr = (thr_hi << 16) | ((thr_lo + 32768) & 0xFFFF)
    cnt_thr = count(lambda r0: key_ref[pl.ds(r0, tk), :] >= thr)
    thr = jnp.maximum(thr, KEY_NEG_INF)
    tied = jnp.where(cnt_thr > top_k, jnp.where(thr > KEY_NEG_INF, 1.0, 0.0), 0.0)

    def select_with_ties():
        n_gt = count(lambda r0: key_ref[pl.ds(r0, tk), :] > thr)
        need = top_k - n_gt

        def aux_body(c, carry):
            r0 = tile_start(c)
            idx = r0 + sub_idx
            tie = jnp.where(key_ref[pl.ds(r0, tk), :] == thr, jnp.where(idx // CHUNK <= t_chunk, idx, IDX_BIG), IDX_BIG)
            aux_ref[pl.ds(r0, tk), :] = tie
            return carry

        lax.fori_loop(0, n_kv, aux_body, 0)
        n_bits = max(1, (s_len - 1).bit_length())

        def tie_body(i, last):
            cand = last | (jnp.int32(1) << (n_bits - 1 - i))
            cnt = count(lambda r0: aux_ref[pl.ds(r0, tk), :] < cand)
            return jnp.where(cnt < need, cand, last)

        last = lax.fori_loop(0, n_bits, tie_body, jnp.zeros((1, tq), jnp.int32))

        def bias_body(c, carry):
            r0 = tile_start(c)
            bias_ref[pl.ds(r0, tk), :] = jnp.where(
                key_ref[pl.ds(r0, tk), :] > thr, 0.0, jnp.where(aux_ref[pl.ds(r0, tk), :] <= last, 0.0, -jnp.inf))
            return carry

        lax.fori_loop(0, n_kv, bias_body, 0)

    def select_no_ties():
        low = jnp.where(thr > KEY_NEG_INF, thr - 1, thr)

        def bias_body(c, carry):
            r0 = tile_start(c)
            bias_ref[pl.ds(r0, tk), :] = jnp.where(key_ref[pl.ds(r0, tk), :] > low, 0.0, -jnp.inf)
            return carry

        lax.fori_loop(0, n_kv, bias_body, 0)

    lax.cond(jnp.max(tied) > 0.0, select_with_ties, select_no_ties)

    hmask = _half_masks(BF16, DSA_HEAD_DIM)
    qa_st = jnp.concatenate([qa_ref[0, :, (hd // 2) * LANES:(hd // 2 + 1) * LANES] * hmask[hd % 2]
                             for hd in range(DSA_HEADS)], axis=0)
    qa_t = _transpose_bf16(qa_st)
    scale = DSA_HEAD_DIM ** -0.5
    n_h = DSA_HEADS
    acc_ref[...] = jnp.zeros_like(acc_ref)

    def att_body(c, carry):
        ms, ls = list(carry[:n_h]), list(carry[n_h:])
        for s0 in range(0, tk, sub):
            r0 = pl.multiple_of(c * tk + s0, sub)
            bias = bias_ref[pl.ds(r0, sub), :]
            ka = ka_ref[0, pl.ds(r0, sub), :]
            for hd in range(n_h):
                st = _dot(ka, qa_t[:, hd * tq:(hd + 1) * tq]) + bias
                ms[hd], ls[hd] = _softmax_step(st, va_ref[c, :, s0:s0 + sub], ms[hd], ls[hd], acc_ref.at[hd],
                                               scale, guard_empty=True)
        return tuple(ms) + tuple(ls)

    init = tuple(jnp.full((1, tq), -jnp.inf, F32) for _ in range(n_h)) + \
        tuple(jnp.zeros((1, tq), F32) for _ in range(n_h))
    carry = lax.fori_loop(0, n_kv, att_body, init)
    for p in range(n_h // 2):
        pair_t = jnp.where(row < DSA_HEAD_DIM, acc_ref[2 * p] / carry[n_h + 2 * p],
                           acc_ref[2 * p + 1] / carry[n_h + 2 * p + 1])
        o_ref[0, :, p * LANES:(p + 1) * LANES] = pair_t.T.astype(o_ref.dtype)


def _dsa_attention(qa, qi, wi, ka2, va2_t, ki2, tq, tk):
    b, s, _ = qa.shape
    top_k = min(DSA_TOPK_MAX, s // 4)
    qspec = lambda w: pl.BlockSpec((1, tq, w), lambda i, j: (i, j, 0))
    kspec = lambda w: pl.BlockSpec((1, s, w), lambda i, j: (i, 0, 0))
    return pl.pallas_call(
        functools.partial(_dsa_kernel, tq=tq, tk=tk, top_k=top_k),
        grid=(b, s // tq),
        in_specs=[qspec(qa.shape[2]), qspec(qi.shape[2]), qspec(LANES), kspec(LANES),
                  pl.BlockSpec((s // tk, LANES, tk), lambda i, j: (i, 0, 0)), kspec(LANES)],
        out_specs=qspec(qa.shape[2]),
        out_shape=jax.ShapeDtypeStruct(qa.shape, BF16),
        scratch_shapes=[pltpu.VMEM((s, tq), jnp.int32), pltpu.VMEM((s, tq), jnp.int16), pltpu.VMEM((s, tq), jnp.int16),
                        pltpu.VMEM((s, tq), jnp.int32), pltpu.VMEM((s, tq), F32),
                        pltpu.VMEM((DSA_HEADS, LANES, tq), F32)],
        compiler_params=_cparams(("parallel", "arbitrary")),
        name="dsa_attention",
    )(qa, qi, wi, ka2, va2_t, ki2)


def _attn_residual(h_ref, oa_ref, ob_ref, oc_ref, wo_ref):
    wa, wb = oa_ref.shape[1], ob_ref.shape[1]
    return (h_ref[...] + _dot(oa_ref[...], wo_ref[0:wa, :]) + _dot(ob_ref[...], wo_ref[wa:wa + wb, :])
            + _dot(oc_ref[...], wo_ref[wa + wb:, :]))


def _swiglu_partial(x, wg, wu, wd):
    g = _dot(x, wg)
    u = _dot(x, wu)
    return _dot(((g * jax.nn.sigmoid(g)) * u).astype(BF16), wd)


def _dense_block_kernel(h_ref, oa_ref, ob_ref, oc_ref, wo_ref, g_ref, wg_ref, wu_ref, wd_ref, fg_ref,
                        out_ref, xn_ref, acc_ref, *, final_norm):
    f = pl.program_id(1)

    @pl.when(f == 0)
    def _():
        h1 = _attn_residual(h_ref, oa_ref, ob_ref, oc_ref, wo_ref)
        acc_ref[...] = h1
        xn_ref[...] = _rms(h1, g_ref[...]).astype(BF16)

    acc_ref[...] += _swiglu_partial(xn_ref[...], wg_ref[...], wu_ref[...], wd_ref[...])

    @pl.when(f == pl.num_programs(1) - 1)
    def _():
        y = acc_ref[...]
        out_ref[...] = _rms(y, fg_ref[...]) if final_norm else y


def _dense_block(h, oa, ob, oc, w_out, gain, wg, wu, wd, final_gain, *, final_norm, tm, tf):
    n, d = h.shape
    dff = wg.shape[1]
    const = lambda a: pl.BlockSpec(a.shape, lambda i, f: (0,) * a.ndim, pipeline_mode=pl.Buffered(1))
    row = lambda w: pl.BlockSpec((tm, w), lambda i, f: (i, 0))
    gain = gain.reshape(1, d)
    final_gain = final_gain.reshape(1, d)
    return pl.pallas_call(
        functools.partial(_dense_block_kernel, final_norm=final_norm),
        grid=(n // tm, dff // tf),
        in_specs=[row(d), row(oa.shape[1]), row(ob.shape[1]), row(oc.shape[1]), const(w_out), const(gain),
                  pl.BlockSpec((d, tf), lambda i, f: (0, f)),
                  pl.BlockSpec((d, tf), lambda i, f: (0, f)),
                  pl.BlockSpec((tf, d), lambda i, f: (f, 0)),
                  const(final_gain)],
        out_specs=row(d),
        out_shape=jax.ShapeDtypeStruct((n, d), F32),
        scratch_shapes=[pltpu.VMEM((tm, d), BF16), pltpu.VMEM((tm, d), F32)],
        compiler_params=_cparams(("parallel", "arbitrary")),
        name="block_dense",
    )(h, oa, ob, oc, w_out, gain, wg, wu, wd, final_gain)


def _moe_block_kernel(h_ref, oa_ref, ob_ref, oc_ref, wo_ref, g_ref, r_ref, tri_ref, wg_ref, wu_ref, wd_ref, fg_ref,
                      out_ref, xn_ref, acc_ref, gate_ref, pos_ref, pos_t_ref, cnt_ref, xs_ref, ye_ref,
                      *, n_exp, mains, rows, final_norm):
    e = pl.program_id(1)
    f = pl.program_id(2)
    n_f = pl.num_programs(2)
    tm = xn_ref.shape[0]
    lane = lax.broadcasted_iota(jnp.int32, (1, LANES), 1)

    @pl.when((e == 0) & (f == 0))
    def _():
        h1 = _attn_residual(h_ref, oa_ref, ob_ref, oc_ref, wo_ref)
        acc_ref[...] = h1
        hn = _rms(h1, g_ref[...])
        xn_ref[...] = hn.astype(BF16)
        hn_hi = hn.astype(BF16)
        hn_lo = (hn - hn_hi.astype(F32)).astype(BF16)
        parts = _dot(hn_hi, r_ref[...]) + _dot(hn_lo, r_ref[...])
        logits = parts[:, :LANES] + parts[:, LANES:]
        logits = jnp.where(lane < n_exp, logits, -jnp.inf)
        m1 = jnp.max(logits, axis=1, keepdims=True)
        i1 = jnp.min(jnp.where(logits == m1, lane, IDX_BIG), axis=1, keepdims=True)
        rest = jnp.where(lane == i1, -jnp.inf, logits)
        m2 = jnp.max(rest, axis=1, keepdims=True)
        i2 = jnp.min(jnp.where(rest == m2, lane, IDX_BIG), axis=1, keepdims=True)
        e2 = jnp.exp(m2 - m1)
        den = 1.0 + e2
        gate_ref[...] = jnp.where(lane == i1, 1.0 / den, jnp.where(lane == i2, e2 / den, 0.0))
        routed = jnp.where(lane == i1, 1.0, jnp.where(lane == i2, 1.0, 0.0))
        before = _dot(tri_ref[...], routed.astype(BF16))
        slot = jnp.where(routed > 0.0, before, -1.0)
        pos_ref[...] = slot
        pos_t_ref[...] = slot.T
        cnt_ref[...] = jnp.sum(routed, axis=0, keepdims=True)

    n_tok = jnp.sum(jnp.where(lane == e, cnt_ref[...], 0.0)).astype(jnp.int32)
    top = mains[-1]
    n_extra = (jnp.maximum(n_tok - top, 0) + rows - 1) // rows

    def for_blocks(fn):
        below = 0
        for size in mains:
            fits = (n_tok > below) if size == top else ((n_tok > below) & (n_tok <= size))

            @pl.when(fits)
            def _(size=size):
                fn(0, size)
                if size == top:
                    lax.fori_loop(0, n_extra, lambda r, c: (
                        fn(pl.multiple_of(top + r * rows, math.gcd(top, rows)), rows), c)[1], 0)

            below = size

    @pl.when(f == 0)
    def _():
        slot_row = pos_t_ref[pl.ds(e, 1), :]

        def gather(r0, nr):
            want = (r0 + lax.broadcasted_iota(jnp.int32, (nr, 1), 0)).astype(F32)
            pick = jnp.where(slot_row == want, 1.0, 0.0).astype(BF16)
            xs_ref[pl.ds(r0, nr), :] = _dot(pick, xn_ref[...]).astype(BF16)
            ye_ref[pl.ds(r0, nr), :] = jnp.zeros((nr, ye_ref.shape[1]), F32)

        for_blocks(gather)

    def expert(r0, nr):
        ye_ref[pl.ds(r0, nr), :] += _swiglu_partial(xs_ref[pl.ds(r0, nr), :], wg_ref[0], wu_ref[0], wd_ref[0])

    for_blocks(expert)

    @pl.when(f == n_f - 1)
    def _():
        full_lane = lax.broadcasted_iota(jnp.int32, (tm, LANES), 1)
        slot_col = jnp.sum(jnp.where(full_lane == e, pos_ref[...], 0.0), axis=1, keepdims=True)
        gate_col = jnp.sum(jnp.where(full_lane == e, gate_ref[...], 0.0), axis=1, keepdims=True)

        def scatter(r0, nr):
            have = (r0 + lax.broadcasted_iota(jnp.int32, (1, nr), 1)).astype(F32)
            place = jnp.where(slot_col == have, 1.0, 0.0).astype(BF16)
            y = ye_ref[pl.ds(r0, nr), :]
            y_hi = y.astype(BF16)
            y_lo = (y - y_hi.astype(F32)).astype(BF16)
            acc_ref[...] += gate_col * (_dot(place, y_hi) + _dot(place, y_lo))

        for_blocks(scatter)

    @pl.when((e == n_exp - 1) & (f == n_f - 1))
    def _():
        y = acc_ref[...]
        out_ref[...] = _rms(y, fg_ref[...]) if final_norm else y


def _moe_block(h, oa, ob, oc, w_out, gain, router_p, wg, wu, wd, final_gain, *, final_norm, tm, tf, mains, rows):
    n, d = h.shape
    n_exp, _, dff = wg.shape
    const = lambda a: pl.BlockSpec(a.shape, lambda i, e, f: (0,) * a.ndim, pipeline_mode=pl.Buffered(1))
    row = lambda w: pl.BlockSpec((tm, w), lambda i, e, f: (i, 0))
    row_in = lambda w: pl.BlockSpec((tm, w), lambda i, e, f: (i, 0), pipeline_mode=pl.Buffered(1))
    gain = gain.reshape(1, d)
    final_gain = final_gain.reshape(1, d)
    tok = np.arange(tm)
    tri = jnp.asarray(tok[None, :] < tok[:, None], dtype=BF16)
    mains = tuple(sorted({min(m, tm) for m in mains}))
    cap = mains[-1] + -(-(tm - mains[-1]) // rows) * rows
    return pl.pallas_call(
        functools.partial(_moe_block_kernel, n_exp=n_exp, mains=mains, rows=rows, final_norm=final_norm),
        grid=(n // tm, n_exp, dff // tf),
        in_specs=[row_in(d), row_in(oa.shape[1]), row_in(ob.shape[1]), row_in(oc.shape[1]), const(w_out), const(gain),
                  const(router_p), const(tri),
                  pl.BlockSpec((1, d, tf), lambda i, e, f: (e, 0, f)),
                  pl.BlockSpec((1, d, tf), lambda i, e, f: (e, 0, f)),
                  pl.BlockSpec((1, tf, d), lambda i, e, f: (e, f, 0)),
                  const(final_gain)],
        out_specs=row(d),
        out_shape=jax.ShapeDtypeStruct((n, d), F32),
        scratch_shapes=[pltpu.VMEM((tm, d), BF16), pltpu.VMEM((tm, d), F32), pltpu.VMEM((tm, LANES), F32),
                        pltpu.VMEM((tm, LANES), F32), pltpu.VMEM((LANES, tm), F32), pltpu.VMEM((1, LANES), F32),
                        pltpu.VMEM((cap, d), BF16), pltpu.VMEM((cap, d), F32)],
        compiler_params=_cparams(("parallel", "arbitrary", "arbitrary")),
        name="block_moe",
    )(h, oa, ob, oc, w_out, gain, router_p, tri, wg, wu, wd, final_gain)


def _pick(n, pref):
    t = min(pref, n)
    while n % t:
        t //= 2
    return t


def kernel(x, positions, attn_norm, w_in, mla_q_norm, w_uq, mla_kv_norm, w_ukv, diff_lambda_q1, diff_lambda_k1, diff_lambda_q2, diff_lambda_k2, diff_subln, w_out, ffn_norm, dense_w_gate, dense_w_up, dense_w_down, moe_router, moe_w_gate, moe_w_up, moe_w_down, final_norm):
    b, s, d = x.shape
    n = b * s
    depth = w_in.shape[0]
    tm_proj = _pick(n, 512)
    tm_blk = _pick(n, 1024)
    tk = _pick(s, KV_TILE)
    tq_dsa = _pick(s, 256)

    tables = _rope_tables(positions.astype(F32).reshape(n, 1), _pick(n, 1024))
    h = x.reshape(n, d)
    r3 = lambda a: a.reshape(b, s, a.shape[-1])
    for layer in range(depth):
        w_p, uq, ukn, uv = _prep_proj_weights(w_in[layer], w_uq[layer], w_ukv[layer])
        (qa, ka2, qi, ki2, qb, kb, va2, wi, vb, qc, kc, vc) = _project(
            h, attn_norm[layer], tables, w_p, mla_q_norm[layer], uq, mla_kv_norm[layer], ukn, uv, tm_proj, tk)
        oa = _dsa_attention(r3(qa), r3(qi), r3(wi), r3(ka2), va2, r3(ki2), tq_dsa, tk)
        lam_rows = jnp.zeros((8, LANES), F32).at[0:4, 0:DIFF_QK_DIM].set(jnp.stack(
            [diff_lambda_q1[layer], diff_lambda_k1[layer], diff_lambda_q2[layer], diff_lambda_k2[layer]]))
        ob = _diff_attention(r3(qb), r3(kb), vb, lam_rows, diff_subln[layer], layer, tk)
        oc = _mla_attention(r3(qc), r3(kc), vc, tk)
        j = layer // 2
        last = layer == depth - 1
        wo = w_out[layer].astype(BF16)
        attn = (oa.reshape(n, -1), ob.reshape(n, -1), oc.reshape(n, -1))
        if layer % 2 == 0:
            h = _dense_block(h, *attn, wo, ffn_norm[layer], dense_w_gate[j].astype(BF16), dense_w_up[j].astype(BF16),
                             dense_w_down[j].astype(BF16), final_norm,
                             final_norm=last, tm=tm_blk, tf=_pick(dense_w_gate.shape[2], FFN_TILE))
        else:
            router_f = jnp.pad(moe_router[j], ((0, 0), (0, LANES - MOE_EXPERTS)))
            router_hi = router_f.astype(BF16)
            router_p = jnp.concatenate([router_hi, (router_f - router_hi.astype(F32)).astype(BF16)], axis=1)
            h = _moe_block(h, *attn, wo, ffn_norm[layer], router_p,
                           moe_w_gate[j].astype(BF16), moe_w_up[j].astype(BF16), moe_w_down[j].astype(BF16), final_norm,
                           final_norm=last, tm=tm_blk, tf=_pick(moe_w_gate.shape[3], FFN_TILE),
                           mains=MOE_MAIN_ROWS, rows=MOE_EXTRA_ROWS)
    return h.reshape(b, s, d)
```

```python
import functools
import math

import jax
import jax.numpy as jnp
import numpy as np
from jax import lax
from jax.experimental import pallas as pl
from jax.experimental.pallas import tpu as pltpu

F32 = jnp.float32
BF16 = jnp.bfloat16

LANES = 128
MXU_COLS = 256
VMEM_LIMIT_BYTES = 56 * 1024 * 1024

D_MODEL = 1024
CHUNK = 64
ROPE_THETA = 500000.0
NORM_EPS = 1e-6
ROPE_FRACTION_DEN = 4

DSA_HEADS = 4
DSA_HEAD_DIM = 64
DSA_IDX_HEADS = 8
DSA_IDX_DIM = 64
DSA_TOPK_MAX = 256
DSA_IDX_SCALE = (DSA_IDX_HEADS * DSA_IDX_DIM) ** -0.5

DIFF_HEADS = 4
DIFF_QK_DIM = 32
DIFF_V_DIM = 2 * DIFF_QK_DIM

MLA_HEADS = 8
MLA_Q_LORA = 256
MLA_KV_LORA = 128
MLA_NOPE_DIM = 64
MLA_ROPE_DIM = 32
MLA_V_DIM = 64

MOE_EXPERTS = 8
MOE_TOP_K = 2
FFN_TILE = 896
MOE_MAIN_ROWS = (256, 320)
MOE_EXTRA_ROWS = 128

IN_SPLITS = (
    DSA_HEADS * DSA_HEAD_DIM, DSA_HEAD_DIM, DSA_HEAD_DIM, DSA_IDX_HEADS * DSA_IDX_DIM, DSA_IDX_DIM,
    DSA_IDX_HEADS, DIFF_HEADS * 2 * DIFF_QK_DIM, DIFF_HEADS * 2 * DIFF_QK_DIM, DIFF_HEADS * DIFF_V_DIM,
    MLA_Q_LORA, MLA_KV_LORA, MLA_ROPE_DIM,
)

INT_MIN = -(2 ** 31)
IDX_BIG = 2 ** 30


def _cparams(sem):
    return pltpu.CompilerParams(dimension_semantics=sem, vmem_limit_bytes=VMEM_LIMIT_BYTES)


def _rms(x, g):
    return x * lax.rsqrt(jnp.mean(x * x, axis=-1, keepdims=True) + NORM_EPS) * g


def _dot(a, b):
    return jnp.dot(a, b, preferred_element_type=F32)


def _transpose_bf16(x):
    return x.astype(F32).T.astype(BF16)


def _split3(x):
    hi = x.astype(BF16)
    r1 = x - hi.astype(F32)
    mid = r1.astype(BF16)
    lo = (r1 - mid.astype(F32)).astype(BF16)
    return hi, mid, lo


def _dot_f32_by_exact(x, m_bf16):
    hi, mid, lo = _split3(x)
    return _dot(hi, m_bf16) + _dot(mid, m_bf16) + _dot(lo, m_bf16)


def _inv_freq(rot_dim):
    half = rot_dim // 2
    return ROPE_THETA ** (-(jnp.arange(half, dtype=F32) * 2.0 / rot_dim))


def _rope_patterns():
    lane = np.arange(LANES)
    rots = (DSA_HEAD_DIM // ROPE_FRACTION_DEN, DIFF_QK_DIM // ROPE_FRACTION_DEN, MLA_ROPE_DIM)
    offs = (lane % DSA_HEAD_DIM, lane % DIFF_QK_DIM, lane - MLA_NOPE_DIM)
    zero_lane = LANES - 1
    freq = jnp.zeros((LANES,), F32)
    signs, expand = [], np.zeros((3, LANES, LANES), np.float32)
    base = 0
    for p, (rot, off) in enumerate(zip(rots, offs)):
        half = rot // 2
        active = (off >= 0) & (off < rot)
        freq = freq.at[base:base + half].set(_inv_freq(rot))
        src = np.where(active, base + np.clip(off, 0, rot - 1) % half, zero_lane)
        expand[p, src, lane] = 1.0
        signs.append(np.where(active, np.where(off < half, -1.0, 1.0), 0.0))
        base += half
    assert base < zero_lane
    rows = jnp.concatenate([freq[None, :], jnp.asarray(np.stack(signs), dtype=F32), jnp.zeros((4, LANES), F32)], axis=0)
    return rows, jnp.asarray(expand, dtype=BF16)


ROPE_HALF = (DSA_HEAD_DIM // ROPE_FRACTION_DEN // 2, DIFF_QK_DIM // ROPE_FRACTION_DEN // 2, MLA_ROPE_DIM // 2)


def _x1_mask(pattern):
    lane = lax.broadcasted_iota(jnp.int32, (1, LANES), 1)
    if pattern == 0:
        return (lane % DSA_HEAD_DIM) < ROPE_HALF[0]
    if pattern == 1:
        return (lane % DIFF_QK_DIM) < ROPE_HALF[1]
    return (lane >= MLA_NOPE_DIM) & (lane < MLA_NOPE_DIM + ROPE_HALF[2])


def _rope_tables_kernel(pos_ref, rows_ref, expand_ref, out_ref):
    ang = pos_ref[...] * rows_ref[0:1, :]
    cos, sin = jnp.cos(ang), jnp.sin(ang)
    for p in range(3):
        out_ref[2 * p] = _dot_f32_by_exact(cos, expand_ref[p])
        out_ref[2 * p + 1] = _dot_f32_by_exact(sin, expand_ref[p]) * rows_ref[1 + p:2 + p, :]


def _rope_tables(pos_f, tm):
    n = pos_f.shape[0]
    rows, expand = _rope_patterns()
    return pl.pallas_call(
        _rope_tables_kernel,
        grid=(n // tm,),
        in_specs=[pl.BlockSpec((tm, 1), lambda i: (i, 0)),
                  pl.BlockSpec((8, LANES), lambda i: (0, 0)),
                  pl.BlockSpec((3, LANES, LANES), lambda i: (0, 0, 0))],
        out_specs=pl.BlockSpec((6, tm, LANES), lambda i: (0, i, 0)),
        out_shape=jax.ShapeDtypeStruct((6, n, LANES), F32),
        compiler_params=_cparams(("parallel",)),
        name="rope_tables",
    )(pos_f, rows, expand)


def _rope128(y, cos, sin, pattern):
    half = ROPE_HALF[pattern]
    up = pltpu.roll(y, LANES - half, 1)
    dn = pltpu.roll(y, half, 1)
    return y * cos + jnp.where(_x1_mask(pattern), up, dn) * sin


PROJ_COLS = (
    ("qa", 256, 0), ("ka2", 128, 0), ("qi", 512, 0), ("ki2", 128, 0),
    ("qb", 256, 1), ("kb", 256, 1),
    ("va2", 128, None), ("wi", 128, None), ("vb", 256, None),
    ("cq", 256, None), ("ckv", 128, None), ("kr", 128, 2),
)
PROJ_WIDTH = sum(c[1] for c in PROJ_COLS)
PROJ_OUTS = (("qa", 256, BF16), ("ka2", 128, BF16), ("qi", 512, BF16), ("ki2", 128, BF16),
             ("qb", 256, BF16), ("kb", 256, BF16), ("va2", 128, BF16), ("wi", 128, F32),
             ("vb", 256, BF16), ("qc", 1024, BF16), ("kc", 1024, BF16), ("vc", 512, BF16))
PROJ_TRANSPOSED = ("va2", "vb", "vc")
KV_TILE = 256
SUB_KEYS = 128


def _prep_proj_weights(w_in, w_uq, w_ukv):
    offs = np.cumsum((0,) + IN_SPLITS)
    (q_a, k_a, v_a, q_i, k_i, w_i, q_b, k_b, v_b, c_q, c_kv, k_r) = [
        w_in[:, offs[j]:offs[j + 1]] for j in range(len(IN_SPLITS))]
    d = w_in.shape[0]
    z = lambda n: jnp.zeros((d, n), w_in.dtype)
    cols = {
        "qa": q_a, "ka2": jnp.concatenate([k_a, k_a], 1), "qi": q_i, "ki2": jnp.concatenate([k_i, k_i], 1),
        "qb": q_b, "kb": k_b, "va2": jnp.concatenate([v_a, v_a], 1),
        "wi": jnp.concatenate([w_i, z(LANES - DSA_IDX_HEADS)], 1), "vb": v_b, "cq": c_q, "ckv": c_kv,
        "kr": jnp.concatenate([z(MLA_NOPE_DIM), k_r, z(LANES - MLA_NOPE_DIM - MLA_ROPE_DIM)], 1),
    }
    w_p = jnp.concatenate([cols[name] for name, _, _ in PROJ_COLS], axis=1).astype(BF16)
    qd = MLA_NOPE_DIM + MLA_ROPE_DIM
    uq = w_uq.reshape(MLA_Q_LORA, MLA_HEADS, qd)
    uq = jnp.pad(uq, ((0, 0), (0, 0), (0, LANES - qd))).reshape(MLA_Q_LORA, MLA_HEADS * LANES).astype(BF16)
    ukv = w_ukv.reshape(MLA_KV_LORA, MLA_HEADS, MLA_NOPE_DIM + MLA_V_DIM)
    ukn = jnp.pad(ukv[:, :, :MLA_NOPE_DIM], ((0, 0), (0, 0), (0, LANES - MLA_NOPE_DIM)))
    ukn = ukn.reshape(MLA_KV_LORA, MLA_HEADS * LANES).astype(BF16)
    uv = ukv[:, :, MLA_NOPE_DIM:].reshape(MLA_KV_LORA, MLA_HEADS * MLA_V_DIM).astype(BF16)
    return w_p, uq, ukn, uv


def _proj_kernel(h_ref, g_ref, tab_ref, w_ref, qn_ref, uq_ref, kvn_ref, ukn_ref, uv_ref, *out_refs):
    outs = {name: ref for (name, _, _), ref in zip(PROJ_OUTS, out_refs)}
    xn = _rms(h_ref[...], g_ref[...]).astype(BF16)

    def roped(y, pattern):
        return _rope128(y, tab_ref[2 * pattern], tab_ref[2 * pattern + 1], pattern)

    def emit(name, s, y):
        ref = outs[name]
        if name in PROJ_TRANSPOSED:
            tk = ref.shape[2]
            for t in range(ref.shape[0]):
                ref[t, s * LANES:(s + 1) * LANES, :] = y[t * tk:(t + 1) * tk].T.astype(ref.dtype)
        else:
            ref[:, s * LANES:(s + 1) * LANES] = y.astype(ref.dtype)

    def wide_dot(x, w, n_slices):
        per = MXU_COLS // LANES
        res = []
        for c in range(0, n_slices, per):
            y = _dot(x, w[:, c * LANES:(c + per) * LANES])
            res += [y[:, k * LANES:(k + 1) * LANES] for k in range(min(per, n_slices - c))]
        return res

    slices = wide_dot(xn, w_ref, PROJ_WIDTH // LANES)
    vals = {}
    i = 0
    for name, width, pattern in PROJ_COLS:
        for s in range(width // LANES):
            y = slices[i] if pattern is None else roped(slices[i], pattern)
            i += 1
            if name in outs:
                emit(name, s, y)
            else:
                vals.setdefault(name, []).append(y)

    cq = jnp.concatenate(vals["cq"], axis=1)
    cqn = _rms(cq, qn_ref[...]).astype(BF16)
    for hd, y in enumerate(wide_dot(cqn, uq_ref, MLA_HEADS)):
        outs["qc"][:, hd * LANES:(hd + 1) * LANES] = roped(y, 2).astype(BF16)

    ckvn = _rms(vals["ckv"][0], kvn_ref[...]).astype(BF16)
    kr = vals["kr"][0]
    for hd, y in enumerate(wide_dot(ckvn, ukn_ref, MLA_HEADS)):
        outs["kc"][:, hd * LANES:(hd + 1) * LANES] = (y + kr).astype(BF16)
    for s, y in enumerate(wide_dot(ckvn, uv_ref, MLA_HEADS * MLA_V_DIM // LANES)):
        emit("vc", s, y)


def _project(h, gain, tables, w_p, q_norm, uq, kv_norm, ukn, uv, tm, tk):
    n, d = h.shape
    full = lambda a: pl.BlockSpec(a.shape, lambda i: (0,) * a.ndim)
    gain = gain.reshape(1, d)
    q_norm = q_norm.reshape(1, -1)
    kv_norm = kv_norm.reshape(1, -1)
    out_specs, out_shape = [], []
    for name, w, dt in PROJ_OUTS:
        if name in PROJ_TRANSPOSED:
            out_specs.append(pl.BlockSpec((tm // tk, w, tk), lambda i: (i, 0, 0)))
            out_shape.append(jax.ShapeDtypeStruct((n // tk, w, tk), dt))
        else:
            out_specs.append(pl.BlockSpec((tm, w), lambda i: (i, 0)))
            out_shape.append(jax.ShapeDtypeStruct((n, w), dt))
    return pl.pallas_call(
        _proj_kernel,
        grid=(n // tm,),
        in_specs=[pl.BlockSpec((tm, d), lambda i: (i, 0)), full(gain),
                  pl.BlockSpec((6, tm, LANES), lambda i: (0, i, 0)),
                  full(w_p), full(q_norm), full(uq), full(kv_norm), full(ukn), full(uv)],
        out_specs=out_specs,
        out_shape=out_shape,
        compiler_params=_cparams(("parallel",)),
        name="projection",
    )(h, gain, tables, w_p, q_norm, uq, kv_norm, ukn, uv)


def _half_masks(dtype, group):
    lane = lax.broadcasted_iota(jnp.int32, (1, LANES), 1)
    return [jnp.where((lane // group) == u, 1.0, 0.0).astype(dtype) for u in range(LANES // group)]


LOG2E = math.log2(math.e)


def _softmax_step(st, v_t, m, l, acc_ref, scale, guard_empty=False):
    c = scale * LOG2E
    m_new = jnp.maximum(m, jnp.max(st, axis=0, keepdims=True))
    m_use = jnp.where(m_new == -jnp.inf, 0.0, m_new) if guard_empty else m_new
    alpha = jnp.exp2((m - m_use) * c)
    p = jnp.exp2((st - m_use) * c)
    l_new = alpha * l + jnp.sum(p, axis=0, keepdims=True)
    acc_ref[...] = alpha * acc_ref[...] + _dot(v_t, p.astype(BF16))
    return m_new, l_new


def _diag_mask(tk, m_cols, q_off):
    key_chunk = lax.broadcasted_iota(jnp.int32, (tk, 1), 0) // CHUNK
    qry_chunk = (q_off + lax.broadcasted_iota(jnp.int32, (1, m_cols), 1)) // CHUNK
    return key_chunk <= qry_chunk


def _causal_flash(problems, k_ref, v_ref, acc_ref, qi, tq, scale, sub):
    problems = [(_transpose_bf16(q), k_lane0, v_row0, q_off) for q, k_lane0, v_row0, q_off in problems]
    m_cols = problems[0][0].shape[1]
    n_p = len(problems)
    acc_ref[...] = jnp.zeros_like(acc_ref)
    init = tuple(jnp.full((1, m_cols), -jnp.inf, F32) for _ in range(n_p)) + \
        tuple(jnp.zeros((1, m_cols), F32) for _ in range(n_p))

    def step(j, carry, masked):
        ms, ls = list(carry[:n_p]), list(carry[n_p:])
        for s0 in range(0, tq, sub):
            r0 = pl.multiple_of(j * tq + s0, sub)
            for p, (q_t, k_lane0, v_row0, q_off) in enumerate(problems):
                st = _dot(k_ref[pl.ds(r0, sub), k_lane0:k_lane0 + LANES], q_t)
                if masked:
                    st = jnp.where(_diag_mask(tq, m_cols, q_off)[s0:s0 + sub], st, -jnp.inf)
                ms[p], ls[p] = _softmax_step(st, v_ref[j, v_row0:v_row0 + LANES, s0:s0 + sub], ms[p], ls[p],
                                             acc_ref.at[p], scale)
        return tuple(ms) + tuple(ls)

    carry = lax.fori_loop(0, qi, lambda j, c: step(j, c, False), init)
    carry = step(qi, carry, True)
    return [acc_ref[p] / carry[n_p + p] for p in range(n_p)]


def _diff_kernel(lam_ref, subln_ref, gsum_ref, q_ref, k_ref, v_ref, o_ref, acc_ref, *, tq, lam_init):
    qi = pl.program_id(1)
    lam_rows = lam_ref[...]
    s1 = jnp.sum(lam_rows[0:1] * lam_rows[1:2], axis=1, keepdims=True)
    s2 = jnp.sum(lam_rows[2:3] * lam_rows[3:4], axis=1, keepdims=True)
    lam = jnp.exp(s1) - jnp.exp(s2) + lam_init
    masks = _half_masks(BF16, DIFF_QK_DIM)
    row = lax.broadcasted_iota(jnp.int32, (LANES, 1), 0)
    n_slices = DIFF_HEADS // 2
    problems = []
    for s in range(n_slices):
        q = q_ref[0, :, s * LANES:(s + 1) * LANES]
        problems += [(q * mk, s * LANES, s * LANES, 0) for mk in masks]
    outs = _causal_flash(problems, k_ref.at[0], v_ref, acc_ref, qi, tq, DIFF_QK_DIM ** -0.5, min(tq, SUB_KEYS))
    for s in range(n_slices):
        o0, o1, o2, o3 = outs[4 * s:4 * s + 4]
        a0 = o0 - lam * o1
        a1 = o2 - lam * o3
        a = jnp.where(row < DIFF_V_DIM, a0, a1).T
        ss = _dot_f32_by_exact(a * a, gsum_ref[...])
        y = a * lax.rsqrt(ss * (1.0 / DIFF_V_DIM) + NORM_EPS) * subln_ref[...]
        o_ref[0, :, s * LANES:(s + 1) * LANES] = (y * (1.0 - lam_init)).astype(o_ref.dtype)


def _diff_attention(qb, kb, vb_t, lam_rows, subln, layer, tq):
    b, s, w = qb.shape
    lam_init = 0.8 - 0.6 * math.exp(-0.3 * layer)
    subln2 = jnp.concatenate([subln, subln]).reshape(1, LANES).astype(F32)
    lane = np.arange(LANES)
    gsum = jnp.asarray((lane[:, None] // DIFF_V_DIM) == (lane[None, :] // DIFF_V_DIM), dtype=BF16)
    return pl.pallas_call(
        functools.partial(_diff_kernel, tq=tq, lam_init=lam_init),
        grid=(b, s // tq),
        in_specs=[pl.BlockSpec((8, LANES), lambda i, j: (0, 0)),
                  pl.BlockSpec((1, LANES), lambda i, j: (0, 0)),
                  pl.BlockSpec((LANES, LANES), lambda i, j: (0, 0)),
                  pl.BlockSpec((1, tq, w), lambda i, j: (i, j, 0)),
                  pl.BlockSpec((1, s, w), lambda i, j: (i, 0, 0)),
                  pl.BlockSpec((s // tq, w, tq), lambda i, j: (i, 0, 0))],
        out_specs=pl.BlockSpec((1, tq, w), lambda i, j: (i, j, 0)),
        out_shape=jax.ShapeDtypeStruct((b, s, w), BF16),
        scratch_shapes=[pltpu.VMEM((2 * DIFF_HEADS, LANES, tq), F32)],
        compiler_params=_cparams(("parallel", "arbitrary")),
        name="diff_attention",
    )(lam_rows, subln2, gsum, qb, kb, vb_t)


def _mla_kernel(q_ref, k_ref, v_ref, o_ref, acc_ref, *, tq):
    qi = pl.program_id(1)
    row = lax.broadcasted_iota(jnp.int32, (LANES, 1), 0)
    scale = (MLA_NOPE_DIM + MLA_ROPE_DIM) ** -0.5
    problems = [(q_ref[0, :, hd * LANES:(hd + 1) * LANES], hd * LANES, (hd // 2) * LANES, 0)
                for hd in range(MLA_HEADS)]
    outs = _causal_flash(problems, k_ref.at[0], v_ref, acc_ref, qi, tq, scale, min(tq, SUB_KEYS))
    for pair in range(MLA_HEADS // 2):
        o_t = jnp.where(row < MLA_V_DIM, outs[2 * pair], outs[2 * pair + 1])
        o_ref[0, :, pair * LANES:(pair + 1) * LANES] = o_t.T.astype(o_ref.dtype)


def _mla_attention(qc, kc, vc_t, tq):
    b, s, wq = qc.shape
    wv = vc_t.shape[1]
    return pl.pallas_call(
        functools.partial(_mla_kernel, tq=tq),
        grid=(b, s // tq),
        in_specs=[pl.BlockSpec((1, tq, wq), lambda i, j: (i, j, 0)),
                  pl.BlockSpec((1, s, wq), lambda i, j: (i, 0, 0)),
                  pl.BlockSpec((s // tq, wv, tq), lambda i, j: (i, 0, 0))],
        out_specs=pl.BlockSpec((1, tq, wv), lambda i, j: (i, j, 0)),
        out_shape=jax.ShapeDtypeStruct((b, s, wv), BF16),
        scratch_shapes=[pltpu.VMEM((MLA_HEADS, LANES, tq), F32)],
        compiler_params=_cparams(("parallel", "arbitrary")),
        name="mla_attention",
    )(qc, kc, vc_t)


KEY_NEG_INF = -2139095041


def _score_keys(score):
    bits = lax.bitcast_convert_type(score, jnp.int32)
    return bits ^ ((bits >> 31) & 0x7FFFFFFF)


def _dsa_kernel(qa_ref, qi_ref, wi_ref, ka_ref, va_ref, ki_ref, o_ref, key_ref, hi_ref, lo_ref, aux_ref, bias_ref, acc_ref,
                *, tq, tk, top_k):
    blk = pl.program_id(1)
    s_len = ka_ref.shape[1]
    n_kv = ((blk + 1) * tq + tk - 1) // tk
    masks = _half_masks(BF16, DSA_IDX_DIM)
    row = lax.broadcasted_iota(jnp.int32, (LANES, 1), 0)
    t_chunk = (blk * tq + lax.broadcasted_iota(jnp.int32, (1, tq), 1)) // CHUNK
    sub_idx = lax.broadcasted_iota(jnp.int32, (tk, 1), 0)

    def tile_start(c):
        return pl.multiple_of(c * tk, tk)

    qi_st = jnp.concatenate([qi_ref[0, :, (hd // 2) * LANES:(hd // 2 + 1) * LANES] * masks[hd % 2]
                             for hd in range(DSA_IDX_HEADS)], axis=0)
    qi_t = _transpose_bf16(qi_st)
    w_t = wi_ref[0].T

    sub = min(tk, SUB_KEYS)
    sub_iota = lax.broadcasted_iota(jnp.int32, (sub, 1), 0)

    def score_body(c, carry):
        for s0 in range(0, tk, sub):
            r0 = pl.multiple_of(c * tk + s0, sub)
            ki = ki_ref[0, pl.ds(r0, sub), :]
            score = jnp.zeros((sub, tq), F32)
            for hd in range(DSA_IDX_HEADS):
                rel = jnp.maximum(_dot(ki, qi_t[:, hd * tq:(hd + 1) * tq]), 0.0)
                score = score + w_t[hd:hd + 1, :] * rel
            score = score * DSA_IDX_SCALE
            score = jnp.where((r0 + sub_iota) // CHUNK <= t_chunk, score, -jnp.inf)
            key = _score_keys(score)
            key_ref[pl.ds(r0, sub), :] = key
            hi_ref[pl.ds(r0, sub), :] = (key >> 16).astype(jnp.int16)
            lo_ref[pl.ds(r0, sub), :] = ((key & 0xFFFF) - 32768).astype(jnp.int16)
        return carry

    lax.fori_loop(0, n_kv, score_body, 0)

    def count16(ref, cand, strict=False):
        cand16 = cand.astype(jnp.int16)

        def body(c, tot):
            x = ref[pl.ds(tile_start(c), tk), :]
            ones = jnp.where((x > cand16) if strict else (x >= cand16), jnp.bfloat16(1), jnp.bfloat16(0))
            parts = [ones[i * 16:(i + 1) * 16] for i in range(tk // 16)]
            while len(parts) > 1:
                parts = [parts[i] + parts[i + 1] for i in range(0, len(parts), 2)]
            return tot + parts[0].astype(F32)
        tot = lax.fori_loop(0, n_kv, body, jnp.zeros((16, tq), F32))
        return jnp.sum(tot, axis=0, keepdims=True)

    def kth_largest16(ref, need):
        start = jnp.where(count16(ref, jnp.zeros((1, tq), jnp.int32)) >= need, 0, -32768).astype(jnp.int32)

        def body(i, t):
            cand = t | (jnp.int32(1) << (14 - i))
            return jnp.where(count16(ref, cand) >= need, cand, t)
        return lax.fori_loop(0, 15, body, start)

    def count(pred_fn):
        n_part = 8

        def body(c, tot):
            ones = jnp.where(pred_fn(tile_start(c)), 1.0, 0.0)
            return tot + jnp.sum(ones.reshape(tk // (8 * n_part), n_part * 8, tq), axis=0)
        tot = lax.fori_loop(0, n_kv, body, jnp.zeros((n_part * 8, tq), F32))
        return jnp.sum(tot, axis=0, keepdims=True)

    thr_hi = kth_largest16(hi_ref, float(top_k))
    need_lo = top_k - count16(hi_ref, thr_hi, strict=True)
    thr_hi16 = thr_hi.astype(jnp.int16)

    def bucket_body(c, carry):
        r0 = tile_start(c)
        lo_ref[pl.ds(r0, tk), :] = jnp.where(hi_ref[pl.ds(r0, tk), :] == thr_hi16, lo_ref[pl.ds(r0, tk), :],
                                             jnp.int16(-32768))
        return carry

    lax.fori_loop(0, n_kv, bucket_body, 0)
    thr_lo = kth_largest16(lo_ref, need_lo)
    thr = (thr_hi << 16) | ((thr_lo + 32768) & 0xFFFF)
    cnt_thr = count(lambda r0: key_ref[pl.ds(r0, tk), :] >= thr)
    thr = jnp.maximum(thr, KEY_NEG_INF)
    tied = jnp.where(cnt_thr > top_k, jnp.where(thr > KEY_NEG_INF, 1.0, 0.0), 0.0)

    def select_with_ties():
        n_gt = count(lambda r0: key_ref[pl.ds(r0, tk), :] > thr)
        need = top_k - n_gt

        def aux_body(c, carry):
            r0 = tile_start(c)
            idx = r0 + sub_idx
            tie = jnp.where(key_ref[pl.ds(r0, tk), :] == thr, jnp.where(idx // CHUNK <= t_chunk, idx, IDX_BIG), IDX_BIG)
            aux_ref[pl.ds(r0, tk), :] = tie
            return carry

        lax.fori_loop(0, n_kv, aux_body, 0)
        n_bits = max(1, (s_len - 1).bit_length())

        def tie_body(i, last):
            cand = last | (jnp.int32(1) << (n_bits - 1 - i))
            cnt = count(lambda r0: aux_ref[pl.ds(r0, tk), :] < cand)
            return jnp.where(cnt < need, cand, last)

        last = lax.fori_loop(0, n_bits, tie_body, jnp.zeros((1, tq), jnp.int32))

        def bias_body(c, carry):
            r0 = tile_start(c)
            bias_ref[pl.ds(r0, tk), :] = jnp.where(
                key_ref[pl.ds(r0, tk), :] > thr, 0.0, jnp.where(aux_ref[pl.ds(r0, tk), :] <= last, 0.0, -jnp.inf))
            return carry

        lax.fori_loop(0, n_kv, bias_body, 0)

    def select_no_ties():
        low = jnp.where(thr > KEY_NEG_INF, thr - 1, thr)

        def bias_body(c, carry):
            r0 = tile_start(c)
            bias_ref[pl.ds(r0, tk), :] = jnp.where(key_ref[pl.ds(r0, tk), :] > low, 0.0, -jnp.inf)
            return carry

        lax.fori_loop(0, n_kv, bias_body, 0)

    lax.cond(jnp.max(tied) > 0.0, select_with_ties, select_no_ties)

    hmask = _half_masks(BF16, DSA_HEAD_DIM)
    qa_st = jnp.concatenate([qa_ref[0, :, (hd // 2) * LANES:(hd // 2 + 1) * LANES] * hmask[hd % 2]
                             for hd in range(DSA_HEADS)], axis=0)
    qa_t = _transpose_bf16(qa_st)
    scale = DSA_HEAD_DIM ** -0.5
    n_h = DSA_HEADS
    acc_ref[...] = jnp.zeros_like(acc_ref)

    def att_body(c, carry):
        ms, ls = list(carry[:n_h]), list(carry[n_h:])
        for s0 in range(0, tk, sub):
            r0 = pl.multiple_of(c * tk + s0, sub)
            bias = bias_ref[pl.ds(r0, sub), :]
            ka = ka_ref[0, pl.ds(r0, sub), :]
            for hd in range(n_h):
                st = _dot(ka, qa_t[:, hd * tq:(hd + 1) * tq]) + bias
                ms[hd], ls[hd] = _softmax_step(st, va_ref[c, :, s0:s0 + sub], ms[hd], ls[hd], acc_ref.at[hd],
                                               scale, guard_empty=True)
        return tuple(ms) + tuple(ls)

    init = tuple(jnp.full((1, tq), -jnp.inf, F32) for _ in range(n_h)) + \
        tuple(jnp.zeros((1, tq), F32) for _ in range(n_h))
    carry = lax.fori_loop(0, n_kv, att_body, init)
    for p in range(n_h // 2):
        pair_t = jnp.where(row < DSA_HEAD_DIM, acc_ref[2 * p] / carry[n_h + 2 * p],
                           acc_ref[2 * p + 1] / carry[n_h + 2 * p + 1])
        o_ref[0, :, p * LANES:(p + 1) * LANES] = pair_t.T.astype(o_ref.dtype)


def _dsa_attention(qa, qi, wi, ka2, va2_t, ki2, tq, tk):
    b, s, _ = qa.shape
    top_k = min(DSA_TOPK_MAX, s // 4)
    qspec = lambda w: pl.BlockSpec((1, tq, w), lambda i, j: (i, j, 0))
    kspec = lambda w: pl.BlockSpec((1, s, w), lambda i, j: (i, 0, 0))
    return pl.pallas_call(
        functools.partial(_dsa_kernel, tq=tq, tk=tk, top_k=top_k),
        grid=(b, s // tq),
        in_specs=[qspec(qa.shape[2]), qspec(qi.shape[2]), qspec(LANES), kspec(LANES),
                  pl.BlockSpec((s // tk, LANES, tk), lambda i, j: (i, 0, 0)), kspec(LANES)],
        out_specs=qspec(qa.shape[2]),
        out_shape=jax.ShapeDtypeStruct(qa.shape, BF16),
        scratch_shapes=[pltpu.VMEM((s, tq), jnp.int32), pltpu.VMEM((s, tq), jnp.int16), pltpu.VMEM((s, tq), jnp.int16),
                        pltpu.VMEM((s, tq), jnp.int32), pltpu.VMEM((s, tq), F32),
                        pltpu.VMEM((DSA_HEADS, LANES, tq), F32)],
        compiler_params=_cparams(("parallel", "arbitrary")),
        name="dsa_attention",
    )(qa, qi, wi, ka2, va2_t, ki2)


def _attn_residual(h_ref, oa_ref, ob_ref, oc_ref, wo_ref):
    wa, wb = oa_ref.shape[1], ob_ref.shape[1]
    return (h_ref[...] + _dot(oa_ref[...], wo_ref[0:wa, :]) + _dot(ob_ref[...], wo_ref[wa:wa + wb, :])
            + _dot(oc_ref[...], wo_ref[wa + wb:, :]))


def _swiglu_partial(x, wg, wu, wd):
    g = _dot(x, wg)
    u = _dot(x, wu)
    return _dot(((g * jax.nn.sigmoid(g)) * u).astype(BF16), wd)


def _dense_block_kernel(h_ref, oa_ref, ob_ref, oc_ref, wo_ref, g_ref, wg_ref, wu_ref, wd_ref, fg_ref,
                        out_ref, xn_ref, acc_ref, *, final_norm):
    f = pl.program_id(1)

    @pl.when(f == 0)
    def _():
        h1 = _attn_residual(h_ref, oa_ref, ob_ref, oc_ref, wo_ref)
        acc_ref[...] = h1
        xn_ref[...] = _rms(h1, g_ref[...]).astype(BF16)

    acc_ref[...] += _swiglu_partial(xn_ref[...], wg_ref[...], wu_ref[...], wd_ref[...])

    @pl.when(f == pl.num_programs(1) - 1)
    def _():
        y = acc_ref[...]
        out_ref[...] = _rms(y, fg_ref[...]) if final_norm else y


def _dense_block(h, oa, ob, oc, w_out, gain, wg, wu, wd, final_gain, *, final_norm, tm, tf):
    n, d = h.shape
    dff = wg.shape[1]
    const = lambda a: pl.BlockSpec(a.shape, lambda i, f: (0,) * a.ndim, pipeline_mode=pl.Buffered(1))
    row = lambda w: pl.BlockSpec((tm, w), lambda i, f: (i, 0))
    gain = gain.reshape(1, d)
    final_gain = final_gain.reshape(1, d)
    return pl.pallas_call(
        functools.partial(_dense_block_kernel, final_norm=final_norm),
        grid=(n // tm, dff // tf),
        in_specs=[row(d), row(oa.shape[1]), row(ob.shape[1]), row(oc.shape[1]), const(w_out), const(gain),
                  pl.BlockSpec((d, tf), lambda i, f: (0, f)),
                  pl.BlockSpec((d, tf), lambda i, f: (0, f)),
                  pl.BlockSpec((tf, d), lambda i, f: (f, 0)),
                  const(final_gain)],
        out_specs=row(d),
        out_shape=jax.ShapeDtypeStruct((n, d), F32),
        scratch_shapes=[pltpu.VMEM((tm, d), BF16), pltpu.VMEM((tm, d), F32)],
        compiler_params=_cparams(("parallel", "arbitrary")),
        name="block_dense",
    )(h, oa, ob, oc, w_out, gain, wg, wu, wd, final_gain)


def _moe_block_kernel(h_ref, oa_ref, ob_ref, oc_ref, wo_ref, g_ref, r_ref, tri_ref, wg_ref, wu_ref, wd_ref, fg_ref,
                      out_ref, xn_ref, acc_ref, gate_ref, pos_ref, pos_t_ref, cnt_ref, xs_ref, ye_ref,
                      *, n_exp, mains, rows, final_norm):
    e = pl.program_id(1)
    f = pl.program_id(2)
    n_f = pl.num_programs(2)
    tm = xn_ref.shape[0]
    lane = lax.broadcasted_iota(jnp.int32, (1, LANES), 1)

    @pl.when((e == 0) & (f == 0))
    def _():
        h1 = _attn_residual(h_ref, oa_ref, ob_ref, oc_ref, wo_ref)
        acc_ref[...] = h1
        hn = _rms(h1, g_ref[...])
        xn_ref[...] = hn.astype(BF16)
        hn_hi = hn.astype(BF16)
        hn_lo = (hn - hn_hi.astype(F32)).astype(BF16)
        parts = _dot(hn_hi, r_ref[...]) + _dot(hn_lo, r_ref[...])
        logits = parts[:, :LANES] + parts[:, LANES:]
        logits = jnp.where(lane < n_exp, logits, -jnp.inf)
        m1 = jnp.max(logits, axis=1, keepdims=True)
        i1 = jnp.min(jnp.where(logits == m1, lane, IDX_BIG), axis=1, keepdims=True)
        rest = jnp.where(lane == i1, -jnp.inf, logits)
        m2 = jnp.max(rest, axis=1, keepdims=True)
        i2 = jnp.min(jnp.where(rest == m2, lane, IDX_BIG), axis=1, keepdims=True)
        e2 = jnp.exp(m2 - m1)
        den = 1.0 + e2
        gate_ref[...] = jnp.where(lane == i1, 1.0 / den, jnp.where(lane == i2, e2 / den, 0.0))
        routed = jnp.where(lane == i1, 1.0, jnp.where(lane == i2, 1.0, 0.0))
        before = _dot(tri_ref[...], routed.astype(BF16))
        slot = jnp.where(routed > 0.0, before, -1.0)
        pos_ref[...] = slot
        pos_t_ref[...] = slot.T
        cnt = jnp.sum(routed, axis=0, keepdims=True)
        for k in range(n_exp):
            cnt_ref[k] = jnp.sum(jnp.where(lane == k, cnt, 0.0)).astype(jnp.int32)

    n_tok = cnt_ref[e]
    top = mains[-1]
    n_extra = (jnp.maximum(n_tok - top, 0) + rows - 1) // rows

    def for_blocks(fn):
        below = 0
        for size in mains:
            fits = (n_tok > below) if size == top else ((n_tok > below) & (n_tok <= size))

            @pl.when(fits)
            def _(size=size):
                fn(0, size)
                if size == top:
                    lax.fori_loop(0, n_extra, lambda r, c: (
                        fn(pl.multiple_of(top + r * rows, math.gcd(top, rows)), rows), c)[1], 0)

            below = size

    @pl.when(f == 0)
    def _():
        slot_row = pos_t_ref[pl.ds(e, 1), :]

        def gather(r0, nr):
            want = (r0 + lax.broadcasted_iota(jnp.int32, (nr, 1), 0)).astype(F32)
            pick = jnp.where(slot_row == want, 1.0, 0.0).astype(BF16)
            xs_ref[pl.ds(r0, nr), :] = _dot(pick, xn_ref[...]).astype(BF16)
            ye_ref[pl.ds(r0, nr), :] = jnp.zeros((nr, ye_ref.shape[1]), F32)

        for_blocks(gather)

    def expert(r0, nr):
        ye_ref[pl.ds(r0, nr), :] += _swiglu_partial(xs_ref[pl.ds(r0, nr), :], wg_ref[0], wu_ref[0], wd_ref[0])

    for_blocks(expert)

    @pl.when(f == n_f - 1)
    def _():
        full_lane = lax.broadcasted_iota(jnp.int32, (tm, LANES), 1)
        slot_col = jnp.sum(jnp.where(full_lane == e, pos_ref[...], 0.0), axis=1, keepdims=True)
        gate_col = jnp.sum(jnp.where(full_lane == e, gate_ref[...], 0.0), axis=1, keepdims=True)

        def scatter(r0, nr):
            have = (r0 + lax.broadcasted_iota(jnp.int32, (1, nr), 1)).astype(F32)
            place = jnp.where(slot_col == have, 1.0, 0.0).astype(BF16)
            y = ye_ref[pl.ds(r0, nr), :]
            y_hi = y.astype(BF16)
            y_lo = (y - y_hi.astype(F32)).astype(BF16)
            acc_ref[...] += gate_col * (_dot(place, y_hi) + _dot(place, y_lo))

        for_blocks(scatter)

    @pl.when((e == n_exp - 1) & (f == n_f - 1))
    def _():
        y = acc_ref[...]
        out_ref[...] = _rms(y, fg_ref[...]) if final_norm else y


def _moe_block(h, oa, ob, oc, w_out, gain, router_p, wg, wu, wd, final_gain, *, final_norm, tm, tf, mains, rows):
    n, d = h.shape
    n_exp, _, dff = wg.shape
    const = lambda a: pl.BlockSpec(a.shape, lambda i, e, f: (0,) * a.ndim, pipeline_mode=pl.Buffered(1))
    row = lambda w: pl.BlockSpec((tm, w), lambda i, e, f: (i, 0))
    row_in = lambda w: pl.BlockSpec((tm, w), lambda i, e, f: (i, 0), pipeline_mode=pl.Buffered(1))
    gain = gain.reshape(1, d)
    final_gain = final_gain.reshape(1, d)
    tok = np.arange(tm)
    tri = jnp.asarray(tok[None, :] < tok[:, None], dtype=BF16)
    mains = tuple(sorted({min(m, tm) for m in mains}))
    cap = mains[-1] + -(-(tm - mains[-1]) // rows) * rows
    return pl.pallas_call(
        functools.partial(_moe_block_kernel, n_exp=n_exp, mains=mains, rows=rows, final_norm=final_norm),
        grid=(n // tm, n_exp, dff // tf),
        in_specs=[row_in(d), row_in(oa.shape[1]), row_in(ob.shape[1]), row_in(oc.shape[1]), const(w_out), const(gain),
                  const(router_p), const(tri),
                  pl.BlockSpec((1, d, tf), lambda i, e, f: (e, 0, f)),
                  pl.BlockSpec((1, d, tf), lambda i, e, f: (e, 0, f)),
                  pl.BlockSpec((1, tf, d), lambda i, e, f: (e, f, 0)),
                  const(final_gain)],
        out_specs=row(d),
        out_shape=jax.ShapeDtypeStruct((n, d), F32),
        scratch_shapes=[pltpu.VMEM((tm, d), BF16), pltpu.VMEM((tm, d), F32), pltpu.VMEM((tm, LANES), F32),
                        pltpu.VMEM((tm, LANES), F32), pltpu.VMEM((LANES, tm), F32), pltpu.SMEM((n_exp,), jnp.int32),
                        pltpu.VMEM((cap, d), BF16), pltpu.VMEM((cap, d), F32)],
        compiler_params=_cparams(("parallel", "arbitrary", "arbitrary")),
        name="block_moe",
    )(h, oa, ob, oc, w_out, gain, router_p, tri, wg, wu, wd, final_gain)


def _pick(n, pref):
    t = min(pref, n)
    while n % t:
        t //= 2
    return t


def kernel(x, positions, attn_norm, w_in, mla_q_norm, w_uq, mla_kv_norm, w_ukv, diff_lambda_q1, diff_lambda_k1, diff_lambda_q2, diff_lambda_k2, diff_subln, w_out, ffn_norm, dense_w_gate, dense_w_up, dense_w_down, moe_router, moe_w_gate, moe_w_up, moe_w_down, final_norm):
    b, s, d = x.shape
    n = b * s
    depth = w_in.shape[0]
    tm_proj = _pick(n, 512)
    tm_blk = _pick(n, 1024)
    tk = _pick(s, KV_TILE)
    tq_dsa = _pick(s, 256)

    tables = _rope_tables(positions.astype(F32).reshape(n, 1), _pick(n, 1024))
    h = x.reshape(n, d)
    r3 = lambda a: a.reshape(b, s, a.shape[-1])
    for layer in range(depth):
        w_p, uq, ukn, uv = _prep_proj_weights(w_in[layer], w_uq[layer], w_ukv[layer])
        (qa, ka2, qi, ki2, qb, kb, va2, wi, vb, qc, kc, vc) = _project(
            h, attn_norm[layer], tables, w_p, mla_q_norm[layer], uq, mla_kv_norm[layer], ukn, uv, tm_proj, tk)
        oa = _dsa_attention(r3(qa), r3(qi), r3(wi), r3(ka2), va2, r3(ki2), tq_dsa, tk)
        lam_rows = jnp.zeros((8, LANES), F32).at[0:4, 0:DIFF_QK_DIM].set(jnp.stack(
            [diff_lambda_q1[layer], diff_lambda_k1[layer], diff_lambda_q2[layer], diff_lambda_k2[layer]]))
        ob = _diff_attention(r3(qb), r3(kb), vb, lam_rows, diff_subln[layer], layer, tk)
        oc = _mla_attention(r3(qc), r3(kc), vc, tk)
        j = layer // 2
        last = layer == depth - 1
        wo = w_out[layer].astype(BF16)
        attn = (oa.reshape(n, -1), ob.reshape(n, -1), oc.reshape(n, -1))
        if layer % 2 == 0:
            h = _dense_block(h, *attn, wo, ffn_norm[layer], dense_w_gate[j].astype(BF16), dense_w_up[j].astype(BF16),
                             dense_w_down[j].astype(BF16), final_norm,
                             final_norm=last, tm=tm_blk, tf=_pick(dense_w_gate.shape[2], FFN_TILE))
        else:
            router_f = jnp.pad(moe_router[j], ((0, 0), (0, LANES - MOE_EXPERTS)))
            router_hi = router_f.astype(BF16)
            router_p = jnp.concatenate([router_hi, (router_f - router_hi.astype(F32)).astype(BF16)], axis=1)
            h = _moe_block(h, *attn, wo, ffn_norm[layer], router_p,
                           moe_w_gate[j].astype(BF16), moe_w_up[j].astype(BF16), moe_w_down[j].astype(BF16), final_norm,
                           final_norm=last, tm=tm_blk, tf=_pick(moe_w_gate.shape[3], FFN_TILE),
                           mains=MOE_MAIN_ROWS, rows=MOE_EXTRA_ROWS)
    return h.reshape(b, s, d)
```

```python
import functools
import math

import jax
import jax.numpy as jnp
import numpy as np
from jax import lax
from jax.experimental import pallas as pl
from jax.experimental.pallas import tpu as pltpu

F32 = jnp.float32
BF16 = jnp.bfloat16

LANES = 128
MXU_COLS = 256
VMEM_LIMIT_BYTES = 56 * 1024 * 1024

D_MODEL = 1024
CHUNK = 64
ROPE_THETA = 500000.0
NORM_EPS = 1e-6
ROPE_FRACTION_DEN = 4

DSA_HEADS = 4
DSA_HEAD_DIM = 64
DSA_IDX_HEADS = 8
DSA_IDX_DIM = 64
DSA_TOPK_MAX = 256
DSA_IDX_SCALE = (DSA_IDX_HEADS * DSA_IDX_DIM) ** -0.5

DIFF_HEADS = 4
DIFF_QK_DIM = 32
DIFF_V_DIM = 2 * DIFF_QK_DIM

MLA_HEADS = 8
MLA_Q_LORA = 256
MLA_KV_LORA = 128
MLA_NOPE_DIM = 64
MLA_ROPE_DIM = 32
MLA_V_DIM = 64

MOE_EXPERTS = 8
MOE_TOP_K = 2
FFN_TILE = 896
MOE_MAIN_ROWS = (256, 320)
MOE_EXTRA_ROWS = 128

IN_SPLITS = (
    DSA_HEADS * DSA_HEAD_DIM, DSA_HEAD_DIM, DSA_HEAD_DIM, DSA_IDX_HEADS * DSA_IDX_DIM, DSA_IDX_DIM,
    DSA_IDX_HEADS, DIFF_HEADS * 2 * DIFF_QK_DIM, DIFF_HEADS * 2 * DIFF_QK_DIM, DIFF_HEADS * DIFF_V_DIM,
    MLA_Q_LORA, MLA_KV_LORA, MLA_ROPE_DIM,
)

INT_MIN = -(2 ** 31)
IDX_BIG = 2 ** 30


def _cparams(sem):
    return pltpu.CompilerParams(dimension_semantics=sem, vmem_limit_bytes=VMEM_LIMIT_BYTES)


def _rms(x, g):
    return x * lax.rsqrt(jnp.mean(x * x, axis=-1, keepdims=True) + NORM_EPS) * g


def _dot(a, b):
    return jnp.dot(a, b, preferred_element_type=F32)


def _transpose_bf16(x):
    return x.astype(F32).T.astype(BF16)


def _split3(x):
    hi = x.astype(BF16)
    r1 = x - hi.astype(F32)
    mid = r1.astype(BF16)
    lo = (r1 - mid.astype(F32)).astype(BF16)
    return hi, mid, lo


def _dot_f32_by_exact(x, m_bf16):
    hi, mid, lo = _split3(x)
    return _dot(hi, m_bf16) + _dot(mid, m_bf16) + _dot(lo, m_bf16)


def _inv_freq(rot_dim):
    half = rot_dim // 2
    return ROPE_THETA ** (-(jnp.arange(half, dtype=F32) * 2.0 / rot_dim))


def _rope_patterns():
    lane = np.arange(LANES)
    rots = (DSA_HEAD_DIM // ROPE_FRACTION_DEN, DIFF_QK_DIM // ROPE_FRACTION_DEN, MLA_ROPE_DIM)
    offs = (lane % DSA_HEAD_DIM, lane % DIFF_QK_DIM, lane - MLA_NOPE_DIM)
    zero_lane = LANES - 1
    freq = jnp.zeros((LANES,), F32)
    signs, expand = [], np.zeros((3, LANES, LANES), np.float32)
    base = 0
    for p, (rot, off) in enumerate(zip(rots, offs)):
        half = rot // 2
        active = (off >= 0) & (off < rot)
        freq = freq.at[base:base + half].set(_inv_freq(rot))
        src = np.where(active, base + np.clip(off, 0, rot - 1) % half, zero_lane)
        expand[p, src, lane] = 1.0
        signs.append(np.where(active, np.where(off < half, -1.0, 1.0), 0.0))
        base += half
    assert base < zero_lane
    rows = jnp.concatenate([freq[None, :], jnp.asarray(np.stack(signs), dtype=F32), jnp.zeros((4, LANES), F32)], axis=0)
    return rows, jnp.asarray(expand, dtype=BF16)


ROPE_HALF = (DSA_HEAD_DIM // ROPE_FRACTION_DEN // 2, DIFF_QK_DIM // ROPE_FRACTION_DEN // 2, MLA_ROPE_DIM // 2)


def _x1_mask(pattern):
    lane = lax.broadcasted_iota(jnp.int32, (1, LANES), 1)
    if pattern == 0:
        return (lane % DSA_HEAD_DIM) < ROPE_HALF[0]
    if pattern == 1:
        return (lane % DIFF_QK_DIM) < ROPE_HALF[1]
    return (lane >= MLA_NOPE_DIM) & (lane < MLA_NOPE_DIM + ROPE_HALF[2])


def _rope_tables_kernel(pos_ref, rows_ref, expand_ref, out_ref):
    ang = pos_ref[...] * rows_ref[0:1, :]
    cos, sin = jnp.cos(ang), jnp.sin(ang)
    for p in range(3):
        out_ref[2 * p] = _dot_f32_by_exact(cos, expand_ref[p])
        out_ref[2 * p + 1] = _dot_f32_by_exact(sin, expand_ref[p]) * rows_ref[1 + p:2 + p, :]


def _rope_tables(pos_f, tm):
    n = pos_f.shape[0]
    rows, expand = _rope_patterns()
    return pl.pallas_call(
        _rope_tables_kernel,
        grid=(n // tm,),
        in_specs=[pl.BlockSpec((tm, 1), lambda i: (i, 0)),
                  pl.BlockSpec((8, LANES), lambda i: (0, 0)),
                  pl.BlockSpec((3, LANES, LANES), lambda i: (0, 0, 0))],
        out_specs=pl.BlockSpec((6, tm, LANES), lambda i: (0, i, 0)),
        out_shape=jax.ShapeDtypeStruct((6, n, LANES), F32),
        compiler_params=_cparams(("parallel",)),
        name="rope_tables",
    )(pos_f, rows, expand)


def _rope128(y, cos, sin, pattern):
    half = ROPE_HALF[pattern]
    up = pltpu.roll(y, LANES - half, 1)
    dn = pltpu.roll(y, half, 1)
    return y * cos + jnp.where(_x1_mask(pattern), up, dn) * sin


PROJ_COLS = (
    ("cq", 256, None), ("ckv", 128, None), ("kr", 128, 2),
    ("qa", 256, 0), ("ka2", 128, 0), ("qi", 512, 0), ("ki2", 128, 0),
    ("qb", 256, 1), ("kb", 256, 1),
    ("va2", 128, None), ("wi", 128, None), ("vb", 256, None),
)
PROJ_WIDTH = sum(c[1] for c in PROJ_COLS)
PROJ_OUTS = (("qa", 256, BF16), ("ka2", 128, BF16), ("qi", 512, BF16), ("ki2", 128, BF16),
             ("qb", 256, BF16), ("kb", 256, BF16), ("va2", 128, BF16), ("wi", 128, F32),
             ("vb", 256, BF16), ("qc", 1024, BF16), ("kc", 1024, BF16), ("vc", 512, BF16))
PROJ_TRANSPOSED = ("va2", "vb", "vc")
LOG2E = math.log2(math.e)
QUERY_LOG2_SCALE = {"qa": DSA_HEAD_DIM ** -0.5 * LOG2E, "qb": DIFF_QK_DIM ** -0.5 * LOG2E,
                    "qc": (MLA_NOPE_DIM + MLA_ROPE_DIM) ** -0.5 * LOG2E}
KV_TILE = 256
SUB_KEYS = 128


def _prep_proj_weights(w_in, w_uq, w_ukv):
    offs = np.cumsum((0,) + IN_SPLITS)
    (q_a, k_a, v_a, q_i, k_i, w_i, q_b, k_b, v_b, c_q, c_kv, k_r) = [
        w_in[:, offs[j]:offs[j + 1]] for j in range(len(IN_SPLITS))]
    d = w_in.shape[0]
    z = lambda n: jnp.zeros((d, n), w_in.dtype)
    cols = {
        "qa": q_a, "ka2": jnp.concatenate([k_a, k_a], 1), "qi": q_i, "ki2": jnp.concatenate([k_i, k_i], 1),
        "qb": q_b, "kb": k_b, "va2": jnp.concatenate([v_a, v_a], 1),
        "wi": jnp.concatenate([w_i, z(LANES - DSA_IDX_HEADS)], 1), "vb": v_b, "cq": c_q, "ckv": c_kv,
        "kr": jnp.concatenate([z(MLA_NOPE_DIM), k_r, z(LANES - MLA_NOPE_DIM - MLA_ROPE_DIM)], 1),
    }
    w_p = jnp.concatenate([cols[name] for name, _, _ in PROJ_COLS], axis=1).astype(BF16)
    qd = MLA_NOPE_DIM + MLA_ROPE_DIM
    uq = w_uq.reshape(MLA_Q_LORA, MLA_HEADS, qd)
    uq = jnp.pad(uq, ((0, 0), (0, 0), (0, LANES - qd))).reshape(MLA_Q_LORA, MLA_HEADS * LANES).astype(BF16)
    ukv = w_ukv.reshape(MLA_KV_LORA, MLA_HEADS, MLA_NOPE_DIM + MLA_V_DIM)
    ukn = jnp.pad(ukv[:, :, :MLA_NOPE_DIM], ((0, 0), (0, 0), (0, LANES - MLA_NOPE_DIM)))
    ukn = ukn.reshape(MLA_KV_LORA, MLA_HEADS * LANES).astype(BF16)
    uv = ukv[:, :, MLA_NOPE_DIM:].reshape(MLA_KV_LORA, MLA_HEADS * MLA_V_DIM).astype(BF16)
    return w_p, uq, ukn, uv


def _proj_kernel(h_ref, g_ref, tab_ref, w_ref, qn_ref, uq_ref, kvn_ref, ukn_ref, uv_ref, *out_refs):
    outs = {name: ref for (name, _, _), ref in zip(PROJ_OUTS, out_refs)}
    xn = _rms(h_ref[...], g_ref[...]).astype(BF16)

    def roped(y, pattern):
        return _rope128(y, tab_ref[2 * pattern], tab_ref[2 * pattern + 1], pattern)

    def emit(name, s, y):
        ref = outs[name]
        if name in PROJ_TRANSPOSED:
            tk = ref.shape[2]
            for t in range(ref.shape[0]):
                ref[t, s * LANES:(s + 1) * LANES, :] = y[t * tk:(t + 1) * tk].T.astype(ref.dtype)
        else:
            ref[:, s * LANES:(s + 1) * LANES] = y.astype(ref.dtype)

    def wide_dot(x, w, n_slices):
        per = MXU_COLS // LANES
        res = []
        for c in range(0, n_slices, per):
            y = _dot(x, w[:, c * LANES:(c + per) * LANES])
            res += [y[:, k * LANES:(k + 1) * LANES] for k in range(min(per, n_slices - c))]
        return res

    slices = wide_dot(xn, w_ref, PROJ_WIDTH // LANES)
    vals = {}
    i = 0
    for name, width, pattern in PROJ_COLS:
        for s in range(width // LANES):
            y = slices[i] if pattern is None else roped(slices[i], pattern)
            i += 1
            if name in outs:
                emit(name, s, y * QUERY_LOG2_SCALE[name] if name in QUERY_LOG2_SCALE else y)
            else:
                vals.setdefault(name, []).append(y)

    cq = jnp.concatenate(vals["cq"], axis=1)
    cqn = _rms(cq, qn_ref[...]).astype(BF16)
    for hd, y in enumerate(wide_dot(cqn, uq_ref, MLA_HEADS)):
        outs["qc"][:, hd * LANES:(hd + 1) * LANES] = (roped(y, 2) * QUERY_LOG2_SCALE["qc"]).astype(BF16)

    ckvn = _rms(vals["ckv"][0], kvn_ref[...]).astype(BF16)
    kr = vals["kr"][0]
    for hd, y in enumerate(wide_dot(ckvn, ukn_ref, MLA_HEADS)):
        outs["kc"][:, hd * LANES:(hd + 1) * LANES] = (y + kr).astype(BF16)
    for s, y in enumerate(wide_dot(ckvn, uv_ref, MLA_HEADS * MLA_V_DIM // LANES)):
        emit("vc", s, y)


def _project(h, gain, tables, w_p, q_norm, uq, kv_norm, ukn, uv, tm, tk):
    n, d = h.shape
    full = lambda a: pl.BlockSpec(a.shape, lambda i: (0,) * a.ndim)
    gain = gain.reshape(1, d)
    q_norm = q_norm.reshape(1, -1)
    kv_norm = kv_norm.reshape(1, -1)
    out_specs, out_shape = [], []
    for name, w, dt in PROJ_OUTS:
        if name in PROJ_TRANSPOSED:
            out_specs.append(pl.BlockSpec((tm // tk, w, tk), lambda i: (i, 0, 0)))
            out_shape.append(jax.ShapeDtypeStruct((n // tk, w, tk), dt))
        else:
            out_specs.append(pl.BlockSpec((tm, w), lambda i: (i, 0)))
            out_shape.append(jax.ShapeDtypeStruct((n, w), dt))
    return pl.pallas_call(
        _proj_kernel,
        grid=(n // tm,),
        in_specs=[pl.BlockSpec((tm, d), lambda i: (i, 0)), full(gain),
                  pl.BlockSpec((6, tm, LANES), lambda i: (0, i, 0)),
                  full(w_p), full(q_norm), full(uq), full(kv_norm), full(ukn), full(uv)],
        out_specs=out_specs,
        out_shape=out_shape,
        compiler_params=_cparams(("parallel",)),
        name="projection",
    )(h, gain, tables, w_p, q_norm, uq, kv_norm, ukn, uv)


def _half_masks(dtype, group):
    lane = lax.broadcasted_iota(jnp.int32, (1, LANES), 1)
    return [jnp.where((lane // group) == u, 1.0, 0.0).astype(dtype) for u in range(LANES // group)]


def _no_neg_inf(m):
    return jnp.where(m == -jnp.inf, 0.0, m)


def _softmax_piece(st, m_run, guard_empty):
    m_run = jnp.maximum(m_run, jnp.max(st, axis=0, keepdims=True))
    return jnp.exp2(st - (_no_neg_inf(m_run) if guard_empty else m_run)), m_run


def _softmax_finish(pieces, v_t, m, l, acc_ref, guard_empty=False):
    m_new = pieces[-1][1]
    m_use = _no_neg_inf(m_new) if guard_empty else m_new
    alpha = jnp.exp2(m - m_use)
    l_new = alpha * l
    scaled = []
    for i, (p, m_i) in enumerate(pieces):
        if i < len(pieces) - 1:
            p = p * jnp.exp2(m_i - m_use)
        l_new = l_new + jnp.sum(p, axis=0, keepdims=True)
        scaled.append(p.astype(BF16))
    acc_ref[...] = alpha * acc_ref[...] + _dot(v_t, jnp.concatenate(scaled, axis=0))
    return m_new, l_new


def _diag_mask(tk, m_cols, q_off):
    key_chunk = lax.broadcasted_iota(jnp.int32, (tk, 1), 0) // CHUNK
    qry_chunk = (q_off + lax.broadcasted_iota(jnp.int32, (1, m_cols), 1)) // CHUNK
    return key_chunk <= qry_chunk


def _causal_flash(problems, k_ref, v_ref, acc_ref, qi, tq, sub):
    problems = [(_transpose_bf16(q), k_lane0, v_row0, q_off) for q, k_lane0, v_row0, q_off in problems]
    m_cols = problems[0][0].shape[1]
    n_p = len(problems)
    acc_ref[...] = jnp.zeros_like(acc_ref)
    init = tuple(jnp.full((1, m_cols), -jnp.inf, F32) for _ in range(n_p)) + \
        tuple(jnp.zeros((1, m_cols), F32) for _ in range(n_p))

    def step(j, carry, masked):
        ms, ls = list(carry[:n_p]), list(carry[n_p:])
        pieces = [[] for _ in range(n_p)]
        for s0 in range(0, tq, sub):
            r0 = pl.multiple_of(j * tq + s0, sub)
            for p, (q_t, k_lane0, v_row0, q_off) in enumerate(problems):
                st = _dot(k_ref[pl.ds(r0, sub), k_lane0:k_lane0 + LANES], q_t)
                if masked:
                    st = jnp.where(_diag_mask(tq, m_cols, q_off)[s0:s0 + sub], st, -jnp.inf)
                m_run = pieces[p][-1][1] if pieces[p] else ms[p]
                pieces[p].append(_softmax_piece(st, m_run, False))
        for p, (q_t, k_lane0, v_row0, q_off) in enumerate(problems):
            ms[p], ls[p] = _softmax_finish(pieces[p], v_ref[j, v_row0:v_row0 + LANES, :], ms[p], ls[p], acc_ref.at[p])
        return tuple(ms) + tuple(ls)

    carry = lax.fori_loop(0, qi, lambda j, c: step(j, c, False), init)
    carry = step(qi, carry, True)
    return [acc_ref[p] / carry[n_p + p] for p in range(n_p)]


def _diff_kernel(lam_ref, subln_ref, gsum_ref, q_ref, k_ref, v_ref, o_ref, acc_ref, *, tq, lam_init):
    qi = pl.program_id(1)
    lam_rows = lam_ref[...]
    s1 = jnp.sum(lam_rows[0:1] * lam_rows[1:2], axis=1, keepdims=True)
    s2 = jnp.sum(lam_rows[2:3] * lam_rows[3:4], axis=1, keepdims=True)
    lam = jnp.exp(s1) - jnp.exp(s2) + lam_init
    masks = _half_masks(BF16, DIFF_QK_DIM)
    row = lax.broadcasted_iota(jnp.int32, (LANES, 1), 0)
    n_slices = DIFF_HEADS // 2
    problems = []
    for s in range(n_slices):
        q = q_ref[0, :, s * LANES:(s + 1) * LANES]
        problems += [(q * mk, s * LANES, s * LANES, 0) for mk in masks]
    outs = _causal_flash(problems, k_ref.at[0], v_ref, acc_ref, qi, tq, min(tq, SUB_KEYS))
    for s in range(n_slices):
        o0, o1, o2, o3 = outs[4 * s:4 * s + 4]
        a0 = o0 - lam * o1
        a1 = o2 - lam * o3
        a = jnp.where(row < DIFF_V_DIM, a0, a1).T
        ss = _dot_f32_by_exact(a * a, gsum_ref[...])
        y = a * lax.rsqrt(ss * (1.0 / DIFF_V_DIM) + NORM_EPS) * subln_ref[...]
        o_ref[0, :, s * LANES:(s + 1) * LANES] = (y * (1.0 - lam_init)).astype(o_ref.dtype)


def _diff_attention(qb, kb, vb_t, lam_rows, subln, layer, tq):
    b, s, w = qb.shape
    lam_init = 0.8 - 0.6 * math.exp(-0.3 * layer)
    subln2 = jnp.concatenate([subln, subln]).reshape(1, LANES).astype(F32)
    lane = np.arange(LANES)
    gsum = jnp.asarray((lane[:, None] // DIFF_V_DIM) == (lane[None, :] // DIFF_V_DIM), dtype=BF16)
    return pl.pallas_call(
        functools.partial(_diff_kernel, tq=tq, lam_init=lam_init),
        grid=(b, s // tq),
        in_specs=[pl.BlockSpec((8, LANES), lambda i, j: (0, 0)),
                  pl.BlockSpec((1, LANES), lambda i, j: (0, 0)),
                  pl.BlockSpec((LANES, LANES), lambda i, j: (0, 0)),
                  pl.BlockSpec((1, tq, w), lambda i, j: (i, j, 0)),
                  pl.BlockSpec((1, s, w), lambda i, j: (i, 0, 0)),
                  pl.BlockSpec((s // tq, w, tq), lambda i, j: (i, 0, 0))],
        out_specs=pl.BlockSpec((1, tq, w), lambda i, j: (i, j, 0)),
        out_shape=jax.ShapeDtypeStruct((b, s, w), BF16),
        scratch_shapes=[pltpu.VMEM((2 * DIFF_HEADS, LANES, tq), F32)],
        compiler_params=_cparams(("parallel", "arbitrary")),
        name="diff_attention",
    )(lam_rows, subln2, gsum, qb, kb, vb_t)


def _mla_kernel(q_ref, k_ref, v_ref, o_ref, acc_ref, *, tq):
    qi = pl.program_id(1)
    row = lax.broadcasted_iota(jnp.int32, (LANES, 1), 0)
    problems = [(q_ref[0, :, hd * LANES:(hd + 1) * LANES], hd * LANES, (hd // 2) * LANES, 0)
                for hd in range(MLA_HEADS)]
    outs = _causal_flash(problems, k_ref.at[0], v_ref, acc_ref, qi, tq, min(tq, SUB_KEYS))
    for pair in range(MLA_HEADS // 2):
        o_t = jnp.where(row < MLA_V_DIM, outs[2 * pair], outs[2 * pair + 1])
        o_ref[0, :, pair * LANES:(pair + 1) * LANES] = o_t.T.astype(o_ref.dtype)


def _mla_attention(qc, kc, vc_t, tq):
    b, s, wq = qc.shape
    wv = vc_t.shape[1]
    return pl.pallas_call(
        functools.partial(_mla_kernel, tq=tq),
        grid=(b, s // tq),
        in_specs=[pl.BlockSpec((1, tq, wq), lambda i, j: (i, j, 0)),
                  pl.BlockSpec((1, s, wq), lambda i, j: (i, 0, 0)),
                  pl.BlockSpec((s // tq, wv, tq), lambda i, j: (i, 0, 0))],
        out_specs=pl.BlockSpec((1, tq, wv), lambda i, j: (i, j, 0)),
        out_shape=jax.ShapeDtypeStruct((b, s, wv), BF16),
        scratch_shapes=[pltpu.VMEM((MLA_HEADS, LANES, tq), F32)],
        compiler_params=_cparams(("parallel", "arbitrary")),
        name="mla_attention",
    )(qc, kc, vc_t)


KEY_NEG_INF = -2139095041


def _score_keys(score):
    bits = lax.bitcast_convert_type(score, jnp.int32)
    return bits ^ ((bits >> 31) & 0x7FFFFFFF)


def _dsa_kernel(qa_ref, qi_ref, wi_ref, ka_ref, va_ref, ki_ref, o_ref, key_ref, hi_ref, lo_ref, aux_ref, bias_ref, acc_ref,
                *, tq, tk, top_k):
    blk = pl.program_id(1)
    s_len = ka_ref.shape[1]
    n_kv = ((blk + 1) * tq + tk - 1) // tk
    masks = _half_masks(BF16, DSA_IDX_DIM)
    row = lax.broadcasted_iota(jnp.int32, (LANES, 1), 0)
    t_chunk = (blk * tq + lax.broadcasted_iota(jnp.int32, (1, tq), 1)) // CHUNK
    sub_idx = lax.broadcasted_iota(jnp.int32, (tk, 1), 0)

    def tile_start(c):
        return pl.multiple_of(c * tk, tk)

    qi_st = jnp.concatenate([qi_ref[0, :, (hd // 2) * LANES:(hd // 2 + 1) * LANES] * masks[hd % 2]
                             for hd in range(DSA_IDX_HEADS)], axis=0)
    qi_t = _transpose_bf16(qi_st)
    w_t = wi_ref[0].T

    sub = min(tk, SUB_KEYS)
    sub_iota = lax.broadcasted_iota(jnp.int32, (sub, 1), 0)

    def score_body(c, carry):
        for s0 in range(0, tk, sub):
            r0 = pl.multiple_of(c * tk + s0, sub)
            ki = ki_ref[0, pl.ds(r0, sub), :]
            score = jnp.zeros((sub, tq), F32)
            for hd in range(DSA_IDX_HEADS):
                rel = jnp.maximum(_dot(ki, qi_t[:, hd * tq:(hd + 1) * tq]), 0.0)
                score = score + w_t[hd:hd + 1, :] * rel
            score = score * DSA_IDX_SCALE
            score = jnp.where((r0 + sub_iota) // CHUNK <= t_chunk, score, -jnp.inf)
            key = _score_keys(score)
            key_ref[pl.ds(r0, sub), :] = key
            hi_ref[pl.ds(r0, sub), :] = (key >> 16).astype(jnp.int16)
            lo_ref[pl.ds(r0, sub), :] = ((key & 0xFFFF) - 32768).astype(jnp.int16)
        return carry

    lax.fori_loop(0, n_kv, score_body, 0)

    def count16(ref, cand, strict=False):
        cand16 = cand.astype(jnp.int16)

        def body(c, tot):
            x = ref[pl.ds(tile_start(c), tk), :]
            ones = jnp.where((x > cand16) if strict else (x >= cand16), jnp.bfloat16(1), jnp.bfloat16(0))
            parts = [ones[i * 16:(i + 1) * 16] for i in range(tk // 16)]
            while len(parts) > 1:
                parts = [parts[i] + parts[i + 1] for i in range(0, len(parts), 2)]
            return tot + parts[0].astype(F32)
        tot = lax.fori_loop(0, n_kv, body, jnp.zeros((16, tq), F32))
        return jnp.sum(tot, axis=0, keepdims=True)

    def kth_largest16(ref, need):
        start = jnp.where(count16(ref, jnp.zeros((1, tq), jnp.int32)) >= need, 0, -32768).astype(jnp.int32)

        def body(i, t):
            cand = t | (jnp.int32(1) << (14 - i))
            return jnp.where(count16(ref, cand) >= need, cand, t)
        return lax.fori_loop(0, 15, body, start)

    def count(pred_fn):
        n_part = 8

        def body(c, tot):
            ones = jnp.where(pred_fn(tile_start(c)), 1.0, 0.0)
            return tot + jnp.sum(ones.reshape(tk // (8 * n_part), n_part * 8, tq), axis=0)
        tot = lax.fori_loop(0, n_kv, body, jnp.zeros((n_part * 8, tq), F32))
        return jnp.sum(tot, axis=0, keepdims=True)

    thr_hi = kth_largest16(hi_ref, float(top_k))
    need_lo = top_k - count16(hi_ref, thr_hi, strict=True)
    thr_hi16 = thr_hi.astype(jnp.int16)

    def bucket_body(c, carry):
        r0 = tile_start(c)
        lo_ref[pl.ds(r0, tk), :] = jnp.where(hi_ref[pl.ds(r0, tk), :] == thr_hi16, lo_ref[pl.ds(r0, tk), :],
                                             jnp.int16(-32768))
        return carry

    lax.fori_loop(0, n_kv, bucket_body, 0)
    thr_lo = kth_largest16(lo_ref, need_lo)
    thr = (thr_hi << 16) | ((thr_lo + 32768) & 0xFFFF)
    cnt_thr = count(lambda r0: key_ref[pl.ds(r0, tk), :] >= thr)
    thr = jnp.maximum(thr, KEY_NEG_INF)
    tied = jnp.where(cnt_thr > top_k, jnp.where(thr > KEY_NEG_INF, 1.0, 0.0), 0.0)

    def select_with_ties():
        n_gt = count(lambda r0: key_ref[pl.ds(r0, tk), :] > thr)
        need = top_k - n_gt

        def aux_body(c, carry):
            r0 = tile_start(c)
            idx = r0 + sub_idx
            tie = jnp.where(key_ref[pl.ds(r0, tk), :] == thr, jnp.where(idx // CHUNK <= t_chunk, idx, IDX_BIG), IDX_BIG)
            aux_ref[pl.ds(r0, tk), :] = tie
            return carry

        lax.fori_loop(0, n_kv, aux_body, 0)
        n_bits = max(1, (s_len - 1).bit_length())

        def tie_body(i, last):
            cand = last | (jnp.int32(1) << (n_bits - 1 - i))
            cnt = count(lambda r0: aux_ref[pl.ds(r0, tk), :] < cand)
            return jnp.where(cnt < need, cand, last)

        last = lax.fori_loop(0, n_bits, tie_body, jnp.zeros((1, tq), jnp.int32))

        def bias_body(c, carry):
            r0 = tile_start(c)
            bias_ref[pl.ds(r0, tk), :] = jnp.where(
                key_ref[pl.ds(r0, tk), :] > thr, 0.0, jnp.where(aux_ref[pl.ds(r0, tk), :] <= last, 0.0, -jnp.inf))
            return carry

        lax.fori_loop(0, n_kv, bias_body, 0)

    def select_no_ties():
        low = jnp.where(thr > KEY_NEG_INF, thr - 1, thr)

        def bias_body(c, carry):
            r0 = tile_start(c)
            bias_ref[pl.ds(r0, tk), :] = jnp.where(key_ref[pl.ds(r0, tk), :] > low, 0.0, -jnp.inf)
            return carry

        lax.fori_loop(0, n_kv, bias_body, 0)

    lax.cond(jnp.max(tied) > 0.0, select_with_ties, select_no_ties)

    hmask = _half_masks(BF16, DSA_HEAD_DIM)
    qa_st = jnp.concatenate([qa_ref[0, :, (hd // 2) * LANES:(hd // 2 + 1) * LANES] * hmask[hd % 2]
                             for hd in range(DSA_HEADS)], axis=0)
    qa_t = _transpose_bf16(qa_st)
    n_h = DSA_HEADS
    acc_ref[...] = jnp.zeros_like(acc_ref)

    def att_body(c, carry):
        ms, ls = list(carry[:n_h]), list(carry[n_h:])
        pieces = [[] for _ in range(n_h)]
        for s0 in range(0, tk, sub):
            r0 = pl.multiple_of(c * tk + s0, sub)
            bias = bias_ref[pl.ds(r0, sub), :]
            ka = ka_ref[0, pl.ds(r0, sub), :]
            for hd in range(n_h):
                st = _dot(ka, qa_t[:, hd * tq:(hd + 1) * tq]) + bias
                m_run = pieces[hd][-1][1] if pieces[hd] else ms[hd]
                pieces[hd].append(_softmax_piece(st, m_run, True))
        for hd in range(n_h):
            ms[hd], ls[hd] = _softmax_finish(pieces[hd], va_ref[c], ms[hd], ls[hd], acc_ref.at[hd], guard_empty=True)
        return tuple(ms) + tuple(ls)

    init = tuple(jnp.full((1, tq), -jnp.inf, F32) for _ in range(n_h)) + \
        tuple(jnp.zeros((1, tq), F32) for _ in range(n_h))
    carry = lax.fori_loop(0, n_kv, att_body, init)
    for p in range(n_h // 2):
        pair_t = jnp.where(row < DSA_HEAD_DIM, acc_ref[2 * p] / carry[n_h + 2 * p],
                           acc_ref[2 * p + 1] / carry[n_h + 2 * p + 1])
        o_ref[0, :, p * LANES:(p + 1) * LANES] = pair_t.T.astype(o_ref.dtype)


def _dsa_attention(qa, qi, wi, ka2, va2_t, ki2, tq, tk):
    b, s, _ = qa.shape
    top_k = min(DSA_TOPK_MAX, s // 4)
    qspec = lambda w: pl.BlockSpec((1, tq, w), lambda i, j: (i, j, 0))
    kspec = lambda w: pl.BlockSpec((1, s, w), lambda i, j: (i, 0, 0))
    return pl.pallas_call(
        functools.partial(_dsa_kernel, tq=tq, tk=tk, top_k=top_k),
        grid=(b, s // tq),
        in_specs=[qspec(qa.shape[2]), qspec(qi.shape[2]), qspec(LANES), kspec(LANES),
                  pl.BlockSpec((s // tk, LANES, tk), lambda i, j: (i, 0, 0)), kspec(LANES)],
        out_specs=qspec(qa.shape[2]),
        out_shape=jax.ShapeDtypeStruct(qa.shape, BF16),
        scratch_shapes=[pltpu.VMEM((s, tq), jnp.int32), pltpu.VMEM((s, tq), jnp.int16), pltpu.VMEM((s, tq), jnp.int16),
                        pltpu.VMEM((s, tq), jnp.int32), pltpu.VMEM((s, tq), F32),
                        pltpu.VMEM((DSA_HEADS, LANES, tq), F32)],
        compiler_params=_cparams(("parallel", "arbitrary")),
        name="dsa_attention",
    )(qa, qi, wi, ka2, va2_t, ki2)


def _attn_residual(h_ref, oa_ref, ob_ref, oc_ref, wo_ref):
    wa, wb = oa_ref.shape[1], ob_ref.shape[1]
    return (h_ref[...] + _dot(oa_ref[...], wo_ref[0:wa, :]) + _dot(ob_ref[...], wo_ref[wa:wa + wb, :])
            + _dot(oc_ref[...], wo_ref[wa + wb:, :]))


def _swiglu_partial(x, wg, wu, wd):
    g = _dot(x, wg)
    u = _dot(x, wu)
    return _dot(((g * jax.nn.sigmoid(g)) * u).astype(BF16), wd)


def _dense_block_kernel(h_ref, oa_ref, ob_ref, oc_ref, wo_ref, g_ref, wg_ref, wu_ref, wd_ref, fg_ref,
                        out_ref, xn_ref, acc_ref, *, final_norm):
    f = pl.program_id(1)

    @pl.when(f == 0)
    def _():
        h1 = _attn_residual(h_ref, oa_ref, ob_ref, oc_ref, wo_ref)
        acc_ref[...] = h1
        xn_ref[...] = _rms(h1, g_ref[...]).astype(BF16)

    acc_ref[...] += _swiglu_partial(xn_ref[...], wg_ref[...], wu_ref[...], wd_ref[...])

    @pl.when(f == pl.num_programs(1) - 1)
    def _():
        y = acc_ref[...]
        out_ref[...] = _rms(y, fg_ref[...]) if final_norm else y


def _dense_block(h, oa, ob, oc, w_out, gain, wg, wu, wd, final_gain, *, final_norm, tm, tf):
    n, d = h.shape
    dff = wg.shape[1]
    const = lambda a: pl.BlockSpec(a.shape, lambda i, f: (0,) * a.ndim, pipeline_mode=pl.Buffered(1))
    row = lambda w: pl.BlockSpec((tm, w), lambda i, f: (i, 0))
    gain = gain.reshape(1, d)
    final_gain = final_gain.reshape(1, d)
    return pl.pallas_call(
        functools.partial(_dense_block_kernel, final_norm=final_norm),
        grid=(n // tm, dff // tf),
        in_specs=[row(d), row(oa.shape[1]), row(ob.shape[1]), row(oc.shape[1]), const(w_out), const(gain),
                  pl.BlockSpec((d, tf), lambda i, f: (0, f)),
                  pl.BlockSpec((d, tf), lambda i, f: (0, f)),
                  pl.BlockSpec((tf, d), lambda i, f: (f, 0)),
                  const(final_gain)],
        out_specs=row(d),
        out_shape=jax.ShapeDtypeStruct((n, d), F32),
        scratch_shapes=[pltpu.VMEM((tm, d), BF16), pltpu.VMEM((tm, d), F32)],
        compiler_params=_cparams(("parallel", "arbitrary")),
        name="block_dense",
    )(h, oa, ob, oc, w_out, gain, wg, wu, wd, final_gain)


def _moe_block_kernel(h_ref, oa_ref, ob_ref, oc_ref, wo_ref, g_ref, r_ref, tri_ref, wg_ref, wu_ref, wd_ref, fg_ref,
                      out_ref, xn_ref, acc_ref, gate_ref, pos_ref, pos_t_ref, cnt_ref, xs_ref, ye_ref,
                      *, n_exp, mains, rows, final_norm):
    e = pl.program_id(1)
    f = pl.program_id(2)
    n_f = pl.num_programs(2)
    tm = xn_ref.shape[0]
    lane = lax.broadcasted_iota(jnp.int32, (1, LANES), 1)

    @pl.when((e == 0) & (f == 0))
    def _():
        h1 = _attn_residual(h_ref, oa_ref, ob_ref, oc_ref, wo_ref)
        acc_ref[...] = h1
        hn = _rms(h1, g_ref[...])
        xn_ref[...] = hn.astype(BF16)
        hn_hi = hn.astype(BF16)
        hn_lo = (hn - hn_hi.astype(F32)).astype(BF16)
        parts = _dot(hn_hi, r_ref[...]) + _dot(hn_lo, r_ref[...])
        logits = parts[:, :LANES] + parts[:, LANES:]
        logits = jnp.where(lane < n_exp, logits, -jnp.inf)
        m1 = jnp.max(logits, axis=1, keepdims=True)
        i1 = jnp.min(jnp.where(logits == m1, lane, IDX_BIG), axis=1, keepdims=True)
        rest = jnp.where(lane == i1, -jnp.inf, logits)
        m2 = jnp.max(rest, axis=1, keepdims=True)
        i2 = jnp.min(jnp.where(rest == m2, lane, IDX_BIG), axis=1, keepdims=True)
        e2 = jnp.exp(m2 - m1)
        den = 1.0 + e2
        gate_ref[...] = jnp.where(lane == i1, 1.0 / den, jnp.where(lane == i2, e2 / den, 0.0))
        routed = jnp.where(lane == i1, 1.0, jnp.where(lane == i2, 1.0, 0.0))
        before = _dot(tri_ref[...], routed.astype(BF16))
        slot = jnp.where(routed > 0.0, before, -1.0)
        pos_ref[...] = slot
        pos_t_ref[...] = slot.T
        cnt = jnp.sum(routed, axis=0, keepdims=True)
        for k in range(n_exp):
            cnt_ref[k] = jnp.sum(jnp.where(lane == k, cnt, 0.0)).astype(jnp.int32)

    n_tok = cnt_ref[e]
    top = mains[-1]
    n_extra = (jnp.maximum(n_tok - top, 0) + rows - 1) // rows

    def for_blocks(fn):
        below = 0
        for size in mains:
            fits = (n_tok > below) if size == top else ((n_tok > below) & (n_tok <= size))

            @pl.when(fits)
            def _(size=size):
                fn(0, size)
                if size == top:
                    lax.fori_loop(0, n_extra, lambda r, c: (
                        fn(pl.multiple_of(top + r * rows, math.gcd(top, rows)), rows), c)[1], 0)

            below = size

    @pl.when(f == 0)
    def _():
        slot_row = pos_t_ref[pl.ds(e, 1), :]

        def gather(r0, nr):
            want = (r0 + lax.broadcasted_iota(jnp.int32, (nr, 1), 0)).astype(F32)
            pick = jnp.where(slot_row == want, 1.0, 0.0).astype(BF16)
            xs_ref[pl.ds(r0, nr), :] = _dot(pick, xn_ref[...]).astype(BF16)
            ye_ref[pl.ds(r0, nr), :] = jnp.zeros((nr, ye_ref.shape[1]), F32)

        for_blocks(gather)

    def expert(r0, nr):
        ye_ref[pl.ds(r0, nr), :] += _swiglu_partial(xs_ref[pl.ds(r0, nr), :], wg_ref[0], wu_ref[0], wd_ref[0])

    for_blocks(expert)

    @pl.when(f == n_f - 1)
    def _():
        full_lane = lax.broadcasted_iota(jnp.int32, (tm, LANES), 1)
        slot_col = jnp.sum(jnp.where(full_lane == e, pos_ref[...], 0.0), axis=1, keepdims=True)
        gate_col = jnp.sum(jnp.where(full_lane == e, gate_ref[...], 0.0), axis=1, keepdims=True)

        def scatter(r0, nr):
            have = (r0 + lax.broadcasted_iota(jnp.int32, (1, nr), 1)).astype(F32)
            place = jnp.where(slot_col == have, 1.0, 0.0).astype(BF16)
            y = ye_ref[pl.ds(r0, nr), :]
            y_hi = y.astype(BF16)
            y_lo = (y - y_hi.astype(F32)).astype(BF16)
            acc_ref[...] += gate_col * (_dot(place, y_hi) + _dot(place, y_lo))

        for_blocks(scatter)

    @pl.when((e == n_exp - 1) & (f == n_f - 1))
    def _():
        y = acc_ref[...]
        out_ref[...] = _rms(y, fg_ref[...]) if final_norm else y


def _moe_block(h, oa, ob, oc, w_out, gain, router_p, wg, wu, wd, final_gain, *, final_norm, tm, tf, mains, rows):
    n, d = h.shape
    n_exp, _, dff = wg.shape
    const = lambda a: pl.BlockSpec(a.shape, lambda i, e, f: (0,) * a.ndim, pipeline_mode=pl.Buffered(1))
    row = lambda w: pl.BlockSpec((tm, w), lambda i, e, f: (i, 0))
    row_in = lambda w: pl.BlockSpec((tm, w), lambda i, e, f: (i, 0), pipeline_mode=pl.Buffered(1))
    gain = gain.reshape(1, d)
    final_gain = final_gain.reshape(1, d)
    tok = np.arange(tm)
    tri = jnp.asarray(tok[None, :] < tok[:, None], dtype=BF16)
    mains = tuple(sorted({min(m, tm) for m in mains}))
    cap = mains[-1] + -(-(tm - mains[-1]) // rows) * rows
    return pl.pallas_call(
        functools.partial(_moe_block_kernel, n_exp=n_exp, mains=mains, rows=rows, final_norm=final_norm),
        grid=(n // tm, n_exp, dff // tf),
        in_specs=[row_in(d), row_in(oa.shape[1]), row_in(ob.shape[1]), row_in(oc.shape[1]), const(w_out), const(gain),
                  const(router_p), const(tri),
                  pl.BlockSpec((1, d, tf), lambda i, e, f: (e, 0, f)),
                  pl.BlockSpec((1, d, tf), lambda i, e, f: (e, 0, f)),
                  pl.BlockSpec((1, tf, d), lambda i, e, f: (e, f, 0)),
                  const(final_gain)],
        out_specs=row(d),
        out_shape=jax.ShapeDtypeStruct((n, d), F32),
        scratch_shapes=[pltpu.VMEM((tm, d), BF16), pltpu.VMEM((tm, d), F32), pltpu.VMEM((tm, LANES), F32),
                        pltpu.VMEM((tm, LANES), F32), pltpu.VMEM((LANES, tm), F32), pltpu.SMEM((n_exp,), jnp.int32),
                        pltpu.VMEM((cap, d), BF16), pltpu.VMEM((cap, d), F32)],
        compiler_params=_cparams(("parallel", "arbitrary", "arbitrary")),
        name="block_moe",
    )(h, oa, ob, oc, w_out, gain, router_p, tri, wg, wu, wd, final_gain)


def _pick(n, pref):
    t = min(pref, n)
    while n % t:
        t //= 2
    return t


def kernel(x, positions, attn_norm, w_in, mla_q_norm, w_uq, mla_kv_norm, w_ukv, diff_lambda_q1, diff_lambda_k1, diff_lambda_q2, diff_lambda_k2, diff_subln, w_out, ffn_norm, dense_w_gate, dense_w_up, dense_w_down, moe_router, moe_w_gate, moe_w_up, moe_w_down, final_norm):
    b, s, d = x.shape
    n = b * s
    depth = w_in.shape[0]
    tm_proj = _pick(n, 512)
    tm_blk = _pick(n, 1024)
    tk = _pick(s, KV_TILE)
    tq_dsa = _pick(s, 256)

    tables = _rope_tables(positions.astype(F32).reshape(n, 1), _pick(n, 1024))
    h = x.reshape(n, d)
    r3 = lambda a: a.reshape(b, s, a.shape[-1])
    for layer in range(depth):
        w_p, uq, ukn, uv = _prep_proj_weights(w_in[layer], w_uq[layer], w_ukv[layer])
        (qa, ka2, qi, ki2, qb, kb, va2, wi, vb, qc, kc, vc) = _project(
            h, attn_norm[layer], tables, w_p, mla_q_norm[layer], uq, mla_kv_norm[layer], ukn, uv, tm_proj, tk)
        oa = _dsa_attention(r3(qa), r3(qi), r3(wi), r3(ka2), va2, r3(ki2), tq_dsa, tk)
        lam_rows = jnp.zeros((8, LANES), F32).at[0:4, 0:DIFF_QK_DIM].set(jnp.stack(
            [diff_lambda_q1[layer], diff_lambda_k1[layer], diff_lambda_q2[layer], diff_lambda_k2[layer]]))
        ob = _diff_attention(r3(qb), r3(kb), vb, lam_rows, diff_subln[layer], layer, tk)
        oc = _mla_attention(r3(qc), r3(kc), vc, tk)
        j = layer // 2
        last = layer == depth - 1
        wo = w_out[layer].astype(BF16)
        attn = (oa.reshape(n, -1), ob.reshape(n, -1), oc.reshape(n, -1))
        if layer % 2 == 0:
            h = _dense_block(h, *attn, wo, ffn_norm[layer], dense_w_gate[j].astype(BF16), dense_w_up[j].astype(BF16),
                             dense_w_down[j].astype(BF16), final_norm,
                             final_norm=last, tm=tm_blk, tf=_pick(dense_w_gate.shape[2], FFN_TILE))
        else:
            router_f = jnp.pad(moe_router[j], ((0, 0), (0, LANES - MOE_EXPERTS)))
            router_hi = router_f.astype(BF16)
            router_p = jnp.concatenate([router_hi, (router_f - router_hi.astype(F32)).astype(BF16)], axis=1)
            h = _moe_block(h, *attn, wo, ffn_norm[layer], router_p,
                           moe_w_gate[j].astype(BF16), moe_w_up[j].astype(BF16), moe_w_down[j].astype(BF16), final_norm,
                           final_norm=last, tm=tm_blk, tf=_pick(moe_w_gate.shape[3], FFN_TILE),
                           mains=MOE_MAIN_ROWS, rows=MOE_EXTRA_ROWS)
    return h.reshape(b, s, d)
```

```python
import functools
import math

import jax
import jax.numpy as jnp
import numpy as np
from jax import lax
from jax.experimental import pallas as pl
from jax.experimental.pallas import tpu as pltpu

F32 = jnp.float32
BF16 = jnp.bfloat16

LANES = 128
MXU_COLS = 256
VMEM_LIMIT_BYTES = 56 * 1024 * 1024

D_MODEL = 1024
CHUNK = 64
ROPE_THETA = 500000.0
NORM_EPS = 1e-6
ROPE_FRACTION_DEN = 4

DSA_HEADS = 4
DSA_HEAD_DIM = 64
DSA_IDX_HEADS = 8
DSA_IDX_DIM = 64
DSA_TOPK_MAX = 256
DSA_IDX_SCALE = (DSA_IDX_HEADS * DSA_IDX_DIM) ** -0.5

DIFF_HEADS = 4
DIFF_QK_DIM = 32
DIFF_V_DIM = 2 * DIFF_QK_DIM

MLA_HEADS = 8
MLA_Q_LORA = 256
MLA_KV_LORA = 128
MLA_NOPE_DIM = 64
MLA_ROPE_DIM = 32
MLA_V_DIM = 64

MOE_EXPERTS = 8
MOE_TOP_K = 2
FFN_TILE = 896
MOE_MAIN_ROWS = (256, 320)
MOE_EXTRA_ROWS = 128

IN_SPLITS = (
    DSA_HEADS * DSA_HEAD_DIM, DSA_HEAD_DIM, DSA_HEAD_DIM, DSA_IDX_HEADS * DSA_IDX_DIM, DSA_IDX_DIM,
    DSA_IDX_HEADS, DIFF_HEADS * 2 * DIFF_QK_DIM, DIFF_HEADS * 2 * DIFF_QK_DIM, DIFF_HEADS * DIFF_V_DIM,
    MLA_Q_LORA, MLA_KV_LORA, MLA_ROPE_DIM,
)

INT_MIN = -(2 ** 31)
IDX_BIG = 2 ** 30


def _cparams(sem):
    return pltpu.CompilerParams(dimension_semantics=sem, vmem_limit_bytes=VMEM_LIMIT_BYTES)


def _rms(x, g):
    return x * lax.rsqrt(jnp.mean(x * x, axis=-1, keepdims=True) + NORM_EPS) * g


def _dot(a, b):
    return jnp.dot(a, b, preferred_element_type=F32)


def _transpose_bf16(x):
    return x.astype(F32).T.astype(BF16)


def _split3(x):
    hi = x.astype(BF16)
    r1 = x - hi.astype(F32)
    mid = r1.astype(BF16)
    lo = (r1 - mid.astype(F32)).astype(BF16)
    return hi, mid, lo


def _dot_f32_by_exact(x, m_bf16):
    hi, mid, lo = _split3(x)
    return _dot(hi, m_bf16) + _dot(mid, m_bf16) + _dot(lo, m_bf16)


def _inv_freq(rot_dim):
    half = rot_dim // 2
    return ROPE_THETA ** (-(jnp.arange(half, dtype=F32) * 2.0 / rot_dim))


def _rope_patterns():
    lane = np.arange(LANES)
    rots = (DSA_HEAD_DIM // ROPE_FRACTION_DEN, DIFF_QK_DIM // ROPE_FRACTION_DEN, MLA_ROPE_DIM)
    offs = (lane % DSA_HEAD_DIM, lane % DIFF_QK_DIM, lane - MLA_NOPE_DIM)
    zero_lane = LANES - 1
    freq = jnp.zeros((LANES,), F32)
    signs, expand = [], np.zeros((3, LANES, LANES), np.float32)
    base = 0
    for p, (rot, off) in enumerate(zip(rots, offs)):
        half = rot // 2
        active = (off >= 0) & (off < rot)
        freq = freq.at[base:base + half].set(_inv_freq(rot))
        src = np.where(active, base + np.clip(off, 0, rot - 1) % half, zero_lane)
        expand[p, src, lane] = 1.0
        signs.append(np.where(active, np.where(off < half, -1.0, 1.0), 0.0))
        base += half
    assert base < zero_lane
    rows = jnp.concatenate([freq[None, :], jnp.asarray(np.stack(signs), dtype=F32), jnp.zeros((4, LANES), F32)], axis=0)
    return rows, jnp.asarray(expand, dtype=BF16)


ROPE_HALF = (DSA_HEAD_DIM // ROPE_FRACTION_DEN // 2, DIFF_QK_DIM // ROPE_FRACTION_DEN // 2, MLA_ROPE_DIM // 2)


def _x1_mask(pattern):
    lane = lax.broadcasted_iota(jnp.int32, (1, LANES), 1)
    if pattern == 0:
        return (lane % DSA_HEAD_DIM) < ROPE_HALF[0]
    if pattern == 1:
        return (lane % DIFF_QK_DIM) < ROPE_HALF[1]
    return (lane >= MLA_NOPE_DIM) & (lane < MLA_NOPE_DIM + ROPE_HALF[2])


def _rope_tables_kernel(pos_ref, rows_ref, expand_ref, out_ref):
    ang = pos_ref[...] * rows_ref[0:1, :]
    cos, sin = jnp.cos(ang), jnp.sin(ang)
    for p in range(3):
        out_ref[2 * p] = _dot_f32_by_exact(cos, expand_ref[p])
        out_ref[2 * p + 1] = _dot_f32_by_exact(sin, expand_ref[p]) * rows_ref[1 + p:2 + p, :]


def _rope_tables(pos_f, tm):
    n = pos_f.shape[0]
    rows, expand = _rope_patterns()
    return pl.pallas_call(
        _rope_tables_kernel,
        grid=(n // tm,),
        in_specs=[pl.BlockSpec((tm, 1), lambda i: (i, 0)),
                  pl.BlockSpec((8, LANES), lambda i: (0, 0)),
                  pl.BlockSpec((3, LANES, LANES), lambda i: (0, 0, 0))],
        out_specs=pl.BlockSpec((6, tm, LANES), lambda i: (0, i, 0)),
        out_shape=jax.ShapeDtypeStruct((6, n, LANES), F32),
        compiler_params=_cparams(("parallel",)),
        name="rope_tables",
    )(pos_f, rows, expand)


def _rope128(y, cos, sin, pattern):
    half = ROPE_HALF[pattern]
    up = pltpu.roll(y, LANES - half, 1)
    dn = pltpu.roll(y, half, 1)
    return y * cos + jnp.where(_x1_mask(pattern), up, dn) * sin


PROJ_COLS = (
    ("cq", 256, None), ("ckv", 128, None), ("kr", 128, 2),
    ("qa", 256, 0), ("ka2", 128, 0), ("qi", 512, 0), ("ki2", 128, 0),
    ("qb", 256, 1), ("kb", 256, 1),
    ("va2", 128, None), ("wi", 128, None), ("vb", 256, None),
)
PROJ_WIDTH = sum(c[1] for c in PROJ_COLS)
PROJ_OUTS = (("qa", 256, BF16), ("ka2", 128, BF16), ("qi", 512, BF16), ("ki2", 128, BF16),
             ("qb", 256, BF16), ("kb", 256, BF16), ("va2", 128, BF16), ("wi", 128, F32),
             ("vb", 256, BF16), ("qc", 1024, BF16), ("kc", 1024, BF16), ("vc", 512, BF16))
PROJ_TRANSPOSED = ("va2", "vb", "vc")
LOG2E = math.log2(math.e)
QUERY_LOG2_SCALE = {"qa": DSA_HEAD_DIM ** -0.5 * LOG2E, "qb": DIFF_QK_DIM ** -0.5 * LOG2E,
                    "qc": (MLA_NOPE_DIM + MLA_ROPE_DIM) ** -0.5 * LOG2E}
KV_TILE = 256
SUB_KEYS = 128


def _prep_proj_weights(w_in, w_uq, w_ukv):
    offs = np.cumsum((0,) + IN_SPLITS)
    (q_a, k_a, v_a, q_i, k_i, w_i, q_b, k_b, v_b, c_q, c_kv, k_r) = [
        w_in[:, offs[j]:offs[j + 1]] for j in range(len(IN_SPLITS))]
    d = w_in.shape[0]
    z = lambda n: jnp.zeros((d, n), w_in.dtype)
    cols = {
        "qa": q_a, "ka2": jnp.concatenate([k_a, k_a], 1), "qi": q_i, "ki2": jnp.concatenate([k_i, k_i], 1),
        "qb": q_b, "kb": k_b, "va2": jnp.concatenate([v_a, v_a], 1),
        "wi": jnp.concatenate([w_i, z(LANES - DSA_IDX_HEADS)], 1), "vb": v_b, "cq": c_q, "ckv": c_kv,
        "kr": jnp.concatenate([z(MLA_NOPE_DIM), k_r, z(LANES - MLA_NOPE_DIM - MLA_ROPE_DIM)], 1),
    }
    w_p = jnp.concatenate([cols[name] for name, _, _ in PROJ_COLS], axis=1).astype(BF16)
    qd = MLA_NOPE_DIM + MLA_ROPE_DIM
    uq = w_uq.reshape(MLA_Q_LORA, MLA_HEADS, qd)
    uq = jnp.pad(uq, ((0, 0), (0, 0), (0, LANES - qd))).reshape(MLA_Q_LORA, MLA_HEADS * LANES).astype(BF16)
    ukv = w_ukv.reshape(MLA_KV_LORA, MLA_HEADS, MLA_NOPE_DIM + MLA_V_DIM)
    ukn = jnp.pad(ukv[:, :, :MLA_NOPE_DIM], ((0, 0), (0, 0), (0, LANES - MLA_NOPE_DIM)))
    ukn = ukn.reshape(MLA_KV_LORA, MLA_HEADS * LANES).astype(BF16)
    uv = ukv[:, :, MLA_NOPE_DIM:].reshape(MLA_KV_LORA, MLA_HEADS * MLA_V_DIM).astype(BF16)
    return w_p, uq, ukn, uv


def _proj_kernel(h_ref, g_ref, tab_ref, w_ref, qn_ref, uq_ref, kvn_ref, ukn_ref, uv_ref, *out_refs):
    outs = {name: ref for (name, _, _), ref in zip(PROJ_OUTS, out_refs)}
    xn = _rms(h_ref[...], g_ref[...]).astype(BF16)

    def roped(y, pattern):
        return _rope128(y, tab_ref[2 * pattern], tab_ref[2 * pattern + 1], pattern)

    def emit(name, s, y):
        ref = outs[name]
        if name in PROJ_TRANSPOSED:
            tk = ref.shape[2]
            for t in range(ref.shape[0]):
                ref[t, s * LANES:(s + 1) * LANES, :] = y[t * tk:(t + 1) * tk].T.astype(ref.dtype)
        else:
            ref[:, s * LANES:(s + 1) * LANES] = y.astype(ref.dtype)

    def wide_dot(x, w, n_slices):
        per = MXU_COLS // LANES
        res = []
        for c in range(0, n_slices, per):
            y = _dot(x, w[:, c * LANES:(c + per) * LANES])
            res += [y[:, k * LANES:(k + 1) * LANES] for k in range(min(per, n_slices - c))]
        return res

    slices = wide_dot(xn, w_ref, PROJ_WIDTH // LANES)
    vals = {}
    i = 0
    for name, width, pattern in PROJ_COLS:
        for s in range(width // LANES):
            y = slices[i] if pattern is None else roped(slices[i], pattern)
            i += 1
            if name in outs:
                emit(name, s, y * QUERY_LOG2_SCALE[name] if name in QUERY_LOG2_SCALE else y)
            else:
                vals.setdefault(name, []).append(y)

    cq = jnp.concatenate(vals["cq"], axis=1)
    cqn = _rms(cq, qn_ref[...]).astype(BF16)
    for hd, y in enumerate(wide_dot(cqn, uq_ref, MLA_HEADS)):
        outs["qc"][:, hd * LANES:(hd + 1) * LANES] = (roped(y, 2) * QUERY_LOG2_SCALE["qc"]).astype(BF16)

    ckvn = _rms(vals["ckv"][0], kvn_ref[...]).astype(BF16)
    kr = vals["kr"][0]
    for hd, y in enumerate(wide_dot(ckvn, ukn_ref, MLA_HEADS)):
        outs["kc"][:, hd * LANES:(hd + 1) * LANES] = (y + kr).astype(BF16)
    for s, y in enumerate(wide_dot(ckvn, uv_ref, MLA_HEADS * MLA_V_DIM // LANES)):
        emit("vc", s, y)


def _project(h, gain, tables, w_p, q_norm, uq, kv_norm, ukn, uv, tm, tk):
    n, d = h.shape
    full = lambda a: pl.BlockSpec(a.shape, lambda i: (0,) * a.ndim)
    gain = gain.reshape(1, d)
    q_norm = q_norm.reshape(1, -1)
    kv_norm = kv_norm.reshape(1, -1)
    out_specs, out_shape = [], []
    for name, w, dt in PROJ_OUTS:
        if name in PROJ_TRANSPOSED:
            out_specs.append(pl.BlockSpec((tm // tk, w, tk), lambda i: (i, 0, 0)))
            out_shape.append(jax.ShapeDtypeStruct((n // tk, w, tk), dt))
        else:
            out_specs.append(pl.BlockSpec((tm, w), lambda i: (i, 0)))
            out_shape.append(jax.ShapeDtypeStruct((n, w), dt))
    return pl.pallas_call(
        _proj_kernel,
        grid=(n // tm,),
        in_specs=[pl.BlockSpec((tm, d), lambda i: (i, 0)), full(gain),
                  pl.BlockSpec((6, tm, LANES), lambda i: (0, i, 0)),
                  full(w_p), full(q_norm), full(uq), full(kv_norm), full(ukn), full(uv)],
        out_specs=out_specs,
        out_shape=out_shape,
        compiler_params=_cparams(("parallel",)),
        name="projection",
    )(h, gain, tables, w_p, q_norm, uq, kv_norm, ukn, uv)


def _half_masks(dtype, group):
    lane = lax.broadcasted_iota(jnp.int32, (1, LANES), 1)
    return [jnp.where((lane // group) == u, 1.0, 0.0).astype(dtype) for u in range(LANES // group)]


def _softmax_step(st, v_t, m, l, acc_ref, guard_empty=False):
    m_new = jnp.maximum(m, jnp.max(st, axis=0, keepdims=True))
    m_use = jnp.where(m_new == -jnp.inf, 0.0, m_new) if guard_empty else m_new
    alpha = jnp.exp2(m - m_use)
    p = jnp.exp2(st - m_use)
    l_new = alpha * l + jnp.sum(p, axis=0, keepdims=True)
    acc_ref[...] = alpha * acc_ref[...] + _dot(v_t, p.astype(BF16))
    return m_new, l_new


def _diag_mask(tk, m_cols, q_off):
    key_chunk = lax.broadcasted_iota(jnp.int32, (tk, 1), 0) // CHUNK
    qry_chunk = (q_off + lax.broadcasted_iota(jnp.int32, (1, m_cols), 1)) // CHUNK
    return key_chunk <= qry_chunk


def _causal_flash(problems, k_ref, v_ref, acc_ref, qi, tq, sub):
    problems = [(_transpose_bf16(q), k_lane0, v_row0, q_off) for q, k_lane0, v_row0, q_off in problems]
    m_cols = problems[0][0].shape[1]
    n_p = len(problems)
    acc_ref[...] = jnp.zeros_like(acc_ref)
    init = tuple(jnp.full((1, m_cols), -jnp.inf, F32) for _ in range(n_p)) + \
        tuple(jnp.zeros((1, m_cols), F32) for _ in range(n_p))

    def step(j, carry, masked):
        ms, ls = list(carry[:n_p]), list(carry[n_p:])
        for s0 in range(0, tq, sub):
            r0 = pl.multiple_of(j * tq + s0, sub)
            for p, (q_t, k_lane0, v_row0, q_off) in enumerate(problems):
                st = _dot(k_ref[pl.ds(r0, sub), k_lane0:k_lane0 + LANES], q_t)
                if masked:
                    st = jnp.where(_diag_mask(tq, m_cols, q_off)[s0:s0 + sub], st, -jnp.inf)
                ms[p], ls[p] = _softmax_step(st, v_ref[j, v_row0:v_row0 + LANES, s0:s0 + sub], ms[p], ls[p],
                                             acc_ref.at[p])
        return tuple(ms) + tuple(ls)

    carry = lax.fori_loop(0, qi, lambda j, c: step(j, c, False), init)
    carry = step(qi, carry, True)
    return [acc_ref[p] / carry[n_p + p] for p in range(n_p)]


def _diff_kernel(lam_ref, subln_ref, gsum_ref, q_ref, k_ref, v_ref, o_ref, acc_ref, *, tq, lam_init):
    qi = pl.program_id(1)
    lam_rows = lam_ref[...]
    s1 = jnp.sum(lam_rows[0:1] * lam_rows[1:2], axis=1, keepdims=True)
    s2 = jnp.sum(lam_rows[2:3] * lam_rows[3:4], axis=1, keepdims=True)
    lam = jnp.exp(s1) - jnp.exp(s2) + lam_init
    masks = _half_masks(BF16, DIFF_QK_DIM)
    row = lax.broadcasted_iota(jnp.int32, (LANES, 1), 0)
    n_slices = DIFF_HEADS // 2
    problems = []
    for s in range(n_slices):
        q = q_ref[0, :, s * LANES:(s + 1) * LANES]
        problems += [(q * mk, s * LANES, s * LANES, 0) for mk in masks]
    outs = _causal_flash(problems, k_ref.at[0], v_ref, acc_ref, qi, tq, min(tq, SUB_KEYS))
    for s in range(n_slices):
        o0, o1, o2, o3 = outs[4 * s:4 * s + 4]
        a0 = o0 - lam * o1
        a1 = o2 - lam * o3
        a = jnp.where(row < DIFF_V_DIM, a0, a1).T
        ss = _dot_f32_by_exact(a * a, gsum_ref[...])
        y = a * lax.rsqrt(ss * (1.0 / DIFF_V_DIM) + NORM_EPS) * subln_ref[...]
        o_ref[0, :, s * LANES:(s + 1) * LANES] = (y * (1.0 - lam_init)).astype(o_ref.dtype)


def _diff_attention(qb, kb, vb_t, lam_rows, subln, layer, tq):
    b, s, w = qb.shape
    lam_init = 0.8 - 0.6 * math.exp(-0.3 * layer)
    subln2 = jnp.concatenate([subln, subln]).reshape(1, LANES).astype(F32)
    lane = np.arange(LANES)
    gsum = jnp.asarray((lane[:, None] // DIFF_V_DIM) == (lane[None, :] // DIFF_V_DIM), dtype=BF16)
    return pl.pallas_call(
        functools.partial(_diff_kernel, tq=tq, lam_init=lam_init),
        grid=(b, s // tq),
        in_specs=[pl.BlockSpec((8, LANES), lambda i, j: (0, 0)),
                  pl.BlockSpec((1, LANES), lambda i, j: (0, 0)),
                  pl.BlockSpec((LANES, LANES), lambda i, j: (0, 0)),
                  pl.BlockSpec((1, tq, w), lambda i, j: (i, j, 0)),
                  pl.BlockSpec((1, s, w), lambda i, j: (i, 0, 0)),
                  pl.BlockSpec((s // tq, w, tq), lambda i, j: (i, 0, 0))],
        out_specs=pl.BlockSpec((1, tq, w), lambda i, j: (i, j, 0)),
        out_shape=jax.ShapeDtypeStruct((b, s, w), BF16),
        scratch_shapes=[pltpu.VMEM((2 * DIFF_HEADS, LANES, tq), F32)],
        compiler_params=_cparams(("parallel", "arbitrary")),
        name="diff_attention",
    )(lam_rows, subln2, gsum, qb, kb, vb_t)


def _mla_kernel(q_ref, k_ref, v_ref, o_ref, acc_ref, *, tq):
    qi = pl.program_id(1)
    row = lax.broadcasted_iota(jnp.int32, (LANES, 1), 0)
    problems = [(q_ref[0, :, hd * LANES:(hd + 1) * LANES], hd * LANES, (hd // 2) * LANES, 0)
                for hd in range(MLA_HEADS)]
    outs = _causal_flash(problems, k_ref.at[0], v_ref, acc_ref, qi, tq, min(tq, SUB_KEYS))
    for pair in range(MLA_HEADS // 2):
        o_t = jnp.where(row < MLA_V_DIM, outs[2 * pair], outs[2 * pair + 1])
        o_ref[0, :, pair * LANES:(pair + 1) * LANES] = o_t.T.astype(o_ref.dtype)


def _mla_attention(qc, kc, vc_t, tq):
    b, s, wq = qc.shape
    wv = vc_t.shape[1]
    return pl.pallas_call(
        functools.partial(_mla_kernel, tq=tq),
        grid=(b, s // tq),
        in_specs=[pl.BlockSpec((1, tq, wq), lambda i, j: (i, j, 0)),
                  pl.BlockSpec((1, s, wq), lambda i, j: (i, 0, 0)),
                  pl.BlockSpec((s // tq, wv, tq), lambda i, j: (i, 0, 0))],
        out_specs=pl.BlockSpec((1, tq, wv), lambda i, j: (i, j, 0)),
        out_shape=jax.ShapeDtypeStruct((b, s, wv), BF16),
        scratch_shapes=[pltpu.VMEM((MLA_HEADS, LANES, tq), F32)],
        compiler_params=_cparams(("parallel", "arbitrary")),
        name="mla_attention",
    )(qc, kc, vc_t)


KEY_NEG_INF = -2139095041


def _score_keys(score):
    bits = lax.bitcast_convert_type(score, jnp.int32)
    return bits ^ ((bits >> 31) & 0x7FFFFFFF)


def _dsa_kernel(qa_ref, qi_ref, wi_ref, ka_ref, va_ref, ki_ref, o_ref, key_ref, hi_ref, lo_ref, aux_ref, bias_ref, acc_ref,
                *, tq, tk, top_k):
    blk = pl.program_id(1)
    s_len = ka_ref.shape[1]
    n_kv = ((blk + 1) * tq + tk - 1) // tk
    masks = _half_masks(BF16, DSA_IDX_DIM)
    row = lax.broadcasted_iota(jnp.int32, (LANES, 1), 0)
    t_chunk = (blk * tq + lax.broadcasted_iota(jnp.int32, (1, tq), 1)) // CHUNK
    sub_idx = lax.broadcasted_iota(jnp.int32, (tk, 1), 0)

    def tile_start(c):
        return pl.multiple_of(c * tk, tk)

    qi_st = jnp.concatenate([qi_ref[0, :, (hd // 2) * LANES:(hd // 2 + 1) * LANES] * masks[hd % 2]
                             for hd in range(DSA_IDX_HEADS)], axis=0)
    qi_t = _transpose_bf16(qi_st)
    w_t = wi_ref[0].T

    sub = min(tk, SUB_KEYS)
    sub_iota = lax.broadcasted_iota(jnp.int32, (sub, 1), 0)

    def score_body(c, carry):
        for s0 in range(0, tk, sub):
            r0 = pl.multiple_of(c * tk + s0, sub)
            ki = ki_ref[0, pl.ds(r0, sub), :]
            score = jnp.zeros((sub, tq), F32)
            for hd in range(DSA_IDX_HEADS):
                rel = jnp.maximum(_dot(ki, qi_t[:, hd * tq:(hd + 1) * tq]), 0.0)
                score = score + w_t[hd:hd + 1, :] * rel
            score = score * DSA_IDX_SCALE
            score = jnp.where((r0 + sub_iota) // CHUNK <= t_chunk, score, -jnp.inf)
            key = _score_keys(score)
            key_ref[pl.ds(r0, sub), :] = key
            hi_ref[pl.ds(r0, sub), :] = (key >> 16).astype(jnp.int16)
            lo_ref[pl.ds(r0, sub), :] = ((key & 0xFFFF) - 32768).astype(jnp.int16)
        return carry

    lax.fori_loop(0, n_kv, score_body, 0)

    def count16(ref, cand, strict=False):
        cand16 = cand.astype(jnp.int16)

        def body(c, tot):
            x = ref[pl.ds(tile_start(c), tk), :]
            ones = jnp.where((x > cand16) if strict else (x >= cand16), jnp.bfloat16(1), jnp.bfloat16(0))
            parts = [ones[i * 16:(i + 1) * 16] for i in range(tk // 16)]
            while len(parts) > 1:
                parts = [parts[i] + parts[i + 1] for i in range(0, len(parts), 2)]
            return tot + parts[0].astype(F32)
        tot = lax.fori_loop(0, n_kv, body, jnp.zeros((16, tq), F32))
        return jnp.sum(tot, axis=0, keepdims=True)

    def kth_largest16(ref, need):
        start = jnp.where(count16(ref, jnp.zeros((1, tq), jnp.int32)) >= need, 0, -32768).astype(jnp.int32)

        def body(i, t):
            cand = t | (jnp.int32(1) << (14 - i))
            return jnp.where(count16(ref, cand) >= need, cand, t)
        return lax.fori_loop(0, 15, body, start)

    def count(pred_fn):
        n_part = 8

        def body(c, tot):
            ones = jnp.where(pred_fn(tile_start(c)), 1.0, 0.0)
            return tot + jnp.sum(ones.reshape(tk // (8 * n_part), n_part * 8, tq), axis=0)
        tot = lax.fori_loop(0, n_kv, body, jnp.zeros((n_part * 8, tq), F32))
        return jnp.sum(tot, axis=0, keepdims=True)

    thr_hi = kth_largest16(hi_ref, float(top_k))
    need_lo = top_k - count16(hi_ref, thr_hi, strict=True)
    thr_hi16 = thr_hi.astype(jnp.int16)

    def bucket_body(c, carry):
        r0 = tile_start(c)
        lo_ref[pl.ds(r0, tk), :] = jnp.where(hi_ref[pl.ds(r0, tk), :] == thr_hi16, lo_ref[pl.ds(r0, tk), :],
                                             jnp.int16(-32768))
        return carry

    lax.fori_loop(0, n_kv, bucket_body, 0)
    thr_lo = kth_largest16(lo_ref, need_lo)
    thr = (thr_hi << 16) | ((thr_lo + 32768) & 0xFFFF)
    cnt_thr = count(lambda r0: key_ref[pl.ds(r0, tk), :] >= thr)
    thr = jnp.maximum(thr, KEY_NEG_INF)
    tied = jnp.where(cnt_thr > top_k, jnp.where(thr > KEY_NEG_INF, 1.0, 0.0), 0.0)

    def select_with_ties():
        n_gt = count(lambda r0: key_ref[pl.ds(r0, tk), :] > thr)
        need = top_k - n_gt

        def aux_body(c, carry):
            r0 = tile_start(c)
            idx = r0 + sub_idx
            tie = jnp.where(key_ref[pl.ds(r0, tk), :] == thr, jnp.where(idx // CHUNK <= t_chunk, idx, IDX_BIG), IDX_BIG)
            aux_ref[pl.ds(r0, tk), :] = tie
            return carry

        lax.fori_loop(0, n_kv, aux_body, 0)
        n_bits = max(1, (s_len - 1).bit_length())

        def tie_body(i, last):
            cand = last | (jnp.int32(1) << (n_bits - 1 - i))
            cnt = count(lambda r0: aux_ref[pl.ds(r0, tk), :] < cand)
            return jnp.where(cnt < need, cand, last)

        last = lax.fori_loop(0, n_bits, tie_body, jnp.zeros((1, tq), jnp.int32))

        def bias_body(c, carry):
            r0 = tile_start(c)
            bias_ref[pl.ds(r0, tk), :] = jnp.where(
                key_ref[pl.ds(r0, tk), :] > thr, 0.0, jnp.where(aux_ref[pl.ds(r0, tk), :] <= last, 0.0, -jnp.inf))
            return carry

        lax.fori_loop(0, n_kv, bias_body, 0)

    def select_no_ties():
        low = jnp.where(thr > KEY_NEG_INF, thr - 1, thr)

        def bias_body(c, carry):
            r0 = tile_start(c)
            bias_ref[pl.ds(r0, tk), :] = jnp.where(key_ref[pl.ds(r0, tk), :] > low, 0.0, -jnp.inf)
            return carry

        lax.fori_loop(0, n_kv, bias_body, 0)

    lax.cond(jnp.max(tied) > 0.0, select_with_ties, select_no_ties)

    hmask = _half_masks(BF16, DSA_HEAD_DIM)
    qa_st = jnp.concatenate([qa_ref[0, :, (hd // 2) * LANES:(hd // 2 + 1) * LANES] * hmask[hd % 2]
                             for hd in range(DSA_HEADS)], axis=0)
    qa_t = _transpose_bf16(qa_st)
    n_h = DSA_HEADS
    acc_ref[...] = jnp.zeros_like(acc_ref)

    def att_body(c, carry):
        ms, ls = list(carry[:n_h]), list(carry[n_h:])
        for s0 in range(0, tk, sub):
            r0 = pl.multiple_of(c * tk + s0, sub)
            bias = bias_ref[pl.ds(r0, sub), :]
            ka = ka_ref[0, pl.ds(r0, sub), :]
            for hd in range(n_h):
                st = _dot(ka, qa_t[:, hd * tq:(hd + 1) * tq]) + bias
                ms[hd], ls[hd] = _softmax_step(st, va_ref[c, :, s0:s0 + sub], ms[hd], ls[hd], acc_ref.at[hd],
                                               guard_empty=True)
        return tuple(ms) + tuple(ls)

    init = tuple(jnp.full((1, tq), -jnp.inf, F32) for _ in range(n_h)) + \
        tuple(jnp.zeros((1, tq), F32) for _ in range(n_h))
    carry = lax.fori_loop(0, n_kv, att_body, init)
    for p in range(n_h // 2):
        pair_t = jnp.where(row < DSA_HEAD_DIM, acc_ref[2 * p] / carry[n_h + 2 * p],
                           acc_ref[2 * p + 1] / carry[n_h + 2 * p + 1])
        o_ref[0, :, p * LANES:(p + 1) * LANES] = pair_t.T.astype(o_ref.dtype)


def _dsa_attention(qa, qi, wi, ka2, va2_t, ki2, tq, tk):
    b, s, _ = qa.shape
    top_k = min(DSA_TOPK_MAX, s // 4)
    qspec = lambda w: pl.BlockSpec((1, tq, w), lambda i, j: (i, j, 0))
    kspec = lambda w: pl.BlockSpec((1, s, w), lambda i, j: (i, 0, 0))
    return pl.pallas_call(
        functools.partial(_dsa_kernel, tq=tq, tk=tk, top_k=top_k),
        grid=(b, s // tq),
        in_specs=[qspec(qa.shape[2]), qspec(qi.shape[2]), qspec(LANES), kspec(LANES),
                  pl.BlockSpec((s // tk, LANES, tk), lambda i, j: (i, 0, 0)), kspec(LANES)],
        out_specs=qspec(qa.shape[2]),
        out_shape=jax.ShapeDtypeStruct(qa.shape, BF16),
        scratch_shapes=[pltpu.VMEM((s, tq), jnp.int32), pltpu.VMEM((s, tq), jnp.int16), pltpu.VMEM((s, tq), jnp.int16),
                        pltpu.VMEM((s, tq), jnp.int32), pltpu.VMEM((s, tq), F32),
                        pltpu.VMEM((DSA_HEADS, LANES, tq), F32)],
        compiler_params=_cparams(("parallel", "arbitrary")),
        name="dsa_attention",
    )(qa, qi, wi, ka2, va2_t, ki2)


def _attn_residual(h_ref, oa_ref, ob_ref, oc_ref, wo_ref):
    wa, wb = oa_ref.shape[1], ob_ref.shape[1]
    return (h_ref[...] + _dot(oa_ref[...], wo_ref[0:wa, :]) + _dot(ob_ref[...], wo_ref[wa:wa + wb, :])
            + _dot(oc_ref[...], wo_ref[wa + wb:, :]))


def _swiglu_partial(x, wg, wu, wd):
    g = _dot(x, wg)
    u = _dot(x, wu)
    return _dot(((g * jax.nn.sigmoid(g)) * u).astype(BF16), wd)


def _dense_block_kernel(h_ref, oa_ref, ob_ref, oc_ref, wo_ref, g_ref, wg_ref, wu_ref, wd_ref, fg_ref,
                        out_ref, xn_ref, acc_ref, *, final_norm):
    f = pl.program_id(1)

    @pl.when(f == 0)
    def _():
        h1 = _attn_residual(h_ref, oa_ref, ob_ref, oc_ref, wo_ref)
        acc_ref[...] = h1
        xn_ref[...] = _rms(h1, g_ref[...]).astype(BF16)

    acc_ref[...] += _swiglu_partial(xn_ref[...], wg_ref[...], wu_ref[...], wd_ref[...])

    @pl.when(f == pl.num_programs(1) - 1)
    def _():
        y = acc_ref[...]
        out_ref[...] = _rms(y, fg_ref[...]) if final_norm else y


def _dense_block(h, oa, ob, oc, w_out, gain, wg, wu, wd, final_gain, *, final_norm, tm, tf):
    n, d = h.shape
    dff = wg.shape[1]
    const = lambda a: pl.BlockSpec(a.shape, lambda i, f: (0,) * a.ndim, pipeline_mode=pl.Buffered(1))
    row = lambda w: pl.BlockSpec((tm, w), lambda i, f: (i, 0))
    gain = gain.reshape(1, d)
    final_gain = final_gain.reshape(1, d)
    return pl.pallas_call(
        functools.partial(_dense_block_kernel, final_norm=final_norm),
        grid=(n // tm, dff // tf),
        in_specs=[row(d), row(oa.shape[1]), row(ob.shape[1]), row(oc.shape[1]), const(w_out), const(gain),
                  pl.BlockSpec((d, tf), lambda i, f: (0, f)),
                  pl.BlockSpec((d, tf), lambda i, f: (0, f)),
                  pl.BlockSpec((tf, d), lambda i, f: (f, 0)),
                  const(final_gain)],
        out_specs=row(d),
        out_shape=jax.ShapeDtypeStruct((n, d), F32),
        scratch_shapes=[pltpu.VMEM((tm, d), BF16), pltpu.VMEM((tm, d), F32)],
        compiler_params=_cparams(("parallel", "arbitrary")),
        name="block_dense",
    )(h, oa, ob, oc, w_out, gain, wg, wu, wd, final_gain)


def _moe_block_kernel(h_ref, oa_ref, ob_ref, oc_ref, wo_ref, g_ref, r_ref, tri_ref, wg_ref, wu_ref, wd_ref, fg_ref,
                      out_ref, xn_ref, acc_ref, gate_ref, pos_ref, pos_t_ref, cnt_ref, xs_ref, ye_ref,
                      *, n_exp, mains, rows, final_norm):
    e = pl.program_id(1)
    f = pl.program_id(2)
    n_f = pl.num_programs(2)
    tm = xn_ref.shape[0]
    lane = lax.broadcasted_iota(jnp.int32, (1, LANES), 1)

    @pl.when((e == 0) & (f == 0))
    def _():
        h1 = _attn_residual(h_ref, oa_ref, ob_ref, oc_ref, wo_ref)
        acc_ref[...] = h1
        hn = _rms(h1, g_ref[...])
        xn_ref[...] = hn.astype(BF16)
        hn_hi = hn.astype(BF16)
        hn_lo = (hn - hn_hi.astype(F32)).astype(BF16)
        parts = _dot(hn_hi, r_ref[...]) + _dot(hn_lo, r_ref[...])
        logits = parts[:, :LANES] + parts[:, LANES:]
        logits = jnp.where(lane < n_exp, logits, -jnp.inf)
        m1 = jnp.max(logits, axis=1, keepdims=True)
        i1 = jnp.min(jnp.where(logits == m1, lane, IDX_BIG), axis=1, keepdims=True)
        rest = jnp.where(lane == i1, -jnp.inf, logits)
        m2 = jnp.max(rest, axis=1, keepdims=True)
        i2 = jnp.min(jnp.where(rest == m2, lane, IDX_BIG), axis=1, keepdims=True)
        e2 = jnp.exp(m2 - m1)
        den = 1.0 + e2
        gate_ref[...] = jnp.where(lane == i1, 1.0 / den, jnp.where(lane == i2, e2 / den, 0.0))
        routed = jnp.where(lane == i1, 1.0, jnp.where(lane == i2, 1.0, 0.0))
        before = _dot(tri_ref[...], routed.astype(BF16))
        slot = jnp.where(routed > 0.0, before, -1.0)
        pos_ref[...] = slot
        pos_t_ref[...] = slot.T
        cnt = jnp.sum(routed, axis=0, keepdims=True)
        for k in range(n_exp):
            cnt_ref[k] = jnp.sum(jnp.where(lane == k, cnt, 0.0)).astype(jnp.int32)

    n_tok = cnt_ref[e]
    top = mains[-1]
    n_extra = (jnp.maximum(n_tok - top, 0) + rows - 1) // rows

    def for_blocks(fn):
        below = 0
        for size in mains:
            fits = (n_tok > below) if size == top else ((n_tok > below) & (n_tok <= size))

            @pl.when(fits)
            def _(size=size):
                fn(0, size)
                if size == top:
                    lax.fori_loop(0, n_extra, lambda r, c: (
                        fn(pl.multiple_of(top + r * rows, math.gcd(top, rows)), rows), c)[1], 0)

            below = size

    @pl.when(f == 0)
    def _():
        slot_row = pos_t_ref[pl.ds(e, 1), :]

        def gather(r0, nr):
            want = (r0 + lax.broadcasted_iota(jnp.int32, (nr, 1), 0)).astype(F32)
            pick = jnp.where(slot_row == want, 1.0, 0.0).astype(BF16)
            xs_ref[pl.ds(r0, nr), :] = _dot(pick, xn_ref[...]).astype(BF16)
            ye_ref[pl.ds(r0, nr), :] = jnp.zeros((nr, ye_ref.shape[1]), F32)

        for_blocks(gather)

    def expert(r0, nr):
        ye_ref[pl.ds(r0, nr), :] += _swiglu_partial(xs_ref[pl.ds(r0, nr), :], wg_ref[0], wu_ref[0], wd_ref[0])

    for_blocks(expert)

    @pl.when(f == n_f - 1)
    def _():
        full_lane = lax.broadcasted_iota(jnp.int32, (tm, LANES), 1)
        slot_col = jnp.sum(jnp.where(full_lane == e, pos_ref[...], 0.0), axis=1, keepdims=True)
        gate_col = jnp.sum(jnp.where(full_lane == e, gate_ref[...], 0.0), axis=1, keepdims=True)

        def scatter(r0, nr):
            have = (r0 + lax.broadcasted_iota(jnp.int32, (1, nr), 1)).astype(F32)
            place = jnp.where(slot_col == have, 1.0, 0.0).astype(BF16)
            y = ye_ref[pl.ds(r0, nr), :]
            y_hi = y.astype(BF16)
            y_lo = (y - y_hi.astype(F32)).astype(BF16)
            acc_ref[...] += gate_col * (_dot(place, y_hi) + _dot(place, y_lo))

        for_blocks(scatter)

    @pl.when((e == n_exp - 1) & (f == n_f - 1))
    def _():
        y = acc_ref[...]
        out_ref[...] = _rms(y, fg_ref[...]) if final_norm else y


def _moe_block(h, oa, ob, oc, w_out, gain, router_p, wg, wu, wd, final_gain, *, final_norm, tm, tf, mains, rows):
    n, d = h.shape
    n_exp, _, dff = wg.shape
    const = lambda a: pl.BlockSpec(a.shape, lambda i, e, f: (0,) * a.ndim, pipeline_mode=pl.Buffered(1))
    row = lambda w: pl.BlockSpec((tm, w), lambda i, e, f: (i, 0))
    row_in = lambda w: pl.BlockSpec((tm, w), lambda i, e, f: (i, 0), pipeline_mode=pl.Buffered(1))
    gain = gain.reshape(1, d)
    final_gain = final_gain.reshape(1, d)
    tok = np.arange(tm)
    tri = jnp.asarray(tok[None, :] < tok[:, None], dtype=BF16)
    mains = tuple(sorted({min(m, tm) for m in mains}))
    cap = mains[-1] + -(-(tm - mains[-1]) // rows) * rows
    return pl.pallas_call(
        functools.partial(_moe_block_kernel, n_exp=n_exp, mains=mains, rows=rows, final_norm=final_norm),
        grid=(n // tm, n_exp, dff // tf),
        in_specs=[row_in(d), row_in(oa.shape[1]), row_in(ob.shape[1]), row_in(oc.shape[1]), const(w_out), const(gain),
                  const(router_p), const(tri),
                  pl.BlockSpec((1, d, tf), lambda i, e, f: (e, 0, f)),
                  pl.BlockSpec((1, d, tf), lambda i, e, f: (e, 0, f)),
                  pl.BlockSpec((1, tf, d), lambda i, e, f: (e, f, 0)),
                  const(final_gain)],
        out_specs=row(d),
        out_shape=jax.ShapeDtypeStruct((n, d), F32),
        scratch_shapes=[pltpu.VMEM((tm, d), BF16), pltpu.VMEM((tm, d), F32), pltpu.VMEM((tm, LANES), F32),
                        pltpu.VMEM((tm, LANES), F32), pltpu.VMEM((LANES, tm), F32), pltpu.SMEM((n_exp,), jnp.int32),
                        pltpu.VMEM((cap, d), BF16), pltpu.VMEM((cap, d), F32)],
        compiler_params=_cparams(("parallel", "arbitrary", "arbitrary")),
        name="block_moe",
    )(h, oa, ob, oc, w_out, gain, router_p, tri, wg, wu, wd, final_gain)


def _pick(n, pref):
    t = min(pref, n)
    while n % t:
        t //= 2
    return t


def kernel(x, positions, attn_norm, w_in, mla_q_norm, w_uq, mla_kv_norm, w_ukv, diff_lambda_q1, diff_lambda_k1, diff_lambda_q2, diff_lambda_k2, diff_subln, w_out, ffn_norm, dense_w_gate, dense_w_up, dense_w_down, moe_router, moe_w_gate, moe_w_up, moe_w_down, final_norm):
    b, s, d = x.shape
    n = b * s
    depth = w_in.shape[0]
    tm_proj = _pick(n, 512)
    tm_blk = _pick(n, 1024)
    tk = _pick(s, KV_TILE)
    tq_dsa = _pick(s, 256)

    tables = _rope_tables(positions.astype(F32).reshape(n, 1), _pick(n, 1024))
    h = x.reshape(n, d)
    r3 = lambda a: a.reshape(b, s, a.shape[-1])
    for layer in range(depth):
        w_p, uq, ukn, uv = _prep_proj_weights(w_in[layer], w_uq[layer], w_ukv[layer])
        (qa, ka2, qi, ki2, qb, kb, va2, wi, vb, qc, kc, vc) = _project(
            h, attn_norm[layer], tables, w_p, mla_q_norm[layer], uq, mla_kv_norm[layer], ukn, uv, tm_proj, tk)
        oa = _dsa_attention(r3(qa), r3(qi), r3(wi), r3(ka2), va2, r3(ki2), tq_dsa, tk)
        lam_rows = jnp.zeros((8, LANES), F32).at[0:4, 0:DIFF_QK_DIM].set(jnp.stack(
            [diff_lambda_q1[layer], diff_lambda_k1[layer], diff_lambda_q2[layer], diff_lambda_k2[layer]]))
        ob = _diff_attention(r3(qb), r3(kb), vb, lam_rows, diff_subln[layer], layer, tk)
        oc = _mla_attention(r3(qc), r3(kc), vc, tk)
        j = layer // 2
        last = layer == depth - 1
        wo = w_out[layer].astype(BF16)
        attn = (oa.reshape(n, -1), ob.reshape(n, -1), oc.reshape(n, -1))
        if layer % 2 == 0:
            h = _dense_block(h, *attn, wo, ffn_norm[layer], dense_w_gate[j].astype(BF16), dense_w_up[j].astype(BF16),
                             dense_w_down[j].astype(BF16), final_norm,
                             final_norm=last, tm=tm_blk, tf=_pick(dense_w_gate.shape[2], FFN_TILE))
        else:
            router_f = jnp.pad(moe_router[j], ((0, 0), (0, LANES - MOE_EXPERTS)))
            router_hi = router_f.astype(BF16)
            router_p = jnp.concatenate([router_hi, (router_f - router_hi.astype(F32)).astype(BF16)], axis=1)
            h = _moe_block(h, *attn, wo, ffn_norm[layer], router_p,
                           moe_w_gate[j].astype(BF16), moe_w_up[j].astype(BF16), moe_w_down[j].astype(BF16), final_norm,
                           final_norm=last, tm=tm_blk, tf=_pick(moe_w_gate.shape[3], FFN_TILE),
                           mains=MOE_MAIN_ROWS, rows=MOE_EXTRA_ROWS)
    return h.reshape(b, s, d)
```

```python
import functools
import math

import jax
import jax.numpy as jnp
import numpy as np
from jax import lax
from jax.experimental import pallas as pl
from jax.experimental.pallas import tpu as pltpu

F32 = jnp.float32
BF16 = jnp.bfloat16

LANES = 128
MXU_COLS = 256
VMEM_LIMIT_BYTES = 56 * 1024 * 1024

D_MODEL = 1024
CHUNK = 64
ROPE_THETA = 500000.0
NORM_EPS = 1e-6
ROPE_FRACTION_DEN = 4

DSA_HEADS = 4
DSA_HEAD_DIM = 64
DSA_IDX_HEADS = 8
DSA_IDX_DIM = 64
DSA_TOPK_MAX = 256
DSA_IDX_SCALE = (DSA_IDX_HEADS * DSA_IDX_DIM) ** -0.5

DIFF_HEADS = 4
DIFF_QK_DIM = 32
DIFF_V_DIM = 2 * DIFF_QK_DIM

MLA_HEADS = 8
MLA_Q_LORA = 256
MLA_KV_LORA = 128
MLA_NOPE_DIM = 64
MLA_ROPE_DIM = 32
MLA_V_DIM = 64

MOE_EXPERTS = 8
MOE_TOP_K = 2
FFN_TILE = 896
MOE_MAIN_ROWS = (288,)
MOE_EXTRA_ROWS = 128

IN_SPLITS = (
    DSA_HEADS * DSA_HEAD_DIM, DSA_HEAD_DIM, DSA_HEAD_DIM, DSA_IDX_HEADS * DSA_IDX_DIM, DSA_IDX_DIM,
    DSA_IDX_HEADS, DIFF_HEADS * 2 * DIFF_QK_DIM, DIFF_HEADS * 2 * DIFF_QK_DIM, DIFF_HEADS * DIFF_V_DIM,
    MLA_Q_LORA, MLA_KV_LORA, MLA_ROPE_DIM,
)

INT_MIN = -(2 ** 31)
IDX_BIG = 2 ** 30


def _cparams(sem):
    return pltpu.CompilerParams(dimension_semantics=sem, vmem_limit_bytes=VMEM_LIMIT_BYTES)


def _rms(x, g):
    return x * lax.rsqrt(jnp.mean(x * x, axis=-1, keepdims=True) + NORM_EPS) * g


def _dot(a, b):
    return jnp.dot(a, b, preferred_element_type=F32)


def _transpose_bf16(x):
    return x.astype(F32).T.astype(BF16)


def _split3(x):
    hi = x.astype(BF16)
    r1 = x - hi.astype(F32)
    mid = r1.astype(BF16)
    lo = (r1 - mid.astype(F32)).astype(BF16)
    return hi, mid, lo


def _dot_f32_by_exact(x, m_bf16):
    hi, mid, lo = _split3(x)
    return _dot(hi, m_bf16) + _dot(mid, m_bf16) + _dot(lo, m_bf16)


def _inv_freq(rot_dim):
    half = rot_dim // 2
    return ROPE_THETA ** (-(jnp.arange(half, dtype=F32) * 2.0 / rot_dim))


def _rope_patterns():
    lane = np.arange(LANES)
    rots = (DSA_HEAD_DIM // ROPE_FRACTION_DEN, DIFF_QK_DIM // ROPE_FRACTION_DEN, MLA_ROPE_DIM)
    offs = (lane % DSA_HEAD_DIM, lane % DIFF_QK_DIM, lane - MLA_NOPE_DIM)
    zero_lane = LANES - 1
    freq = jnp.zeros((LANES,), F32)
    signs, expand = [], np.zeros((3, LANES, LANES), np.float32)
    base = 0
    for p, (rot, off) in enumerate(zip(rots, offs)):
        half = rot // 2
        active = (off >= 0) & (off < rot)
        freq = freq.at[base:base + half].set(_inv_freq(rot))
        src = np.where(active, base + np.clip(off, 0, rot - 1) % half, zero_lane)
        expand[p, src, lane] = 1.0
        signs.append(np.where(active, np.where(off < half, -1.0, 1.0), 0.0))
        base += half
    assert base < zero_lane
    rows = jnp.concatenate([freq[None, :], jnp.asarray(np.stack(signs), dtype=F32), jnp.zeros((4, LANES), F32)], axis=0)
    return rows, jnp.asarray(expand, dtype=BF16)


ROPE_HALF = (DSA_HEAD_DIM // ROPE_FRACTION_DEN // 2, DIFF_QK_DIM // ROPE_FRACTION_DEN // 2, MLA_ROPE_DIM // 2)


def _x1_mask(pattern):
    lane = lax.broadcasted_iota(jnp.int32, (1, LANES), 1)
    if pattern == 0:
        return (lane % DSA_HEAD_DIM) < ROPE_HALF[0]
    if pattern == 1:
        return (lane % DIFF_QK_DIM) < ROPE_HALF[1]
    return (lane >= MLA_NOPE_DIM) & (lane < MLA_NOPE_DIM + ROPE_HALF[2])


def _rope_tables_kernel(pos_ref, rows_ref, expand_ref, out_ref):
    ang = pos_ref[...] * rows_ref[0:1, :]
    cos, sin = jnp.cos(ang), jnp.sin(ang)
    for p in range(3):
        out_ref[2 * p] = _dot_f32_by_exact(cos, expand_ref[p])
        out_ref[2 * p + 1] = _dot_f32_by_exact(sin, expand_ref[p]) * rows_ref[1 + p:2 + p, :]


def _rope_tables(pos_f, tm):
    n = pos_f.shape[0]
    rows, expand = _rope_patterns()
    return pl.pallas_call(
        _rope_tables_kernel,
        grid=(n // tm,),
        in_specs=[pl.BlockSpec((tm, 1), lambda i: (i, 0)),
                  pl.BlockSpec((8, LANES), lambda i: (0, 0)),
                  pl.BlockSpec((3, LANES, LANES), lambda i: (0, 0, 0))],
        out_specs=pl.BlockSpec((6, tm, LANES), lambda i: (0, i, 0)),
        out_shape=jax.ShapeDtypeStruct((6, n, LANES), F32),
        compiler_params=_cparams(("parallel",)),
        name="rope_tables",
    )(pos_f, rows, expand)


def _rope128(y, cos, sin, pattern):
    half = ROPE_HALF[pattern]
    up = pltpu.roll(y, LANES - half, 1)
    dn = pltpu.roll(y, half, 1)
    return y * cos + jnp.where(_x1_mask(pattern), up, dn) * sin


PROJ_COLS = (
    ("cq", 256, None), ("ckv", 128, None), ("kr", 128, 2),
    ("qa", 256, 0), ("ka2", 128, 0), ("qi", 512, 0), ("ki2", 128, 0),
    ("qb", 256, 1), ("kb", 256, 1),
    ("va2", 128, None), ("wi", 128, None), ("vb", 256, None),
)
PROJ_WIDTH = sum(c[1] for c in PROJ_COLS)
PROJ_OUTS = (("qa", 256, BF16), ("ka2", 128, BF16), ("qi", 512, BF16), ("ki2", 128, BF16),
             ("qb", 256, BF16), ("kb", 256, BF16), ("va2", 128, BF16), ("wi", 128, F32),
             ("vb", 256, BF16), ("qc", 1024, BF16), ("kc", 1024, BF16), ("vc", 512, BF16))
PROJ_TRANSPOSED = ("va2", "vb", "vc")
LOG2E = math.log2(math.e)
QUERY_LOG2_SCALE = {"qa": DSA_HEAD_DIM ** -0.5 * LOG2E, "qb": DIFF_QK_DIM ** -0.5 * LOG2E,
                    "qc": (MLA_NOPE_DIM + MLA_ROPE_DIM) ** -0.5 * LOG2E}
KV_TILE = 256
SUB_KEYS = 128


def _prep_proj_weights(w_in, w_uq, w_ukv):
    offs = np.cumsum((0,) + IN_SPLITS)
    (q_a, k_a, v_a, q_i, k_i, w_i, q_b, k_b, v_b, c_q, c_kv, k_r) = [
        w_in[:, offs[j]:offs[j + 1]] for j in range(len(IN_SPLITS))]
    d = w_in.shape[0]
    z = lambda n: jnp.zeros((d, n), w_in.dtype)
    cols = {
        "qa": q_a, "ka2": jnp.concatenate([k_a, k_a], 1), "qi": q_i, "ki2": jnp.concatenate([k_i, k_i], 1),
        "qb": q_b, "kb": k_b, "va2": jnp.concatenate([v_a, v_a], 1),
        "wi": jnp.concatenate([w_i, z(LANES - DSA_IDX_HEADS)], 1), "vb": v_b, "cq": c_q, "ckv": c_kv,
        "kr": jnp.concatenate([z(MLA_NOPE_DIM), k_r, z(LANES - MLA_NOPE_DIM - MLA_ROPE_DIM)], 1),
    }
    w_p = jnp.concatenate([cols[name] for name, _, _ in PROJ_COLS], axis=1).astype(BF16)
    qd = MLA_NOPE_DIM + MLA_ROPE_DIM
    uq = w_uq.reshape(MLA_Q_LORA, MLA_HEADS, qd)
    uq = jnp.pad(uq, ((0, 0), (0, 0), (0, LANES - qd))).reshape(MLA_Q_LORA, MLA_HEADS * LANES).astype(BF16)
    ukv = w_ukv.reshape(MLA_KV_LORA, MLA_HEADS, MLA_NOPE_DIM + MLA_V_DIM)
    ukn = jnp.pad(ukv[:, :, :MLA_NOPE_DIM], ((0, 0), (0, 0), (0, LANES - MLA_NOPE_DIM)))
    ukn = ukn.reshape(MLA_KV_LORA, MLA_HEADS * LANES).astype(BF16)
    uv = ukv[:, :, MLA_NOPE_DIM:].reshape(MLA_KV_LORA, MLA_HEADS * MLA_V_DIM).astype(BF16)
    return w_p, uq, ukn, uv


def _proj_kernel(h_ref, g_ref, tab_ref, w_ref, qn_ref, uq_ref, kvn_ref, ukn_ref, uv_ref, *out_refs):
    outs = {name: ref for (name, _, _), ref in zip(PROJ_OUTS, out_refs)}
    xn = _rms(h_ref[...], g_ref[...]).astype(BF16)

    def roped(y, pattern):
        return _rope128(y, tab_ref[2 * pattern], tab_ref[2 * pattern + 1], pattern)

    def emit(name, s, y):
        ref = outs[name]
        if name in PROJ_TRANSPOSED:
            tk = ref.shape[2]
            for t in range(ref.shape[0]):
                ref[t, s * LANES:(s + 1) * LANES, :] = y[t * tk:(t + 1) * tk].T.astype(ref.dtype)
        else:
            ref[:, s * LANES:(s + 1) * LANES] = y.astype(ref.dtype)

    def wide_dot(x, w, n_slices):
        per = MXU_COLS // LANES
        res = []
        for c in range(0, n_slices, per):
            y = _dot(x, w[:, c * LANES:(c + per) * LANES])
            res += [y[:, k * LANES:(k + 1) * LANES] for k in range(min(per, n_slices - c))]
        return res

    slices = wide_dot(xn, w_ref, PROJ_WIDTH // LANES)
    vals = {}
    i = 0
    for name, width, pattern in PROJ_COLS:
        for s in range(width // LANES):
            y = slices[i] if pattern is None else roped(slices[i], pattern)
            i += 1
            if name in outs:
                emit(name, s, y * QUERY_LOG2_SCALE[name] if name in QUERY_LOG2_SCALE else y)
            else:
                vals.setdefault(name, []).append(y)

    cq = jnp.concatenate(vals["cq"], axis=1)
    cqn = _rms(cq, qn_ref[...]).astype(BF16)
    for hd, y in enumerate(wide_dot(cqn, uq_ref, MLA_HEADS)):
        outs["qc"][:, hd * LANES:(hd + 1) * LANES] = (roped(y, 2) * QUERY_LOG2_SCALE["qc"]).astype(BF16)

    ckvn = _rms(vals["ckv"][0], kvn_ref[...]).astype(BF16)
    kr = vals["kr"][0]
    for hd, y in enumerate(wide_dot(ckvn, ukn_ref, MLA_HEADS)):
        outs["kc"][:, hd * LANES:(hd + 1) * LANES] = (y + kr).astype(BF16)
    for s, y in enumerate(wide_dot(ckvn, uv_ref, MLA_HEADS * MLA_V_DIM // LANES)):
        emit("vc", s, y)


def _project(h, gain, tables, w_p, q_norm, uq, kv_norm, ukn, uv, tm, tk):
    n, d = h.shape
    full = lambda a: pl.BlockSpec(a.shape, lambda i: (0,) * a.ndim)
    gain = gain.reshape(1, d)
    q_norm = q_norm.reshape(1, -1)
    kv_norm = kv_norm.reshape(1, -1)
    out_specs, out_shape = [], []
    for name, w, dt in PROJ_OUTS:
        if name in PROJ_TRANSPOSED:
            out_specs.append(pl.BlockSpec((tm // tk, w, tk), lambda i: (i, 0, 0)))
            out_shape.append(jax.ShapeDtypeStruct((n // tk, w, tk), dt))
        else:
            out_specs.append(pl.BlockSpec((tm, w), lambda i: (i, 0)))
            out_shape.append(jax.ShapeDtypeStruct((n, w), dt))
    return pl.pallas_call(
        _proj_kernel,
        grid=(n // tm,),
        in_specs=[pl.BlockSpec((tm, d), lambda i: (i, 0)), full(gain),
                  pl.BlockSpec((6, tm, LANES), lambda i: (0, i, 0)),
                  full(w_p), full(q_norm), full(uq), full(kv_norm), full(ukn), full(uv)],
        out_specs=out_specs,
        out_shape=out_shape,
        compiler_params=_cparams(("parallel",)),
        name="projection",
    )(h, gain, tables, w_p, q_norm, uq, kv_norm, ukn, uv)


def _half_masks(dtype, group):
    lane = lax.broadcasted_iota(jnp.int32, (1, LANES), 1)
    return [jnp.where((lane // group) == u, 1.0, 0.0).astype(dtype) for u in range(LANES // group)]


def _softmax_step(st, v_t, m, l, acc_ref, guard_empty=False):
    m_new = jnp.maximum(m, jnp.max(st, axis=0, keepdims=True))
    m_use = jnp.where(m_new == -jnp.inf, 0.0, m_new) if guard_empty else m_new
    alpha = jnp.exp2(m - m_use)
    p = jnp.exp2(st - m_use)
    l_new = alpha * l + jnp.sum(p, axis=0, keepdims=True)
    acc_ref[...] = alpha * acc_ref[...] + _dot(v_t, p.astype(BF16))
    return m_new, l_new


def _diag_mask(tk, m_cols, q_off):
    key_chunk = lax.broadcasted_iota(jnp.int32, (tk, 1), 0) // CHUNK
    qry_chunk = (q_off + lax.broadcasted_iota(jnp.int32, (1, m_cols), 1)) // CHUNK
    return key_chunk <= qry_chunk


def _causal_flash(problems, k_ref, v_ref, acc_ref, qi, tq, sub):
    problems = [(_transpose_bf16(q), k_lane0, v_row0, q_off) for q, k_lane0, v_row0, q_off in problems]
    m_cols = problems[0][0].shape[1]
    n_p = len(problems)
    acc_ref[...] = jnp.zeros_like(acc_ref)
    init = tuple(jnp.full((1, m_cols), -jnp.inf, F32) for _ in range(n_p)) + \
        tuple(jnp.zeros((1, m_cols), F32) for _ in range(n_p))

    def step(j, carry, masked):
        ms, ls = list(carry[:n_p]), list(carry[n_p:])
        for s0 in range(0, tq, sub):
            r0 = pl.multiple_of(j * tq + s0, sub)
            for p, (q_t, k_lane0, v_row0, q_off) in enumerate(problems):
                st = _dot(k_ref[pl.ds(r0, sub), k_lane0:k_lane0 + LANES], q_t)
                if masked:
                    st = jnp.where(_diag_mask(tq, m_cols, q_off)[s0:s0 + sub], st, -jnp.inf)
                ms[p], ls[p] = _softmax_step(st, v_ref[j, v_row0:v_row0 + LANES, s0:s0 + sub], ms[p], ls[p],
                                             acc_ref.at[p])
        return tuple(ms) + tuple(ls)

    carry = lax.fori_loop(0, qi, lambda j, c: step(j, c, False), init)
    carry = step(qi, carry, True)
    return [acc_ref[p] / carry[n_p + p] for p in range(n_p)]


def _diff_kernel(lam_ref, subln_ref, gsum_ref, q_ref, k_ref, v_ref, o_ref, acc_ref, *, tq, lam_init):
    qi = pl.program_id(1)
    lam_rows = lam_ref[...]
    s1 = jnp.sum(lam_rows[0:1] * lam_rows[1:2], axis=1, keepdims=True)
    s2 = jnp.sum(lam_rows[2:3] * lam_rows[3:4], axis=1, keepdims=True)
    lam = jnp.exp(s1) - jnp.exp(s2) + lam_init
    masks = _half_masks(BF16, DIFF_QK_DIM)
    row = lax.broadcasted_iota(jnp.int32, (LANES, 1), 0)
    n_slices = DIFF_HEADS // 2
    problems = []
    for s in range(n_slices):
        q = q_ref[0, :, s * LANES:(s + 1) * LANES]
        problems += [(q * mk, s * LANES, s * LANES, 0) for mk in masks]
    outs = _causal_flash(problems, k_ref.at[0], v_ref, acc_ref, qi, tq, min(tq, SUB_KEYS))
    for s in range(n_slices):
        o0, o1, o2, o3 = outs[4 * s:4 * s + 4]
        a0 = o0 - lam * o1
        a1 = o2 - lam * o3
        a = jnp.where(row < DIFF_V_DIM, a0, a1).T
        ss = _dot_f32_by_exact(a * a, gsum_ref[...])
        y = a * lax.rsqrt(ss * (1.0 / DIFF_V_DIM) + NORM_EPS) * subln_ref[...]
        o_ref[0, :, s * LANES:(s + 1) * LANES] = (y * (1.0 - lam_init)).astype(o_ref.dtype)


def _diff_attention(qb, kb, vb_t, lam_rows, subln, layer, tq):
    b, s, w = qb.shape
    lam_init = 0.8 - 0.6 * math.exp(-0.3 * layer)
    subln2 = jnp.concatenate([subln, subln]).reshape(1, LANES).astype(F32)
    lane = np.arange(LANES)
    gsum = jnp.asarray((lane[:, None] // DIFF_V_DIM) == (lane[None, :] // DIFF_V_DIM), dtype=BF16)
    return pl.pallas_call(
        functools.partial(_diff_kernel, tq=tq, lam_init=lam_init),
        grid=(b, s // tq),
        in_specs=[pl.BlockSpec((8, LANES), lambda i, j: (0, 0)),
                  pl.BlockSpec((1, LANES), lambda i, j: (0, 0)),
                  pl.BlockSpec((LANES, LANES), lambda i, j: (0, 0)),
                  pl.BlockSpec((1, tq, w), lambda i, j: (i, j, 0)),
                  pl.BlockSpec((1, s, w), lambda i, j: (i, 0, 0)),
                  pl.BlockSpec((s // tq, w, tq), lambda i, j: (i, 0, 0))],
        out_specs=pl.BlockSpec((1, tq, w), lambda i, j: (i, j, 0)),
        out_shape=jax.ShapeDtypeStruct((b, s, w), BF16),
        scratch_shapes=[pltpu.VMEM((2 * DIFF_HEADS, LANES, tq), F32)],
        compiler_params=_cparams(("parallel", "arbitrary")),
        name="diff_attention",
    )(lam_rows, subln2, gsum, qb, kb, vb_t)


def _mla_kernel(q_ref, k_ref, v_ref, o_ref, acc_ref, *, tq):
    qi = pl.program_id(1)
    row = lax.broadcasted_iota(jnp.int32, (LANES, 1), 0)
    problems = [(q_ref[0, :, hd * LANES:(hd + 1) * LANES], hd * LANES, (hd // 2) * LANES, 0)
                for hd in range(MLA_HEADS)]
    outs = _causal_flash(problems, k_ref.at[0], v_ref, acc_ref, qi, tq, min(tq, SUB_KEYS))
    for pair in range(MLA_HEADS // 2):
        o_t = jnp.where(row < MLA_V_DIM, outs[2 * pair], outs[2 * pair + 1])
        o_ref[0, :, pair * LANES:(pair + 1) * LANES] = o_t.T.astype(o_ref.dtype)


def _mla_attention(qc, kc, vc_t, tq):
    b, s, wq = qc.shape
    wv = vc_t.shape[1]
    return pl.pallas_call(
        functools.partial(_mla_kernel, tq=tq),
        grid=(b, s // tq),
        in_specs=[pl.BlockSpec((1, tq, wq), lambda i, j: (i, j, 0)),
                  pl.BlockSpec((1, s, wq), lambda i, j: (i, 0, 0)),
                  pl.BlockSpec((s // tq, wv, tq), lambda i, j: (i, 0, 0))],
        out_specs=pl.BlockSpec((1, tq, wv), lambda i, j: (i, j, 0)),
        out_shape=jax.ShapeDtypeStruct((b, s, wv), BF16),
        scratch_shapes=[pltpu.VMEM((MLA_HEADS, LANES, tq), F32)],
        compiler_params=_cparams(("parallel", "arbitrary")),
        name="mla_attention",
    )(qc, kc, vc_t)


KEY_NEG_INF = -2139095041


def _score_keys(score):
    bits = lax.bitcast_convert_type(score, jnp.int32)
    return bits ^ ((bits >> 31) & 0x7FFFFFFF)


def _dsa_kernel(qa_ref, qi_ref, wi_ref, ka_ref, va_ref, ki_ref, o_ref, key_ref, hi_ref, lo_ref, aux_ref, bias_ref, acc_ref,
                *, tq, tk, top_k):
    blk = pl.program_id(1)
    s_len = ka_ref.shape[1]
    n_kv = ((blk + 1) * tq + tk - 1) // tk
    masks = _half_masks(BF16, DSA_IDX_DIM)
    row = lax.broadcasted_iota(jnp.int32, (LANES, 1), 0)
    t_chunk = (blk * tq + lax.broadcasted_iota(jnp.int32, (1, tq), 1)) // CHUNK
    sub_idx = lax.broadcasted_iota(jnp.int32, (tk, 1), 0)

    def tile_start(c):
        return pl.multiple_of(c * tk, tk)

    qi_st = jnp.concatenate([qi_ref[0, :, (hd // 2) * LANES:(hd // 2 + 1) * LANES] * masks[hd % 2]
                             for hd in range(DSA_IDX_HEADS)], axis=0)
    qi_t = _transpose_bf16(qi_st)
    w_t = wi_ref[0].T

    sub = min(tk, SUB_KEYS)
    sub_iota = lax.broadcasted_iota(jnp.int32, (sub, 1), 0)

    def score_body(c, carry):
        for s0 in range(0, tk, sub):
            r0 = pl.multiple_of(c * tk + s0, sub)
            ki = ki_ref[0, pl.ds(r0, sub), :]
            score = jnp.zeros((sub, tq), F32)
            for hd in range(DSA_IDX_HEADS):
                rel = jnp.maximum(_dot(ki, qi_t[:, hd * tq:(hd + 1) * tq]), 0.0)
                score = score + w_t[hd:hd + 1, :] * rel
            score = score * DSA_IDX_SCALE
            score = jnp.where((r0 + sub_iota) // CHUNK <= t_chunk, score, -jnp.inf)
            key = _score_keys(score)
            key_ref[pl.ds(r0, sub), :] = key
            hi_ref[pl.ds(r0, sub), :] = (key >> 16).astype(jnp.int16)
            lo_ref[pl.ds(r0, sub), :] = ((key & 0xFFFF) - 32768).astype(jnp.int16)
        return carry

    lax.fori_loop(0, n_kv, score_body, 0)

    def count16(ref, cand, strict=False):
        cand16 = cand.astype(jnp.int16)

        def body(c, tot):
            x = ref[pl.ds(tile_start(c), tk), :]
            ones = jnp.where((x > cand16) if strict else (x >= cand16), jnp.bfloat16(1), jnp.bfloat16(0))
            parts = [ones[i * 16:(i + 1) * 16] for i in range(tk // 16)]
            while len(parts) > 1:
                parts = [parts[i] + parts[i + 1] for i in range(0, len(parts), 2)]
            return tot + parts[0].astype(F32)
        tot = lax.fori_loop(0, n_kv, body, jnp.zeros((16, tq), F32))
        return jnp.sum(tot, axis=0, keepdims=True)

    def kth_largest16(ref, need):
        start = jnp.where(count16(ref, jnp.zeros((1, tq), jnp.int32)) >= need, 0, -32768).astype(jnp.int32)

        def body(i, t):
            cand = t | (jnp.int32(1) << (14 - i))
            return jnp.where(count16(ref, cand) >= need, cand, t)
        return lax.fori_loop(0, 15, body, start)

    def count(pred_fn):
        n_part = 8

        def body(c, tot):
            ones = jnp.where(pred_fn(tile_start(c)), 1.0, 0.0)
            return tot + jnp.sum(ones.reshape(tk // (8 * n_part), n_part * 8, tq), axis=0)
        tot = lax.fori_loop(0, n_kv, body, jnp.zeros((n_part * 8, tq), F32))
        return jnp.sum(tot, axis=0, keepdims=True)

    thr_hi = kth_largest16(hi_ref, float(top_k))
    need_lo = top_k - count16(hi_ref, thr_hi, strict=True)
    thr_hi16 = thr_hi.astype(jnp.int16)

    def bucket_body(c, carry):
        r0 = tile_start(c)
        lo_ref[pl.ds(r0, tk), :] = jnp.where(hi_ref[pl.ds(r0, tk), :] == thr_hi16, lo_ref[pl.ds(r0, tk), :],
                                             jnp.int16(-32768))
        return carry

    lax.fori_loop(0, n_kv, bucket_body, 0)
    thr_lo = kth_largest16(lo_ref, need_lo)
    thr = (thr_hi << 16) | ((thr_lo + 32768) & 0xFFFF)
    cnt_thr = count(lambda r0: key_ref[pl.ds(r0, tk), :] >= thr)
    thr = jnp.maximum(thr, KEY_NEG_INF)
    tied = jnp.where(cnt_thr > top_k, jnp.where(thr > KEY_NEG_INF, 1.0, 0.0), 0.0)

    def select_with_ties():
        n_gt = count(lambda r0: key_ref[pl.ds(r0, tk), :] > thr)
        need = top_k - n_gt

        def aux_body(c, carry):
            r0 = tile_start(c)
            idx = r0 + sub_idx
            tie = jnp.where(key_ref[pl.ds(r0, tk), :] == thr, jnp.where(idx // CHUNK <= t_chunk, idx, IDX_BIG), IDX_BIG)
            aux_ref[pl.ds(r0, tk), :] = tie
            return carry

        lax.fori_loop(0, n_kv, aux_body, 0)
        n_bits = max(1, (s_len - 1).bit_length())

        def tie_body(i, last):
            cand = last | (jnp.int32(1) << (n_bits - 1 - i))
            cnt = count(lambda r0: aux_ref[pl.ds(r0, tk), :] < cand)
            return jnp.where(cnt < need, cand, last)

        last = lax.fori_loop(0, n_bits, tie_body, jnp.zeros((1, tq), jnp.int32))

        def bias_body(c, carry):
            r0 = tile_start(c)
            bias_ref[pl.ds(r0, tk), :] = jnp.where(
                key_ref[pl.ds(r0, tk), :] > thr, 0.0, jnp.where(aux_ref[pl.ds(r0, tk), :] <= last, 0.0, -jnp.inf))
            return carry

        lax.fori_loop(0, n_kv, bias_body, 0)

    def select_no_ties():
        low = jnp.where(thr > KEY_NEG_INF, thr - 1, thr)

        def bias_body(c, carry):
            r0 = tile_start(c)
            bias_ref[pl.ds(r0, tk), :] = jnp.where(key_ref[pl.ds(r0, tk), :] > low, 0.0, -jnp.inf)
            return carry

        lax.fori_loop(0, n_kv, bias_body, 0)

    lax.cond(jnp.max(tied) > 0.0, select_with_ties, select_no_ties)

    hmask = _half_masks(BF16, DSA_HEAD_DIM)
    qa_st = jnp.concatenate([qa_ref[0, :, (hd // 2) * LANES:(hd // 2 + 1) * LANES] * hmask[hd % 2]
                             for hd in range(DSA_HEADS)], axis=0)
    qa_t = _transpose_bf16(qa_st)
    n_h = DSA_HEADS
    acc_ref[...] = jnp.zeros_like(acc_ref)

    def att_body(c, carry):
        ms, ls = list(carry[:n_h]), list(carry[n_h:])
        for s0 in range(0, tk, sub):
            r0 = pl.multiple_of(c * tk + s0, sub)
            bias = bias_ref[pl.ds(r0, sub), :]
            ka = ka_ref[0, pl.ds(r0, sub), :]
            for hd in range(n_h):
                st = _dot(ka, qa_t[:, hd * tq:(hd + 1) * tq]) + bias
                ms[hd], ls[hd] = _softmax_step(st, va_ref[c, :, s0:s0 + sub], ms[hd], ls[hd], acc_ref.at[hd],
                                               guard_empty=True)
        return tuple(ms) + tuple(ls)

    init = tuple(jnp.full((1, tq), -jnp.inf, F32) for _ in range(n_h)) + \
        tuple(jnp.zeros((1, tq), F32) for _ in range(n_h))
    carry = lax.fori_loop(0, n_kv, att_body, init)
    for p in range(n_h // 2):
        pair_t = jnp.where(row < DSA_HEAD_DIM, acc_ref[2 * p] / carry[n_h + 2 * p],
                           acc_ref[2 * p + 1] / carry[n_h + 2 * p + 1])
        o_ref[0, :, p * LANES:(p + 1) * LANES] = pair_t.T.astype(o_ref.dtype)


def _dsa_attention(qa, qi, wi, ka2, va2_t, ki2, tq, tk):
    b, s, _ = qa.shape
    top_k = min(DSA_TOPK_MAX, s // 4)
    qspec = lambda w: pl.BlockSpec((1, tq, w), lambda i, j: (i, j, 0))
    kspec = lambda w: pl.BlockSpec((1, s, w), lambda i, j: (i, 0, 0))
    return pl.pallas_call(
        functools.partial(_dsa_kernel, tq=tq, tk=tk, top_k=top_k),
        grid=(b, s // tq),
        in_specs=[qspec(qa.shape[2]), qspec(qi.shape[2]), qspec(LANES), kspec(LANES),
                  pl.BlockSpec((s // tk, LANES, tk), lambda i, j: (i, 0, 0)), kspec(LANES)],
        out_specs=qspec(qa.shape[2]),
        out_shape=jax.ShapeDtypeStruct(qa.shape, BF16),
        scratch_shapes=[pltpu.VMEM((s, tq), jnp.int32), pltpu.VMEM((s, tq), jnp.int16), pltpu.VMEM((s, tq), jnp.int16),
                        pltpu.VMEM((s, tq), jnp.int32), pltpu.VMEM((s, tq), F32),
                        pltpu.VMEM((DSA_HEADS, LANES, tq), F32)],
        compiler_params=_cparams(("parallel", "arbitrary")),
        name="dsa_attention",
    )(qa, qi, wi, ka2, va2_t, ki2)


def _attn_residual(h_ref, oa_ref, ob_ref, oc_ref, wo_ref):
    wa, wb = oa_ref.shape[1], ob_ref.shape[1]
    return (h_ref[...] + _dot(oa_ref[...], wo_ref[0:wa, :]) + _dot(ob_ref[...], wo_ref[wa:wa + wb, :])
            + _dot(oc_ref[...], wo_ref[wa + wb:, :]))


def _swiglu_partial(x, wg, wu, wd):
    g = _dot(x, wg)
    u = _dot(x, wu)
    return _dot(((g * jax.nn.sigmoid(g)) * u).astype(BF16), wd)


def _dense_block_kernel(h_ref, oa_ref, ob_ref, oc_ref, wo_ref, g_ref, wg_ref, wu_ref, wd_ref, fg_ref,
                        out_ref, xn_ref, acc_ref, *, final_norm):
    f = pl.program_id(1)

    @pl.when(f == 0)
    def _():
        h1 = _attn_residual(h_ref, oa_ref, ob_ref, oc_ref, wo_ref)
        acc_ref[...] = h1
        xn_ref[...] = _rms(h1, g_ref[...]).astype(BF16)

    acc_ref[...] += _swiglu_partial(xn_ref[...], wg_ref[...], wu_ref[...], wd_ref[...])

    @pl.when(f == pl.num_programs(1) - 1)
    def _():
        y = acc_ref[...]
        out_ref[...] = _rms(y, fg_ref[...]) if final_norm else y


def _dense_block(h, oa, ob, oc, w_out, gain, wg, wu, wd, final_gain, *, final_norm, tm, tf):
    n, d = h.shape
    dff = wg.shape[1]
    const = lambda a: pl.BlockSpec(a.shape, lambda i, f: (0,) * a.ndim, pipeline_mode=pl.Buffered(1))
    row = lambda w: pl.BlockSpec((tm, w), lambda i, f: (i, 0))
    gain = gain.reshape(1, d)
    final_gain = final_gain.reshape(1, d)
    return pl.pallas_call(
        functools.partial(_dense_block_kernel, final_norm=final_norm),
        grid=(n // tm, dff // tf),
        in_specs=[row(d), row(oa.shape[1]), row(ob.shape[1]), row(oc.shape[1]), const(w_out), const(gain),
                  pl.BlockSpec((d, tf), lambda i, f: (0, f)),
                  pl.BlockSpec((d, tf), lambda i, f: (0, f)),
                  pl.BlockSpec((tf, d), lambda i, f: (f, 0)),
                  const(final_gain)],
        out_specs=row(d),
        out_shape=jax.ShapeDtypeStruct((n, d), F32),
        scratch_shapes=[pltpu.VMEM((tm, d), BF16), pltpu.VMEM((tm, d), F32)],
        compiler_params=_cparams(("parallel", "arbitrary")),
        name="block_dense",
    )(h, oa, ob, oc, w_out, gain, wg, wu, wd, final_gain)


def _moe_block_kernel(h_ref, oa_ref, ob_ref, oc_ref, wo_ref, g_ref, r_ref, tri_ref, wg_ref, wu_ref, wd_ref, fg_ref,
                      out_ref, xn_ref, acc_ref, gate_ref, pos_ref, pos_t_ref, cnt_ref, xs_ref, ye_ref,
                      *, n_exp, mains, rows, final_norm):
    e = pl.program_id(1)
    f = pl.program_id(2)
    n_f = pl.num_programs(2)
    tm = xn_ref.shape[0]
    lane = lax.broadcasted_iota(jnp.int32, (1, LANES), 1)

    @pl.when((e == 0) & (f == 0))
    def _():
        h1 = _attn_residual(h_ref, oa_ref, ob_ref, oc_ref, wo_ref)
        acc_ref[...] = h1
        hn = _rms(h1, g_ref[...])
        xn_ref[...] = hn.astype(BF16)
        hn_hi = hn.astype(BF16)
        hn_lo = (hn - hn_hi.astype(F32)).astype(BF16)
        parts = _dot(hn_hi, r_ref[...]) + _dot(hn_lo, r_ref[...])
        logits = parts[:, :LANES] + parts[:, LANES:]
        logits = jnp.where(lane < n_exp, logits, -jnp.inf)
        m1 = jnp.max(logits, axis=1, keepdims=True)
        i1 = jnp.min(jnp.where(logits == m1, lane, IDX_BIG), axis=1, keepdims=True)
        rest = jnp.where(lane == i1, -jnp.inf, logits)
        m2 = jnp.max(rest, axis=1, keepdims=True)
        i2 = jnp.min(jnp.where(rest == m2, lane, IDX_BIG), axis=1, keepdims=True)
        e2 = jnp.exp(m2 - m1)
        den = 1.0 + e2
        gate_ref[...] = jnp.where(lane == i1, 1.0 / den, jnp.where(lane == i2, e2 / den, 0.0))
        routed = jnp.where(lane == i1, 1.0, jnp.where(lane == i2, 1.0, 0.0))
        before = _dot(tri_ref[...], routed.astype(BF16))
        slot = jnp.where(routed > 0.0, before, -1.0)
        pos_ref[...] = slot
        pos_t_ref[...] = slot.T
        cnt = jnp.sum(routed, axis=0, keepdims=True)
        for k in range(n_exp):
            cnt_ref[k] = jnp.sum(jnp.where(lane == k, cnt, 0.0)).astype(jnp.int32)

    n_tok = cnt_ref[e]
    top = mains[-1]
    n_extra = (jnp.maximum(n_tok - top, 0) + rows - 1) // rows

    def for_blocks(fn):
        below = 0
        for size in mains:
            fits = (n_tok > below) if size == top else ((n_tok > below) & (n_tok <= size))

            @pl.when(fits)
            def _(size=size):
                fn(0, size)
                if size == top:
                    lax.fori_loop(0, n_extra, lambda r, c: (
                        fn(pl.multiple_of(top + r * rows, math.gcd(top, rows)), rows), c)[1], 0)

            below = size

    @pl.when(f == 0)
    def _():
        slot_row = pos_t_ref[pl.ds(e, 1), :]

        def gather(r0, nr):
            want = (r0 + lax.broadcasted_iota(jnp.int32, (nr, 1), 0)).astype(F32)
            pick = jnp.where(slot_row == want, 1.0, 0.0).astype(BF16)
            xs_ref[pl.ds(r0, nr), :] = _dot(pick, xn_ref[...]).astype(BF16)
            ye_ref[pl.ds(r0, nr), :] = jnp.zeros((nr, ye_ref.shape[1]), F32)

        for_blocks(gather)

    def expert(r0, nr):
        ye_ref[pl.ds(r0, nr), :] += _swiglu_partial(xs_ref[pl.ds(r0, nr), :], wg_ref[0], wu_ref[0], wd_ref[0])

    for_blocks(expert)

    @pl.when(f == n_f - 1)
    def _():
        full_lane = lax.broadcasted_iota(jnp.int32, (tm, LANES), 1)
        slot_col = jnp.sum(jnp.where(full_lane == e, pos_ref[...], 0.0), axis=1, keepdims=True)
        gate_col = jnp.sum(jnp.where(full_lane == e, gate_ref[...], 0.0), axis=1, keepdims=True)

        def scatter(r0, nr):
            have = (r0 + lax.broadcasted_iota(jnp.int32, (1, nr), 1)).astype(F32)
            place = jnp.where(slot_col == have, 1.0, 0.0).astype(BF16)
            y = ye_ref[pl.ds(r0, nr), :]
            y_hi = y.astype(BF16)
            y_lo = (y - y_hi.astype(F32)).astype(BF16)
            acc_ref[...] += gate_col * (_dot(place, y_hi) + _dot(place, y_lo))

        for_blocks(scatter)

    @pl.when((e == n_exp - 1) & (f == n_f - 1))
    def _():
        y = acc_ref[...]
        out_ref[...] = _rms(y, fg_ref[...]) if final_norm else y


def _moe_block(h, oa, ob, oc, w_out, gain, router_p, wg, wu, wd, final_gain, *, final_norm, tm, tf, mains, rows):
    n, d = h.shape
    n_exp, _, dff = wg.shape
    const = lambda a: pl.BlockSpec(a.shape, lambda i, e, f: (0,) * a.ndim, pipeline_mode=pl.Buffered(1))
    row = lambda w: pl.BlockSpec((tm, w), lambda i, e, f: (i, 0))
    row_in = lambda w: pl.BlockSpec((tm, w), lambda i, e, f: (i, 0), pipeline_mode=pl.Buffered(1))
    gain = gain.reshape(1, d)
    final_gain = final_gain.reshape(1, d)
    tok = np.arange(tm)
    tri = jnp.asarray(tok[None, :] < tok[:, None], dtype=BF16)
    mains = tuple(sorted({min(m, tm) for m in mains}))
    cap = mains[-1] + -(-(tm - mains[-1]) // rows) * rows
    return pl.pallas_call(
        functools.partial(_moe_block_kernel, n_exp=n_exp, mains=mains, rows=rows, final_norm=final_norm),
        grid=(n // tm, n_exp, dff // tf),
        in_specs=[row_in(d), row_in(oa.shape[1]), row_in(ob.shape[1]), row_in(oc.shape[1]), const(w_out), const(gain),
                  const(router_p), const(tri),
                  pl.BlockSpec((1, d, tf), lambda i, e, f: (e, 0, f)),
                  pl.BlockSpec((1, d, tf), lambda i, e, f: (e, 0, f)),
                  pl.BlockSpec((1, tf, d), lambda i, e, f: (e, f, 0)),
                  const(final_gain)],
        out_specs=row(d),
        out_shape=jax.ShapeDtypeStruct((n, d), F32),
        scratch_shapes=[pltpu.VMEM((tm, d), BF16), pltpu.VMEM((tm, d), F32), pltpu.VMEM((tm, LANES), F32),
                        pltpu.VMEM((tm, LANES), F32), pltpu.VMEM((LANES, tm), F32), pltpu.SMEM((n_exp,), jnp.int32),
                        pltpu.VMEM((cap, d), BF16), pltpu.VMEM((cap, d), F32)],
        compiler_params=_cparams(("parallel", "arbitrary", "arbitrary")),
        name="block_moe",
    )(h, oa, ob, oc, w_out, gain, router_p, tri, wg, wu, wd, final_gain)


def _pick(n, pref):
    t = min(pref, n)
    while n % t:
        t //= 2
    return t


def kernel(x, positions, attn_norm, w_in, mla_q_norm, w_uq, mla_kv_norm, w_ukv, diff_lambda_q1, diff_lambda_k1, diff_lambda_q2, diff_lambda_k2, diff_subln, w_out, ffn_norm, dense_w_gate, dense_w_up, dense_w_down, moe_router, moe_w_gate, moe_w_up, moe_w_down, final_norm):
    b, s, d = x.shape
    n = b * s
    depth = w_in.shape[0]
    tm_proj = _pick(n, 512)
    tm_blk = _pick(n, 1024)
    tk = _pick(s, KV_TILE)
    tq_dsa = _pick(s, 256)

    tables = _rope_tables(positions.astype(F32).reshape(n, 1), _pick(n, 1024))
    h = x.reshape(n, d)
    r3 = lambda a: a.reshape(b, s, a.shape[-1])
    for layer in range(depth):
        w_p, uq, ukn, uv = _prep_proj_weights(w_in[layer], w_uq[layer], w_ukv[layer])
        (qa, ka2, qi, ki2, qb, kb, va2, wi, vb, qc, kc, vc) = _project(
            h, attn_norm[layer], tables, w_p, mla_q_norm[layer], uq, mla_kv_norm[layer], ukn, uv, tm_proj, tk)
        oa = _dsa_attention(r3(qa), r3(qi), r3(wi), r3(ka2), va2, r3(ki2), tq_dsa, tk)
        lam_rows = jnp.zeros((8, LANES), F32).at[0:4, 0:DIFF_QK_DIM].set(jnp.stack(
            [diff_lambda_q1[layer], diff_lambda_k1[layer], diff_lambda_q2[layer], diff_lambda_k2[layer]]))
        ob = _diff_attention(r3(qb), r3(kb), vb, lam_rows, diff_subln[layer], layer, tk)
        oc = _mla_attention(r3(qc), r3(kc), vc, tk)
        j = layer // 2
        last = layer == depth - 1
        wo = w_out[layer].astype(BF16)
        attn = (oa.reshape(n, -1), ob.reshape(n, -1), oc.reshape(n, -1))
        if layer % 2 == 0:
            h = _dense_block(h, *attn, wo, ffn_norm[layer], dense_w_gate[j].astype(BF16), dense_w_up[j].astype(BF16),
                             dense_w_down[j].astype(BF16), final_norm,
                             final_norm=last, tm=tm_blk, tf=_pick(dense_w_gate.shape[2], FFN_TILE))
        else:
            router_f = jnp.pad(moe_router[j], ((0, 0), (0, LANES - MOE_EXPERTS)))
            router_hi = router_f.astype(BF16)
            router_p = jnp.concatenate([router_hi, (router_f - router_hi.astype(F32)).astype(BF16)], axis=1)
            h = _moe_block(h, *attn, wo, ffn_norm[layer], router_p,
                           moe_w_gate[j].astype(BF16), moe_w_up[j].astype(BF16), moe_w_down[j].astype(BF16), final_norm,
                           final_norm=last, tm=tm_blk, tf=_pick(moe_w_gate.shape[3], FFN_TILE),
                           mains=MOE_MAIN_ROWS, rows=MOE_EXTRA_ROWS)
    return h.reshape(b, s, d)
```

```python
import functools
import math

import jax
import jax.numpy as jnp
import numpy as np
from jax import lax
from jax.experimental import pallas as pl
from jax.experimental.pallas import tpu as pltpu

F32 = jnp.float32
BF16 = jnp.bfloat16

LANES = 128
MXU_COLS = 256
VMEM_LIMIT_BYTES = 56 * 1024 * 1024

CHUNK = 64
ROPE_THETA = 500000.0
NORM_EPS = 1e-6
ROPE_FRACTION_DEN = 4

DSA_HEADS = 4
DSA_HEAD_DIM = 64
DSA_IDX_HEADS = 8
DSA_IDX_DIM = 64
DSA_TOPK_MAX = 256
DSA_IDX_SCALE = (DSA_IDX_HEADS * DSA_IDX_DIM) ** -0.5

DIFF_HEADS = 4
DIFF_QK_DIM = 32
DIFF_V_DIM = 2 * DIFF_QK_DIM

MLA_HEADS = 8
MLA_Q_LORA = 256
MLA_KV_LORA = 128
MLA_NOPE_DIM = 64
MLA_ROPE_DIM = 32
MLA_V_DIM = 64

MOE_EXPERTS = 8
FFN_TILE = 896
DENSE_FFN_TILE = 512
MOE_MAIN_ROWS = (256, 320)
MOE_EXTRA_ROWS = 128

IN_SPLITS = (
    DSA_HEADS * DSA_HEAD_DIM, DSA_HEAD_DIM, DSA_HEAD_DIM, DSA_IDX_HEADS * DSA_IDX_DIM, DSA_IDX_DIM,
    DSA_IDX_HEADS, DIFF_HEADS * 2 * DIFF_QK_DIM, DIFF_HEADS * 2 * DIFF_QK_DIM, DIFF_HEADS * DIFF_V_DIM,
    MLA_Q_LORA, MLA_KV_LORA, MLA_ROPE_DIM,
)

IDX_BIG = 2 ** 30


def _cparams(sem):
    return pltpu.CompilerParams(dimension_semantics=sem, vmem_limit_bytes=VMEM_LIMIT_BYTES)


def _rms(x, g):
    return x * lax.rsqrt(jnp.mean(x * x, axis=-1, keepdims=True) + NORM_EPS) * g


def _dot(a, b):
    return jnp.dot(a, b, preferred_element_type=F32)


def _transpose_bf16(x):
    return x.astype(F32).T.astype(BF16)


def _split3(x):
    hi = x.astype(BF16)
    r1 = x - hi.astype(F32)
    mid = r1.astype(BF16)
    lo = (r1 - mid.astype(F32)).astype(BF16)
    return hi, mid, lo


def _dot_f32_by_exact(x, m_bf16):
    hi, mid, lo = _split3(x)
    return _dot(hi, m_bf16) + _dot(mid, m_bf16) + _dot(lo, m_bf16)


def _inv_freq(rot_dim):
    half = rot_dim // 2
    return ROPE_THETA ** (-(jnp.arange(half, dtype=F32) * 2.0 / rot_dim))


def _rope_patterns():
    lane = np.arange(LANES)
    rots = (DSA_HEAD_DIM // ROPE_FRACTION_DEN, DIFF_QK_DIM // ROPE_FRACTION_DEN, MLA_ROPE_DIM)
    offs = (lane % DSA_HEAD_DIM, lane % DIFF_QK_DIM, lane - MLA_NOPE_DIM)
    zero_lane = LANES - 1
    freq = jnp.zeros((LANES,), F32)
    signs, expand = [], np.zeros((3, LANES, LANES), np.float32)
    base = 0
    for p, (rot, off) in enumerate(zip(rots, offs)):
        half = rot // 2
        active = (off >= 0) & (off < rot)
        freq = freq.at[base:base + half].set(_inv_freq(rot))
        src = np.where(active, base + np.clip(off, 0, rot - 1) % half, zero_lane)
        expand[p, src, lane] = 1.0
        signs.append(np.where(active, np.where(off < half, -1.0, 1.0), 0.0))
        base += half
    assert base < zero_lane
    rows = jnp.concatenate([freq[None, :], jnp.asarray(np.stack(signs), dtype=F32), jnp.zeros((4, LANES), F32)], axis=0)
    return rows, jnp.asarray(expand, dtype=BF16)


ROPE_HALF = (DSA_HEAD_DIM // ROPE_FRACTION_DEN // 2, DIFF_QK_DIM // ROPE_FRACTION_DEN // 2, MLA_ROPE_DIM // 2)


def _x1_mask(pattern):
    lane = lax.broadcasted_iota(jnp.int32, (1, LANES), 1)
    if pattern == 0:
        return (lane % DSA_HEAD_DIM) < ROPE_HALF[0]
    if pattern == 1:
        return (lane % DIFF_QK_DIM) < ROPE_HALF[1]
    return (lane >= MLA_NOPE_DIM) & (lane < MLA_NOPE_DIM + ROPE_HALF[2])


def _rope_tables_kernel(pos_ref, rows_ref, expand_ref, out_ref):
    ang = pos_ref[...] * rows_ref[0:1, :]
    cos, sin = jnp.cos(ang), jnp.sin(ang)
    for p in range(3):
        out_ref[2 * p] = _dot_f32_by_exact(cos, expand_ref[p])
        out_ref[2 * p + 1] = _dot_f32_by_exact(sin, expand_ref[p]) * rows_ref[1 + p:2 + p, :]


def _rope_tables(pos_f, tm):
    n = pos_f.shape[0]
    rows, expand = _rope_patterns()
    return pl.pallas_call(
        _rope_tables_kernel,
        grid=(n // tm,),
        in_specs=[pl.BlockSpec((tm, 1), lambda i: (i, 0)),
                  pl.BlockSpec((8, LANES), lambda i: (0, 0)),
                  pl.BlockSpec((3, LANES, LANES), lambda i: (0, 0, 0))],
        out_specs=pl.BlockSpec((6, tm, LANES), lambda i: (0, i, 0)),
        out_shape=jax.ShapeDtypeStruct((6, n, LANES), F32),
        compiler_params=_cparams(("parallel",)),
        name="rope_tables",
    )(pos_f, rows, expand)


def _rope128(y, cos, sin, pattern):
    half = ROPE_HALF[pattern]
    up = pltpu.roll(y, LANES - half, 1)
    dn = pltpu.roll(y, half, 1)
    return y * cos + jnp.where(_x1_mask(pattern), up, dn) * sin


PROJ_COLS = (
    ("cq", 256, None), ("ckv", 128, None), ("kr", 128, 2),
    ("qa", 256, 0), ("ka2", 128, 0), ("qi", 512, 0), ("ki2", 128, 0),
    ("qb", 256, 1), ("kb", 256, 1),
    ("va2", 128, None), ("wi", 128, None), ("vb", 256, None),
)
PROJ_WIDTH = sum(c[1] for c in PROJ_COLS)
PROJ_OUTS = (("qa", 256, BF16), ("ka2", 128, BF16), ("qi", 512, BF16), ("ki2", 128, BF16),
             ("qb", 256, BF16), ("kb", 256, BF16), ("va2", 128, BF16), ("wi", 128, F32),
             ("vb", 256, BF16), ("qc", 1024, BF16), ("kc", 1024, BF16), ("vc", 512, BF16))
PROJ_TRANSPOSED = ("va2", "vb", "vc")
LOG2E = math.log2(math.e)
QUERY_LOG2_SCALE = {"qa": DSA_HEAD_DIM ** -0.5 * LOG2E, "qb": DIFF_QK_DIM ** -0.5 * LOG2E,
                    "qc": (MLA_NOPE_DIM + MLA_ROPE_DIM) ** -0.5 * LOG2E}
KV_TILE = 256
DSA_KV_TILE = 512
SUB_KEYS = 128


def _prep_proj_weights(w_in, w_uq, w_ukv):
    offs = np.cumsum((0,) + IN_SPLITS)
    (q_a, k_a, v_a, q_i, k_i, w_i, q_b, k_b, v_b, c_q, c_kv, k_r) = [
        w_in[:, offs[j]:offs[j + 1]] for j in range(len(IN_SPLITS))]
    d = w_in.shape[0]
    z = lambda n: jnp.zeros((d, n), w_in.dtype)
    cols = {
        "qa": q_a, "ka2": jnp.concatenate([k_a, k_a], 1), "qi": q_i, "ki2": jnp.concatenate([k_i, k_i], 1),
        "qb": q_b, "kb": k_b, "va2": jnp.concatenate([v_a, v_a], 1),
        "wi": jnp.concatenate([w_i, z(LANES - DSA_IDX_HEADS)], 1), "vb": v_b, "cq": c_q, "ckv": c_kv,
        "kr": jnp.concatenate([z(MLA_NOPE_DIM), k_r, z(LANES - MLA_NOPE_DIM - MLA_ROPE_DIM)], 1),
    }
    w_p = jnp.concatenate([cols[name] for name, _, _ in PROJ_COLS], axis=1).astype(BF16)
    qd = MLA_NOPE_DIM + MLA_ROPE_DIM
    uq = w_uq.reshape(MLA_Q_LORA, MLA_HEADS, qd)
    uq = jnp.pad(uq, ((0, 0), (0, 0), (0, LANES - qd))).reshape(MLA_Q_LORA, MLA_HEADS * LANES).astype(BF16)
    ukv = w_ukv.reshape(MLA_KV_LORA, MLA_HEADS, MLA_NOPE_DIM + MLA_V_DIM)
    ukn = jnp.pad(ukv[:, :, :MLA_NOPE_DIM], ((0, 0), (0, 0), (0, LANES - MLA_NOPE_DIM)))
    ukn = ukn.reshape(MLA_KV_LORA, MLA_HEADS * LANES).astype(BF16)
    uv = ukv[:, :, MLA_NOPE_DIM:].reshape(MLA_KV_LORA, MLA_HEADS * MLA_V_DIM).astype(BF16)
    return w_p, uq, ukn, uv


def _proj_kernel(h_ref, g_ref, tab_ref, w_ref, qn_ref, uq_ref, kvn_ref, ukn_ref, uv_ref, *out_refs):
    outs = {name: ref for (name, _, _), ref in zip(PROJ_OUTS, out_refs)}
    xn = _rms(h_ref[...], g_ref[...]).astype(BF16)

    def roped(y, pattern):
        return _rope128(y, tab_ref[2 * pattern], tab_ref[2 * pattern + 1], pattern)

    def emit(name, s, y):
        ref = outs[name]
        if name in PROJ_TRANSPOSED:
            tk = ref.shape[2]
            for t in range(ref.shape[0]):
                ref[t, s * LANES:(s + 1) * LANES, :] = y[t * tk:(t + 1) * tk].T.astype(ref.dtype)
        else:
            ref[:, s * LANES:(s + 1) * LANES] = y.astype(ref.dtype)

    def wide_dot(x, w, n_slices):
        per = MXU_COLS // LANES
        res = []
        for c in range(0, n_slices, per):
            y = _dot(x, w[:, c * LANES:(c + per) * LANES])
            res += [y[:, k * LANES:(k + 1) * LANES] for k in range(min(per, n_slices - c))]
        return res

    slices = wide_dot(xn, w_ref, PROJ_WIDTH // LANES)
    vals = {}
    i = 0
    for name, width, pattern in PROJ_COLS:
        for s in range(width // LANES):
            y = slices[i] if pattern is None else roped(slices[i], pattern)
            i += 1
            if name in outs:
                emit(name, s, y * QUERY_LOG2_SCALE[name] if name in QUERY_LOG2_SCALE else y)
            else:
                vals.setdefault(name, []).append(y)

    cq = jnp.concatenate(vals["cq"], axis=1)
    cqn = _rms(cq, qn_ref[...]).astype(BF16)
    for hd, y in enumerate(wide_dot(cqn, uq_ref, MLA_HEADS)):
        outs["qc"][:, hd * LANES:(hd + 1) * LANES] = (roped(y, 2) * QUERY_LOG2_SCALE["qc"]).astype(BF16)

    ckvn = _rms(vals["ckv"][0], kvn_ref[...]).astype(BF16)
    kr = vals["kr"][0]
    for hd, y in enumerate(wide_dot(ckvn, ukn_ref, MLA_HEADS)):
        outs["kc"][:, hd * LANES:(hd + 1) * LANES] = (y + kr).astype(BF16)
    for s, y in enumerate(wide_dot(ckvn, uv_ref, MLA_HEADS * MLA_V_DIM // LANES)):
        emit("vc", s, y)


def _project(h, gain, tables, w_p, q_norm, uq, kv_norm, ukn, uv, tm, key_tiles):
    n, d = h.shape
    full = lambda a: pl.BlockSpec(a.shape, lambda i: (0,) * a.ndim)
    gain = gain.reshape(1, d)
    q_norm = q_norm.reshape(1, -1)
    kv_norm = kv_norm.reshape(1, -1)
    out_specs, out_shape = [], []
    for name, w, dt in PROJ_OUTS:
        if name in PROJ_TRANSPOSED:
            tk = key_tiles[name]
            out_specs.append(pl.BlockSpec((tm // tk, w, tk), lambda i: (i, 0, 0)))
            out_shape.append(jax.ShapeDtypeStruct((n // tk, w, tk), dt))
        else:
            out_specs.append(pl.BlockSpec((tm, w), lambda i: (i, 0)))
            out_shape.append(jax.ShapeDtypeStruct((n, w), dt))
    return pl.pallas_call(
        _proj_kernel,
        grid=(n // tm,),
        in_specs=[pl.BlockSpec((tm, d), lambda i: (i, 0)), full(gain),
                  pl.BlockSpec((6, tm, LANES), lambda i: (0, i, 0)),
                  full(w_p), full(q_norm), full(uq), full(kv_norm), full(ukn), full(uv)],
        out_specs=out_specs,
        out_shape=out_shape,
        compiler_params=_cparams(("parallel",)),
        name="projection",
    )(h, gain, tables, w_p, q_norm, uq, kv_norm, ukn, uv)


def _half_masks(dtype, group):
    lane = lax.broadcasted_iota(jnp.int32, (1, LANES), 1)
    return [jnp.where((lane // group) == u, 1.0, 0.0).astype(dtype) for u in range(LANES // group)]


def _softmax_step(st, v_t, m, l, acc_ref, guard_empty=False):
    m_new = jnp.maximum(m, jnp.max(st, axis=0, keepdims=True))
    m_use = jnp.where(m_new == -jnp.inf, 0.0, m_new) if guard_empty else m_new
    alpha = jnp.exp2(m - m_use)
    p = jnp.exp2(st - m_use)
    l_new = alpha * l + jnp.sum(p, axis=0, keepdims=True)
    acc_ref[...] = alpha * acc_ref[...] + _dot(v_t, p.astype(BF16))
    return m_new, l_new


def _diag_mask(tk, m_cols, q_off):
    key_chunk = lax.broadcasted_iota(jnp.int32, (tk, 1), 0) // CHUNK
    qry_chunk = (q_off + lax.broadcasted_iota(jnp.int32, (1, m_cols), 1)) // CHUNK
    return key_chunk <= qry_chunk


def _causal_flash(problems, k_ref, v_ref, acc_ref, qi, tq, sub):
    problems = [(_transpose_bf16(q), k_lane0, v_row0, q_off) for q, k_lane0, v_row0, q_off in problems]
    m_cols = problems[0][0].shape[1]
    n_p = len(problems)
    acc_ref[...] = jnp.zeros_like(acc_ref)
    init = tuple(jnp.full((1, m_cols), -jnp.inf, F32) for _ in range(n_p)) + \
        tuple(jnp.zeros((1, m_cols), F32) for _ in range(n_p))

    def step(j, carry, masked):
        ms, ls = list(carry[:n_p]), list(carry[n_p:])
        for s0 in range(0, tq, sub):
            r0 = pl.multiple_of(j * tq + s0, sub)
            for p, (q_t, k_lane0, v_row0, q_off) in enumerate(problems):
                lo = max(s0 - q_off, 0) if masked else 0
                st = _dot(k_ref[pl.ds(r0, sub), k_lane0:k_lane0 + LANES], q_t[:, lo:])
                if masked:
                    st = jnp.where(_diag_mask(tq, m_cols, q_off)[s0:s0 + sub, lo:], st, -jnp.inf)
                m_new, l_new = _softmax_step(st, v_ref[j, v_row0:v_row0 + LANES, s0:s0 + sub], ms[p][:, lo:],
                                             ls[p][:, lo:], acc_ref.at[p, :, lo:])
                ms[p] = jnp.concatenate([ms[p][:, :lo], m_new], axis=1) if lo else m_new
                ls[p] = jnp.concatenate([ls[p][:, :lo], l_new], axis=1) if lo else l_new
        return tuple(ms) + tuple(ls)

    carry = lax.fori_loop(0, qi, lambda j, c: step(j, c, False), init)
    carry = step(qi, carry, True)
    return [acc_ref[p] / carry[n_p + p] for p in range(n_p)]


def _diff_kernel(lam_ref, subln_ref, gsum_ref, q_ref, k_ref, v_ref, o_ref, acc_ref, *, tq, lam_init):
    qi = pl.program_id(1)
    lam_rows = lam_ref[...]
    s1 = jnp.sum(lam_rows[0:1] * lam_rows[1:2], axis=1, keepdims=True)
    s2 = jnp.sum(lam_rows[2:3] * lam_rows[3:4], axis=1, keepdims=True)
    lam = jnp.exp(s1) - jnp.exp(s2) + lam_init
    masks = _half_masks(BF16, DIFF_QK_DIM)
    row = lax.broadcasted_iota(jnp.int32, (LANES, 1), 0)
    n_slices = DIFF_HEADS // 2
    problems = []
    for s in range(n_slices):
        q = q_ref[0, :, s * LANES:(s + 1) * LANES]
        problems += [(q * mk, s * LANES, s * LANES, 0) for mk in masks]
    outs = _causal_flash(problems, k_ref.at[0], v_ref, acc_ref, qi, tq, min(tq, SUB_KEYS))
    for s in range(n_slices):
        o0, o1, o2, o3 = outs[4 * s:4 * s + 4]
        a0 = o0 - lam * o1
        a1 = o2 - lam * o3
        a = jnp.where(row < DIFF_V_DIM, a0, a1).T
        ss = _dot_f32_by_exact(a * a, gsum_ref[...])
        y = a * lax.rsqrt(ss * (1.0 / DIFF_V_DIM) + NORM_EPS) * subln_ref[...]
        o_ref[0, :, s * LANES:(s + 1) * LANES] = (y * (1.0 - lam_init)).astype(o_ref.dtype)


def _diff_attention(qb, kb, vb_t, lam_rows, subln, layer, tq):
    b, s, w = qb.shape
    lam_init = 0.8 - 0.6 * math.exp(-0.3 * layer)
    subln2 = jnp.concatenate([subln, subln]).reshape(1, LANES).astype(F32)
    lane = np.arange(LANES)
    gsum = jnp.asarray((lane[:, None] // DIFF_V_DIM) == (lane[None, :] // DIFF_V_DIM), dtype=BF16)
    return pl.pallas_call(
        functools.partial(_diff_kernel, tq=tq, lam_init=lam_init),
        grid=(b, s // tq),
        in_specs=[pl.BlockSpec((8, LANES), lambda i, j: (0, 0)),
                  pl.BlockSpec((1, LANES), lambda i, j: (0, 0)),
                  pl.BlockSpec((LANES, LANES), lambda i, j: (0, 0)),
                  pl.BlockSpec((1, tq, w), lambda i, j: (i, j, 0)),
                  pl.BlockSpec((1, s, w), lambda i, j: (i, 0, 0)),
                  pl.BlockSpec((s // tq, w, tq), lambda i, j: (i, 0, 0))],
        out_specs=pl.BlockSpec((1, tq, w), lambda i, j: (i, j, 0)),
        out_shape=jax.ShapeDtypeStruct((b, s, w), BF16),
        scratch_shapes=[pltpu.VMEM((2 * DIFF_HEADS, LANES, tq), F32)],
        compiler_params=_cparams(("parallel", "arbitrary")),
        name="diff_attention",
    )(lam_rows, subln2, gsum, qb, kb, vb_t)


def _mla_kernel(q_ref, k_ref, v_ref, o_ref, acc_ref, *, tq):
    qi = pl.program_id(1)
    row = lax.broadcasted_iota(jnp.int32, (LANES, 1), 0)
    problems = [(q_ref[0, :, hd * LANES:(hd + 1) * LANES], hd * LANES, (hd // 2) * LANES, 0)
                for hd in range(MLA_HEADS)]
    outs = _causal_flash(problems, k_ref.at[0], v_ref, acc_ref, qi, tq, min(tq, SUB_KEYS))
    for pair in range(MLA_HEADS // 2):
        o_t = jnp.where(row < MLA_V_DIM, outs[2 * pair], outs[2 * pair + 1])
        o_ref[0, :, pair * LANES:(pair + 1) * LANES] = o_t.T.astype(o_ref.dtype)


def _mla_attention(qc, kc, vc_t, tq):
    b, s, wq = qc.shape
    wv = vc_t.shape[1]
    return pl.pallas_call(
        functools.partial(_mla_kernel, tq=tq),
        grid=(b, s // tq),
        in_specs=[pl.BlockSpec((1, tq, wq), lambda i, j: (i, j, 0)),
                  pl.BlockSpec((1, s, wq), lambda i, j: (i, 0, 0)),
                  pl.BlockSpec((s // tq, wv, tq), lambda i, j: (i, 0, 0))],
        out_specs=pl.BlockSpec((1, tq, wv), lambda i, j: (i, j, 0)),
        out_shape=jax.ShapeDtypeStruct((b, s, wv), BF16),
        scratch_shapes=[pltpu.VMEM((MLA_HEADS, LANES, tq), F32)],
        compiler_params=_cparams(("parallel", "arbitrary")),
        name="mla_attention",
    )(qc, kc, vc_t)


KEY_NEG_INF = -2139095041


def _score_keys(score):
    bits = lax.bitcast_convert_type(score, jnp.int32)
    return bits ^ ((bits >> 31) & 0x7FFFFFFF)


def _dsa_kernel(qa_ref, qi_ref, wi_ref, ka_ref, va_ref, ki_ref, o_ref, key_ref, hi_ref, lo_ref, aux_ref, bias_ref, acc_ref,
                *, tq, tk, top_k):
    blk = pl.program_id(1)
    s_len = ka_ref.shape[1]
    n_kv = ((blk + 1) * tq + tk - 1) // tk
    masks = _half_masks(BF16, DSA_IDX_DIM)
    row = lax.broadcasted_iota(jnp.int32, (LANES, 1), 0)
    t_chunk = (blk * tq + lax.broadcasted_iota(jnp.int32, (1, tq), 1)) // CHUNK
    sub_idx = lax.broadcasted_iota(jnp.int32, (tk, 1), 0)

    def tile_start(c):
        return pl.multiple_of(c * tk, tk)

    qi_st = jnp.concatenate([qi_ref[0, :, (hd // 2) * LANES:(hd // 2 + 1) * LANES] * masks[hd % 2]
                             for hd in range(DSA_IDX_HEADS)], axis=0)
    qi_t = _transpose_bf16(qi_st)
    w_t = wi_ref[0].T

    sub = min(tk, SUB_KEYS)
    sub_iota = lax.broadcasted_iota(jnp.int32, (sub, 1), 0)

    def score_body(c, carry):
        for s0 in range(0, tk, sub):
            r0 = pl.multiple_of(c * tk + s0, sub)
            ki = ki_ref[0, pl.ds(r0, sub), :]
            score = jnp.zeros((sub, tq), F32)
            for hd in range(DSA_IDX_HEADS):
                rel = jnp.maximum(_dot(ki, qi_t[:, hd * tq:(hd + 1) * tq]), 0.0)
                score = score + w_t[hd:hd + 1, :] * rel
            score = score * DSA_IDX_SCALE
            score = jnp.where((r0 + sub_iota) // CHUNK <= t_chunk, score, -jnp.inf)
            key = _score_keys(score)
            key_ref[pl.ds(r0, sub), :] = key
            hi_ref[pl.ds(r0, sub), :] = (key >> 16).astype(jnp.int16)
            lo_ref[pl.ds(r0, sub), :] = ((key & 0xFFFF) - 32768).astype(jnp.int16)
        return carry

    lax.fori_loop(0, n_kv, score_body, 0)

    def count16(ref, cand, strict=False):
        cand16 = cand.astype(jnp.int16)

        def body(c, tot):
            x = ref[pl.ds(tile_start(c), tk), :]
            ones = jnp.where((x > cand16) if strict else (x >= cand16), jnp.bfloat16(1), jnp.bfloat16(0))
            parts = [ones[i * 16:(i + 1) * 16] for i in range(tk // 16)]
            while len(parts) > 1:
                parts = [parts[i] + parts[i + 1] for i in range(0, len(parts), 2)]
            return tot + parts[0].astype(F32)
        tot = lax.fori_loop(0, n_kv, body, jnp.zeros((16, tq), F32))
        return jnp.sum(tot, axis=0, keepdims=True)

    def kth_largest16(ref, need):
        start = jnp.where(count16(ref, jnp.zeros((1, tq), jnp.int32)) >= need, 0, -32768).astype(jnp.int32)

        def body(i, t):
            cand = t | (jnp.int32(1) << (14 - i))
            return jnp.where(count16(ref, cand) >= need, cand, t)
        return lax.fori_loop(0, 15, body, start)

    def count(pred_fn):
        n_part = 8

        def body(c, tot):
            ones = jnp.where(pred_fn(tile_start(c)), 1.0, 0.0)
            return tot + jnp.sum(ones.reshape(tk // (8 * n_part), n_part * 8, tq), axis=0)
        tot = lax.fori_loop(0, n_kv, body, jnp.zeros((n_part * 8, tq), F32))
        return jnp.sum(tot, axis=0, keepdims=True)

    thr_hi = kth_largest16(hi_ref, float(top_k))
    need_lo = top_k - count16(hi_ref, thr_hi, strict=True)
    thr_hi16 = thr_hi.astype(jnp.int16)

    def bucket_body(c, carry):
        r0 = tile_start(c)
        lo_ref[pl.ds(r0, tk), :] = jnp.where(hi_ref[pl.ds(r0, tk), :] == thr_hi16, lo_ref[pl.ds(r0, tk), :],
                                             jnp.int16(-32768))
        return carry

    lax.fori_loop(0, n_kv, bucket_body, 0)
    thr_lo = kth_largest16(lo_ref, need_lo)
    thr = (thr_hi << 16) | ((thr_lo + 32768) & 0xFFFF)
    cnt_thr = count(lambda r0: key_ref[pl.ds(r0, tk), :] >= thr)
    thr = jnp.maximum(thr, KEY_NEG_INF)
    tied = jnp.where(cnt_thr > top_k, jnp.where(thr > KEY_NEG_INF, 1.0, 0.0), 0.0)

    def select_with_ties():
        n_gt = count(lambda r0: key_ref[pl.ds(r0, tk), :] > thr)
        need = top_k - n_gt

        def aux_body(c, carry):
            r0 = tile_start(c)
            idx = r0 + sub_idx
            tie = jnp.where(key_ref[pl.ds(r0, tk), :] == thr, jnp.where(idx // CHUNK <= t_chunk, idx, IDX_BIG), IDX_BIG)
            aux_ref[pl.ds(r0, tk), :] = tie
            return carry

        lax.fori_loop(0, n_kv, aux_body, 0)
        n_bits = max(1, (s_len - 1).bit_length())

        def tie_body(i, last):
            cand = last | (jnp.int32(1) << (n_bits - 1 - i))
            cnt = count(lambda r0: aux_ref[pl.ds(r0, tk), :] < cand)
            return jnp.where(cnt < need, cand, last)

        last = lax.fori_loop(0, n_bits, tie_body, jnp.zeros((1, tq), jnp.int32))

        def bias_body(c, carry):
            r0 = tile_start(c)
            bias_ref[pl.ds(r0, tk), :] = jnp.where(
                key_ref[pl.ds(r0, tk), :] > thr, 0.0, jnp.where(aux_ref[pl.ds(r0, tk), :] <= last, 0.0, -jnp.inf))
            return carry

        lax.fori_loop(0, n_kv, bias_body, 0)

    def select_no_ties():
        low = jnp.where(thr > KEY_NEG_INF, thr - 1, thr)

        def bias_body(c, carry):
            r0 = tile_start(c)
            bias_ref[pl.ds(r0, tk), :] = jnp.where(key_ref[pl.ds(r0, tk), :] > low, 0.0, -jnp.inf)
            return carry

        lax.fori_loop(0, n_kv, bias_body, 0)

    lax.cond(jnp.max(tied) > 0.0, select_with_ties, select_no_ties)

    hmask = _half_masks(BF16, DSA_HEAD_DIM)
    qa_st = jnp.concatenate([qa_ref[0, :, (hd // 2) * LANES:(hd // 2 + 1) * LANES] * hmask[hd % 2]
                             for hd in range(DSA_HEADS)], axis=0)
    qa_t = _transpose_bf16(qa_st)
    n_h = DSA_HEADS
    acc_ref[...] = jnp.zeros_like(acc_ref)

    def att_body(c, carry):
        ms, ls = list(carry[:n_h]), list(carry[n_h:])
        for s0 in range(0, tk, sub):
            r0 = pl.multiple_of(c * tk + s0, sub)
            bias = bias_ref[pl.ds(r0, sub), :]
            ka = ka_ref[0, pl.ds(r0, sub), :]
            for hd in range(n_h):
                st = _dot(ka, qa_t[:, hd * tq:(hd + 1) * tq]) + bias
                ms[hd], ls[hd] = _softmax_step(st, va_ref[c, :, s0:s0 + sub], ms[hd], ls[hd], acc_ref.at[hd],
                                               guard_empty=True)
        return tuple(ms) + tuple(ls)

    init = tuple(jnp.full((1, tq), -jnp.inf, F32) for _ in range(n_h)) + \
        tuple(jnp.zeros((1, tq), F32) for _ in range(n_h))
    carry = lax.fori_loop(0, n_kv, att_body, init)
    for p in range(n_h // 2):
        pair_t = jnp.where(row < DSA_HEAD_DIM, acc_ref[2 * p] / carry[n_h + 2 * p],
                           acc_ref[2 * p + 1] / carry[n_h + 2 * p + 1])
        o_ref[0, :, p * LANES:(p + 1) * LANES] = pair_t.T.astype(o_ref.dtype)


def _dsa_attention(qa, qi, wi, ka2, va2_t, ki2, tq, tk):
    b, s, _ = qa.shape
    top_k = min(DSA_TOPK_MAX, s // 4)
    qspec = lambda w: pl.BlockSpec((1, tq, w), lambda i, j: (i, j, 0))
    kspec = lambda w: pl.BlockSpec((1, s, w), lambda i, j: (i, 0, 0))
    return pl.pallas_call(
        functools.partial(_dsa_kernel, tq=tq, tk=tk, top_k=top_k),
        grid=(b, s // tq),
        in_specs=[qspec(qa.shape[2]), qspec(qi.shape[2]), qspec(LANES), kspec(LANES),
                  pl.BlockSpec((s // tk, LANES, tk), lambda i, j: (i, 0, 0)), kspec(LANES)],
        out_specs=qspec(qa.shape[2]),
        out_shape=jax.ShapeDtypeStruct(qa.shape, BF16),
        scratch_shapes=[pltpu.VMEM((s, tq), jnp.int32), pltpu.VMEM((s, tq), jnp.int16), pltpu.VMEM((s, tq), jnp.int16),
                        pltpu.VMEM((s, tq), jnp.int32), pltpu.VMEM((s, tq), F32),
                        pltpu.VMEM((DSA_HEADS, LANES, tq), F32)],
        compiler_params=_cparams(("parallel", "arbitrary")),
        name="dsa_attention",
    )(qa, qi, wi, ka2, va2_t, ki2)


def _attn_residual(h_ref, oa_ref, ob_ref, oc_ref, wo_ref):
    wa, wb = oa_ref.shape[1], ob_ref.shape[1]
    return (h_ref[...] + _dot(oa_ref[...], wo_ref[0:wa, :]) + _dot(ob_ref[...], wo_ref[wa:wa + wb, :])
            + _dot(oc_ref[...], wo_ref[wa + wb:, :]))


def _swiglu_partial(x, wg, wu, wd):
    g = _dot(x, wg)
    u = _dot(x, wu)
    return _dot(((g * jax.nn.sigmoid(g)) * u).astype(BF16), wd)


def _dense_block_kernel(h_ref, oa_ref, ob_ref, oc_ref, wo_ref, g_ref, wg_ref, wu_ref, wd_ref, fg_ref,
                        out_ref, xn_ref, acc_ref, *, final_norm):
    f = pl.program_id(1)

    @pl.when(f == 0)
    def _():
        h1 = _attn_residual(h_ref, oa_ref, ob_ref, oc_ref, wo_ref)
        acc_ref[...] = h1
        xn_ref[...] = _rms(h1, g_ref[...]).astype(BF16)

    acc_ref[...] += _swiglu_partial(xn_ref[...], wg_ref[...], wu_ref[...], wd_ref[...])

    @pl.when(f == pl.num_programs(1) - 1)
    def _():
        y = acc_ref[...]
        out_ref[...] = _rms(y, fg_ref[...]) if final_norm else y


def _dense_block(h, oa, ob, oc, w_out, gain, wg, wu, wd, final_gain, *, final_norm, tm, tf):
    n, d = h.shape
    dff = wg.shape[1]
    const = lambda a: pl.BlockSpec(a.shape, lambda i, f: (0,) * a.ndim, pipeline_mode=pl.Buffered(1))
    row = lambda w: pl.BlockSpec((tm, w), lambda i, f: (i, 0))
    gain = gain.reshape(1, d)
    final_gain = final_gain.reshape(1, d)
    return pl.pallas_call(
        functools.partial(_dense_block_kernel, final_norm=final_norm),
        grid=(n // tm, dff // tf),
        in_specs=[row(d), row(oa.shape[1]), row(ob.shape[1]), row(oc.shape[1]), const(w_out), const(gain),
                  pl.BlockSpec((d, tf), lambda i, f: (0, f)),
                  pl.BlockSpec((d, tf), lambda i, f: (0, f)),
                  pl.BlockSpec((tf, d), lambda i, f: (f, 0)),
                  const(final_gain)],
        out_specs=row(d),
        out_shape=jax.ShapeDtypeStruct((n, d), F32),
        scratch_shapes=[pltpu.VMEM((tm, d), BF16), pltpu.VMEM((tm, d), F32)],
        compiler_params=_cparams(("parallel", "arbitrary")),
        name="block_dense",
    )(h, oa, ob, oc, w_out, gain, wg, wu, wd, final_gain)


def _moe_block_kernel(h_ref, oa_ref, ob_ref, oc_ref, wo_ref, g_ref, r_ref, tri_ref, wg_ref, wu_ref, wd_ref, fg_ref,
                      out_ref, xn_ref, acc_ref, gate_ref, pos_ref, pos_t_ref, cnt_ref, xs_ref, ye_ref,
                      *, n_exp, mains, rows, final_norm):
    e = pl.program_id(1)
    f = pl.program_id(2)
    n_f = pl.num_programs(2)
    tm = xn_ref.shape[0]
    lane = lax.broadcasted_iota(jnp.int32, (1, LANES), 1)

    @pl.when((e == 0) & (f == 0))
    def _():
        h1 = _attn_residual(h_ref, oa_ref, ob_ref, oc_ref, wo_ref)
        acc_ref[...] = h1
        hn = _rms(h1, g_ref[...])
        xn_ref[...] = hn.astype(BF16)
        hn_hi = hn.astype(BF16)
        hn_lo = (hn - hn_hi.astype(F32)).astype(BF16)
        parts = _dot(hn_hi, r_ref[...]) + _dot(hn_lo, r_ref[...])
        logits = parts[:, :LANES] + parts[:, LANES:]
        logits = jnp.where(lane < n_exp, logits, -jnp.inf)
        m1 = jnp.max(logits, axis=1, keepdims=True)
        i1 = jnp.min(jnp.where(logits == m1, lane, IDX_BIG), axis=1, keepdims=True)
        rest = jnp.where(lane == i1, -jnp.inf, logits)
        m2 = jnp.max(rest, axis=1, keepdims=True)
        i2 = jnp.min(jnp.where(rest == m2, lane, IDX_BIG), axis=1, keepdims=True)
        e2 = jnp.exp(m2 - m1)
        den = 1.0 + e2
        gate_ref[...] = jnp.where(lane == i1, 1.0 / den, jnp.where(lane == i2, e2 / den, 0.0))
        routed = jnp.where(lane == i1, 1.0, jnp.where(lane == i2, 1.0, 0.0))
        before = _dot(tri_ref[...], routed.astype(BF16))
        slot = jnp.where(routed > 0.0, before, -1.0)
        pos_ref[...] = slot
        pos_t_ref[...] = slot.T
        cnt = jnp.sum(routed, axis=0, keepdims=True)
        for k in range(n_exp):
            cnt_ref[k] = jnp.sum(jnp.where(lane == k, cnt, 0.0)).astype(jnp.int32)

    n_tok = cnt_ref[e]
    top = mains[-1]
    n_extra = (jnp.maximum(n_tok - top, 0) + rows - 1) // rows

    def for_blocks(fn):
        below = 0
        for size in mains:
            fits = (n_tok > below) if size == top else ((n_tok > below) & (n_tok <= size))

            @pl.when(fits)
            def _(size=size):
                fn(0, size)
                if size == top:
                    lax.fori_loop(0, n_extra, lambda r, c: (
                        fn(pl.multiple_of(top + r * rows, math.gcd(top, rows)), rows), c)[1], 0)

            below = size

    @pl.when(f == 0)
    def _():
        slot_row = pos_t_ref[pl.ds(e, 1), :]

        def gather(r0, nr):
            want = (r0 + lax.broadcasted_iota(jnp.int32, (nr, 1), 0)).astype(F32)
            pick = jnp.where(slot_row == want, 1.0, 0.0).astype(BF16)
            xs_ref[pl.ds(r0, nr), :] = _dot(pick, xn_ref[...]).astype(BF16)
            ye_ref[pl.ds(r0, nr), :] = jnp.zeros((nr, ye_ref.shape[1]), F32)

        for_blocks(gather)

    def expert(r0, nr):
        ye_ref[pl.ds(r0, nr), :] += _swiglu_partial(xs_ref[pl.ds(r0, nr), :], wg_ref[0], wu_ref[0], wd_ref[0])

    for_blocks(expert)

    @pl.when(f == n_f - 1)
    def _():
        full_lane = lax.broadcasted_iota(jnp.int32, (tm, LANES), 1)
        slot_col = jnp.sum(jnp.where(full_lane == e, pos_ref[...], 0.0), axis=1, keepdims=True)
        gate_col = jnp.sum(jnp.where(full_lane == e, gate_ref[...], 0.0), axis=1, keepdims=True)

        def scatter(r0, nr):
            have = (r0 + lax.broadcasted_iota(jnp.int32, (1, nr), 1)).astype(F32)
            place = jnp.where(slot_col == have, 1.0, 0.0).astype(BF16)
            y = ye_ref[pl.ds(r0, nr), :]
            acc_ref[...] += gate_col * _dot(place, y.astype(BF16))

        for_blocks(scatter)

    @pl.when((e == n_exp - 1) & (f == n_f - 1))
    def _():
        y = acc_ref[...]
        out_ref[...] = _rms(y, fg_ref[...]) if final_norm else y


def _moe_block(h, oa, ob, oc, w_out, gain, router_p, wg, wu, wd, final_gain, *, final_norm, tm, tf, mains, rows):
    n, d = h.shape
    n_exp, _, dff = wg.shape
    const = lambda a: pl.BlockSpec(a.shape, lambda i, e, f: (0,) * a.ndim, pipeline_mode=pl.Buffered(1))
    row = lambda w: pl.BlockSpec((tm, w), lambda i, e, f: (i, 0))
    row_in = lambda w: pl.BlockSpec((tm, w), lambda i, e, f: (i, 0), pipeline_mode=pl.Buffered(1))
    gain = gain.reshape(1, d)
    final_gain = final_gain.reshape(1, d)
    tok = np.arange(tm)
    tri = jnp.asarray(tok[None, :] < tok[:, None], dtype=BF16)
    mains = tuple(sorted({min(m, tm) for m in mains}))
    cap = mains[-1] + -(-(tm - mains[-1]) // rows) * rows
    return pl.pallas_call(
        functools.partial(_moe_block_kernel, n_exp=n_exp, mains=mains, rows=rows, final_norm=final_norm),
        grid=(n // tm, n_exp, dff // tf),
        in_specs=[row_in(d), row_in(oa.shape[1]), row_in(ob.shape[1]), row_in(oc.shape[1]), const(w_out), const(gain),
                  const(router_p), const(tri),
                  pl.BlockSpec((1, d, tf), lambda i, e, f: (e, 0, f)),
                  pl.BlockSpec((1, d, tf), lambda i, e, f: (e, 0, f)),
                  pl.BlockSpec((1, tf, d), lambda i, e, f: (e, f, 0)),
                  const(final_gain)],
        out_specs=row(d),
        out_shape=jax.ShapeDtypeStruct((n, d), F32),
        scratch_shapes=[pltpu.VMEM((tm, d), BF16), pltpu.VMEM((tm, d), F32), pltpu.VMEM((tm, LANES), F32),
                        pltpu.VMEM((tm, LANES), F32), pltpu.VMEM((LANES, tm), F32), pltpu.SMEM((n_exp,), jnp.int32),
                        pltpu.VMEM((cap, d), BF16), pltpu.VMEM((cap, d), F32)],
        compiler_params=_cparams(("parallel", "arbitrary", "arbitrary")),
        name="block_moe",
    )(h, oa, ob, oc, w_out, gain, router_p, tri, wg, wu, wd, final_gain)


def _pick(n, pref):
    t = min(pref, n)
    while n % t:
        t //= 2
    return t


def kernel(x, positions, attn_norm, w_in, mla_q_norm, w_uq, mla_kv_norm, w_ukv, diff_lambda_q1, diff_lambda_k1, diff_lambda_q2, diff_lambda_k2, diff_subln, w_out, ffn_norm, dense_w_gate, dense_w_up, dense_w_down, moe_router, moe_w_gate, moe_w_up, moe_w_down, final_norm):
    b, s, d = x.shape
    n = b * s
    depth = w_in.shape[0]
    tm_proj = _pick(n, 1024)
    tm_blk = _pick(n, 1024)
    tk = _pick(s, KV_TILE)
    tk_dsa = _pick(s, DSA_KV_TILE)
    key_tiles = {"va2": tk_dsa, "vb": tk, "vc": tk}
    tq_dsa = _pick(s, 256)

    tables = _rope_tables(positions.astype(F32).reshape(n, 1), _pick(n, 1024))
    h = x.reshape(n, d)
    r3 = lambda a: a.reshape(b, s, a.shape[-1])
    for layer in range(depth):
        w_p, uq, ukn, uv = _prep_proj_weights(w_in[layer], w_uq[layer], w_ukv[layer])
        (qa, ka2, qi, ki2, qb, kb, va2, wi, vb, qc, kc, vc) = _project(
            h, attn_norm[layer], tables, w_p, mla_q_norm[layer], uq, mla_kv_norm[layer], ukn, uv, tm_proj, key_tiles)
        oa = _dsa_attention(r3(qa), r3(qi), r3(wi), r3(ka2), va2, r3(ki2), tq_dsa, tk_dsa)
        lam_rows = jnp.zeros((8, LANES), F32).at[0:4, 0:DIFF_QK_DIM].set(jnp.stack(
            [diff_lambda_q1[layer], diff_lambda_k1[layer], diff_lambda_q2[layer], diff_lambda_k2[layer]]))
        ob = _diff_attention(r3(qb), r3(kb), vb, lam_rows, diff_subln[layer], layer, tk)
        oc = _mla_attention(r3(qc), r3(kc), vc, tk)
        j = layer // 2
        last = layer == depth - 1
        wo = w_out[layer].astype(BF16)
        attn = (oa.reshape(n, -1), ob.reshape(n, -1), oc.reshape(n, -1))
        if layer % 2 == 0:
            h = _dense_block(h, *attn, wo, ffn_norm[layer], dense_w_gate[j].astype(BF16), dense_w_up[j].astype(BF16),
                             dense_w_down[j].astype(BF16), final_norm,
                             final_norm=last, tm=tm_blk, tf=_pick(dense_w_gate.shape[2], DENSE_FFN_TILE))
        else:
            router_f = jnp.pad(moe_router[j], ((0, 0), (0, LANES - MOE_EXPERTS)))
            router_hi = router_f.astype(BF16)
            router_p = jnp.concatenate([router_hi, (router_f - router_hi.astype(F32)).astype(BF16)], axis=1)
            h = _moe_block(h, *attn, wo, ffn_norm[layer], router_p,
                           moe_w_gate[j].astype(BF16), moe_w_up[j].astype(BF16), moe_w_down[j].astype(BF16), final_norm,
                           final_norm=last, tm=tm_blk, tf=_pick(moe_w_gate.shape[3], FFN_TILE),
                           mains=MOE_MAIN_ROWS, rows=MOE_EXTRA_ROWS)
    return h.reshape(b, s, d)
```

```python
import functools
import math

import jax
import jax.numpy as jnp
import numpy as np
from jax import lax
from jax.experimental import pallas as pl
from jax.experimental.pallas import tpu as pltpu

F32 = jnp.float32
BF16 = jnp.bfloat16

LANES = 128
MXU_COLS = 256
VMEM_LIMIT_BYTES = 56 * 1024 * 1024

CHUNK = 64
ROPE_THETA = 500000.0
NORM_EPS = 1e-6
ROPE_FRACTION_DEN = 4

DSA_HEADS = 4
DSA_HEAD_DIM = 64
DSA_IDX_HEADS = 8
DSA_IDX_DIM = 64
DSA_TOPK_MAX = 256
DSA_IDX_SCALE = (DSA_IDX_HEADS * DSA_IDX_DIM) ** -0.5

DIFF_HEADS = 4
DIFF_QK_DIM = 32
DIFF_V_DIM = 2 * DIFF_QK_DIM

MLA_HEADS = 8
MLA_Q_LORA = 256
MLA_KV_LORA = 128
MLA_NOPE_DIM = 64
MLA_ROPE_DIM = 32
MLA_V_DIM = 64

MOE_EXPERTS = 8
FFN_TILE = 896
DENSE_FFN_TILE = 512
MOE_MAIN_ROWS = (256, 320)
MOE_EXTRA_ROWS = 128

IN_SPLITS = (
    DSA_HEADS * DSA_HEAD_DIM, DSA_HEAD_DIM, DSA_HEAD_DIM, DSA_IDX_HEADS * DSA_IDX_DIM, DSA_IDX_DIM,
    DSA_IDX_HEADS, DIFF_HEADS * 2 * DIFF_QK_DIM, DIFF_HEADS * 2 * DIFF_QK_DIM, DIFF_HEADS * DIFF_V_DIM,
    MLA_Q_LORA, MLA_KV_LORA, MLA_ROPE_DIM,
)

IDX_BIG = 2 ** 30


def _cparams(sem):
    return pltpu.CompilerParams(dimension_semantics=sem, vmem_limit_bytes=VMEM_LIMIT_BYTES)


def _rms(x, g):
    return x * lax.rsqrt(jnp.mean(x * x, axis=-1, keepdims=True) + NORM_EPS) * g


def _dot(a, b):
    return jnp.dot(a, b, preferred_element_type=F32)


def _transpose_bf16(x):
    return x.astype(F32).T.astype(BF16)


def _split3(x):
    hi = x.astype(BF16)
    r1 = x - hi.astype(F32)
    mid = r1.astype(BF16)
    lo = (r1 - mid.astype(F32)).astype(BF16)
    return hi, mid, lo


def _dot_f32_by_exact(x, m_bf16):
    hi, mid, lo = _split3(x)
    return _dot(hi, m_bf16) + _dot(mid, m_bf16) + _dot(lo, m_bf16)


def _inv_freq(rot_dim):
    half = rot_dim // 2
    return ROPE_THETA ** (-(jnp.arange(half, dtype=F32) * 2.0 / rot_dim))


def _rope_patterns():
    lane = np.arange(LANES)
    rots = (DSA_HEAD_DIM // ROPE_FRACTION_DEN, DIFF_QK_DIM // ROPE_FRACTION_DEN, MLA_ROPE_DIM)
    offs = (lane % DSA_HEAD_DIM, lane % DIFF_QK_DIM, lane - MLA_NOPE_DIM)
    zero_lane = LANES - 1
    freq = jnp.zeros((LANES,), F32)
    signs, expand = [], np.zeros((3, LANES, LANES), np.float32)
    base = 0
    for p, (rot, off) in enumerate(zip(rots, offs)):
        half = rot // 2
        active = (off >= 0) & (off < rot)
        freq = freq.at[base:base + half].set(_inv_freq(rot))
        src = np.where(active, base + np.clip(off, 0, rot - 1) % half, zero_lane)
        expand[p, src, lane] = 1.0
        signs.append(np.where(active, np.where(off < half, -1.0, 1.0), 0.0))
        base += half
    assert base < zero_lane
    rows = jnp.concatenate([freq[None, :], jnp.asarray(np.stack(signs), dtype=F32), jnp.zeros((4, LANES), F32)], axis=0)
    return rows, jnp.asarray(expand, dtype=BF16)


ROPE_HALF = (DSA_HEAD_DIM // ROPE_FRACTION_DEN // 2, DIFF_QK_DIM // ROPE_FRACTION_DEN // 2, MLA_ROPE_DIM // 2)


def _x1_mask(pattern):
    lane = lax.broadcasted_iota(jnp.int32, (1, LANES), 1)
    if pattern == 0:
        return (lane % DSA_HEAD_DIM) < ROPE_HALF[0]
    if pattern == 1:
        return (lane % DIFF_QK_DIM) < ROPE_HALF[1]
    return (lane >= MLA_NOPE_DIM) & (lane < MLA_NOPE_DIM + ROPE_HALF[2])


def _rope_tables_kernel(pos_ref, rows_ref, expand_ref, out_ref):
    ang = pos_ref[...] * rows_ref[0:1, :]
    cos, sin = jnp.cos(ang), jnp.sin(ang)
    for p in range(3):
        out_ref[2 * p] = _dot_f32_by_exact(cos, expand_ref[p])
        out_ref[2 * p + 1] = _dot_f32_by_exact(sin, expand_ref[p]) * rows_ref[1 + p:2 + p, :]


def _rope_tables(pos_f, tm):
    n = pos_f.shape[0]
    rows, expand = _rope_patterns()
    return pl.pallas_call(
        _rope_tables_kernel,
        grid=(n // tm,),
        in_specs=[pl.BlockSpec((tm, 1), lambda i: (i, 0)),
                  pl.BlockSpec((8, LANES), lambda i: (0, 0)),
                  pl.BlockSpec((3, LANES, LANES), lambda i: (0, 0, 0))],
        out_specs=pl.BlockSpec((6, tm, LANES), lambda i: (0, i, 0)),
        out_shape=jax.ShapeDtypeStruct((6, n, LANES), F32),
        compiler_params=_cparams(("parallel",)),
        name="rope_tables",
    )(pos_f, rows, expand)


def _rope128(y, cos, sin, pattern):
    half = ROPE_HALF[pattern]
    up = pltpu.roll(y, LANES - half, 1)
    dn = pltpu.roll(y, half, 1)
    return y * cos + jnp.where(_x1_mask(pattern), up, dn) * sin


PROJ_COLS = (
    ("cq", 256, None), ("ckv", 128, None), ("kr", 128, 2),
    ("qa", 256, 0), ("ka2", 128, 0), ("qi", 512, 0), ("ki2", 128, 0),
    ("qb", 256, 1), ("kb", 256, 1),
    ("va2", 128, None), ("wi", 128, None), ("vb", 256, None),
)
PROJ_WIDTH = sum(c[1] for c in PROJ_COLS)
PROJ_OUTS = (("qa", 256, BF16), ("ka2", 128, BF16), ("qi", 512, BF16), ("ki2", 128, BF16),
             ("qb", 256, BF16), ("kb", 256, BF16), ("va2", 128, BF16), ("wi", 128, F32),
             ("vb", 256, BF16), ("qc", 1024, BF16), ("kc", 1024, BF16), ("vc", 512, BF16))
PROJ_TRANSPOSED = ("va2", "vb", "vc")
LOG2E = math.log2(math.e)
QUERY_LOG2_SCALE = {"qa": DSA_HEAD_DIM ** -0.5 * LOG2E, "qb": DIFF_QK_DIM ** -0.5 * LOG2E,
                    "qc": (MLA_NOPE_DIM + MLA_ROPE_DIM) ** -0.5 * LOG2E}
KV_TILE = 256
DSA_KV_TILE = 512
SUB_KEYS = 128


def _prep_proj_weights(w_in, w_uq, w_ukv):
    offs = np.cumsum((0,) + IN_SPLITS)
    (q_a, k_a, v_a, q_i, k_i, w_i, q_b, k_b, v_b, c_q, c_kv, k_r) = [
        w_in[:, offs[j]:offs[j + 1]] for j in range(len(IN_SPLITS))]
    d = w_in.shape[0]
    z = lambda n: jnp.zeros((d, n), w_in.dtype)
    cols = {
        "qa": q_a, "ka2": jnp.concatenate([k_a, k_a], 1), "qi": q_i, "ki2": jnp.concatenate([k_i, k_i], 1),
        "qb": q_b, "kb": k_b, "va2": jnp.concatenate([v_a, v_a], 1),
        "wi": jnp.concatenate([w_i, z(LANES - DSA_IDX_HEADS)], 1), "vb": v_b, "cq": c_q, "ckv": c_kv,
        "kr": jnp.concatenate([z(MLA_NOPE_DIM), k_r, z(LANES - MLA_NOPE_DIM - MLA_ROPE_DIM)], 1),
    }
    w_p = jnp.concatenate([cols[name] for name, _, _ in PROJ_COLS], axis=1).astype(BF16)
    qd = MLA_NOPE_DIM + MLA_ROPE_DIM
    uq = w_uq.reshape(MLA_Q_LORA, MLA_HEADS, qd)
    uq = jnp.pad(uq, ((0, 0), (0, 0), (0, LANES - qd))).reshape(MLA_Q_LORA, MLA_HEADS * LANES).astype(BF16)
    ukv = w_ukv.reshape(MLA_KV_LORA, MLA_HEADS, MLA_NOPE_DIM + MLA_V_DIM)
    ukn = jnp.pad(ukv[:, :, :MLA_NOPE_DIM], ((0, 0), (0, 0), (0, LANES - MLA_NOPE_DIM)))
    ukn = ukn.reshape(MLA_KV_LORA, MLA_HEADS * LANES).astype(BF16)
    uv = ukv[:, :, MLA_NOPE_DIM:].reshape(MLA_KV_LORA, MLA_HEADS * MLA_V_DIM).astype(BF16)
    return w_p, uq, ukn, uv


def _proj_kernel(h_ref, g_ref, tab_ref, w_ref, qn_ref, uq_ref, kvn_ref, ukn_ref, uv_ref, *out_refs):
    outs = {name: ref for (name, _, _), ref in zip(PROJ_OUTS, out_refs)}
    xn = _rms(h_ref[...], g_ref[...]).astype(BF16)

    def roped(y, pattern):
        return _rope128(y, tab_ref[2 * pattern], tab_ref[2 * pattern + 1], pattern)

    def emit(name, s, y):
        ref = outs[name]
        if name in PROJ_TRANSPOSED:
            tk = ref.shape[2]
            for t in range(ref.shape[0]):
                ref[t, s * LANES:(s + 1) * LANES, :] = y[t * tk:(t + 1) * tk].T.astype(ref.dtype)
        else:
            ref[:, s * LANES:(s + 1) * LANES] = y.astype(ref.dtype)

    def wide_dot(x, w, n_slices):
        per = MXU_COLS // LANES
        res = []
        for c in range(0, n_slices, per):
            y = _dot(x, w[:, c * LANES:(c + per) * LANES])
            res += [y[:, k * LANES:(k + 1) * LANES] for k in range(min(per, n_slices - c))]
        return res

    slices = wide_dot(xn, w_ref, PROJ_WIDTH // LANES)
    vals = {}
    i = 0
    for name, width, pattern in PROJ_COLS:
        for s in range(width // LANES):
            y = slices[i] if pattern is None else roped(slices[i], pattern)
            i += 1
            if name in outs:
                emit(name, s, y * QUERY_LOG2_SCALE[name] if name in QUERY_LOG2_SCALE else y)
            else:
                vals.setdefault(name, []).append(y)

    cq = jnp.concatenate(vals["cq"], axis=1)
    cqn = _rms(cq, qn_ref[...]).astype(BF16)
    for hd, y in enumerate(wide_dot(cqn, uq_ref, MLA_HEADS)):
        outs["qc"][:, hd * LANES:(hd + 1) * LANES] = (roped(y, 2) * QUERY_LOG2_SCALE["qc"]).astype(BF16)

    ckvn = _rms(vals["ckv"][0], kvn_ref[...]).astype(BF16)
    kr = vals["kr"][0]
    for hd, y in enumerate(wide_dot(ckvn, ukn_ref, MLA_HEADS)):
        outs["kc"][:, hd * LANES:(hd + 1) * LANES] = (y + kr).astype(BF16)
    for s, y in enumerate(wide_dot(ckvn, uv_ref, MLA_HEADS * MLA_V_DIM // LANES)):
        emit("vc", s, y)


def _project(h, gain, tables, w_p, q_norm, uq, kv_norm, ukn, uv, tm, key_tiles):
    n, d = h.shape
    full = lambda a: pl.BlockSpec(a.shape, lambda i: (0,) * a.ndim)
    gain = gain.reshape(1, d)
    q_norm = q_norm.reshape(1, -1)
    kv_norm = kv_norm.reshape(1, -1)
    out_specs, out_shape = [], []
    for name, w, dt in PROJ_OUTS:
        if name in PROJ_TRANSPOSED:
            tk = key_tiles[name]
            out_specs.append(pl.BlockSpec((tm // tk, w, tk), lambda i: (i, 0, 0)))
            out_shape.append(jax.ShapeDtypeStruct((n // tk, w, tk), dt))
        else:
            out_specs.append(pl.BlockSpec((tm, w), lambda i: (i, 0)))
            out_shape.append(jax.ShapeDtypeStruct((n, w), dt))
    return pl.pallas_call(
        _proj_kernel,
        grid=(n // tm,),
        in_specs=[pl.BlockSpec((tm, d), lambda i: (i, 0)), full(gain),
                  pl.BlockSpec((6, tm, LANES), lambda i: (0, i, 0)),
                  full(w_p), full(q_norm), full(uq), full(kv_norm), full(ukn), full(uv)],
        out_specs=out_specs,
        out_shape=out_shape,
        compiler_params=_cparams(("parallel",)),
        name="projection",
    )(h, gain, tables, w_p, q_norm, uq, kv_norm, ukn, uv)


def _half_masks(dtype, group):
    lane = lax.broadcasted_iota(jnp.int32, (1, LANES), 1)
    return [jnp.where((lane // group) == u, 1.0, 0.0).astype(dtype) for u in range(LANES // group)]


def _softmax_step(st, v_t, m, l, acc_ref, guard_empty=False):
    m_new = jnp.maximum(m, jnp.max(st, axis=0, keepdims=True))
    m_use = jnp.where(m_new == -jnp.inf, 0.0, m_new) if guard_empty else m_new
    alpha = jnp.exp2(m - m_use)
    p = jnp.exp2(st - m_use)
    l_new = alpha * l + jnp.sum(p, axis=0, keepdims=True)
    acc_ref[...] = alpha * acc_ref[...] + _dot(v_t, p.astype(BF16))
    return m_new, l_new


def _diag_mask(tk, m_cols, q_off):
    key_chunk = lax.broadcasted_iota(jnp.int32, (tk, 1), 0) // CHUNK
    qry_chunk = (q_off + lax.broadcasted_iota(jnp.int32, (1, m_cols), 1)) // CHUNK
    return key_chunk <= qry_chunk


def _causal_flash(problems, k_ref, v_ref, acc_ref, qi, tq, sub):
    problems = [(_transpose_bf16(q), k_lane0, v_row0, q_off) for q, k_lane0, v_row0, q_off in problems]
    m_cols = problems[0][0].shape[1]
    n_p = len(problems)
    acc_ref[...] = jnp.zeros_like(acc_ref)
    init = tuple(jnp.full((1, m_cols), -jnp.inf, F32) for _ in range(n_p)) + \
        tuple(jnp.zeros((1, m_cols), F32) for _ in range(n_p))

    def step(j, carry, masked):
        ms, ls = list(carry[:n_p]), list(carry[n_p:])
        for s0 in range(0, tq, sub):
            r0 = pl.multiple_of(j * tq + s0, sub)
            for p, (q_t, k_lane0, v_row0, q_off) in enumerate(problems):
                lo = max(s0 - q_off, 0) if masked else 0
                st = _dot(k_ref[pl.ds(r0, sub), k_lane0:k_lane0 + LANES], q_t[:, lo:])
                if masked:
                    st = jnp.where(_diag_mask(tq, m_cols, q_off)[s0:s0 + sub, lo:], st, -jnp.inf)
                m_new, l_new = _softmax_step(st, v_ref[j, v_row0:v_row0 + LANES, s0:s0 + sub], ms[p][:, lo:],
                                             ls[p][:, lo:], acc_ref.at[p, :, lo:])
                ms[p] = jnp.concatenate([ms[p][:, :lo], m_new], axis=1) if lo else m_new
                ls[p] = jnp.concatenate([ls[p][:, :lo], l_new], axis=1) if lo else l_new
        return tuple(ms) + tuple(ls)

    carry = lax.fori_loop(0, qi, lambda j, c: step(j, c, False), init)
    carry = step(qi, carry, True)
    return [acc_ref[p] / carry[n_p + p] for p in range(n_p)]


def _diff_kernel(lam_ref, subln_ref, gsum_ref, q_ref, k_ref, v_ref, o_ref, acc_ref, *, tq, lam_init):
    qi = pl.program_id(1)
    lam_rows = lam_ref[...]
    s1 = jnp.sum(lam_rows[0:1] * lam_rows[1:2], axis=1, keepdims=True)
    s2 = jnp.sum(lam_rows[2:3] * lam_rows[3:4], axis=1, keepdims=True)
    lam = jnp.exp(s1) - jnp.exp(s2) + lam_init
    masks = _half_masks(BF16, DIFF_QK_DIM)
    row = lax.broadcasted_iota(jnp.int32, (LANES, 1), 0)
    n_slices = DIFF_HEADS // 2
    problems = []
    for s in range(n_slices):
        q = q_ref[0, :, s * LANES:(s + 1) * LANES]
        problems += [(q * mk, s * LANES, s * LANES, 0) for mk in masks]
    outs = _causal_flash(problems, k_ref.at[0], v_ref, acc_ref, qi, tq, min(tq, SUB_KEYS))
    for s in range(n_slices):
        o0, o1, o2, o3 = outs[4 * s:4 * s + 4]
        a0 = o0 - lam * o1
        a1 = o2 - lam * o3
        a = jnp.where(row < DIFF_V_DIM, a0, a1).T
        ss = _dot_f32_by_exact(a * a, gsum_ref[...])
        y = a * lax.rsqrt(ss * (1.0 / DIFF_V_DIM) + NORM_EPS) * subln_ref[...]
        o_ref[0, :, s * LANES:(s + 1) * LANES] = (y * (1.0 - lam_init)).astype(o_ref.dtype)


def _diff_attention(qb, kb, vb_t, lam_rows, subln, layer, tq):
    b, s, w = qb.shape
    lam_init = 0.8 - 0.6 * math.exp(-0.3 * layer)
    subln2 = jnp.concatenate([subln, subln]).reshape(1, LANES).astype(F32)
    lane = np.arange(LANES)
    gsum = jnp.asarray((lane[:, None] // DIFF_V_DIM) == (lane[None, :] // DIFF_V_DIM), dtype=BF16)
    return pl.pallas_call(
        functools.partial(_diff_kernel, tq=tq, lam_init=lam_init),
        grid=(b, s // tq),
        in_specs=[pl.BlockSpec((8, LANES), lambda i, j: (0, 0)),
                  pl.BlockSpec((1, LANES), lambda i, j: (0, 0)),
                  pl.BlockSpec((LANES, LANES), lambda i, j: (0, 0)),
                  pl.BlockSpec((1, tq, w), lambda i, j: (i, j, 0)),
                  pl.BlockSpec((1, s, w), lambda i, j: (i, 0, 0)),
                  pl.BlockSpec((s // tq, w, tq), lambda i, j: (i, 0, 0))],
        out_specs=pl.BlockSpec((1, tq, w), lambda i, j: (i, j, 0)),
        out_shape=jax.ShapeDtypeStruct((b, s, w), BF16),
        scratch_shapes=[pltpu.VMEM((2 * DIFF_HEADS, LANES, tq), F32)],
        compiler_params=_cparams(("parallel", "arbitrary")),
        name="diff_attention",
    )(lam_rows, subln2, gsum, qb, kb, vb_t)


def _mla_kernel(q_ref, k_ref, v_ref, o_ref, acc_ref, *, tq):
    qi = pl.program_id(1)
    row = lax.broadcasted_iota(jnp.int32, (LANES, 1), 0)
    problems = [(q_ref[0, :, hd * LANES:(hd + 1) * LANES], hd * LANES, (hd // 2) * LANES, 0)
                for hd in range(MLA_HEADS)]
    outs = _causal_flash(problems, k_ref.at[0], v_ref, acc_ref, qi, tq, min(tq, SUB_KEYS))
    for pair in range(MLA_HEADS // 2):
        o_t = jnp.where(row < MLA_V_DIM, outs[2 * pair], outs[2 * pair + 1])
        o_ref[0, :, pair * LANES:(pair + 1) * LANES] = o_t.T.astype(o_ref.dtype)


def _mla_attention(qc, kc, vc_t, tq):
    b, s, wq = qc.shape
    wv = vc_t.shape[1]
    return pl.pallas_call(
        functools.partial(_mla_kernel, tq=tq),
        grid=(b, s // tq),
        in_specs=[pl.BlockSpec((1, tq, wq), lambda i, j: (i, j, 0)),
                  pl.BlockSpec((1, s, wq), lambda i, j: (i, 0, 0)),
                  pl.BlockSpec((s // tq, wv, tq), lambda i, j: (i, 0, 0))],
        out_specs=pl.BlockSpec((1, tq, wv), lambda i, j: (i, j, 0)),
        out_shape=jax.ShapeDtypeStruct((b, s, wv), BF16),
        scratch_shapes=[pltpu.VMEM((MLA_HEADS, LANES, tq), F32)],
        compiler_params=_cparams(("parallel", "arbitrary")),
        name="mla_attention",
    )(qc, kc, vc_t)


KEY_NEG_INF = -2139095041


def _score_keys(score):
    bits = lax.bitcast_convert_type(score, jnp.int32)
    return bits ^ ((bits >> 31) & 0x7FFFFFFF)


def _dsa_kernel(qa_ref, qi_ref, wi_ref, ka_ref, va_ref, ki_ref, o_ref, key_ref, hi_ref, lo_ref, aux_ref, bias_ref, acc_ref,
                *, tq, tk, top_k):
    blk = pl.program_id(1)
    s_len = ka_ref.shape[1]
    n_kv = ((blk + 1) * tq + tk - 1) // tk
    masks = _half_masks(BF16, DSA_IDX_DIM)
    row = lax.broadcasted_iota(jnp.int32, (LANES, 1), 0)
    t_chunk = (blk * tq + lax.broadcasted_iota(jnp.int32, (1, tq), 1)) // CHUNK
    sub_idx = lax.broadcasted_iota(jnp.int32, (tk, 1), 0)

    def tile_start(c):
        return pl.multiple_of(c * tk, tk)

    qi_st = jnp.concatenate([qi_ref[0, :, (hd // 2) * LANES:(hd // 2 + 1) * LANES] * masks[hd % 2]
                             for hd in range(DSA_IDX_HEADS)], axis=0)
    qi_t = _transpose_bf16(qi_st)
    w_t = wi_ref[0].T

    sub = min(tk, SUB_KEYS)
    sub_iota = lax.broadcasted_iota(jnp.int32, (sub, 1), 0)

    def score_body(c, carry):
        for s0 in range(0, tk, sub):
            r0 = pl.multiple_of(c * tk + s0, sub)
            ki = ki_ref[0, pl.ds(r0, sub), :]
            score = jnp.zeros((sub, tq), F32)
            for hd in range(DSA_IDX_HEADS):
                rel = jnp.maximum(_dot(ki, qi_t[:, hd * tq:(hd + 1) * tq]), 0.0)
                score = score + w_t[hd:hd + 1, :] * rel
            score = score * DSA_IDX_SCALE
            score = jnp.where((r0 + sub_iota) // CHUNK <= t_chunk, score, -jnp.inf)
            key = _score_keys(score)
            key_ref[pl.ds(r0, sub), :] = key
            hi_ref[pl.ds(r0, sub), :] = (key >> 16).astype(jnp.int16)
            lo_ref[pl.ds(r0, sub), :] = ((key & 0xFFFF) - 32768).astype(jnp.int16)
        return carry

    lax.fori_loop(0, n_kv, score_body, 0)

    def count16(ref, cand, strict=False):
        cand16 = cand.astype(jnp.int16)

        def body(c, tot):
            x = ref[pl.ds(tile_start(c), tk), :]
            ones = jnp.where((x > cand16) if strict else (x >= cand16), jnp.bfloat16(1), jnp.bfloat16(0))
            parts = [ones[i * 16:(i + 1) * 16] for i in range(tk // 16)]
            while len(parts) > 1:
                parts = [parts[i] + parts[i + 1] for i in range(0, len(parts), 2)]
            return tot + parts[0].astype(F32)
        tot = lax.fori_loop(0, n_kv, body, jnp.zeros((16, tq), F32))
        return jnp.sum(tot, axis=0, keepdims=True)

    def kth_largest16(ref, need):
        start = jnp.where(count16(ref, jnp.zeros((1, tq), jnp.int32)) >= need, 0, -32768).astype(jnp.int32)

        def body(i, t):
            cand = t | (jnp.int32(1) << (14 - i))
            return jnp.where(count16(ref, cand) >= need, cand, t)
        return lax.fori_loop(0, 15, body, start)

    def count(pred_fn):
        n_part = 8

        def body(c, tot):
            ones = jnp.where(pred_fn(tile_start(c)), 1.0, 0.0)
            return tot + jnp.sum(ones.reshape(tk // (8 * n_part), n_part * 8, tq), axis=0)
        tot = lax.fori_loop(0, n_kv, body, jnp.zeros((n_part * 8, tq), F32))
        return jnp.sum(tot, axis=0, keepdims=True)

    thr_hi = kth_largest16(hi_ref, float(top_k))
    need_lo = top_k - count16(hi_ref, thr_hi, strict=True)
    thr_hi16 = thr_hi.astype(jnp.int16)

    def bucket_body(c, carry):
        r0 = tile_start(c)
        lo_ref[pl.ds(r0, tk), :] = jnp.where(hi_ref[pl.ds(r0, tk), :] == thr_hi16, lo_ref[pl.ds(r0, tk), :],
                                             jnp.int16(-32768))
        return carry

    lax.fori_loop(0, n_kv, bucket_body, 0)
    thr_lo = kth_largest16(lo_ref, need_lo)
    thr = (thr_hi << 16) | ((thr_lo + 32768) & 0xFFFF)
    cnt_thr = count(lambda r0: key_ref[pl.ds(r0, tk), :] >= thr)
    thr = jnp.maximum(thr, KEY_NEG_INF)
    tied = jnp.where(cnt_thr > top_k, jnp.where(thr > KEY_NEG_INF, 1.0, 0.0), 0.0)

    def select_with_ties():
        n_gt = count(lambda r0: key_ref[pl.ds(r0, tk), :] > thr)
        need = top_k - n_gt

        def aux_body(c, carry):
            r0 = tile_start(c)
            idx = r0 + sub_idx
            tie = jnp.where(key_ref[pl.ds(r0, tk), :] == thr, jnp.where(idx // CHUNK <= t_chunk, idx, IDX_BIG), IDX_BIG)
            aux_ref[pl.ds(r0, tk), :] = tie
            return carry

        lax.fori_loop(0, n_kv, aux_body, 0)
        n_bits = max(1, (s_len - 1).bit_length())

        def tie_body(i, last):
            cand = last | (jnp.int32(1) << (n_bits - 1 - i))
            cnt = count(lambda r0: aux_ref[pl.ds(r0, tk), :] < cand)
            return jnp.where(cnt < need, cand, last)

        last = lax.fori_loop(0, n_bits, tie_body, jnp.zeros((1, tq), jnp.int32))

        def bias_body(c, carry):
            r0 = tile_start(c)
            bias_ref[pl.ds(r0, tk), :] = jnp.where(
                key_ref[pl.ds(r0, tk), :] > thr, 0.0, jnp.where(aux_ref[pl.ds(r0, tk), :] <= last, 0.0, -jnp.inf))
            return carry

        lax.fori_loop(0, n_kv, bias_body, 0)

    def select_no_ties():
        low = jnp.where(thr > KEY_NEG_INF, thr - 1, thr)

        def bias_body(c, carry):
            r0 = tile_start(c)
            bias_ref[pl.ds(r0, tk), :] = jnp.where(key_ref[pl.ds(r0, tk), :] > low, 0.0, -jnp.inf)
            return carry

        lax.fori_loop(0, n_kv, bias_body, 0)

    lax.cond(jnp.max(tied) > 0.0, select_with_ties, select_no_ties)

    hmask = _half_masks(BF16, DSA_HEAD_DIM)
    qa_st = jnp.concatenate([qa_ref[0, :, (hd // 2) * LANES:(hd // 2 + 1) * LANES] * hmask[hd % 2]
                             for hd in range(DSA_HEADS)], axis=0)
    qa_t = _transpose_bf16(qa_st)
    n_h = DSA_HEADS
    acc_ref[...] = jnp.zeros_like(acc_ref)

    def att_body(c, carry):
        ms, ls = list(carry[:n_h]), list(carry[n_h:])
        for s0 in range(0, tk, sub):
            r0 = pl.multiple_of(c * tk + s0, sub)
            bias = bias_ref[pl.ds(r0, sub), :]
            ka = ka_ref[0, pl.ds(r0, sub), :]
            for hd in range(n_h):
                st = _dot(ka, qa_t[:, hd * tq:(hd + 1) * tq]) + bias
                ms[hd], ls[hd] = _softmax_step(st, va_ref[c, :, s0:s0 + sub], ms[hd], ls[hd], acc_ref.at[hd],
                                               guard_empty=True)
        return tuple(ms) + tuple(ls)

    init = tuple(jnp.full((1, tq), -jnp.inf, F32) for _ in range(n_h)) + \
        tuple(jnp.zeros((1, tq), F32) for _ in range(n_h))
    carry = lax.fori_loop(0, n_kv, att_body, init)
    for p in range(n_h // 2):
        pair_t = jnp.where(row < DSA_HEAD_DIM, acc_ref[2 * p] / carry[n_h + 2 * p],
                           acc_ref[2 * p + 1] / carry[n_h + 2 * p + 1])
        o_ref[0, :, p * LANES:(p + 1) * LANES] = pair_t.T.astype(o_ref.dtype)


def _dsa_attention(qa, qi, wi, ka2, va2_t, ki2, tq, tk):
    b, s, _ = qa.shape
    top_k = min(DSA_TOPK_MAX, s // 4)
    qspec = lambda w: pl.BlockSpec((1, tq, w), lambda i, j: (i, j, 0))
    kspec = lambda w: pl.BlockSpec((1, s, w), lambda i, j: (i, 0, 0))
    return pl.pallas_call(
        functools.partial(_dsa_kernel, tq=tq, tk=tk, top_k=top_k),
        grid=(b, s // tq),
        in_specs=[qspec(qa.shape[2]), qspec(qi.shape[2]), qspec(LANES), kspec(LANES),
                  pl.BlockSpec((s // tk, LANES, tk), lambda i, j: (i, 0, 0)), kspec(LANES)],
        out_specs=qspec(qa.shape[2]),
        out_shape=jax.ShapeDtypeStruct(qa.shape, BF16),
        scratch_shapes=[pltpu.VMEM((s, tq), jnp.int32), pltpu.VMEM((s, tq), jnp.int16), pltpu.VMEM((s, tq), jnp.int16),
                        pltpu.VMEM((s, tq), jnp.int32), pltpu.VMEM((s, tq), F32),
                        pltpu.VMEM((DSA_HEADS, LANES, tq), F32)],
        compiler_params=_cparams(("parallel", "arbitrary")),
        name="dsa_attention",
    )(qa, qi, wi, ka2, va2_t, ki2)


def _attn_residual(h_ref, oa_ref, ob_ref, oc_ref, wo_ref):
    wa, wb = oa_ref.shape[1], ob_ref.shape[1]
    return (h_ref[...] + _dot(oa_ref[...], wo_ref[0:wa, :]) + _dot(ob_ref[...], wo_ref[wa:wa + wb, :])
            + _dot(oc_ref[...], wo_ref[wa + wb:, :]))


def _swiglu_partial(x, wg, wu, wd, group=None):
    group = group or wg.shape[1]
    out = None
    for c0 in range(0, wg.shape[1], group):
        c1 = min(c0 + group, wg.shape[1])
        g = _dot(x, wg[:, c0:c1])
        u = _dot(x, wu[:, c0:c1])
        part = _dot(((g * jax.nn.sigmoid(g)) * u).astype(BF16), wd[c0:c1, :])
        out = part if out is None else out + part
    return out


def _dense_block_kernel(h_ref, oa_ref, ob_ref, oc_ref, wo_ref, g_ref, wg_ref, wu_ref, wd_ref, fg_ref,
                        out_ref, xn_ref, acc_ref, *, final_norm):
    f = pl.program_id(1)

    @pl.when(f == 0)
    def _():
        h1 = _attn_residual(h_ref, oa_ref, ob_ref, oc_ref, wo_ref)
        acc_ref[...] = h1
        xn_ref[...] = _rms(h1, g_ref[...]).astype(BF16)

    acc_ref[...] += _swiglu_partial(xn_ref[...], wg_ref[...], wu_ref[...], wd_ref[...], group=MXU_COLS)

    @pl.when(f == pl.num_programs(1) - 1)
    def _():
        y = acc_ref[...]
        out_ref[...] = _rms(y, fg_ref[...]) if final_norm else y


def _dense_block(h, oa, ob, oc, w_out, gain, wg, wu, wd, final_gain, *, final_norm, tm, tf):
    n, d = h.shape
    dff = wg.shape[1]
    const = lambda a: pl.BlockSpec(a.shape, lambda i, f: (0,) * a.ndim, pipeline_mode=pl.Buffered(1))
    row = lambda w: pl.BlockSpec((tm, w), lambda i, f: (i, 0))
    gain = gain.reshape(1, d)
    final_gain = final_gain.reshape(1, d)
    return pl.pallas_call(
        functools.partial(_dense_block_kernel, final_norm=final_norm),
        grid=(n // tm, dff // tf),
        in_specs=[row(d), row(oa.shape[1]), row(ob.shape[1]), row(oc.shape[1]), const(w_out), const(gain),
                  pl.BlockSpec((d, tf), lambda i, f: (0, f)),
                  pl.BlockSpec((d, tf), lambda i, f: (0, f)),
                  pl.BlockSpec((tf, d), lambda i, f: (f, 0)),
                  const(final_gain)],
        out_specs=row(d),
        out_shape=jax.ShapeDtypeStruct((n, d), F32),
        scratch_shapes=[pltpu.VMEM((tm, d), BF16), pltpu.VMEM((tm, d), F32)],
        compiler_params=_cparams(("parallel", "arbitrary")),
        name="block_dense",
    )(h, oa, ob, oc, w_out, gain, wg, wu, wd, final_gain)


def _moe_block_kernel(h_ref, oa_ref, ob_ref, oc_ref, wo_ref, g_ref, r_ref, tri_ref, wg_ref, wu_ref, wd_ref, fg_ref,
                      out_ref, xn_ref, acc_ref, gate_ref, pos_ref, pos_t_ref, cnt_ref, xs_ref, ye_ref,
                      *, n_exp, mains, rows, final_norm):
    e = pl.program_id(1)
    f = pl.program_id(2)
    n_f = pl.num_programs(2)
    tm = xn_ref.shape[0]
    lane = lax.broadcasted_iota(jnp.int32, (1, LANES), 1)

    @pl.when((e == 0) & (f == 0))
    def _():
        h1 = _attn_residual(h_ref, oa_ref, ob_ref, oc_ref, wo_ref)
        acc_ref[...] = h1
        hn = _rms(h1, g_ref[...])
        xn_ref[...] = hn.astype(BF16)
        hn_hi = hn.astype(BF16)
        hn_lo = (hn - hn_hi.astype(F32)).astype(BF16)
        parts = _dot(hn_hi, r_ref[...]) + _dot(hn_lo, r_ref[...])
        logits = parts[:, :LANES] + parts[:, LANES:]
        logits = jnp.where(lane < n_exp, logits, -jnp.inf)
        m1 = jnp.max(logits, axis=1, keepdims=True)
        i1 = jnp.min(jnp.where(logits == m1, lane, IDX_BIG), axis=1, keepdims=True)
        rest = jnp.where(lane == i1, -jnp.inf, logits)
        m2 = jnp.max(rest, axis=1, keepdims=True)
        i2 = jnp.min(jnp.where(rest == m2, lane, IDX_BIG), axis=1, keepdims=True)
        e2 = jnp.exp(m2 - m1)
        den = 1.0 + e2
        gate_ref[...] = jnp.where(lane == i1, 1.0 / den, jnp.where(lane == i2, e2 / den, 0.0))
        routed = jnp.where(lane == i1, 1.0, jnp.where(lane == i2, 1.0, 0.0))
        before = _dot(tri_ref[...], routed.astype(BF16))
        slot = jnp.where(routed > 0.0, before, -1.0)
        pos_ref[...] = slot
        pos_t_ref[...] = slot.T
        cnt = jnp.sum(routed, axis=0, keepdims=True)
        for k in range(n_exp):
            cnt_ref[k] = jnp.sum(jnp.where(lane == k, cnt, 0.0)).astype(jnp.int32)

    n_tok = cnt_ref[e]
    top = mains[-1]
    n_extra = (jnp.maximum(n_tok - top, 0) + rows - 1) // rows

    def for_blocks(fn):
        below = 0
        for size in mains:
            fits = (n_tok > below) if size == top else ((n_tok > below) & (n_tok <= size))

            @pl.when(fits)
            def _(size=size):
                fn(0, size)
                if size == top:
                    lax.fori_loop(0, n_extra, lambda r, c: (
                        fn(pl.multiple_of(top + r * rows, math.gcd(top, rows)), rows), c)[1], 0)

            below = size

    @pl.when(f == 0)
    def _():
        slot_row = pos_t_ref[pl.ds(e, 1), :]

        def gather(r0, nr):
            want = (r0 + lax.broadcasted_iota(jnp.int32, (nr, 1), 0)).astype(F32)
            pick = jnp.where(slot_row == want, 1.0, 0.0).astype(BF16)
            xs_ref[pl.ds(r0, nr), :] = _dot(pick, xn_ref[...]).astype(BF16)
            ye_ref[pl.ds(r0, nr), :] = jnp.zeros((nr, ye_ref.shape[1]), F32)

        for_blocks(gather)

    def expert(r0, nr):
        ye_ref[pl.ds(r0, nr), :] += _swiglu_partial(xs_ref[pl.ds(r0, nr), :], wg_ref[0], wu_ref[0], wd_ref[0])

    for_blocks(expert)

    @pl.when(f == n_f - 1)
    def _():
        full_lane = lax.broadcasted_iota(jnp.int32, (tm, LANES), 1)
        slot_col = jnp.sum(jnp.where(full_lane == e, pos_ref[...], 0.0), axis=1, keepdims=True)
        gate_col = jnp.sum(jnp.where(full_lane == e, gate_ref[...], 0.0), axis=1, keepdims=True)

        def scatter(r0, nr):
            have = (r0 + lax.broadcasted_iota(jnp.int32, (1, nr), 1)).astype(F32)
            place = jnp.where(slot_col == have, 1.0, 0.0).astype(BF16)
            y = ye_ref[pl.ds(r0, nr), :]
            acc_ref[...] += gate_col * _dot(place, y.astype(BF16))

        for_blocks(scatter)

    @pl.when((e == n_exp - 1) & (f == n_f - 1))
    def _():
        y = acc_ref[...]
        out_ref[...] = _rms(y, fg_ref[...]) if final_norm else y


def _moe_block(h, oa, ob, oc, w_out, gain, router_p, wg, wu, wd, final_gain, *, final_norm, tm, tf, mains, rows):
    n, d = h.shape
    n_exp, _, dff = wg.shape
    const = lambda a: pl.BlockSpec(a.shape, lambda i, e, f: (0,) * a.ndim, pipeline_mode=pl.Buffered(1))
    row = lambda w: pl.BlockSpec((tm, w), lambda i, e, f: (i, 0))
    row_in = lambda w: pl.BlockSpec((tm, w), lambda i, e, f: (i, 0), pipeline_mode=pl.Buffered(1))
    gain = gain.reshape(1, d)
    final_gain = final_gain.reshape(1, d)
    tok = np.arange(tm)
    tri = jnp.asarray(tok[None, :] < tok[:, None], dtype=BF16)
    mains = tuple(sorted({min(m, tm) for m in mains}))
    cap = mains[-1] + -(-(tm - mains[-1]) // rows) * rows
    return pl.pallas_call(
        functools.partial(_moe_block_kernel, n_exp=n_exp, mains=mains, rows=rows, final_norm=final_norm),
        grid=(n // tm, n_exp, dff // tf),
        in_specs=[row_in(d), row_in(oa.shape[1]), row_in(ob.shape[1]), row_in(oc.shape[1]), const(w_out), const(gain),
                  const(router_p), const(tri),
                  pl.BlockSpec((1, d, tf), lambda i, e, f: (e, 0, f)),
                  pl.BlockSpec((1, d, tf), lambda i, e, f: (e, 0, f)),
                  pl.BlockSpec((1, tf, d), lambda i, e, f: (e, f, 0)),
                  const(final_gain)],
        out_specs=row(d),
        out_shape=jax.ShapeDtypeStruct((n, d), F32),
        scratch_shapes=[pltpu.VMEM((tm, d), BF16), pltpu.VMEM((tm, d), F32), pltpu.VMEM((tm, LANES), F32),
                        pltpu.VMEM((tm, LANES), F32), pltpu.VMEM((LANES, tm), F32), pltpu.SMEM((n_exp,), jnp.int32),
                        pltpu.VMEM((cap, d), BF16), pltpu.VMEM((cap, d), F32)],
        compiler_params=_cparams(("parallel", "arbitrary", "arbitrary")),
        name="block_moe",
    )(h, oa, ob, oc, w_out, gain, router_p, tri, wg, wu, wd, final_gain)


def _pick(n, pref):
    t = min(pref, n)
    while n % t:
        t //= 2
    return t


def kernel(x, positions, attn_norm, w_in, mla_q_norm, w_uq, mla_kv_norm, w_ukv, diff_lambda_q1, diff_lambda_k1, diff_lambda_q2, diff_lambda_k2, diff_subln, w_out, ffn_norm, dense_w_gate, dense_w_up, dense_w_down, moe_router, moe_w_gate, moe_w_up, moe_w_down, final_norm):
    b, s, d = x.shape
    n = b * s
    depth = w_in.shape[0]
    tm_proj = _pick(n, 1024)
    tm_blk = _pick(n, 1024)
    tk = _pick(s, KV_TILE)
    tk_dsa = _pick(s, DSA_KV_TILE)
    key_tiles = {"va2": tk_dsa, "vb": tk, "vc": tk}
    tq_dsa = _pick(s, 256)

    tables = _rope_tables(positions.astype(F32).reshape(n, 1), _pick(n, 1024))
    h = x.reshape(n, d)
    r3 = lambda a: a.reshape(b, s, a.shape[-1])
    for layer in range(depth):
        w_p, uq, ukn, uv = _prep_proj_weights(w_in[layer], w_uq[layer], w_ukv[layer])
        (qa, ka2, qi, ki2, qb, kb, va2, wi, vb, qc, kc, vc) = _project(
            h, attn_norm[layer], tables, w_p, mla_q_norm[layer], uq, mla_kv_norm[layer], ukn, uv, tm_proj, key_tiles)
        oa = _dsa_attention(r3(qa), r3(qi), r3(wi), r3(ka2), va2, r3(ki2), tq_dsa, tk_dsa)
        lam_rows = jnp.zeros((8, LANES), F32).at[0:4, 0:DIFF_QK_DIM].set(jnp.stack(
            [diff_lambda_q1[layer], diff_lambda_k1[layer], diff_lambda_q2[layer], diff_lambda_k2[layer]]))
        ob = _diff_attention(r3(qb), r3(kb), vb, lam_rows, diff_subln[layer], layer, tk)
        oc = _mla_attention(r3(qc), r3(kc), vc, tk)
        j = layer // 2
        last = layer == depth - 1
        wo = w_out[layer].astype(BF16)
        attn = (oa.reshape(n, -1), ob.reshape(n, -1), oc.reshape(n, -1))
        if layer % 2 == 0:
            h = _dense_block(h, *attn, wo, ffn_norm[layer], dense_w_gate[j].astype(BF16), dense_w_up[j].astype(BF16),
                             dense_w_down[j].astype(BF16), final_norm,
                             final_norm=last, tm=tm_blk, tf=_pick(dense_w_gate.shape[2], DENSE_FFN_TILE))
        else:
            router_f = jnp.pad(moe_router[j], ((0, 0), (0, LANES - MOE_EXPERTS)))
            router_hi = router_f.astype(BF16)
            router_p = jnp.concatenate([router_hi, (router_f - router_hi.astype(F32)).astype(BF16)], axis=1)
            h = _moe_block(h, *attn, wo, ffn_norm[layer], router_p,
                           moe_w_gate[j].astype(BF16), moe_w_up[j].astype(BF16), moe_w_down[j].astype(BF16), final_norm,
                           final_norm=last, tm=tm_blk, tf=_pick(moe_w_gate.shape[3], FFN_TILE),
                           mains=MOE_MAIN_ROWS, rows=MOE_EXTRA_ROWS)
    return h.reshape(b, s, d)
```
